```python
import math
import jax, jax.numpy as jnp
from jax import lax
import numpy as np

D_MODEL = 1024
BATCH = 4
SEQ = 8192
DEPTH = 1
DEC_BATCH = 128
DEC_SEQ = 8
PAST_LEN = 16384
PAGE_SIZE = 128

SSM_WIDTH = D_MODEL // 2
SSM_GROUP = 16
N_SSM_GROUPS = SSM_WIDTH // SSM_GROUP
SSM_STATE = 64
ATTN_WIDTH = D_MODEL - SSM_WIDTH
HEAD_DIM = 64
N_HEADS = ATTN_WIDTH // HEAD_DIM
N_KV_HEADS = 2
GQA = N_HEADS // N_KV_HEADS
KV_WIDTH = N_KV_HEADS * HEAD_DIM
IN_WIDTH = SSM_WIDTH + ATTN_WIDTH + 2 * KV_WIDTH
WINDOW = 128
BLOCK = 128
NUM_BUCKETS = 32
MAX_DISTANCE = 128
N_EXPERT_GROUPS = 4
EXPERTS_PER_GROUP = 4
N_EXPERTS = N_EXPERT_GROUPS * EXPERTS_PER_GROUP
TOP_K = 2
D_EXPERT = 256
N_MOD = 6
EPS = 1e-6
NEG_INF = -1e30

kernel_name = "hymba_s5_swa_sink_hmoe_decode_step"


def rms_norm(x, g):
    x32 = x.astype(jnp.float32)
    y = x32 * lax.rsqrt(jnp.mean(x32 * x32, axis=-1, keepdims=True) + EPS)
    return (y * g.astype(jnp.float32)).astype(x.dtype)


def rel_bucket(dist):
    max_exact = NUM_BUCKETS // 2
    d = jnp.maximum(dist, 0)
    log_ratio = jnp.log(jnp.maximum(d, 1).astype(jnp.float32) / max_exact) / math.log(MAX_DISTANCE / max_exact)
    large = jnp.minimum(max_exact + (log_ratio * (NUM_BUCKETS - max_exact)).astype(jnp.int32), NUM_BUCKETS - 1)
    return jnp.where(d < max_exact, d, large)


def rel_bias(dist, table):
    b = table[rel_bucket(dist)].astype(jnp.float32)
    b = jnp.moveaxis(b, -1, 0)
    return b.reshape((N_KV_HEADS, GQA) + dist.shape)


def sink_probs(s, sinks):
    sk = sinks.astype(jnp.float32).reshape(N_KV_HEADS, GQA, 1, 1)
    m = jnp.maximum(jnp.max(s, axis=-1, keepdims=True), sk)
    p = jnp.exp(s - m)
    return p / (jnp.sum(p, axis=-1, keepdims=True) + jnp.exp(sk - m))


def attn_prompt(q, k, v, sinks, table):
    b, s = q.shape[:2]
    nb = s // BLOCK
    qb = q.reshape(b, nb, BLOCK, N_KV_HEADS, GQA, HEAD_DIM)
    kb = k.reshape(b, nb, BLOCK, N_KV_HEADS, HEAD_DIM)
    vb = v.reshape(b, nb, BLOCK, N_KV_HEADS, HEAD_DIM)
    kk = jnp.concatenate([jnp.concatenate([jnp.zeros_like(kb[:, :1]), kb[:, :-1]], axis=1), kb], axis=2)
    vv = jnp.concatenate([jnp.concatenate([jnp.zeros_like(vb[:, :1]), vb[:, :-1]], axis=1), vb], axis=2)
    sc = jnp.einsum('bnqhgd,bnkhd->bnhgqk', qb, kk).astype(jnp.float32) * (HEAD_DIM ** -0.5)
    qi = jnp.arange(BLOCK)[:, None]
    kj = jnp.arange(2 * BLOCK)[None, :]
    dist = qi + BLOCK - kj
    band = (dist >= 0) & (dist <= WINDOW)
    exists = (jnp.arange(nb)[:, None] * BLOCK - BLOCK + kj) >= 0
    valid = band[None] & exists[:, None, :]
    sc = jnp.where(valid[None, :, None, None], sc + rel_bias(dist, table), NEG_INF)
    p = sink_probs(sc, sinks).astype(v.dtype)
    out = jnp.einsum('bnhgqk,bnkhd->bnqhgd', p, vv).reshape(b, s, ATTN_WIDTH)
    w = min(WINDOW, s)
    return out, k[:, s - w:], v[:, s - w:]


def attn_sample(q, k, v, k_cache, v_cache, sinks, table):
    t = q.shape[1]
    w = k_cache.shape[1]
    kk = jnp.concatenate([k_cache.astype(k.dtype), k], axis=1)
    vv = jnp.concatenate([v_cache.astype(v.dtype), v], axis=1)
    sc = jnp.einsum('bqhgd,bkhd->bhgqk', q, kk).astype(jnp.float32) * (HEAD_DIM ** -0.5)
    dist = jnp.arange(t)[:, None] + w - jnp.arange(w + t)[None, :]
    valid = (dist >= 0) & (dist <= WINDOW)
    sc = jnp.where(valid, sc + rel_bias(dist, table), NEG_INF)
    p = sink_probs(sc, sinks).astype(v.dtype)
    out = jnp.einsum('bhgqk,bkhd->bqhgd', p, vv).reshape(q.shape[0], t, ATTN_WIDTH)
    return out, kk[:, t:], vv[:, t:]


def s5_scan(u, lam_re, lam_im, log_dt, b_re, b_im, c_re, c_im, d_skip, h0_re=None, h0_im=None):
    bsz, l = u.shape[:2]
    u32 = u.astype(jnp.float32).reshape(bsz, l, N_SSM_GROUPS, SSM_GROUP)
    lam = lax.complex(lam_re.astype(jnp.float32), lam_im.astype(jnp.float32))
    dt = jnp.exp(log_dt.astype(jnp.float32))[:, None]
    lbar = jnp.exp(lam * dt)
    bbar = ((lbar - 1.0) / lam)[..., None] * lax.complex(b_re.astype(jnp.float32), b_im.astype(jnp.float32))
    cmat = lax.complex(c_re.astype(jnp.float32), c_im.astype(jnp.float32))
    bu = jnp.einsum('blgc,gpc->lbgp', u32.astype(jnp.complex64), bbar)
    if h0_re is not None:
        h0 = lax.complex(h0_re.astype(jnp.float32), h0_im.astype(jnp.float32))
        bu = bu.at[0].add(lbar * h0)
    a = jnp.broadcast_to(lbar, (l, 1) + lbar.shape)

    def combine(e1, e2):
        a1, b1 = e1
        a2, b2 = e2
        return a2 * a1, a2 * b1 + b2

    _, h = lax.associative_scan(combine, (a, bu), axis=0)
    y = jnp.einsum('lbgp,gcp->blgc', h, cmat).real
    y = y + d_skip.astype(jnp.float32).reshape(N_SSM_GROUPS, SSM_GROUP) * u32
    return y.reshape(bsz, l, SSM_WIDTH).astype(u.dtype), h[-1]


def hier_moe(x, w_gr, b_gr, w_er, b_er, w_gate, w_up, w_down):
    shp = x.shape
    xt = x.reshape(-1, D_MODEL)
    g_logits = (xt @ w_gr + b_gr).astype(jnp.float32)
    g_prob = jax.nn.softmax(g_logits, axis=-1)
    g_idx = jnp.argmax(g_logits, axis=-1)
    g_p = jnp.take_along_axis(g_prob, g_idx[:, None], axis=-1)
    e_all = (jnp.einsum('td,dge->tge', xt, w_er) + b_er).astype(jnp.float32)
    e_logits = jnp.take_along_axis(e_all, g_idx[:, None, None], axis=1)[:, 0]
    top_p, top_i = lax.top_k(jax.nn.softmax(e_logits, axis=-1), TOP_K)
    wts = g_p * top_p / jnp.sum(top_p, axis=-1, keepdims=True)
    eid = g_idx[:, None] * EXPERTS_PER_GROUP + top_i
    gates = jnp.sum(jax.nn.one_hot(eid, N_EXPERTS, dtype=jnp.float32) * wts[..., None], axis=1).astype(x.dtype)
    hg = jnp.einsum('td,edf->tef', xt, w_gate)
    hu = jnp.einsum('td,edf->tef', xt, w_up)
    act = jax.nn.silu(hg) * hu * gates[..., None]
    return jnp.einsum('tef,efd->td', act, w_down).reshape(shp)


def layer(x, c, p, table, kv_cache=None, h0=None):
    bsz, l = x.shape[:2]
    mod = (jax.nn.silu(c) @ p['w_ada'] + p['b_ada'])[:, None, :]
    sh1, sc1, gt1, sh2, sc2, gt2 = jnp.split(mod, N_MOD, axis=-1)
    hn = rms_norm(x, p['g_pre_mix']) * (1.0 + sc1) + sh1
    proj = hn @ p['w_in']
    u, q, k, v = jnp.split(proj, [SSM_WIDTH, SSM_WIDTH + ATTN_WIDTH, SSM_WIDTH + ATTN_WIDTH + KV_WIDTH], axis=-1)
    q = q.reshape(bsz, l, N_KV_HEADS, GQA, HEAD_DIM)
    k = k.reshape(bsz, l, N_KV_HEADS, HEAD_DIM)
    v = v.reshape(bsz, l, N_KV_HEADS, HEAD_DIM)
    if kv_cache is None:
        attn, k_st, v_st = attn_prompt(q, k, v, p['attn_sinks'], table)
        ssm_y, h_last = s5_scan(u, p['ssm_lam_re'], p['ssm_lam_im'], p['ssm_log_dt'], p['ssm_b_re'], p['ssm_b_im'],
                                p['ssm_c_re'], p['ssm_c_im'], p['ssm_d'])
    else:
        attn, k_st, v_st = attn_sample(q, k, v, kv_cache[0], kv_cache[1], p['attn_sinks'], table)
        ssm_y, h_last = s5_scan(u, p['ssm_lam_re'], p['ssm_lam_im'], p['ssm_log_dt'], p['ssm_b_re'], p['ssm_b_im'],
                                p['ssm_c_re'], p['ssm_c_im'], p['ssm_d'], h0[0], h0[1])
    z = jax.nn.gelu(ssm_y)
    ssm_out = z * jax.nn.sigmoid(z @ p['w_glu'] + p['b_glu'])
    mixed = jnp.concatenate([rms_norm(ssm_out, p['g_ssm_out']), rms_norm(attn, p['g_attn_out'])], axis=-1) @ p['w_out']
    x = x + gt1 * rms_norm(mixed, p['g_post_mix'])
    hn2 = rms_norm(x, p['g_pre_ffn']) * (1.0 + sc2) + sh2
    ffn = hier_moe(hn2, p['w_group_router'], p['b_group_router'], p['w_expert_router'], p['b_expert_router'],
                   p['w_exp_gate'], p['w_exp_up'], p['w_exp_down'])
    x = x + gt2 * rms_norm(ffn, p['g_post_ffn'])
    return x, k_st, v_st, h_last.real.astype(x.dtype), h_last.imag.astype(x.dtype)


def setup_inputs(seed: int = 0) -> dict:
    key = jax.random.key(seed)
    ks = iter(jax.random.split(key, 48))

    def nrm(shape, scale):
        return jax.random.normal(next(ks), shape, jnp.float32) * scale

    wbuf = min(WINDOW, PAST_LEN)
    lam_im = jnp.broadcast_to(jnp.pi * jnp.arange(SSM_STATE, dtype=jnp.float32), (DEPTH, N_SSM_GROUPS, SSM_STATE))
    return {
        'x_prompt': nrm((BATCH, SEQ, D_MODEL), 1.0),
        'x_sample': nrm((DEC_BATCH, DEC_SEQ, D_MODEL), 1.0),
        'cache_k': nrm((DEPTH, DEC_BATCH, wbuf, N_KV_HEADS, HEAD_DIM), 1.0),
        'cache_v': nrm((DEPTH, DEC_BATCH, wbuf, N_KV_HEADS, HEAD_DIM), 1.0),
        'state_ssm_re': nrm((DEPTH, DEC_BATCH, N_SSM_GROUPS, SSM_STATE), 0.3),
        'state_ssm_im': nrm((DEPTH, DEC_BATCH, N_SSM_GROUPS, SSM_STATE), 0.3),
        'c_prompt': nrm((BATCH, D_MODEL), 1.0),
        'c_sample': nrm((DEC_BATCH, D_MODEL), 1.0),
        'rel_bias_table': nrm((NUM_BUCKETS, N_HEADS), 0.5),
        'w_ada': nrm((DEPTH, D_MODEL, N_MOD * D_MODEL), 0.5 * D_MODEL ** -0.5),
        'b_ada': nrm((DEPTH, N_MOD * D_MODEL), 0.02),
        'g_pre_mix': 1.0 + nrm((DEPTH, D_MODEL), 0.05),
        'g_post_mix': 1.0 + nrm((DEPTH, D_MODEL), 0.05),
        'g_pre_ffn': 1.0 + nrm((DEPTH, D_MODEL), 0.05),
        'g_post_ffn': 1.0 + nrm((DEPTH, D_MODEL), 0.05),
        'w_in': nrm((DEPTH, D_MODEL, IN_WIDTH), D_MODEL ** -0.5),
        'ssm_lam_re': -0.5 + nrm((DEPTH, N_SSM_GROUPS, SSM_STATE), 0.01),
        'ssm_lam_im': lam_im + nrm((DEPTH, N_SSM_GROUPS, SSM_STATE), 0.01),
        'ssm_log_dt': jax.random.uniform(next(ks), (DEPTH, N_SSM_GROUPS), jnp.float32, math.log(1e-3), math.log(1e-1)),
        'ssm_b_re': nrm((DEPTH, N_SSM_GROUPS, SSM_STATE, SSM_GROUP), (2 * SSM_GROUP) ** -0.5),
        'ssm_b_im': nrm((DEPTH, N_SSM_GROUPS, SSM_STATE, SSM_GROUP), (2 * SSM_GROUP) ** -0.5),
        'ssm_c_re': nrm((DEPTH, N_SSM_GROUPS, SSM_GROUP, SSM_STATE), (2 * SSM_STATE) ** -0.5),
        'ssm_c_im': nrm((DEPTH, N_SSM_GROUPS, SSM_GROUP, SSM_STATE), (2 * SSM_STATE) ** -0.5),
        'ssm_d': nrm((DEPTH, SSM_WIDTH), 0.5),
        'w_glu': nrm((DEPTH, SSM_WIDTH, SSM_WIDTH), SSM_WIDTH ** -0.5),
        'b_glu': nrm((DEPTH, SSM_WIDTH), 0.02),
        'attn_sinks': nrm((DEPTH, N_HEADS), 0.5),
        'g_ssm_out': 1.0 + nrm((DEPTH, SSM_WIDTH), 0.05),
        'g_attn_out': 1.0 + nrm((DEPTH, ATTN_WIDTH), 0.05),
        'w_out': nrm((DEPTH, D_MODEL, D_MODEL), D_MODEL ** -0.5),
        'w_group_router': nrm((DEPTH, D_MODEL, N_EXPERT_GROUPS), D_MODEL ** -0.5),
        'b_group_router': nrm((DEPTH, N_EXPERT_GROUPS), 0.01),
        'w_expert_router': nrm((DEPTH, D_MODEL, N_EXPERT_GROUPS, EXPERTS_PER_GROUP), D_MODEL ** -0.5),
        'b_expert_router': nrm((DEPTH, N_EXPERT_GROUPS, EXPERTS_PER_GROUP), 0.01),
        'w_exp_gate': nrm((DEPTH, N_EXPERTS, D_MODEL, D_EXPERT), D_MODEL ** -0.5),
        'w_exp_up': nrm((DEPTH, N_EXPERTS, D_MODEL, D_EXPERT), D_MODEL ** -0.5),
        'w_exp_down': nrm((DEPTH, N_EXPERTS, D_EXPERT, D_MODEL), D_EXPERT ** -0.5),
    }


def reference(x_prompt, x_sample, cache_k, cache_v, state_ssm_re, state_ssm_im, c_prompt, c_sample,
              rel_bias_table, w_ada, b_ada, g_pre_mix, g_post_mix, g_pre_ffn, g_post_ffn, w_in,
              ssm_lam_re, ssm_lam_im, ssm_log_dt, ssm_b_re, ssm_b_im, ssm_c_re, ssm_c_im, ssm_d,
              w_glu, b_glu, attn_sinks, g_ssm_out, g_attn_out, w_out,
              w_group_router, b_group_router, w_expert_router, b_expert_router,
              w_exp_gate, w_exp_up, w_exp_down):
    yp, ys = x_prompt, x_sample
    kp, vp, hrp, hip, ksl, vsl, hrs, his = [], [], [], [], [], [], [], []
    for li in range(DEPTH):
        p = {
            'w_ada': w_ada[li], 'b_ada': b_ada[li],
            'g_pre_mix': g_pre_mix[li], 'g_post_mix': g_post_mix[li],
            'g_pre_ffn': g_pre_ffn[li], 'g_post_ffn': g_post_ffn[li],
            'w_in': w_in[li],
            'ssm_lam_re': ssm_lam_re[li], 'ssm_lam_im': ssm_lam_im[li], 'ssm_log_dt': ssm_log_dt[li],
            'ssm_b_re': ssm_b_re[li], 'ssm_b_im': ssm_b_im[li],
            'ssm_c_re': ssm_c_re[li], 'ssm_c_im': ssm_c_im[li], 'ssm_d': ssm_d[li],
            'w_glu': w_glu[li], 'b_glu': b_glu[li], 'attn_sinks': attn_sinks[li],
            'g_ssm_out': g_ssm_out[li], 'g_attn_out': g_attn_out[li], 'w_out': w_out[li],
            'w_group_router': w_group_router[li], 'b_group_router': b_group_router[li],
            'w_expert_router': w_expert_router[li], 'b_expert_router': b_expert_router[li],
            'w_exp_gate': w_exp_gate[li], 'w_exp_up': w_exp_up[li], 'w_exp_down': w_exp_down[li],
        }
        yp, k1, v1, r1, i1 = layer(yp, c_prompt, p, rel_bias_table)
        ys, k2, v2, r2, i2 = layer(ys, c_sample, p, rel_bias_table,
                                   (cache_k[li], cache_v[li]), (state_ssm_re[li], state_ssm_im[li]))
        kp.append(k1); vp.append(v1); hrp.append(r1); hip.append(i1)
        ksl.append(k2); vsl.append(v2); hrs.append(r2); his.append(i2)
    return (yp, ys, jnp.stack(kp), jnp.stack(vp), jnp.stack(hrp), jnp.stack(hip),
            jnp.stack(ksl), jnp.stack(vsl), jnp.stack(hrs), jnp.stack(his))
```

```python
import functools
import math

import jax
import jax.numpy as jnp
from jax import lax
from jax.experimental import pallas as pl
from jax.experimental.pallas import tpu as pltpu

F32 = jnp.float32
BF16 = jnp.bfloat16
HIGHEST = lax.Precision.HIGHEST

EPS = 1e-6
NEG_INF = -1e30

SSM_GROUP = 16
SSM_STATE = 64
HEAD_DIM = 64
N_KV_HEADS = 2
GQA = 4
N_HEADS = N_KV_HEADS * GQA
WINDOW = 128
NUM_BUCKETS = 32
MAX_DISTANCE = 128
N_EXPERT_GROUPS = 4
EXPERTS_PER_GROUP = 4
N_EXPERTS = N_EXPERT_GROUPS * EXPERTS_PER_GROUP
N_MOD = 6

LANES = 128
VMEM_LIMIT_BYTES = 56 * 1024 * 1024

TOKEN_TILE = 512
ATTN_TILE = 512
SSM_TILE = 256
SSM_LANE_CHUNK = 512


def _cparams(*sem):
    return pltpu.CompilerParams(dimension_semantics=sem, vmem_limit_bytes=VMEM_LIMIT_BYTES)


def _resident(shape):
    zeros = (0,) * len(shape)
    return pl.BlockSpec(shape, lambda *_: zeros, pipeline_mode=pl.Buffered(1))


def _rms(x, g):
    return x * lax.rsqrt(jnp.mean(x * x, axis=-1, keepdims=True) + EPS) * g


def _mod_body(c_ref, w_ref, b_ref, o_ref):
    a = jax.nn.silu(c_ref[...])
    o_ref[...] = jnp.dot(a, w_ref[...], precision=HIGHEST, preferred_element_type=F32) + b_ref[...]


def _modulation(c, w_ada, b_ada):
    rows, d = c.shape
    n = w_ada.shape[1]
    tn = 1024
    return pl.pallas_call(
        _mod_body,
        out_shape=jax.ShapeDtypeStruct((rows, n), F32),
        grid=(n // tn,),
        in_specs=[pl.BlockSpec((rows, d), lambda j: (0, 0)),
                  pl.BlockSpec((d, tn), lambda j: (0, j)),
                  pl.BlockSpec((1, tn), lambda j: (0, j))],
        out_specs=pl.BlockSpec((rows, tn), lambda j: (0, j)),
        compiler_params=_cparams("arbitrary"),
        name="modulation",
    )(c, w_ada, b_ada.reshape(1, n))


def _inproj_body(x_ref, mod_ref, g_ref, w_ref, u_ref, q_ref, k_ref, v_ref):
    x = x_ref[...]
    g, r, d = x.shape
    hn = _rms(x, g_ref[...]) * (1.0 + mod_ref[:, 1]) + mod_ref[:, 0]
    hn = hn.reshape(g * r, d).astype(BF16)
    proj = jnp.dot(hn, w_ref[...], preferred_element_type=F32)
    ssm_w = u_ref.shape[1]
    attn_w = q_ref.shape[1]
    kv_w = k_ref.shape[1]
    u_ref[...] = proj[:, :ssm_w]
    q_ref[...] = (proj[:, ssm_w:ssm_w + attn_w] * (HEAD_DIM ** -0.5)).astype(BF16)
    k_ref[...] = proj[:, ssm_w + attn_w:ssm_w + attn_w + kv_w]
    v_ref[...] = proj[:, ssm_w + attn_w + kv_w:]


def _inproj(x3, mod4, g_pre, w_in_bf, gb, rb):
    nb, nr, d = x3.shape
    assert gb == 1 or rb == nr
    n_in = w_in_bf.shape[1]
    kv_w = N_KV_HEADS * HEAD_DIM
    attn_w = N_HEADS * HEAD_DIM
    ssm_w = n_in - attn_w - 2 * kv_w
    rows = nb * nr
    nj = nr // rb
    tm = gb * rb
    out_map = lambda i, j: (i * nj + j, 0)
    return pl.pallas_call(
        _inproj_body,
        out_shape=(jax.ShapeDtypeStruct((rows, ssm_w), F32),
                   jax.ShapeDtypeStruct((rows, attn_w), BF16),
                   jax.ShapeDtypeStruct((rows, kv_w), F32),
                   jax.ShapeDtypeStruct((rows, kv_w), F32)),
        grid=(nb // gb, nj),
        in_specs=[pl.BlockSpec((gb, rb, d), lambda i, j: (i, j, 0)),
                  pl.BlockSpec((gb, 2, 1, d), lambda i, j: (i, 0, 0, 0)),
                  _resident((1, 1, d)),
                  _resident((d, n_in))],
        out_specs=(pl.BlockSpec((tm, ssm_w), out_map),
                   pl.BlockSpec((tm, attn_w), out_map),
                   pl.BlockSpec((tm, kv_w), out_map),
                   pl.BlockSpec((tm, kv_w), out_map)),
        compiler_params=_cparams("arbitrary", "arbitrary"),
        name="in_projection",
    )(x3, mod4, g_pre.reshape(1, 1, d), w_in_bf)


def _sink_softmax(s, sk):
    m = jnp.maximum(jnp.max(s, axis=-1, keepdims=True), sk)
    p = jnp.exp(s - m)
    return p / (jnp.sum(p, axis=-1, keepdims=True) + jnp.exp(sk - m))


def _attn_prompt_body(sinks_ref, q_ref, kc_ref, vc_ref, kp_ref, vp_ref, bias_ref, band_ref, g_ref, o_ref):
    tq = q_ref.shape[0]
    blk = band_ref.shape[0]
    first_tile = pl.program_id(1) == 0
    kk = jnp.concatenate([kp_ref[...], kc_ref[...]], axis=0).astype(BF16)
    vv = jnp.concatenate([vp_ref[...], vc_ref[...]], axis=0).astype(BF16)
    band = band_ref[...] > 0.5
    col = lax.broadcasted_iota(jnp.int32, band.shape, 1)
    band_first = band & ((col >= blk) | jnp.logical_not(first_tile))
    for j in range(tq // blk):
        kw = kk[j * blk:(j + 2) * blk]
        vw = vv[j * blk:(j + 2) * blk]
        valid = band_first if j == 0 else band
        outs = []
        for h in range(N_HEADS):
            hk = h // GQA
            qh = q_ref[j * blk:(j + 1) * blk, h * HEAD_DIM:(h + 1) * HEAD_DIM]
            s = lax.dot_general(qh, kw[:, hk * HEAD_DIM:(hk + 1) * HEAD_DIM],
                                (((1,), (1,)), ((), ())), preferred_element_type=F32)
            s = jnp.where(valid, s + bias_ref[h], NEG_INF)
            p = _sink_softmax(s, sinks_ref[h]).astype(BF16)
            outs.append(jnp.dot(p, vw[:, hk * HEAD_DIM:(hk + 1) * HEAD_DIM], preferred_element_type=F32))
        att = jnp.concatenate(outs, axis=1)
        o_ref[j * blk:(j + 1) * blk, :] = _rms(att, g_ref[...]).astype(BF16)


def _attn_prompt(q, k, v, sinks, bias, band, g_attn, nb, seq):
    rows, attn_w = q.shape
    kv_w = k.shape[1]
    blk = band.shape[0]
    tq = ATTN_TILE
    nt = seq // tq
    per = tq // blk
    cur = lambda b, i: (b * nt + i, 0)
    prev = lambda b, i: (b * nt * per + jnp.maximum(i * per - 1, 0), 0)
    return pl.pallas_call(
        _attn_prompt_body,
        out_shape=jax.ShapeDtypeStruct((rows, attn_w), BF16),
        grid=(nb, nt),
        in_specs=[pl.BlockSpec(memory_space=pltpu.SMEM),
                  pl.BlockSpec((tq, attn_w), cur),
                  pl.BlockSpec((tq, kv_w), cur),
                  pl.BlockSpec((tq, kv_w), cur),
                  pl.BlockSpec((blk, kv_w), prev),
                  pl.BlockSpec((blk, kv_w), prev),
                  _resident(bias.shape),
                  _resident(band.shape),
                  _resident((1, attn_w))],
        out_specs=pl.BlockSpec((tq, attn_w), cur),
        compiler_params=_cparams("arbitrary", "arbitrary"),
        name="attention_prompt",
    )(sinks, q, k, v, k, v, bias, band, g_attn.reshape(1, attn_w))


def _attn_sample_body(sinks_ref, q_ref, kn_ref, vn_ref, ck_ref, cv_ref, bias_ref, band_ref, g_ref,
                      o_ref, ko_ref, vo_ref):
    t = q_ref.shape[1]
    kk = jnp.concatenate([ck_ref[...], kn_ref[...]], axis=1)
    vv = jnp.concatenate([cv_ref[...], vn_ref[...]], axis=1)
    ko_ref[...] = kk[:, t:, :]
    vo_ref[...] = vv[:, t:, :]
    kkb = kk.astype(BF16)
    vvb = vv.astype(BF16)
    valid = band_ref[...] > 0.5
    outs = []
    for h in range(N_HEADS):
        hk = h // GQA
        qh = q_ref[:, :, h * HEAD_DIM:(h + 1) * HEAD_DIM]
        s = jnp.einsum('bqd,bkd->bqk', qh, kkb[:, :, hk * HEAD_DIM:(hk + 1) * HEAD_DIM],
                       preferred_element_type=F32)
        s = jnp.where(valid, s + bias_ref[h], NEG_INF)
        p = _sink_softmax(s, sinks_ref[h]).astype(BF16)
        outs.append(jnp.einsum('bqk,bkd->bqd', p, vvb[:, :, hk * HEAD_DIM:(hk + 1) * HEAD_DIM],
                               preferred_element_type=F32))
    att = jnp.concatenate(outs, axis=2)
    o_ref[...] = _rms(att, g_ref[...]).astype(BF16)


def _attn_sample(q3, k3, v3, ck, cv, sinks, bias, band, g_attn, gb):
    nb, t, attn_w = q3.shape
    w, kv_w = ck.shape[1], ck.shape[2]
    blk3 = lambda last: pl.BlockSpec((gb, last[0], last[1]), lambda i: (i, 0, 0))
    return pl.pallas_call(
        _attn_sample_body,
        out_shape=(jax.ShapeDtypeStruct((nb, t, attn_w), BF16),
                   jax.ShapeDtypeStruct((nb, w, kv_w), F32),
                   jax.ShapeDtypeStruct((nb, w, kv_w), F32)),
        grid=(nb // gb,),
        in_specs=[pl.BlockSpec(memory_space=pltpu.SMEM),
                  blk3((t, attn_w)), blk3((t, kv_w)), blk3((t, kv_w)),
                  blk3((w, kv_w)), blk3((w, kv_w)),
                  _resident(bias.shape), _resident(band.shape), _resident((1, 1, attn_w))],
        out_specs=(blk3((t, attn_w)), blk3((w, kv_w)), blk3((w, kv_w))),
        compiler_params=_cparams("arbitrary"),
        name="attention_sample",
    )(sinks, q3, k3, v3, ck, cv, bias, band, g_attn.reshape(1, 1, attn_w))


def _ssm_tail(y, u, d_ref, wglu_ref, bglu_ref, g_ref):
    z = jax.nn.gelu(y + d_ref[...] * u)
    gate = jax.nn.sigmoid(jnp.dot(z.astype(BF16), wglu_ref[...], preferred_element_type=F32) + bglu_ref[...])
    return _rms(z * gate, g_ref[...]).astype(BF16)


def _ssm_prompt_body(u_ref, bre_ref, bim_ref, cre_ref, cim_ref, lre_ref, lim_ref, d_ref, wglu_ref, bglu_ref,
                     g_ref, o_ref, hre_ref, him_ref, xre_scr, xim_scr, hre_scr, him_scr):
    ts = u_ref.shape[0]
    n_state = xre_scr.shape[1]
    half_in = bre_ref.shape[1]
    half_st = bre_ref.shape[2]

    @pl.when(pl.program_id(1) == 0)
    def _():
        hre_scr[...] = jnp.zeros_like(hre_scr)
        him_scr[...] = jnp.zeros_like(him_scr)

    u = u_ref[...]
    ub = u.astype(BF16)
    for hf in range(2):
        uh = ub[:, hf * half_in:(hf + 1) * half_in]
        xre_scr[:, hf * half_st:(hf + 1) * half_st] = jnp.dot(uh, bre_ref[hf], preferred_element_type=F32)
        xim_scr[:, hf * half_st:(hf + 1) * half_st] = jnp.dot(uh, bim_ref[hf], preferred_element_type=F32)

    for c in range(n_state // SSM_LANE_CHUNK):
        lanes = pl.ds(c * SSM_LANE_CHUNK, SSM_LANE_CHUNK)
        ar = lre_ref[:, lanes]
        ai = lim_ref[:, lanes]

        def step(t, carry):
            hr, hi = carry
            row = pl.ds(t, 1)
            nr = ar * hr - ai * hi + xre_scr[row, lanes]
            ni = ar * hi + ai * hr + xim_scr[row, lanes]
            xre_scr[row, lanes] = nr
            xim_scr[row, lanes] = ni
            return nr, ni

        hr, hi = lax.fori_loop(0, ts, step, (hre_scr[:, lanes], him_scr[:, lanes]), unroll=8)
        hre_scr[:, lanes] = hr
        him_scr[:, lanes] = hi

    hre_ref[...] = hre_scr[...]
    him_ref[...] = him_scr[...]

    ys = []
    for hf in range(2):
        hr = xre_scr[:, hf * half_st:(hf + 1) * half_st].astype(BF16)
        hi = xim_scr[:, hf * half_st:(hf + 1) * half_st].astype(BF16)
        ys.append(jnp.dot(hr, cre_ref[hf], preferred_element_type=F32)
                  + jnp.dot(hi, cim_ref[hf], preferred_element_type=F32))
    y = jnp.concatenate(ys, axis=1)
    o_ref[...] = _ssm_tail(y, u, d_ref, wglu_ref, bglu_ref, g_ref)


def _ssm_prompt(u, sw, nb, seq):
    rows, ssm_w = u.shape
    n_state = sw["lre"].shape[1]
    ts = SSM_TILE
    nt = seq // ts
    tile = lambda b, i: (b * nt + i, 0)
    state_spec = pl.BlockSpec((None, 1, n_state), lambda b, i: (b, 0, 0))
    return pl.pallas_call(
        _ssm_prompt_body,
        out_shape=(jax.ShapeDtypeStruct((rows, ssm_w), BF16),
                   jax.ShapeDtypeStruct((nb, 1, n_state), F32),
                   jax.ShapeDtypeStruct((nb, 1, n_state), F32)),
        grid=(nb, nt),
        in_specs=[pl.BlockSpec((ts, ssm_w), tile),
                  _resident(sw["bre"].shape), _resident(sw["bim"].shape),
                  _resident(sw["cre"].shape), _resident(sw["cim"].shape),
                  _resident(sw["lre"].shape), _resident(sw["lim"].shape),
                  _resident((1, ssm_w)), _resident((ssm_w, ssm_w)), _resident((1, ssm_w)), _resident((1, ssm_w))],
        out_specs=(pl.BlockSpec((ts, ssm_w), tile), state_spec, state_spec),
        scratch_shapes=[pltpu.VMEM((ts, n_state), F32), pltpu.VMEM((ts, n_state), F32),
                        pltpu.VMEM((1, n_state), F32), pltpu.VMEM((1, n_state), F32)],
        compiler_params=_cparams("arbitrary", "arbitrary"),
        name="ssm_prompt",
    )(u, sw["bre"].astype(BF16), sw["bim"].astype(BF16), sw["cre"].astype(BF16), sw["cim"].astype(BF16),
      sw["lre"], sw["lim"], sw["d"], sw["wglu"], sw["bglu"], sw["g"])


def _ssm_sample_body(u_ref, h0re_ref, h0im_ref, bre_ref, bim_ref, cre_ref, cim_ref, lre_ref, lim_ref,
                     d_ref, wglu_ref, bglu_ref, g_ref, o_ref, hre_ref, him_ref):
    steps = u_ref.shape[0]
    half_in = bre_ref.shape[1]
    dot32 = functools.partial(jnp.dot, precision=HIGHEST, preferred_element_type=F32)
    ar = lre_ref[...]
    ai = lim_ref[...]
    hr = h0re_ref[...]
    hi = h0im_ref[...]
    for t in range(steps):
        u = u_ref[t]
        xr = jnp.concatenate([dot32(u[:, hf * half_in:(hf + 1) * half_in], bre_ref[hf]) for hf in range(2)], axis=1)
        xi = jnp.concatenate([dot32(u[:, hf * half_in:(hf + 1) * half_in], bim_ref[hf]) for hf in range(2)], axis=1)
        hr, hi = ar * hr - ai * hi + xr, ar * hi + ai * hr + xi
        half_st = hr.shape[1] // 2
        y = jnp.concatenate(
            [dot32(hr[:, hf * half_st:(hf + 1) * half_st], cre_ref[hf])
             + dot32(hi[:, hf * half_st:(hf + 1) * half_st], cim_ref[hf]) for hf in range(2)], axis=1)
        o_ref[t] = _ssm_tail(y, u, d_ref, wglu_ref, bglu_ref, g_ref)
    hre_ref[...] = hr
    him_ref[...] = hi


def _ssm_sample(u_tm, h0re, h0im, sw):
    steps, nb, ssm_w = u_tm.shape
    n_state = h0re.shape[1]
    return pl.pallas_call(
        _ssm_sample_body,
        out_shape=(jax.ShapeDtypeStruct((steps, nb, ssm_w), BF16),
                   jax.ShapeDtypeStruct((nb, n_state), F32),
                   jax.ShapeDtypeStruct((nb, n_state), F32)),
        compiler_params=pltpu.CompilerParams(vmem_limit_bytes=VMEM_LIMIT_BYTES),
        name="ssm_sample",
    )(u_tm, h0re, h0im, sw["bre"], sw["bim"], sw["cre"], sw["cim"], sw["lre"], sw["lim"],
      sw["d"], sw["wglu"], sw["bglu"], sw["g"])


def _ssm_weights(lam_re, lam_im, log_dt, b_re, b_im, c_re, c_im, d_skip, w_glu, b_glu, g_ssm):
    ng, ns = lam_re.shape
    nc = b_re.shape[2]
    dt = jnp.exp(log_dt)[:, None]
    mag = jnp.exp(lam_re * dt)
    lre = mag * jnp.cos(lam_im * dt)
    lim = mag * jnp.sin(lam_im * dt)
    den = lam_re * lam_re + lam_im * lam_im
    fre = ((lre - 1.0) * lam_re + lim * lam_im) / den
    fim = (lim * lam_re - (lre - 1.0) * lam_im) / den
    bbar_re = fre[..., None] * b_re - fim[..., None] * b_im
    bbar_im = fre[..., None] * b_im + fim[..., None] * b_re
    eye = jnp.eye(ng // 2, dtype=F32)

    def in_blocks(b):
        b2 = b.reshape(2, ng // 2, ns, nc)
        return jnp.einsum('hgpc,gk->hgckp', b2, eye).reshape(2, ng // 2 * nc, ng // 2 * ns)

    def out_blocks(c):
        c2 = c.reshape(2, ng // 2, nc, ns)
        return jnp.einsum('hgcp,gk->hgpkc', c2, eye).reshape(2, ng // 2 * ns, ng // 2 * nc)

    ssm_w = ng * nc
    return {
        "bre": in_blocks(bbar_re), "bim": in_blocks(bbar_im),
        "cre": out_blocks(c_re), "cim": out_blocks(-c_im),
        "lre": lre.reshape(1, ng * ns), "lim": lim.reshape(1, ng * ns),
        "d": d_skip.reshape(1, ssm_w), "wglu": w_glu.astype(BF16), "bglu": b_glu.reshape(1, ssm_w),
        "g": g_ssm.reshape(1, ssm_w),
    }


def _route(logits):
    lane = lax.broadcasted_iota(jnp.int32, logits.shape, 1)
    big = jnp.int32(10 ** 6)
    is_group = lane < N_EXPERT_GROUPS
    gl = jnp.where(is_group, logits, NEG_INF)
    gmax = jnp.max(gl, axis=-1, keepdims=True)
    gidx = jnp.min(jnp.where(is_group & (gl == gmax), lane, big), axis=-1, keepdims=True)
    g_p = 1.0 / jnp.sum(jnp.exp(gl - gmax), axis=-1, keepdims=True)
    lo = N_EXPERT_GROUPS + gidx * EXPERTS_PER_GROUP
    in_group = (lane >= lo) & (lane < lo + EXPERTS_PER_GROUP)
    el = jnp.where(in_group, logits, NEG_INF)
    ee = jnp.exp(el - jnp.max(el, axis=-1, keepdims=True))
    prob = ee / jnp.sum(ee, axis=-1, keepdims=True)
    p1 = jnp.max(jnp.where(in_group, prob, -1.0), axis=-1, keepdims=True)
    i1 = jnp.min(jnp.where(in_group & (prob == p1), lane, big), axis=-1, keepdims=True)
    rest = in_group & (lane != i1)
    p2 = jnp.max(jnp.where(rest, prob, -1.0), axis=-1, keepdims=True)
    i2 = jnp.min(jnp.where(rest & (prob == p2), lane, big), axis=-1, keepdims=True)
    tot = p1 + p2
    return jnp.where(lane == i1, g_p * p1 / tot, jnp.where(lane == i2, g_p * p2 / tot, 0.0))


def _ffn_body(x_ref, ssm_ref, attn_ref, mod_ref, gpost_ref, gpre_ref, gffn_ref, wout_ref, wr_ref, br_ref,
              wgu_ref, wd_ref, o_ref):
    x = x_ref[...]
    g, r, d = x.shape
    rows = g * r
    gt1, sh2, sc2, gt2 = mod_ref[:, 2], mod_ref[:, 3], mod_ref[:, 4], mod_ref[:, 5]
    cat = jnp.concatenate([ssm_ref[...], attn_ref[...]], axis=1)
    mixed = jnp.dot(cat, wout_ref[...], preferred_element_type=F32)
    x1 = x + gt1 * _rms(mixed, gpost_ref[0]).reshape(g, r, d)
    hn = (_rms(x1, gpre_ref[...]) * (1.0 + sc2) + sh2).reshape(rows, d)
    logits = jnp.dot(hn, wr_ref[...], precision=HIGHEST, preferred_element_type=F32) + br_ref[...]
    gates = _route(logits)
    hb = hn.astype(BF16)
    lane = lax.broadcasted_iota(jnp.int32, gates.shape, 1)
    d_exp = wd_ref.shape[1]
    acc = jnp.zeros((rows, d), F32)
    for e in range(N_EXPERTS):
        gate = jnp.sum(jnp.where(lane == N_EXPERT_GROUPS + e, gates, 0.0), axis=-1, keepdims=True)
        hgu = jnp.dot(hb, wgu_ref[e], preferred_element_type=F32)
        act = jax.nn.silu(hgu[:, :d_exp]) * hgu[:, d_exp:] * gate
        acc = acc + jnp.dot(act.astype(BF16), wd_ref[e], preferred_element_type=F32)
    o_ref[...] = x1 + gt2 * _rms(acc, gffn_ref[0]).reshape(g, r, d)


def _ffn(x3, ssm_n, attn_n, mod4, fw, gb, rb):
    nb, nr, d = x3.shape
    assert gb == 1 or rb == nr
    half = ssm_n.shape[1]
    nj = nr // rb
    tm = gb * rb
    row_map = lambda i, j: (i * nj + j, 0)
    vec = _resident((1, 1, d))
    return pl.pallas_call(
        _ffn_body,
        out_shape=jax.ShapeDtypeStruct((nb, nr, d), F32),
        grid=(nb // gb, nj),
        in_specs=[pl.BlockSpec((gb, rb, d), lambda i, j: (i, j, 0)),
                  pl.BlockSpec((tm, half), row_map),
                  pl.BlockSpec((tm, attn_n.shape[1]), row_map),
                  pl.BlockSpec((gb, N_MOD, 1, d), lambda i, j: (i, 0, 0, 0)),
                  vec, vec, vec,
                  _resident(fw["wout"].shape), _resident(fw["wr"].shape), _resident(fw["br"].shape),
                  _resident(fw["wgu"].shape), _resident(fw["wd"].shape)],
        out_specs=pl.BlockSpec((gb, rb, d), lambda i, j: (i, j, 0)),
        compiler_params=_cparams("arbitrary", "arbitrary"),
        name="outproj_moe",
    )(x3, ssm_n, attn_n, mod4, fw["gpost"], fw["gpre"], fw["gffn"], fw["wout"], fw["wr"], fw["br"],
      fw["wgu"], fw["wd"])


def _ffn_weights(g_post_mix, g_pre_ffn, g_post_ffn, w_out, w_gr, b_gr, w_er, b_er, w_gate, w_up, w_down):
    d = w_out.shape[0]
    n_route = N_EXPERT_GROUPS + N_EXPERTS
    wr = jnp.concatenate([w_gr, w_er.reshape(d, N_EXPERTS)], axis=1)
    br = jnp.concatenate([b_gr, b_er.reshape(N_EXPERTS)])
    return {
        "gpost": g_post_mix.reshape(1, 1, d), "gpre": g_pre_ffn.reshape(1, 1, d), "gffn": g_post_ffn.reshape(1, 1, d),
        "wout": w_out.astype(BF16),
        "wr": jnp.pad(wr, ((0, 0), (0, LANES - n_route))),
        "br": jnp.pad(br, (0, LANES - n_route)).reshape(1, LANES),
        "wgu": jnp.concatenate([w_gate, w_up], axis=2).astype(BF16),
        "wd": w_down.astype(BF16),
    }


def _rel_bucket(dist):
    max_exact = NUM_BUCKETS // 2
    dd = jnp.maximum(dist, 0)
    log_ratio = jnp.log(jnp.maximum(dd, 1).astype(F32) / max_exact) / math.log(MAX_DISTANCE / max_exact)
    large = jnp.minimum(max_exact + (log_ratio * (NUM_BUCKETS - max_exact)).astype(jnp.int32), NUM_BUCKETS - 1)
    return jnp.where(dd < max_exact, dd, large)


def _bias_and_band(dist, table):
    bias = jnp.moveaxis(table[_rel_bucket(dist)].astype(F32), -1, 0)
    band = ((dist >= 0) & (dist <= WINDOW)).astype(F32)
    return bias, band


def _layer(xp, xs, ck, cv, h0re, h0im, mod_p, mod_s, table, p):
    nb, seq, d = xp.shape
    nbs, t, _ = xs.shape
    w = ck.shape[1]
    kv_w = N_KV_HEADS * HEAD_DIM
    w_in_bf = p["w_in"].astype(BF16)
    sw = _ssm_weights(p["ssm_lam_re"], p["ssm_lam_im"], p["ssm_log_dt"], p["ssm_b_re"], p["ssm_b_im"],
                      p["ssm_c_re"], p["ssm_c_im"], p["ssm_d"], p["w_glu"], p["b_glu"], p["g_ssm_out"])
    fw = _ffn_weights(p["g_post_mix"], p["g_pre_ffn"], p["g_post_ffn"], p["w_out"],
                      p["w_group_router"], p["b_group_router"], p["w_expert_router"], p["b_expert_router"],
                      p["w_exp_gate"], p["w_exp_up"], p["w_exp_down"])
    n_state = sw["lre"].shape[1]
    ng = p["ssm_lam_re"].shape[0]

    u, q, k, v = _inproj(xp, mod_p, p["g_pre_mix"], w_in_bf, 1, TOKEN_TILE)
    blk = WINDOW
    dist_p = jnp.arange(blk)[:, None] + blk - jnp.arange(2 * blk)[None, :]
    bias_p, band_p = _bias_and_band(dist_p, table)
    attn_p = _attn_prompt(q, k, v, p["attn_sinks"], bias_p, band_p, p["g_attn_out"], nb, seq)
    ssm_p, hre_p, him_p = _ssm_prompt(u, sw, nb, seq)
    yp = _ffn(xp, ssm_p, attn_p, mod_p, fw, 1, TOKEN_TILE)
    wp = min(WINDOW, seq)
    k_p = k.reshape(nb, seq, N_KV_HEADS, HEAD_DIM)[:, seq - wp:]
    v_p = v.reshape(nb, seq, N_KV_HEADS, HEAD_DIM)[:, seq - wp:]
    hre_p = hre_p.reshape(nb, ng, n_state // ng)
    him_p = him_p.reshape(nb, ng, n_state // ng)

    gs = TOKEN_TILE // t
    us, qs, ks, vs = _inproj(xs, mod_s, p["g_pre_mix"], w_in_bf, gs, t)
    dist_s = jnp.arange(t)[:, None] + w - jnp.arange(w + t)[None, :]
    bias_s, band_s = _bias_and_band(dist_s, table)
    attn_s, k_s, v_s = _attn_sample(qs.reshape(nbs, t, -1), ks.reshape(nbs, t, kv_w), vs.reshape(nbs, t, kv_w),
                                    ck.reshape(nbs, w, kv_w), cv.reshape(nbs, w, kv_w),
                                    p["attn_sinks"], bias_s, band_s, p["g_attn_out"], 16)
    u_tm = jnp.swapaxes(us.reshape(nbs, t, -1), 0, 1)
    ssm_tm, hre_s, him_s = _ssm_sample(u_tm, h0re.reshape(nbs, n_state), h0im.reshape(nbs, n_state), sw)
    ssm_s = jnp.swapaxes(ssm_tm, 0, 1).reshape(nbs * t, -1)
    ys = _ffn(xs, ssm_s, attn_s.reshape(nbs * t, -1), mod_s, fw, gs, t)
    k_s = k_s.reshape(nbs, w, N_KV_HEADS, HEAD_DIM)
    v_s = v_s.reshape(nbs, w, N_KV_HEADS, HEAD_DIM)
    hre_s = hre_s.reshape(nbs, ng, n_state // ng)
    him_s = him_s.reshape(nbs, ng, n_state // ng)
    return yp, ys, k_p, v_p, hre_p, him_p, k_s, v_s, hre_s, him_s


def kernel(x_prompt, x_sample, cache_k, cache_v, state_ssm_re, state_ssm_im, c_prompt, c_sample, rel_bias_table,
           w_ada, b_ada, g_pre_mix, g_post_mix, g_pre_ffn, g_post_ffn, w_in, ssm_lam_re, ssm_lam_im, ssm_log_dt,
           ssm_b_re, ssm_b_im, ssm_c_re, ssm_c_im, ssm_d, w_glu, b_glu, attn_sinks, g_ssm_out, g_attn_out, w_out,
           w_group_router, b_group_router, w_expert_router, b_expert_router, w_exp_gate, w_exp_up, w_exp_down):
    params = dict(
        w_ada=w_ada, b_ada=b_ada, g_pre_mix=g_pre_mix, g_post_mix=g_post_mix, g_pre_ffn=g_pre_ffn,
        g_post_ffn=g_post_ffn, w_in=w_in, ssm_lam_re=ssm_lam_re, ssm_lam_im=ssm_lam_im, ssm_log_dt=ssm_log_dt,
        ssm_b_re=ssm_b_re, ssm_b_im=ssm_b_im, ssm_c_re=ssm_c_re, ssm_c_im=ssm_c_im, ssm_d=ssm_d, w_glu=w_glu,
        b_glu=b_glu, attn_sinks=attn_sinks, g_ssm_out=g_ssm_out, g_attn_out=g_attn_out, w_out=w_out,
        w_group_router=w_group_router, b_group_router=b_group_router, w_expert_router=w_expert_router,
        b_expert_router=b_expert_router, w_exp_gate=w_exp_gate, w_exp_up=w_exp_up, w_exp_down=w_exp_down)
    depth = w_in.shape[0]
    nb, nbs = x_prompt.shape[0], x_sample.shape[0]
    d = x_prompt.shape[2]
    pad = (-nb) % 8
    yp, ys = x_prompt, x_sample
    outs = [[] for _ in range(8)]
    for li in range(depth):
        p = {name: val[li] for name, val in params.items()}
        c_all = jnp.concatenate([c_prompt, jnp.zeros((pad, d), F32), c_sample], axis=0)
        mod = _modulation(c_all, p["w_ada"], p["b_ada"])
        mod_p = mod[:nb].reshape(nb, N_MOD, 1, d)
        mod_s = mod[nb + pad:].reshape(nbs, N_MOD, 1, d)
        res = _layer(yp, ys, cache_k[li], cache_v[li], state_ssm_re[li], state_ssm_im[li], mod_p, mod_s,
                     rel_bias_table, p)
        yp, ys = res[0], res[1]
        for acc, val in zip(outs, res[2:]):
            acc.append(val)
    return (yp, ys) + tuple(jnp.stack(o) for o in outs)
```

```python
import functools
import math

import jax
import jax.numpy as jnp
from jax import lax
from jax.experimental import pallas as pl
from jax.experimental.pallas import tpu as pltpu

F32 = jnp.float32
BF16 = jnp.bfloat16
HIGHEST = lax.Precision.HIGHEST

EPS = 1e-6
NEG_INF = -1e30

SSM_GROUP = 16
SSM_STATE = 64
HEAD_DIM = 64
N_KV_HEADS = 2
GQA = 4
N_HEADS = N_KV_HEADS * GQA
WINDOW = 128
NUM_BUCKETS = 32
MAX_DISTANCE = 128
N_EXPERT_GROUPS = 4
EXPERTS_PER_GROUP = 4
N_EXPERTS = N_EXPERT_GROUPS * EXPERTS_PER_GROUP
N_MOD = 6

LANES = 128
VMEM_LIMIT_BYTES = 56 * 1024 * 1024

TOKEN_TILE = 512
ATTN_TILE = 512
SSM_CHUNK = 32
SSM_GROUP_BLOCK = 8
SSM_TAIL_BLOCK = 4


def _cparams(*sem):
    return pltpu.CompilerParams(dimension_semantics=sem, vmem_limit_bytes=VMEM_LIMIT_BYTES)


def _resident(shape):
    zeros = (0,) * len(shape)
    return pl.BlockSpec(shape, lambda *_: zeros, pipeline_mode=pl.Buffered(1))


def _rms(x, g):
    return x * lax.rsqrt(jnp.mean(x * x, axis=-1, keepdims=True) + EPS) * g


def _mod_body(c_ref, w_ref, b_ref, o_ref):
    a = jax.nn.silu(c_ref[...])
    o_ref[...] = jnp.dot(a, w_ref[...], precision=HIGHEST, preferred_element_type=F32) + b_ref[...]


def _modulation(c, w_ada, b_ada):
    rows, d = c.shape
    n = w_ada.shape[1]
    tn = 1024
    return pl.pallas_call(
        _mod_body,
        out_shape=jax.ShapeDtypeStruct((rows, n), F32),
        grid=(n // tn,),
        in_specs=[pl.BlockSpec((rows, d), lambda j: (0, 0)),
                  pl.BlockSpec((d, tn), lambda j: (0, j)),
                  pl.BlockSpec((1, tn), lambda j: (0, j))],
        out_specs=pl.BlockSpec((rows, tn), lambda j: (0, j)),
        compiler_params=_cparams("arbitrary"),
        name="modulation",
    )(c, w_ada, b_ada.reshape(1, n))


def _inproj_body(x_ref, mod_ref, g_ref, w_ref, u_ref, q_ref, k_ref, v_ref, *, chunked):
    x = x_ref[...]
    g, r, d = x.shape
    hn = _rms(x, g_ref[...]) * (1.0 + mod_ref[:, 1]) + mod_ref[:, 0]
    hn = hn.reshape(g * r, d).astype(BF16)
    proj = jnp.dot(hn, w_ref[...], preferred_element_type=F32)
    ssm_w = u_ref.shape[-1]
    attn_w = q_ref.shape[1]
    kv_w = k_ref.shape[1]
    if chunked:
        t, nc, _ = u_ref.shape
        u_ref[...] = jnp.swapaxes(proj[:, :ssm_w].reshape(nc, t, ssm_w), 0, 1)
    else:
        u_ref[...] = proj[:, :ssm_w]
    q_ref[...] = (proj[:, ssm_w:ssm_w + attn_w] * (HEAD_DIM ** -0.5)).astype(BF16)
    k_ref[...] = proj[:, ssm_w + attn_w:ssm_w + attn_w + kv_w]
    v_ref[...] = proj[:, ssm_w + attn_w + kv_w:]


def _inproj(x3, mod4, g_pre, w_in_bf, gb, rb, chunk=None):
    nb, nr, d = x3.shape
    assert gb == 1 or rb == nr
    n_in = w_in_bf.shape[1]
    kv_w = N_KV_HEADS * HEAD_DIM
    attn_w = N_HEADS * HEAD_DIM
    ssm_w = n_in - attn_w - 2 * kv_w
    rows = nb * nr
    nj = nr // rb
    tm = gb * rb
    out_map = lambda i, j: (i * nj + j, 0)
    if chunk is None:
        u_shape = jax.ShapeDtypeStruct((rows, ssm_w), F32)
        u_spec = pl.BlockSpec((tm, ssm_w), out_map)
    else:
        assert gb == 1 and rb % chunk == 0
        u_shape = jax.ShapeDtypeStruct((nb, chunk, nr // chunk, ssm_w), F32)
        u_spec = pl.BlockSpec((None, chunk, rb // chunk, ssm_w), lambda i, j: (i, 0, j, 0))
    return pl.pallas_call(
        functools.partial(_inproj_body, chunked=chunk is not None),
        out_shape=(u_shape,
                   jax.ShapeDtypeStruct((rows, attn_w), BF16),
                   jax.ShapeDtypeStruct((rows, kv_w), F32),
                   jax.ShapeDtypeStruct((rows, kv_w), F32)),
        grid=(nb // gb, nj),
        in_specs=[pl.BlockSpec((gb, rb, d), lambda i, j: (i, j, 0)),
                  pl.BlockSpec((gb, 2, 1, d), lambda i, j: (i, 0, 0, 0)),
                  _resident((1, 1, d)),
                  _resident((d, n_in))],
        out_specs=(u_spec,
                   pl.BlockSpec((tm, attn_w), out_map),
                   pl.BlockSpec((tm, kv_w), out_map),
                   pl.BlockSpec((tm, kv_w), out_map)),
        compiler_params=_cparams("arbitrary", "arbitrary"),
        name="in_projection",
    )(x3, mod4, g_pre.reshape(1, 1, d), w_in_bf)


def _sink_softmax(s, sk):
    m = jnp.maximum(jnp.max(s, axis=-1, keepdims=True), sk)
    p = jnp.exp(s - m)
    return p / (jnp.sum(p, axis=-1, keepdims=True) + jnp.exp(sk - m))


def _attn_prompt_body(sinks_ref, q_ref, kc_ref, vc_ref, kp_ref, vp_ref, bias_ref, band_ref, g_ref, o_ref):
    tq = q_ref.shape[0]
    blk = band_ref.shape[0]
    first_tile = pl.program_id(1) == 0
    kk = jnp.concatenate([kp_ref[...], kc_ref[...]], axis=0).astype(BF16)
    vv = jnp.concatenate([vp_ref[...], vc_ref[...]], axis=0).astype(BF16)
    band = band_ref[...] > 0.5
    col = lax.broadcasted_iota(jnp.int32, band.shape, 1)
    band_first = band & ((col >= blk) | jnp.logical_not(first_tile))
    for j in range(tq // blk):
        kw = kk[j * blk:(j + 2) * blk]
        vw = vv[j * blk:(j + 2) * blk]
        valid = band_first if j == 0 else band
        outs = []
        for h in range(N_HEADS):
            hk = h // GQA
            qh = q_ref[j * blk:(j + 1) * blk, h * HEAD_DIM:(h + 1) * HEAD_DIM]
            s = lax.dot_general(qh, kw[:, hk * HEAD_DIM:(hk + 1) * HEAD_DIM],
                                (((1,), (1,)), ((), ())), preferred_element_type=F32)
            s = jnp.where(valid, s + bias_ref[h], NEG_INF)
            p = _sink_softmax(s, sinks_ref[h]).astype(BF16)
            outs.append(jnp.dot(p, vw[:, hk * HEAD_DIM:(hk + 1) * HEAD_DIM], preferred_element_type=F32))
        att = jnp.concatenate(outs, axis=1)
        o_ref[j * blk:(j + 1) * blk, :] = _rms(att, g_ref[...]).astype(BF16)


def _attn_prompt(q, k, v, sinks, bias, band, g_attn, nb, seq):
    rows, attn_w = q.shape
    kv_w = k.shape[1]
    blk = band.shape[0]
    tq = ATTN_TILE
    nt = seq // tq
    per = tq // blk
    cur = lambda b, i: (b * nt + i, 0)
    prev = lambda b, i: (b * nt * per + jnp.maximum(i * per - 1, 0), 0)
    return pl.pallas_call(
        _attn_prompt_body,
        out_shape=jax.ShapeDtypeStruct((rows, attn_w), BF16),
        grid=(nb, nt),
        in_specs=[pl.BlockSpec(memory_space=pltpu.SMEM),
                  pl.BlockSpec((tq, attn_w), cur),
                  pl.BlockSpec((tq, kv_w), cur),
                  pl.BlockSpec((tq, kv_w), cur),
                  pl.BlockSpec((blk, kv_w), prev),
                  pl.BlockSpec((blk, kv_w), prev),
                  _resident(bias.shape),
                  _resident(band.shape),
                  _resident((1, attn_w))],
        out_specs=pl.BlockSpec((tq, attn_w), cur),
        compiler_params=_cparams("arbitrary", "arbitrary"),
        name="attention_prompt",
    )(sinks, q, k, v, k, v, bias, band, g_attn.reshape(1, attn_w))


def _attn_sample_body(sinks_ref, q_ref, kn_ref, vn_ref, ck_ref, cv_ref, bias_ref, band_ref, g_ref,
                      o_ref, ko_ref, vo_ref):
    t = q_ref.shape[1]
    kk = jnp.concatenate([ck_ref[...], kn_ref[...]], axis=1)
    vv = jnp.concatenate([cv_ref[...], vn_ref[...]], axis=1)
    ko_ref[...] = kk[:, t:, :]
    vo_ref[...] = vv[:, t:, :]
    kkb = kk.astype(BF16)
    vvb = vv.astype(BF16)
    valid = band_ref[...] > 0.5
    outs = []
    for h in range(N_HEADS):
        hk = h // GQA
        qh = q_ref[:, :, h * HEAD_DIM:(h + 1) * HEAD_DIM]
        s = jnp.einsum('bqd,bkd->bqk', qh, kkb[:, :, hk * HEAD_DIM:(hk + 1) * HEAD_DIM],
                       preferred_element_type=F32)
        s = jnp.where(valid, s + bias_ref[h], NEG_INF)
        p = _sink_softmax(s, sinks_ref[h]).astype(BF16)
        outs.append(jnp.einsum('bqk,bkd->bqd', p, vvb[:, :, hk * HEAD_DIM:(hk + 1) * HEAD_DIM],
                               preferred_element_type=F32))
    att = jnp.concatenate(outs, axis=2)
    o_ref[...] = _rms(att, g_ref[...]).astype(BF16)


def _attn_sample(q3, k3, v3, ck, cv, sinks, bias, band, g_attn, gb):
    nb, t, attn_w = q3.shape
    w, kv_w = ck.shape[1], ck.shape[2]
    blk3 = lambda last: pl.BlockSpec((gb, last[0], last[1]), lambda i: (i, 0, 0))
    return pl.pallas_call(
        _attn_sample_body,
        out_shape=(jax.ShapeDtypeStruct((nb, t, attn_w), BF16),
                   jax.ShapeDtypeStruct((nb, w, kv_w), F32),
                   jax.ShapeDtypeStruct((nb, w, kv_w), F32)),
        grid=(nb // gb,),
        in_specs=[pl.BlockSpec(memory_space=pltpu.SMEM),
                  blk3((t, attn_w)), blk3((t, kv_w)), blk3((t, kv_w)),
                  blk3((w, kv_w)), blk3((w, kv_w)),
                  _resident(bias.shape), _resident(band.shape), _resident((1, 1, attn_w))],
        out_specs=(blk3((t, attn_w)), blk3((w, kv_w)), blk3((w, kv_w))),
        compiler_params=_cparams("arbitrary"),
        name="attention_sample",
    )(sinks, q3, k3, v3, ck, cv, bias, band, g_attn.reshape(1, 1, attn_w))


def _ssm_tail(y, u, d_ref, wglu_ref, bglu_ref, g_ref):
    z = jax.nn.gelu(y + d_ref[...] * u)
    gate = jax.nn.sigmoid(jnp.dot(z.astype(BF16), wglu_ref[...], preferred_element_type=F32) + bglu_ref[...])
    return _rms(z * gate, g_ref[...]).astype(BF16)


def _ssm_chunk_body(u_ref, m_ref, e_ref, f_ref, w1_ref, w2_ref, yt_ref, hend_ref, ut_scr, s_scr):
    t, nc, ssm_w = u_ref.shape
    gb = m_ref.shape[0]
    n_levels = w1_ref.shape[1]
    two_p = e_ref.shape[1]
    ch = ssm_w // ut_scr.shape[1]

    @pl.when(pl.program_id(1) == 0)
    def _():
        for s in range(t):
            ut_scr[s] = u_ref[s].T.astype(BF16).reshape(ut_scr.shape[1], ch, nc)

    g0 = pl.program_id(1) * gb
    row = lax.broadcasted_iota(jnp.int32, (nc, two_p), 0)
    for gl in range(gb):
        r = ut_scr[:, g0 + gl].reshape(t * ch, nc)
        y = jnp.dot(m_ref[gl], r, preferred_element_type=F32)
        s_scr[...] = jnp.dot(e_ref[gl], r, preferred_element_type=F32)
        h = s_scr[...].T
        for lv in range(n_levels):
            sh = 1 << lv
            prev = jnp.where(row >= sh, pltpu.roll(h, sh, axis=0), 0.0)
            h = h + w1_ref[gl, lv] * prev + w2_ref[gl, lv] * pltpu.roll(prev, two_p // 2, axis=1)
        hend_ref[gl] = h[nc - 1:nc, :]
        s_scr[...] = jnp.where(row >= 1, pltpu.roll(h, 1, axis=0), 0.0).T
        y = y + jnp.dot(f_ref[gl], s_scr[...].astype(BF16), preferred_element_type=F32)
        yt_ref[:, gl * ch:(gl + 1) * ch, :] = y.reshape(t, ch, nc)


def _ssm_chunked(u2, kw):
    nb, t, nc, ssm_w = u2.shape
    ng, tc, _ = kw["m"].shape
    two_p = kw["e"].shape[1]
    n_levels = kw["w1"].shape[1]
    ch = tc // t
    gb = SSM_GROUP_BLOCK
    grp = lambda shape: pl.BlockSpec((gb,) + shape, lambda b, g: (g,) + (0,) * len(shape))
    return pl.pallas_call(
        _ssm_chunk_body,
        out_shape=(jax.ShapeDtypeStruct((nb, t, ssm_w, nc), F32),
                   jax.ShapeDtypeStruct((nb, ng, 1, two_p), F32)),
        grid=(nb, ng // gb),
        in_specs=[pl.BlockSpec((None, t, nc, ssm_w), lambda b, g: (b, 0, 0, 0), pipeline_mode=pl.Buffered(1)),
                  grp((tc, tc)), grp((two_p, tc)), grp((tc, two_p)),
                  grp((n_levels, 1, two_p)), grp((n_levels, 1, two_p))],
        out_specs=(pl.BlockSpec((None, t, gb * ch, nc), lambda b, g: (b, 0, g, 0)),
                   pl.BlockSpec((None, gb, 1, two_p), lambda b, g: (b, g, 0, 0))),
        scratch_shapes=[pltpu.VMEM((t, ng, ch, nc), BF16), pltpu.VMEM((two_p, nc), F32)],
        compiler_params=_cparams("arbitrary", "arbitrary"),
        name="ssm_chunked",
    )(u2, kw["m"], kw["e"], kw["f"], kw["w1"], kw["w2"])


def _ssm_tail_body(yt_ref, u_ref, d_ref, wglu_ref, bglu_ref, g_ref, o_ref):
    for i in range(yt_ref.shape[0]):
        o_ref[i] = _ssm_tail(yt_ref[i].T, u_ref[i], d_ref, wglu_ref, bglu_ref, g_ref)


def _ssm_tail_call(yt, u2, sw):
    nb, t, nc, ssm_w = u2.shape
    tb = SSM_TAIL_BLOCK
    return pl.pallas_call(
        _ssm_tail_body,
        out_shape=jax.ShapeDtypeStruct((nb, t, nc, ssm_w), BF16),
        grid=(nb, t // tb),
        in_specs=[pl.BlockSpec((None, tb, ssm_w, nc), lambda b, i: (b, i, 0, 0)),
                  pl.BlockSpec((None, tb, nc, ssm_w), lambda b, i: (b, i, 0, 0)),
                  _resident((1, ssm_w)), _resident((ssm_w, ssm_w)), _resident((1, ssm_w)), _resident((1, ssm_w))],
        out_specs=pl.BlockSpec((None, tb, nc, ssm_w), lambda b, i: (b, i, 0, 0)),
        compiler_params=_cparams("arbitrary", "arbitrary"),
        name="ssm_tail",
    )(yt, u2, sw["d"], sw["wglu"], sw["bglu"], sw["g"])


def _ssm_chunk_weights(lam_re, lam_im, log_dt, b_re, b_im, c_re, c_im, t, n_chunks):
    ng, ns = lam_re.shape
    nc = b_re.shape[2]
    dt = jnp.exp(log_dt)[:, None]
    are, aim = lam_re * dt, lam_im * dt
    d = jnp.arange(t + 1, dtype=F32)[:, None, None]
    mag = jnp.exp(d * are)
    pre, pim = mag * jnp.cos(d * aim), mag * jnp.sin(d * aim)
    lre, lim = pre[1], pim[1]
    den = lam_re * lam_re + lam_im * lam_im
    fre = ((lre - 1.0) * lam_re + lim * lam_im) / den
    fim = (lim * lam_re - (lre - 1.0) * lam_im) / den
    bbr = fre[..., None] * b_re - fim[..., None] * b_im
    bbi = fre[..., None] * b_im + fim[..., None] * b_re
    xr = c_re[None] * pre[:, :, None, :] - c_im[None] * pim[:, :, None, :]
    xi = c_re[None] * pim[:, :, None, :] + c_im[None] * pre[:, :, None, :]
    kern = (jnp.einsum('dgap,gpc->dgac', xr[:t], bbr, precision=HIGHEST)
            - jnp.einsum('dgap,gpc->dgac', xi[:t], bbi, precision=HIGHEST))
    lag = jnp.arange(t)[:, None] - jnp.arange(t)[None, :]
    onehot = (lag[:, :, None] == jnp.arange(t)[None, None, :]).astype(F32)
    m = jnp.einsum('tsd,dgac->gtasc', onehot, kern, precision=HIGHEST)
    m = m.reshape(ng, t * nc, t * nc).astype(BF16)
    rev_re, rev_im = pre[t - 1::-1][:t], pim[t - 1::-1][:t]
    er = rev_re[..., None] * bbr[None] - rev_im[..., None] * bbi[None]
    ei = rev_re[..., None] * bbi[None] + rev_im[..., None] * bbr[None]
    to_e = lambda a: jnp.transpose(a, (1, 2, 0, 3)).reshape(ng, ns, t * nc)
    e = jnp.concatenate([to_e(er), to_e(ei)], axis=1).astype(BF16)
    to_f = lambda a: jnp.transpose(a[1:], (1, 0, 2, 3)).reshape(ng, t * nc, ns)
    f = jnp.concatenate([to_f(xr), to_f(-xi)], axis=2).astype(BF16)
    wr, wi = pre[t], pim[t]
    w1, w2 = [], []
    for _ in range(max(1, (n_chunks - 1).bit_length())):
        w1.append(jnp.concatenate([wr, wr], axis=1))
        w2.append(jnp.concatenate([-wi, wi], axis=1))
        wr, wi = wr * wr - wi * wi, 2.0 * wr * wi
    w1 = jnp.stack(w1, axis=1)[:, :, None, :]
    w2 = jnp.stack(w2, axis=1)[:, :, None, :]
    return {"m": m, "e": e, "f": f, "w1": w1, "w2": w2}


def _ssm_sample_body(u_ref, h0re_ref, h0im_ref, bre_ref, bim_ref, cre_ref, cim_ref, lre_ref, lim_ref,
                     d_ref, wglu_ref, bglu_ref, g_ref, o_ref, hre_ref, him_ref):
    steps = u_ref.shape[0]
    half_in = bre_ref.shape[1]
    dot32 = functools.partial(jnp.dot, precision=HIGHEST, preferred_element_type=F32)
    ar = lre_ref[...]
    ai = lim_ref[...]
    hr = h0re_ref[...]
    hi = h0im_ref[...]
    for t in range(steps):
        u = u_ref[t]
        xr = jnp.concatenate([dot32(u[:, hf * half_in:(hf + 1) * half_in], bre_ref[hf]) for hf in range(2)], axis=1)
        xi = jnp.concatenate([dot32(u[:, hf * half_in:(hf + 1) * half_in], bim_ref[hf]) for hf in range(2)], axis=1)
        hr, hi = ar * hr - ai * hi + xr, ar * hi + ai * hr + xi
        half_st = hr.shape[1] // 2
        y = jnp.concatenate(
            [dot32(hr[:, hf * half_st:(hf + 1) * half_st], cre_ref[hf])
             + dot32(hi[:, hf * half_st:(hf + 1) * half_st], cim_ref[hf]) for hf in range(2)], axis=1)
        o_ref[t] = _ssm_tail(y, u, d_ref, wglu_ref, bglu_ref, g_ref)
    hre_ref[...] = hr
    him_ref[...] = hi


def _ssm_sample(u_tm, h0re, h0im, sw):
    steps, nb, ssm_w = u_tm.shape
    n_state = h0re.shape[1]
    return pl.pallas_call(
        _ssm_sample_body,
        out_shape=(jax.ShapeDtypeStruct((steps, nb, ssm_w), BF16),
                   jax.ShapeDtypeStruct((nb, n_state), F32),
                   jax.ShapeDtypeStruct((nb, n_state), F32)),
        compiler_params=pltpu.CompilerParams(vmem_limit_bytes=VMEM_LIMIT_BYTES),
        name="ssm_sample",
    )(u_tm, h0re, h0im, sw["bre"], sw["bim"], sw["cre"], sw["cim"], sw["lre"], sw["lim"],
      sw["d"], sw["wglu"], sw["bglu"], sw["g"])


def _ssm_weights(lam_re, lam_im, log_dt, b_re, b_im, c_re, c_im, d_skip, w_glu, b_glu, g_ssm):
    ng, ns = lam_re.shape
    nc = b_re.shape[2]
    dt = jnp.exp(log_dt)[:, None]
    mag = jnp.exp(lam_re * dt)
    lre = mag * jnp.cos(lam_im * dt)
    lim = mag * jnp.sin(lam_im * dt)
    den = lam_re * lam_re + lam_im * lam_im
    fre = ((lre - 1.0) * lam_re + lim * lam_im) / den
    fim = (lim * lam_re - (lre - 1.0) * lam_im) / den
    bbar_re = fre[..., None] * b_re - fim[..., None] * b_im
    bbar_im = fre[..., None] * b_im + fim[..., None] * b_re
    eye = jnp.eye(ng // 2, dtype=F32)

    def in_blocks(b):
        b2 = b.reshape(2, ng // 2, ns, nc)
        return jnp.einsum('hgpc,gk->hgckp', b2, eye).reshape(2, ng // 2 * nc, ng // 2 * ns)

    def out_blocks(c):
        c2 = c.reshape(2, ng // 2, nc, ns)
        return jnp.einsum('hgcp,gk->hgpkc', c2, eye).reshape(2, ng // 2 * ns, ng // 2 * nc)

    ssm_w = ng * nc
    return {
        "bre": in_blocks(bbar_re), "bim": in_blocks(bbar_im),
        "cre": out_blocks(c_re), "cim": out_blocks(-c_im),
        "lre": lre.reshape(1, ng * ns), "lim": lim.reshape(1, ng * ns),
        "d": d_skip.reshape(1, ssm_w), "wglu": w_glu.astype(BF16), "bglu": b_glu.reshape(1, ssm_w),
        "g": g_ssm.reshape(1, ssm_w),
    }


def _route(logits):
    lane = lax.broadcasted_iota(jnp.int32, logits.shape, 1)
    big = jnp.int32(10 ** 6)
    is_group = lane < N_EXPERT_GROUPS
    gl = jnp.where(is_group, logits, NEG_INF)
    gmax = jnp.max(gl, axis=-1, keepdims=True)
    gidx = jnp.min(jnp.where(is_group & (gl == gmax), lane, big), axis=-1, keepdims=True)
    g_p = 1.0 / jnp.sum(jnp.exp(gl - gmax), axis=-1, keepdims=True)
    lo = N_EXPERT_GROUPS + gidx * EXPERTS_PER_GROUP
    in_group = (lane >= lo) & (lane < lo + EXPERTS_PER_GROUP)
    el = jnp.where(in_group, logits, NEG_INF)
    ee = jnp.exp(el - jnp.max(el, axis=-1, keepdims=True))
    prob = ee / jnp.sum(ee, axis=-1, keepdims=True)
    p1 = jnp.max(jnp.where(in_group, prob, -1.0), axis=-1, keepdims=True)
    i1 = jnp.min(jnp.where(in_group & (prob == p1), lane, big), axis=-1, keepdims=True)
    rest = in_group & (lane != i1)
    p2 = jnp.max(jnp.where(rest, prob, -1.0), axis=-1, keepdims=True)
    i2 = jnp.min(jnp.where(rest & (prob == p2), lane, big), axis=-1, keepdims=True)
    tot = p1 + p2
    return jnp.where(lane == i1, g_p * p1 / tot, jnp.where(lane == i2, g_p * p2 / tot, 0.0))


def _ffn_body(x_ref, ssm_ref, attn_ref, mod_ref, gpost_ref, gpre_ref, gffn_ref, wout_ref, wr_ref, br_ref,
              wgu_ref, wd_ref, o_ref, *, chunked):
    x = x_ref[...]
    g, r, d = x.shape
    rows = g * r
    gt1, sh2, sc2, gt2 = mod_ref[:, 2], mod_ref[:, 3], mod_ref[:, 4], mod_ref[:, 5]
    if chunked:
        ssm = jnp.swapaxes(ssm_ref[...].astype(F32), 0, 1).reshape(rows, ssm_ref.shape[-1]).astype(BF16)
    else:
        ssm = ssm_ref[...]
    cat = jnp.concatenate([ssm, attn_ref[...]], axis=1)
    mixed = jnp.dot(cat, wout_ref[...], preferred_element_type=F32)
    x1 = x + gt1 * _rms(mixed, gpost_ref[0]).reshape(g, r, d)
    hn = (_rms(x1, gpre_ref[...]) * (1.0 + sc2) + sh2).reshape(rows, d)
    logits = jnp.dot(hn, wr_ref[...], precision=HIGHEST, preferred_element_type=F32) + br_ref[...]
    gates = _route(logits)
    hb = hn.astype(BF16)
    lane = lax.broadcasted_iota(jnp.int32, gates.shape, 1)
    d_exp = wd_ref.shape[1]
    acc = jnp.zeros((rows, d), F32)
    for e in range(N_EXPERTS):
        gate = jnp.sum(jnp.where(lane == N_EXPERT_GROUPS + e, gates, 0.0), axis=-1, keepdims=True)
        hgu = jnp.dot(hb, wgu_ref[e], preferred_element_type=F32)
        act = jax.nn.silu(hgu[:, :d_exp]) * hgu[:, d_exp:] * gate
        acc = acc + jnp.dot(act.astype(BF16), wd_ref[e], preferred_element_type=F32)
    o_ref[...] = x1 + gt2 * _rms(acc, gffn_ref[0]).reshape(g, r, d)


def _ffn(x3, ssm_n, attn_n, mod4, fw, gb, rb, chunk=None):
    nb, nr, d = x3.shape
    assert gb == 1 or rb == nr
    half = ssm_n.shape[-1]
    nj = nr // rb
    tm = gb * rb
    row_map = lambda i, j: (i * nj + j, 0)
    vec = _resident((1, 1, d))
    if chunk is None:
        ssm_spec = pl.BlockSpec((tm, half), row_map)
    else:
        assert gb == 1 and rb % chunk == 0
        ssm_spec = pl.BlockSpec((None, chunk, rb // chunk, half), lambda i, j: (i, 0, j, 0))
    return pl.pallas_call(
        functools.partial(_ffn_body, chunked=chunk is not None),
        out_shape=jax.ShapeDtypeStruct((nb, nr, d), F32),
        grid=(nb // gb, nj),
        in_specs=[pl.BlockSpec((gb, rb, d), lambda i, j: (i, j, 0)),
                  ssm_spec,
                  pl.BlockSpec((tm, attn_n.shape[1]), row_map),
                  pl.BlockSpec((gb, N_MOD, 1, d), lambda i, j: (i, 0, 0, 0)),
                  vec, vec, vec,
                  _resident(fw["wout"].shape), _resident(fw["wr"].shape), _resident(fw["br"].shape),
                  _resident(fw["wgu"].shape), _resident(fw["wd"].shape)],
        out_specs=pl.BlockSpec((gb, rb, d), lambda i, j: (i, j, 0)),
        compiler_params=_cparams("arbitrary", "arbitrary"),
        name="outproj_moe",
    )(x3, ssm_n, attn_n, mod4, fw["gpost"], fw["gpre"], fw["gffn"], fw["wout"], fw["wr"], fw["br"],
      fw["wgu"], fw["wd"])


def _ffn_weights(g_post_mix, g_pre_ffn, g_post_ffn, w_out, w_gr, b_gr, w_er, b_er, w_gate, w_up, w_down):
    d = w_out.shape[0]
    n_route = N_EXPERT_GROUPS + N_EXPERTS
    wr = jnp.concatenate([w_gr, w_er.reshape(d, N_EXPERTS)], axis=1)
    br = jnp.concatenate([b_gr, b_er.reshape(N_EXPERTS)])
    return {
        "gpost": g_post_mix.reshape(1, 1, d), "gpre": g_pre_ffn.reshape(1, 1, d), "gffn": g_post_ffn.reshape(1, 1, d),
        "wout": w_out.astype(BF16),
        "wr": jnp.pad(wr, ((0, 0), (0, LANES - n_route))),
        "br": jnp.pad(br, (0, LANES - n_route)).reshape(1, LANES),
        "wgu": jnp.concatenate([w_gate, w_up], axis=2).astype(BF16),
        "wd": w_down.astype(BF16),
    }


def _rel_bucket(dist):
    max_exact = NUM_BUCKETS // 2
    dd = jnp.maximum(dist, 0)
    log_ratio = jnp.log(jnp.maximum(dd, 1).astype(F32) / max_exact) / math.log(MAX_DISTANCE / max_exact)
    large = jnp.minimum(max_exact + (log_ratio * (NUM_BUCKETS - max_exact)).astype(jnp.int32), NUM_BUCKETS - 1)
    return jnp.where(dd < max_exact, dd, large)


def _bias_and_band(dist, table):
    onehot = (_rel_bucket(dist)[:, :, None] == jnp.arange(NUM_BUCKETS)[None, None, :]).astype(F32)
    bias = jnp.einsum('qkb,bh->hqk', onehot, table.astype(F32), precision=HIGHEST)
    band = ((dist >= 0) & (dist <= WINDOW)).astype(F32)
    return bias, band


def _layer(xp, xs, ck, cv, h0re, h0im, mod_p, mod_s, table, p):
    nb, seq, d = xp.shape
    nbs, t, _ = xs.shape
    w = ck.shape[1]
    kv_w = N_KV_HEADS * HEAD_DIM
    w_in_bf = p["w_in"].astype(BF16)
    sw = _ssm_weights(p["ssm_lam_re"], p["ssm_lam_im"], p["ssm_log_dt"], p["ssm_b_re"], p["ssm_b_im"],
                      p["ssm_c_re"], p["ssm_c_im"], p["ssm_d"], p["w_glu"], p["b_glu"], p["g_ssm_out"])
    fw = _ffn_weights(p["g_post_mix"], p["g_pre_ffn"], p["g_post_ffn"], p["w_out"],
                      p["w_group_router"], p["b_group_router"], p["w_expert_router"], p["b_expert_router"],
                      p["w_exp_gate"], p["w_exp_up"], p["w_exp_down"])
    n_state = sw["lre"].shape[1]
    ng = p["ssm_lam_re"].shape[0]

    u2, q, k, v = _inproj(xp, mod_p, p["g_pre_mix"], w_in_bf, 1, TOKEN_TILE, chunk=SSM_CHUNK)
    blk = WINDOW
    dist_p = jnp.arange(blk)[:, None] + blk - jnp.arange(2 * blk)[None, :]
    bias_p, band_p = _bias_and_band(dist_p, table)
    attn_p = _attn_prompt(q, k, v, p["attn_sinks"], bias_p, band_p, p["g_attn_out"], nb, seq)
    kw = _ssm_chunk_weights(p["ssm_lam_re"], p["ssm_lam_im"], p["ssm_log_dt"], p["ssm_b_re"], p["ssm_b_im"],
                            p["ssm_c_re"], p["ssm_c_im"], SSM_CHUNK, seq // SSM_CHUNK)
    yt, hend = _ssm_chunked(u2, kw)
    ssm_p = _ssm_tail_call(yt, u2, sw)
    yp = _ffn(xp, ssm_p, attn_p, mod_p, fw, 1, TOKEN_TILE, chunk=SSM_CHUNK)
    wp = min(WINDOW, seq)
    k_p = k.reshape(nb, seq, N_KV_HEADS, HEAD_DIM)[:, seq - wp:]
    v_p = v.reshape(nb, seq, N_KV_HEADS, HEAD_DIM)[:, seq - wp:]
    n_p = n_state // ng
    hre_p = hend[:, :, 0, :n_p]
    him_p = hend[:, :, 0, n_p:]

    gs = TOKEN_TILE // t
    us, qs, ks, vs = _inproj(xs, mod_s, p["g_pre_mix"], w_in_bf, gs, t)
    dist_s = jnp.arange(t)[:, None] + w - jnp.arange(w + t)[None, :]
    bias_s, band_s = _bias_and_band(dist_s, table)
    attn_s, k_s, v_s = _attn_sample(qs.reshape(nbs, t, -1), ks.reshape(nbs, t, kv_w), vs.reshape(nbs, t, kv_w),
                                    ck.reshape(nbs, w, kv_w), cv.reshape(nbs, w, kv_w),
                                    p["attn_sinks"], bias_s, band_s, p["g_attn_out"], 16)
    u_tm = jnp.swapaxes(us.reshape(nbs, t, -1), 0, 1)
    ssm_tm, hre_s, him_s = _ssm_sample(u_tm, h0re.reshape(nbs, n_state), h0im.reshape(nbs, n_state), sw)
    ssm_s = jnp.swapaxes(ssm_tm, 0, 1).reshape(nbs * t, -1)
    ys = _ffn(xs, ssm_s, attn_s.reshape(nbs * t, -1), mod_s, fw, gs, t)
    k_s = k_s.reshape(nbs, w, N_KV_HEADS, HEAD_DIM)
    v_s = v_s.reshape(nbs, w, N_KV_HEADS, HEAD_DIM)
    hre_s = hre_s.reshape(nbs, ng, n_state // ng)
    him_s = him_s.reshape(nbs, ng, n_state // ng)
    return yp, ys, k_p, v_p, hre_p, him_p, k_s, v_s, hre_s, him_s


def kernel(x_prompt, x_sample, cache_k, cache_v, state_ssm_re, state_ssm_im, c_prompt, c_sample, rel_bias_table,
           w_ada, b_ada, g_pre_mix, g_post_mix, g_pre_ffn, g_post_ffn, w_in, ssm_lam_re, ssm_lam_im, ssm_log_dt,
           ssm_b_re, ssm_b_im, ssm_c_re, ssm_c_im, ssm_d, w_glu, b_glu, attn_sinks, g_ssm_out, g_attn_out, w_out,
           w_group_router, b_group_router, w_expert_router, b_expert_router, w_exp_gate, w_exp_up, w_exp_down):
    params = dict(
        w_ada=w_ada, b_ada=b_ada, g_pre_mix=g_pre_mix, g_post_mix=g_post_mix, g_pre_ffn=g_pre_ffn,
        g_post_ffn=g_post_ffn, w_in=w_in, ssm_lam_re=ssm_lam_re, ssm_lam_im=ssm_lam_im, ssm_log_dt=ssm_log_dt,
        ssm_b_re=ssm_b_re, ssm_b_im=ssm_b_im, ssm_c_re=ssm_c_re, ssm_c_im=ssm_c_im, ssm_d=ssm_d, w_glu=w_glu,
        b_glu=b_glu, attn_sinks=attn_sinks, g_ssm_out=g_ssm_out, g_attn_out=g_attn_out, w_out=w_out,
        w_group_router=w_group_router, b_group_router=b_group_router, w_expert_router=w_expert_router,
        b_expert_router=b_expert_router, w_exp_gate=w_exp_gate, w_exp_up=w_exp_up, w_exp_down=w_exp_down)
    depth = w_in.shape[0]
    nb, nbs = x_prompt.shape[0], x_sample.shape[0]
    d = x_prompt.shape[2]
    pad = (-nb) % 8
    yp, ys = x_prompt, x_sample
    outs = [[] for _ in range(8)]
    for li in range(depth):
        p = {name: val[li] for name, val in params.items()}
        c_all = jnp.concatenate([c_prompt, jnp.zeros((pad, d), F32), c_sample], axis=0)
        mod = _modulation(c_all, p["w_ada"], p["b_ada"])
        mod_p = mod[:nb].reshape(nb, N_MOD, 1, d)
        mod_s = mod[nb + pad:].reshape(nbs, N_MOD, 1, d)
        res = _layer(yp, ys, cache_k[li], cache_v[li], state_ssm_re[li], state_ssm_im[li], mod_p, mod_s,
                     rel_bias_table, p)
        yp, ys = res[0], res[1]
        for acc, val in zip(outs, res[2:]):
            acc.append(val)
    return (yp, ys) + tuple(jnp.stack(o) for o in outs)
```

```python
import functools
import math

import jax
import jax.numpy as jnp
from jax import lax
from jax.experimental import pallas as pl
from jax.experimental.pallas import tpu as pltpu

F32 = jnp.float32
BF16 = jnp.bfloat16
HIGHEST = lax.Precision.HIGHEST

EPS = 1e-6
NEG_INF = -1e30

SSM_GROUP = 16
SSM_STATE = 64
HEAD_DIM = 64
N_KV_HEADS = 2
GQA = 4
N_HEADS = N_KV_HEADS * GQA
WINDOW = 128
NUM_BUCKETS = 32
MAX_DISTANCE = 128
N_EXPERT_GROUPS = 4
EXPERTS_PER_GROUP = 4
N_EXPERTS = N_EXPERT_GROUPS * EXPERTS_PER_GROUP
N_MOD = 6

LANES = 128
VMEM_LIMIT_BYTES = 56 * 1024 * 1024

TOKEN_TILE = 512
ATTN_TILE = 512
SSM_CHUNK = 32
SSM_GROUP_BLOCK = 8
SSM_TAIL_BLOCK = 4


def _cparams(*sem):
    return pltpu.CompilerParams(dimension_semantics=sem, vmem_limit_bytes=VMEM_LIMIT_BYTES)


def _resident(shape):
    zeros = (0,) * len(shape)
    return pl.BlockSpec(shape, lambda *_: zeros, pipeline_mode=pl.Buffered(1))


def _rms(x, g):
    return x * lax.rsqrt(jnp.mean(x * x, axis=-1, keepdims=True) + EPS) * g


def _mod_body(c_ref, w_ref, b_ref, o_ref):
    a = jax.nn.silu(c_ref[...])
    o_ref[...] = jnp.dot(a, w_ref[...], precision=HIGHEST, preferred_element_type=F32) + b_ref[...]


def _modulation(c, w_ada, b_ada):
    rows, d = c.shape
    n = w_ada.shape[1]
    tn = 1024
    return pl.pallas_call(
        _mod_body,
        out_shape=jax.ShapeDtypeStruct((rows, n), F32),
        grid=(n // tn,),
        in_specs=[pl.BlockSpec((rows, d), lambda j: (0, 0)),
                  pl.BlockSpec((d, tn), lambda j: (0, j)),
                  pl.BlockSpec((1, tn), lambda j: (0, j))],
        out_specs=pl.BlockSpec((rows, tn), lambda j: (0, j)),
        compiler_params=_cparams("arbitrary"),
        name="modulation",
    )(c, w_ada, b_ada.reshape(1, n))


def _inproj_body(x_ref, mod_ref, g_ref, w_ref, u_ref, q_ref, k_ref, v_ref, *, chunked):
    x = x_ref[...]
    g, r, d = x.shape
    hn = _rms(x, g_ref[...]) * (1.0 + mod_ref[:, 1]) + mod_ref[:, 0]
    hn = hn.reshape(g * r, d).astype(BF16)
    proj = jnp.dot(hn, w_ref[...], preferred_element_type=F32)
    ssm_w = u_ref.shape[-1]
    attn_w = q_ref.shape[1]
    kv_w = k_ref.shape[1]
    if chunked:
        t, nc, _ = u_ref.shape
        u_ref[...] = jnp.swapaxes(proj[:, :ssm_w].reshape(nc, t, ssm_w), 0, 1)
    else:
        u_ref[...] = proj[:, :ssm_w]
    q_ref[...] = (proj[:, ssm_w:ssm_w + attn_w] * (HEAD_DIM ** -0.5)).astype(BF16)
    k_ref[...] = proj[:, ssm_w + attn_w:ssm_w + attn_w + kv_w]
    v_ref[...] = proj[:, ssm_w + attn_w + kv_w:]


def _inproj(x3, mod4, g_pre, w_in_bf, gb, rb, chunk=None):
    nb, nr, d = x3.shape
    assert gb == 1 or rb == nr
    n_in = w_in_bf.shape[1]
    kv_w = N_KV_HEADS * HEAD_DIM
    attn_w = N_HEADS * HEAD_DIM
    ssm_w = n_in - attn_w - 2 * kv_w
    rows = nb * nr
    nj = nr // rb
    tm = gb * rb
    out_map = lambda i, j: (i * nj + j, 0)
    if chunk is None:
        u_shape = jax.ShapeDtypeStruct((rows, ssm_w), F32)
        u_spec = pl.BlockSpec((tm, ssm_w), out_map)
    else:
        assert gb == 1 and rb % chunk == 0
        u_shape = jax.ShapeDtypeStruct((nb, chunk, nr // chunk, ssm_w), F32)
        u_spec = pl.BlockSpec((None, chunk, rb // chunk, ssm_w), lambda i, j: (i, 0, j, 0))
    return pl.pallas_call(
        functools.partial(_inproj_body, chunked=chunk is not None),
        out_shape=(u_shape,
                   jax.ShapeDtypeStruct((rows, attn_w), BF16),
                   jax.ShapeDtypeStruct((rows, kv_w), F32),
                   jax.ShapeDtypeStruct((rows, kv_w), F32)),
        grid=(nb // gb, nj),
        in_specs=[pl.BlockSpec((gb, rb, d), lambda i, j: (i, j, 0)),
                  pl.BlockSpec((gb, 2, 1, d), lambda i, j: (i, 0, 0, 0)),
                  _resident((1, 1, d)),
                  _resident((d, n_in))],
        out_specs=(u_spec,
                   pl.BlockSpec((tm, attn_w), out_map),
                   pl.BlockSpec((tm, kv_w), out_map),
                   pl.BlockSpec((tm, kv_w), out_map)),
        compiler_params=_cparams("arbitrary", "arbitrary"),
        name="in_projection",
    )(x3, mod4, g_pre.reshape(1, 1, d), w_in_bf)


def _sink_softmax(s, sk):
    m = jnp.maximum(jnp.max(s, axis=-1, keepdims=True), sk)
    p = jnp.exp(s - m)
    return p / (jnp.sum(p, axis=-1, keepdims=True) + jnp.exp(sk - m))


def _attn_prompt_body(sinks_ref, q_ref, kc_ref, vc_ref, kp_ref, vp_ref, bias_ref, band_ref, g_ref, o_ref):
    tq = q_ref.shape[0]
    blk = band_ref.shape[0]
    first_tile = pl.program_id(1) == 0
    kk = jnp.concatenate([kp_ref[...], kc_ref[...]], axis=0).astype(BF16)
    vv = jnp.concatenate([vp_ref[...], vc_ref[...]], axis=0).astype(BF16)
    band = band_ref[...] > 0.5
    col = lax.broadcasted_iota(jnp.int32, band.shape, 1)
    band_first = band & ((col >= blk) | jnp.logical_not(first_tile))
    for j in range(tq // blk):
        kw = kk[j * blk:(j + 2) * blk]
        vw = vv[j * blk:(j + 2) * blk]
        valid = band_first if j == 0 else band
        outs = []
        for h in range(N_HEADS):
            hk = h // GQA
            qh = q_ref[j * blk:(j + 1) * blk, h * HEAD_DIM:(h + 1) * HEAD_DIM]
            s = lax.dot_general(qh, kw[:, hk * HEAD_DIM:(hk + 1) * HEAD_DIM],
                                (((1,), (1,)), ((), ())), preferred_element_type=F32)
            s = jnp.where(valid, s + bias_ref[h], NEG_INF)
            p = _sink_softmax(s, sinks_ref[h]).astype(BF16)
            outs.append(jnp.dot(p, vw[:, hk * HEAD_DIM:(hk + 1) * HEAD_DIM], preferred_element_type=F32))
        att = jnp.concatenate(outs, axis=1)
        o_ref[j * blk:(j + 1) * blk, :] = _rms(att, g_ref[...]).astype(BF16)


def _attn_prompt(q, k, v, sinks, bias, band, g_attn, nb, seq):
    rows, attn_w = q.shape
    kv_w = k.shape[1]
    blk = band.shape[0]
    tq = ATTN_TILE
    nt = seq // tq
    per = tq // blk
    cur = lambda b, i: (b * nt + i, 0)
    prev = lambda b, i: (b * nt * per + jnp.maximum(i * per - 1, 0), 0)
    return pl.pallas_call(
        _attn_prompt_body,
        out_shape=jax.ShapeDtypeStruct((rows, attn_w), BF16),
        grid=(nb, nt),
        in_specs=[pl.BlockSpec(memory_space=pltpu.SMEM),
                  pl.BlockSpec((tq, attn_w), cur),
                  pl.BlockSpec((tq, kv_w), cur),
                  pl.BlockSpec((tq, kv_w), cur),
                  pl.BlockSpec((blk, kv_w), prev),
                  pl.BlockSpec((blk, kv_w), prev),
                  _resident(bias.shape),
                  _resident(band.shape),
                  _resident((1, attn_w))],
        out_specs=pl.BlockSpec((tq, attn_w), cur),
        compiler_params=_cparams("arbitrary", "arbitrary"),
        name="attention_prompt",
    )(sinks, q, k, v, k, v, bias, band, g_attn.reshape(1, attn_w))


def _attn_sample_body(sinks_ref, q_ref, kn_ref, vn_ref, ck_ref, cv_ref, bias_ref, band_ref, g_ref,
                      o_ref, ko_ref, vo_ref):
    t = q_ref.shape[1]
    kk = jnp.concatenate([ck_ref[...], kn_ref[...]], axis=1)
    vv = jnp.concatenate([cv_ref[...], vn_ref[...]], axis=1)
    ko_ref[...] = kk[:, t:, :]
    vo_ref[...] = vv[:, t:, :]
    kkb = kk.astype(BF16)
    vvb = vv.astype(BF16)
    valid = band_ref[...] > 0.5
    outs = []
    for h in range(N_HEADS):
        hk = h // GQA
        qh = q_ref[:, :, h * HEAD_DIM:(h + 1) * HEAD_DIM]
        s = jnp.einsum('bqd,bkd->bqk', qh, kkb[:, :, hk * HEAD_DIM:(hk + 1) * HEAD_DIM],
                       preferred_element_type=F32)
        s = jnp.where(valid, s + bias_ref[h], NEG_INF)
        p = _sink_softmax(s, sinks_ref[h]).astype(BF16)
        outs.append(jnp.einsum('bqk,bkd->bqd', p, vvb[:, :, hk * HEAD_DIM:(hk + 1) * HEAD_DIM],
                               preferred_element_type=F32))
    att = jnp.concatenate(outs, axis=2)
    o_ref[...] = _rms(att, g_ref[...]).astype(BF16)


def _attn_sample(q3, k3, v3, ck, cv, sinks, bias, band, g_attn, gb):
    nb, t, attn_w = q3.shape
    w, kv_w = ck.shape[1], ck.shape[2]
    blk3 = lambda last: pl.BlockSpec((gb, last[0], last[1]), lambda i: (i, 0, 0))
    return pl.pallas_call(
        _attn_sample_body,
        out_shape=(jax.ShapeDtypeStruct((nb, t, attn_w), BF16),
                   jax.ShapeDtypeStruct((nb, w, kv_w), F32),
                   jax.ShapeDtypeStruct((nb, w, kv_w), F32)),
        grid=(nb // gb,),
        in_specs=[pl.BlockSpec(memory_space=pltpu.SMEM),
                  blk3((t, attn_w)), blk3((t, kv_w)), blk3((t, kv_w)),
                  blk3((w, kv_w)), blk3((w, kv_w)),
                  _resident(bias.shape), _resident(band.shape), _resident((1, 1, attn_w))],
        out_specs=(blk3((t, attn_w)), blk3((w, kv_w)), blk3((w, kv_w))),
        compiler_params=_cparams("arbitrary"),
        name="attention_sample",
    )(sinks, q3, k3, v3, ck, cv, bias, band, g_attn.reshape(1, 1, attn_w))


def _ssm_tail(y, u, d_ref, wglu_ref, bglu_ref, g_ref):
    z = jax.nn.gelu(y + d_ref[...] * u)
    gate = jax.nn.sigmoid(jnp.dot(z.astype(BF16), wglu_ref[...], preferred_element_type=F32) + bglu_ref[...])
    return _rms(z * gate, g_ref[...]).astype(BF16)


def _ssm_chunk_body(u_ref, m_ref, e_ref, f_ref, w1_ref, w2_ref, yt_ref, hend_ref, ut_scr, s_scr):
    t, nc, ssm_w = u_ref.shape
    gb = m_ref.shape[0]
    n_levels = w1_ref.shape[1]
    two_p = e_ref.shape[1]
    ch = ssm_w // ut_scr.shape[1]

    @pl.when(pl.program_id(1) == 0)
    def _():
        for s in range(t):
            ut_scr[s] = u_ref[s].T.astype(BF16).reshape(ut_scr.shape[1], ch, nc)

    g0 = pl.program_id(1) * gb
    row = lax.broadcasted_iota(jnp.int32, (nc, two_p), 0)
    for gl in range(gb):
        r = ut_scr[:, g0 + gl].reshape(t * ch, nc)
        y = jnp.dot(m_ref[gl], r, preferred_element_type=F32)
        s_scr[...] = jnp.dot(e_ref[gl], r, preferred_element_type=F32)
        h = s_scr[...].T
        for lv in range(n_levels):
            sh = 1 << lv
            prev = jnp.where(row >= sh, pltpu.roll(h, sh, axis=0), 0.0)
            h = h + w1_ref[gl, lv] * prev + w2_ref[gl, lv] * pltpu.roll(prev, two_p // 2, axis=1)
        hend_ref[gl] = h[nc - 1:nc, :]
        s_scr[...] = jnp.where(row >= 1, pltpu.roll(h, 1, axis=0), 0.0).T
        y = y + jnp.dot(f_ref[gl], s_scr[...].astype(BF16), preferred_element_type=F32)
        yt_ref[:, gl * ch:(gl + 1) * ch, :] = y.reshape(t, ch, nc)


def _ssm_chunked(u2, kw):
    nb, t, nc, ssm_w = u2.shape
    ng, tc, _ = kw["m"].shape
    two_p = kw["e"].shape[1]
    n_levels = kw["w1"].shape[1]
    ch = tc // t
    gb = SSM_GROUP_BLOCK
    grp = lambda shape: pl.BlockSpec((gb,) + shape, lambda b, g: (g,) + (0,) * len(shape))
    return pl.pallas_call(
        _ssm_chunk_body,
        out_shape=(jax.ShapeDtypeStruct((nb, t, ssm_w, nc), F32),
                   jax.ShapeDtypeStruct((nb, ng, 1, two_p), F32)),
        grid=(nb, ng // gb),
        in_specs=[pl.BlockSpec((None, t, nc, ssm_w), lambda b, g: (b, 0, 0, 0), pipeline_mode=pl.Buffered(1)),
                  grp((tc, tc)), grp((two_p, tc)), grp((tc, two_p)),
                  grp((n_levels, 1, two_p)), grp((n_levels, 1, two_p))],
        out_specs=(pl.BlockSpec((None, t, gb * ch, nc), lambda b, g: (b, 0, g, 0)),
                   pl.BlockSpec((None, gb, 1, two_p), lambda b, g: (b, g, 0, 0))),
        scratch_shapes=[pltpu.VMEM((t, ng, ch, nc), BF16), pltpu.VMEM((two_p, nc), F32)],
        compiler_params=_cparams("arbitrary", "arbitrary"),
        name="ssm_chunked",
    )(u2, kw["m"], kw["e"], kw["f"], kw["w1"], kw["w2"])


def _ssm_tail_body(yt_ref, u_ref, d_ref, wglu_ref, bglu_ref, g_ref, o_ref):
    for i in range(yt_ref.shape[0]):
        o_ref[i] = _ssm_tail(yt_ref[i].T, u_ref[i], d_ref, wglu_ref, bglu_ref, g_ref)


def _ssm_tail_call(yt, u2, sw):
    nb, t, nc, ssm_w = u2.shape
    tb = SSM_TAIL_BLOCK
    return pl.pallas_call(
        _ssm_tail_body,
        out_shape=jax.ShapeDtypeStruct((nb, t, nc, ssm_w), BF16),
        grid=(nb, t // tb),
        in_specs=[pl.BlockSpec((None, tb, ssm_w, nc), lambda b, i: (b, i, 0, 0)),
                  pl.BlockSpec((None, tb, nc, ssm_w), lambda b, i: (b, i, 0, 0)),
                  _resident((1, ssm_w)), _resident((ssm_w, ssm_w)), _resident((1, ssm_w)), _resident((1, ssm_w))],
        out_specs=pl.BlockSpec((None, tb, nc, ssm_w), lambda b, i: (b, i, 0, 0)),
        compiler_params=_cparams("arbitrary", "arbitrary"),
        name="ssm_tail",
    )(yt, u2, sw["d"], sw["wglu"], sw["bglu"], sw["g"])


def _lag_matrix_body(z_ref, m_ref, *, t):
    gb, ch, _ = z_ref.shape
    width = t * ch
    for gl in range(gb):
        z = z_ref[gl]
        for r in range(t):
            off = (t - 1 - r) * ch
            m_ref[gl, r * ch:(r + 1) * ch, :] = z[:, off:off + width].astype(BF16)


def _lag_matrix(z, t):
    ng, ch, zw = z.shape
    gb = SSM_GROUP_BLOCK
    return pl.pallas_call(
        functools.partial(_lag_matrix_body, t=t),
        out_shape=jax.ShapeDtypeStruct((ng, t * ch, zw // 2), BF16),
        grid=(ng // gb,),
        in_specs=[pl.BlockSpec((gb, ch, zw), lambda g: (g, 0, 0))],
        out_specs=pl.BlockSpec((gb, t * ch, zw // 2), lambda g: (g, 0, 0)),
        compiler_params=_cparams("arbitrary"),
        name="ssm_lag_matrix",
    )(z)


def _ssm_chunk_weights(lam_re, lam_im, log_dt, b_re, b_im, c_re, c_im, t, n_chunks):
    ng, ns = lam_re.shape
    nc = b_re.shape[2]
    dt = jnp.exp(log_dt)[:, None]
    are, aim = lam_re * dt, lam_im * dt
    d = jnp.arange(t + 1, dtype=F32)[:, None, None]
    mag = jnp.exp(d * are)
    pre, pim = mag * jnp.cos(d * aim), mag * jnp.sin(d * aim)
    lre, lim = pre[1], pim[1]
    den = lam_re * lam_re + lam_im * lam_im
    fre = ((lre - 1.0) * lam_re + lim * lam_im) / den
    fim = (lim * lam_re - (lre - 1.0) * lam_im) / den
    bbr = fre[..., None] * b_re - fim[..., None] * b_im
    bbi = fre[..., None] * b_im + fim[..., None] * b_re
    pre_g, pim_g = jnp.transpose(pre, (1, 0, 2)), jnp.transpose(pim, (1, 0, 2))
    xr = c_re[:, None] * pre_g[:, :, None, :] - c_im[:, None] * pim_g[:, :, None, :]
    xi = c_re[:, None] * pim_g[:, :, None, :] + c_im[:, None] * pre_g[:, :, None, :]
    kern = (jnp.einsum('gdap,gpc->gdac', xr[:, :t], bbr, precision=HIGHEST)
            - jnp.einsum('gdap,gpc->gdac', xi[:, :t], bbi, precision=HIGHEST))
    lag_rows = jnp.transpose(kern[:, ::-1], (0, 2, 1, 3)).reshape(ng, nc, t * nc)
    m = _lag_matrix(jnp.concatenate([lag_rows, jnp.zeros_like(lag_rows)], axis=2), t)
    rev_re = jnp.transpose(pre_g[:, t - 1::-1][:, :t], (0, 2, 1))
    rev_im = jnp.transpose(pim_g[:, t - 1::-1][:, :t], (0, 2, 1))
    er = rev_re[..., None] * bbr[:, :, None, :] - rev_im[..., None] * bbi[:, :, None, :]
    ei = rev_re[..., None] * bbi[:, :, None, :] + rev_im[..., None] * bbr[:, :, None, :]
    e = jnp.concatenate([er.reshape(ng, ns, t * nc), ei.reshape(ng, ns, t * nc)], axis=1).astype(BF16)
    f = jnp.concatenate([xr[:, 1:].reshape(ng, t * nc, ns), -xi[:, 1:].reshape(ng, t * nc, ns)], axis=2).astype(BF16)
    wr, wi = pre[t], pim[t]
    w1, w2 = [], []
    for _ in range(max(1, (n_chunks - 1).bit_length())):
        w1.append(jnp.concatenate([wr, wr], axis=1))
        w2.append(jnp.concatenate([-wi, wi], axis=1))
        wr, wi = wr * wr - wi * wi, 2.0 * wr * wi
    w1 = jnp.stack(w1, axis=1)[:, :, None, :]
    w2 = jnp.stack(w2, axis=1)[:, :, None, :]
    return {"m": m, "e": e, "f": f, "w1": w1, "w2": w2}


def _ssm_sample_body(u_ref, h0re_ref, h0im_ref, bre_ref, bim_ref, cre_ref, cim_ref, lre_ref, lim_ref,
                     d_ref, wglu_ref, bglu_ref, g_ref, o_ref, hre_ref, him_ref):
    steps = u_ref.shape[0]
    half_in = bre_ref.shape[1]
    dot32 = functools.partial(jnp.dot, precision=HIGHEST, preferred_element_type=F32)
    ar = lre_ref[...]
    ai = lim_ref[...]
    hr = h0re_ref[...]
    hi = h0im_ref[...]
    for t in range(steps):
        u = u_ref[t]
        xr = jnp.concatenate([dot32(u[:, hf * half_in:(hf + 1) * half_in], bre_ref[hf]) for hf in range(2)], axis=1)
        xi = jnp.concatenate([dot32(u[:, hf * half_in:(hf + 1) * half_in], bim_ref[hf]) for hf in range(2)], axis=1)
        hr, hi = ar * hr - ai * hi + xr, ar * hi + ai * hr + xi
        half_st = hr.shape[1] // 2
        y = jnp.concatenate(
            [dot32(hr[:, hf * half_st:(hf + 1) * half_st], cre_ref[hf])
             + dot32(hi[:, hf * half_st:(hf + 1) * half_st], cim_ref[hf]) for hf in range(2)], axis=1)
        o_ref[t] = _ssm_tail(y, u, d_ref, wglu_ref, bglu_ref, g_ref)
    hre_ref[...] = hr
    him_ref[...] = hi


def _ssm_sample(u_tm, h0re, h0im, sw):
    steps, nb, ssm_w = u_tm.shape
    n_state = h0re.shape[1]
    return pl.pallas_call(
        _ssm_sample_body,
        out_shape=(jax.ShapeDtypeStruct((steps, nb, ssm_w), BF16),
                   jax.ShapeDtypeStruct((nb, n_state), F32),
                   jax.ShapeDtypeStruct((nb, n_state), F32)),
        compiler_params=pltpu.CompilerParams(vmem_limit_bytes=VMEM_LIMIT_BYTES),
        name="ssm_sample",
    )(u_tm, h0re, h0im, sw["bre"], sw["bim"], sw["cre"], sw["cim"], sw["lre"], sw["lim"],
      sw["d"], sw["wglu"], sw["bglu"], sw["g"])


def _ssm_weights(lam_re, lam_im, log_dt, b_re, b_im, c_re, c_im, d_skip, w_glu, b_glu, g_ssm):
    ng, ns = lam_re.shape
    nc = b_re.shape[2]
    dt = jnp.exp(log_dt)[:, None]
    mag = jnp.exp(lam_re * dt)
    lre = mag * jnp.cos(lam_im * dt)
    lim = mag * jnp.sin(lam_im * dt)
    den = lam_re * lam_re + lam_im * lam_im
    fre = ((lre - 1.0) * lam_re + lim * lam_im) / den
    fim = (lim * lam_re - (lre - 1.0) * lam_im) / den
    bbar_re = fre[..., None] * b_re - fim[..., None] * b_im
    bbar_im = fre[..., None] * b_im + fim[..., None] * b_re
    eye = jnp.eye(ng // 2, dtype=F32)

    def in_blocks(b):
        b2 = b.reshape(2, ng // 2, ns, nc)
        return jnp.einsum('hgpc,gk->hgckp', b2, eye).reshape(2, ng // 2 * nc, ng // 2 * ns)

    def out_blocks(c):
        c2 = c.reshape(2, ng // 2, nc, ns)
        return jnp.einsum('hgcp,gk->hgpkc', c2, eye).reshape(2, ng // 2 * ns, ng // 2 * nc)

    ssm_w = ng * nc
    return {
        "bre": in_blocks(bbar_re), "bim": in_blocks(bbar_im),
        "cre": out_blocks(c_re), "cim": out_blocks(-c_im),
        "lre": lre.reshape(1, ng * ns), "lim": lim.reshape(1, ng * ns),
        "d": d_skip.reshape(1, ssm_w), "wglu": w_glu.astype(BF16), "bglu": b_glu.reshape(1, ssm_w),
        "g": g_ssm.reshape(1, ssm_w),
    }


def _route(logits):
    lane = lax.broadcasted_iota(jnp.int32, logits.shape, 1)
    big = jnp.int32(10 ** 6)
    is_group = lane < N_EXPERT_GROUPS
    gl = jnp.where(is_group, logits, NEG_INF)
    gmax = jnp.max(gl, axis=-1, keepdims=True)
    gidx = jnp.min(jnp.where(is_group & (gl == gmax), lane, big), axis=-1, keepdims=True)
    g_p = 1.0 / jnp.sum(jnp.exp(gl - gmax), axis=-1, keepdims=True)
    lo = N_EXPERT_GROUPS + gidx * EXPERTS_PER_GROUP
    in_group = (lane >= lo) & (lane < lo + EXPERTS_PER_GROUP)
    el = jnp.where(in_group, logits, NEG_INF)
    ee = jnp.exp(el - jnp.max(el, axis=-1, keepdims=True))
    prob = ee / jnp.sum(ee, axis=-1, keepdims=True)
    p1 = jnp.max(jnp.where(in_group, prob, -1.0), axis=-1, keepdims=True)
    i1 = jnp.min(jnp.where(in_group & (prob == p1), lane, big), axis=-1, keepdims=True)
    rest = in_group & (lane != i1)
    p2 = jnp.max(jnp.where(rest, prob, -1.0), axis=-1, keepdims=True)
    i2 = jnp.min(jnp.where(rest & (prob == p2), lane, big), axis=-1, keepdims=True)
    tot = p1 + p2
    return jnp.where(lane == i1, g_p * p1 / tot, jnp.where(lane == i2, g_p * p2 / tot, 0.0))


def _ffn_body(x_ref, ssm_ref, attn_ref, mod_ref, gpost_ref, gpre_ref, gffn_ref, wout_ref, wr_ref, br_ref,
              wg_ref, wu_ref, wd_ref, o_ref, *, chunked):
    x = x_ref[...]
    g, r, d = x.shape
    rows = g * r
    gt1, sh2, sc2, gt2 = mod_ref[:, 2], mod_ref[:, 3], mod_ref[:, 4], mod_ref[:, 5]
    if chunked:
        ssm = jnp.swapaxes(ssm_ref[...].astype(F32), 0, 1).reshape(rows, ssm_ref.shape[-1]).astype(BF16)
    else:
        ssm = ssm_ref[...]
    cat = jnp.concatenate([ssm, attn_ref[...]], axis=1)
    mixed = jnp.dot(cat, wout_ref[...], preferred_element_type=F32)
    x1 = x + gt1 * _rms(mixed, gpost_ref[0]).reshape(g, r, d)
    hn = (_rms(x1, gpre_ref[...]) * (1.0 + sc2) + sh2).reshape(rows, d)
    hb = hn.astype(BF16)
    hlo = (hn - hb.astype(F32)).astype(BF16)
    logits = (jnp.dot(hb, wr_ref[0], preferred_element_type=F32) + jnp.dot(hlo, wr_ref[0], preferred_element_type=F32)
              + jnp.dot(hb, wr_ref[1], preferred_element_type=F32) + br_ref[...])
    gates = _route(logits)
    lane = lax.broadcasted_iota(jnp.int32, gates.shape, 1)
    acc = None
    for grp in range(N_EXPERT_GROUPS):
        acts = []
        for e in range(grp * EXPERTS_PER_GROUP, (grp + 1) * EXPERTS_PER_GROUP):
            gate = jnp.sum(jnp.where(lane == N_EXPERT_GROUPS + e, gates, 0.0), axis=-1, keepdims=True)
            hg = jnp.dot(hb, wg_ref[e], preferred_element_type=F32)
            hu = jnp.dot(hb, wu_ref[e], preferred_element_type=F32)
            acts.append((jax.nn.silu(hg) * hu * gate).astype(BF16))
        part = jnp.dot(jnp.concatenate(acts, axis=1), wd_ref[grp], preferred_element_type=F32)
        acc = part if acc is None else acc + part
    o_ref[...] = x1 + gt2 * _rms(acc, gffn_ref[0]).reshape(g, r, d)


def _ffn(x3, ssm_n, attn_n, mod4, fw, gb, rb, chunk=None):
    nb, nr, d = x3.shape
    assert gb == 1 or rb == nr
    half = ssm_n.shape[-1]
    nj = nr // rb
    tm = gb * rb
    row_map = lambda i, j: (i * nj + j, 0)
    vec = _resident((1, 1, d))
    if chunk is None:
        ssm_spec = pl.BlockSpec((tm, half), row_map)
    else:
        assert gb == 1 and rb % chunk == 0
        ssm_spec = pl.BlockSpec((None, chunk, rb // chunk, half), lambda i, j: (i, 0, j, 0))
    return pl.pallas_call(
        functools.partial(_ffn_body, chunked=chunk is not None),
        out_shape=jax.ShapeDtypeStruct((nb, nr, d), F32),
        grid=(nb // gb, nj),
        in_specs=[pl.BlockSpec((gb, rb, d), lambda i, j: (i, j, 0)),
                  ssm_spec,
                  pl.BlockSpec((tm, attn_n.shape[1]), row_map),
                  pl.BlockSpec((gb, N_MOD, 1, d), lambda i, j: (i, 0, 0, 0)),
                  vec, vec, vec,
                  _resident(fw["wout"].shape), _resident(fw["wr"].shape), _resident(fw["br"].shape),
                  _resident(fw["wg"].shape), _resident(fw["wu"].shape), _resident(fw["wd"].shape)],
        out_specs=pl.BlockSpec((gb, rb, d), lambda i, j: (i, j, 0)),
        compiler_params=_cparams("arbitrary", "arbitrary"),
        name="outproj_moe",
    )(x3, ssm_n, attn_n, mod4, fw["gpost"], fw["gpre"], fw["gffn"], fw["wout"], fw["wr"], fw["br"],
      fw["wg"], fw["wu"], fw["wd"])


def _ffn_weights(g_post_mix, g_pre_ffn, g_post_ffn, w_out, w_gr, b_gr, w_er, b_er, w_gate, w_up, w_down):
    d = w_out.shape[0]
    n_route = N_EXPERT_GROUPS + N_EXPERTS
    wr = jnp.concatenate([w_gr, w_er.reshape(d, N_EXPERTS)], axis=1)
    br = jnp.concatenate([b_gr, b_er.reshape(N_EXPERTS)])
    wr = jnp.pad(wr, ((0, 0), (0, LANES - n_route)))
    wr_hi = wr.astype(BF16)
    return {
        "gpost": g_post_mix.reshape(1, 1, d), "gpre": g_pre_ffn.reshape(1, 1, d), "gffn": g_post_ffn.reshape(1, 1, d),
        "wout": w_out.astype(BF16),
        "wr": jnp.stack([wr_hi, (wr - wr_hi.astype(F32)).astype(BF16)]),
        "br": jnp.pad(br, (0, LANES - n_route)).reshape(1, LANES),
        "wg": w_gate.astype(BF16), "wu": w_up.astype(BF16),
        "wd": w_down.astype(BF16).reshape(N_EXPERT_GROUPS, -1, d),
    }


def _rel_bucket(dist):
    max_exact = NUM_BUCKETS // 2
    dd = jnp.maximum(dist, 0)
    log_ratio = jnp.log(jnp.maximum(dd, 1).astype(F32) / max_exact) / math.log(MAX_DISTANCE / max_exact)
    large = jnp.minimum(max_exact + (log_ratio * (NUM_BUCKETS - max_exact)).astype(jnp.int32), NUM_BUCKETS - 1)
    return jnp.where(dd < max_exact, dd, large)


def _bias_and_band(dist, table):
    onehot = (_rel_bucket(dist)[:, :, None] == jnp.arange(NUM_BUCKETS)[None, None, :]).astype(F32)
    bias = jnp.einsum('qkb,bh->hqk', onehot, table.astype(F32), precision=HIGHEST)
    band = ((dist >= 0) & (dist <= WINDOW)).astype(F32)
    return bias, band


def _layer(xp, xs, ck, cv, h0re, h0im, mod_p, mod_s, table, p):
    nb, seq, d = xp.shape
    nbs, t, _ = xs.shape
    w = ck.shape[1]
    kv_w = N_KV_HEADS * HEAD_DIM
    w_in_bf = p["w_in"].astype(BF16)
    sw = _ssm_weights(p["ssm_lam_re"], p["ssm_lam_im"], p["ssm_log_dt"], p["ssm_b_re"], p["ssm_b_im"],
                      p["ssm_c_re"], p["ssm_c_im"], p["ssm_d"], p["w_glu"], p["b_glu"], p["g_ssm_out"])
    fw = _ffn_weights(p["g_post_mix"], p["g_pre_ffn"], p["g_post_ffn"], p["w_out"],
                      p["w_group_router"], p["b_group_router"], p["w_expert_router"], p["b_expert_router"],
                      p["w_exp_gate"], p["w_exp_up"], p["w_exp_down"])
    n_state = sw["lre"].shape[1]
    ng = p["ssm_lam_re"].shape[0]

    u2, q, k, v = _inproj(xp, mod_p, p["g_pre_mix"], w_in_bf, 1, TOKEN_TILE, chunk=SSM_CHUNK)
    blk = WINDOW
    dist_p = jnp.arange(blk)[:, None] + blk - jnp.arange(2 * blk)[None, :]
    bias_p, band_p = _bias_and_band(dist_p, table)
    attn_p = _attn_prompt(q, k, v, p["attn_sinks"], bias_p, band_p, p["g_attn_out"], nb, seq)
    kw = _ssm_chunk_weights(p["ssm_lam_re"], p["ssm_lam_im"], p["ssm_log_dt"], p["ssm_b_re"], p["ssm_b_im"],
                            p["ssm_c_re"], p["ssm_c_im"], SSM_CHUNK, seq // SSM_CHUNK)
    yt, hend = _ssm_chunked(u2, kw)
    ssm_p = _ssm_tail_call(yt, u2, sw)
    yp = _ffn(xp, ssm_p, attn_p, mod_p, fw, 1, TOKEN_TILE, chunk=SSM_CHUNK)
    wp = min(WINDOW, seq)
    k_p = k.reshape(nb, seq, kv_w)[:, seq - wp:].reshape(nb, wp, N_KV_HEADS, HEAD_DIM)
    v_p = v.reshape(nb, seq, kv_w)[:, seq - wp:].reshape(nb, wp, N_KV_HEADS, HEAD_DIM)
    n_p = n_state // ng
    hre_p = hend[:, :, 0, :n_p]
    him_p = hend[:, :, 0, n_p:]

    gs = TOKEN_TILE // t
    us, qs, ks, vs = _inproj(xs, mod_s, p["g_pre_mix"], w_in_bf, gs, t)
    dist_s = jnp.arange(t)[:, None] + w - jnp.arange(w + t)[None, :]
    bias_s, band_s = _bias_and_band(dist_s, table)
    attn_s, k_s, v_s = _attn_sample(qs.reshape(nbs, t, -1), ks.reshape(nbs, t, kv_w), vs.reshape(nbs, t, kv_w),
                                    ck.reshape(nbs, w, kv_w), cv.reshape(nbs, w, kv_w),
                                    p["attn_sinks"], bias_s, band_s, p["g_attn_out"], 16)
    u_tm = jnp.swapaxes(us.reshape(nbs, t, -1), 0, 1)
    ssm_tm, hre_s, him_s = _ssm_sample(u_tm, h0re.reshape(nbs, n_state), h0im.reshape(nbs, n_state), sw)
    ssm_s = jnp.swapaxes(ssm_tm, 0, 1).reshape(nbs * t, -1)
    ys = _ffn(xs, ssm_s, attn_s.reshape(nbs * t, -1), mod_s, fw, gs, t)
    k_s = k_s.reshape(nbs, w, N_KV_HEADS, HEAD_DIM)
    v_s = v_s.reshape(nbs, w, N_KV_HEADS, HEAD_DIM)
    hre_s = hre_s.reshape(nbs, ng, n_state // ng)
    him_s = him_s.reshape(nbs, ng, n_state // ng)
    return yp, ys, k_p, v_p, hre_p, him_p, k_s, v_s, hre_s, him_s


def kernel(x_prompt, x_sample, cache_k, cache_v, state_ssm_re, state_ssm_im, c_prompt, c_sample, rel_bias_table,
           w_ada, b_ada, g_pre_mix, g_post_mix, g_pre_ffn, g_post_ffn, w_in, ssm_lam_re, ssm_lam_im, ssm_log_dt,
           ssm_b_re, ssm_b_im, ssm_c_re, ssm_c_im, ssm_d, w_glu, b_glu, attn_sinks, g_ssm_out, g_attn_out, w_out,
           w_group_router, b_group_router, w_expert_router, b_expert_router, w_exp_gate, w_exp_up, w_exp_down):
    params = dict(
        w_ada=w_ada, b_ada=b_ada, g_pre_mix=g_pre_mix, g_post_mix=g_post_mix, g_pre_ffn=g_pre_ffn,
        g_post_ffn=g_post_ffn, w_in=w_in, ssm_lam_re=ssm_lam_re, ssm_lam_im=ssm_lam_im, ssm_log_dt=ssm_log_dt,
        ssm_b_re=ssm_b_re, ssm_b_im=ssm_b_im, ssm_c_re=ssm_c_re, ssm_c_im=ssm_c_im, ssm_d=ssm_d, w_glu=w_glu,
        b_glu=b_glu, attn_sinks=attn_sinks, g_ssm_out=g_ssm_out, g_attn_out=g_attn_out, w_out=w_out,
        w_group_router=w_group_router, b_group_router=b_group_router, w_expert_router=w_expert_router,
        b_expert_router=b_expert_router, w_exp_gate=w_exp_gate, w_exp_up=w_exp_up, w_exp_down=w_exp_down)
    depth = w_in.shape[0]
    nb, nbs = x_prompt.shape[0], x_sample.shape[0]
    d = x_prompt.shape[2]
    pad = (-nb) % 8
    yp, ys = x_prompt, x_sample
    outs = [[] for _ in range(8)]
    for li in range(depth):
        p = {name: val[li] for name, val in params.items()}
        c_all = jnp.concatenate([c_prompt, jnp.zeros((pad, d), F32), c_sample], axis=0)
        mod = _modulation(c_all, p["w_ada"], p["b_ada"])
        mod_p = mod[:nb].reshape(nb, N_MOD, 1, d)
        mod_s = mod[nb + pad:].reshape(nbs, N_MOD, 1, d)
        res = _layer(yp, ys, cache_k[li], cache_v[li], state_ssm_re[li], state_ssm_im[li], mod_p, mod_s,
                     rel_bias_table, p)
        yp, ys = res[0], res[1]
        for acc, val in zip(outs, res[2:]):
            acc.append(val)
    return (yp, ys) + tuple(jnp.stack(o) for o in outs)
```

```python
import functools
import math

import jax
import jax.numpy as jnp
from jax import lax
from jax.experimental import pallas as pl
from jax.experimental.pallas import tpu as pltpu

F32 = jnp.float32
BF16 = jnp.bfloat16
HIGHEST = lax.Precision.HIGHEST

EPS = 1e-6
NEG_INF = -1e30

SSM_GROUP = 16
SSM_STATE = 64
HEAD_DIM = 64
N_KV_HEADS = 2
GQA = 4
N_HEADS = N_KV_HEADS * GQA
WINDOW = 128
NUM_BUCKETS = 32
MAX_DISTANCE = 128
N_EXPERT_GROUPS = 4
EXPERTS_PER_GROUP = 4
N_EXPERTS = N_EXPERT_GROUPS * EXPERTS_PER_GROUP
N_MOD = 6

LANES = 128
VMEM_LIMIT_BYTES = 56 * 1024 * 1024

TOKEN_TILE = 512
ATTN_TILE = 512
SSM_CHUNK = 32
SSM_GROUP_BLOCK = 8
SSM_TAIL_BLOCK = 4
MOE_ROW_BLOCK = 128


def _cparams(*sem):
    return pltpu.CompilerParams(dimension_semantics=sem, vmem_limit_bytes=VMEM_LIMIT_BYTES)


def _resident(shape):
    zeros = (0,) * len(shape)
    return pl.BlockSpec(shape, lambda *_: zeros, pipeline_mode=pl.Buffered(1))


def _rms(x, g):
    return x * lax.rsqrt(jnp.mean(x * x, axis=-1, keepdims=True) + EPS) * g


def _mod_body(c_ref, w_ref, b_ref, o_ref):
    a = jax.nn.silu(c_ref[...])
    o_ref[...] = jnp.dot(a, w_ref[...], precision=HIGHEST, preferred_element_type=F32) + b_ref[...]


def _modulation(c, w_ada, b_ada):
    rows, d = c.shape
    n = w_ada.shape[1]
    tn = 1024
    return pl.pallas_call(
        _mod_body,
        out_shape=jax.ShapeDtypeStruct((rows, n), F32),
        grid=(n // tn,),
        in_specs=[pl.BlockSpec((rows, d), lambda j: (0, 0)),
                  pl.BlockSpec((d, tn), lambda j: (0, j)),
                  pl.BlockSpec((1, tn), lambda j: (0, j))],
        out_specs=pl.BlockSpec((rows, tn), lambda j: (0, j)),
        compiler_params=_cparams("arbitrary"),
        name="modulation",
    )(c, w_ada, b_ada.reshape(1, n))


def _inproj_body(x_ref, mod_ref, g_ref, w_ref, u_ref, q_ref, k_ref, v_ref, *, chunked):
    x = x_ref[...]
    g, r, d = x.shape
    hn = _rms(x, g_ref[...]) * (1.0 + mod_ref[:, 1]) + mod_ref[:, 0]
    hn = hn.reshape(g * r, d).astype(BF16)
    proj = jnp.dot(hn, w_ref[...], preferred_element_type=F32)
    ssm_w = u_ref.shape[-1]
    attn_w = q_ref.shape[1]
    kv_w = k_ref.shape[1]
    if chunked:
        t, nc, _ = u_ref.shape
        u_ref[...] = jnp.swapaxes(proj[:, :ssm_w].reshape(nc, t, ssm_w), 0, 1)
    else:
        u_ref[...] = proj[:, :ssm_w]
    q_ref[...] = (proj[:, ssm_w:ssm_w + attn_w] * (HEAD_DIM ** -0.5)).astype(BF16)
    k_ref[...] = proj[:, ssm_w + attn_w:ssm_w + attn_w + kv_w]
    v_ref[...] = proj[:, ssm_w + attn_w + kv_w:]


def _inproj(x3, mod4, g_pre, w_in_bf, gb, rb, chunk=None):
    nb, nr, d = x3.shape
    assert gb == 1 or rb == nr
    n_in = w_in_bf.shape[1]
    kv_w = N_KV_HEADS * HEAD_DIM
    attn_w = N_HEADS * HEAD_DIM
    ssm_w = n_in - attn_w - 2 * kv_w
    rows = nb * nr
    nj = nr // rb
    tm = gb * rb
    out_map = lambda i, j: (i * nj + j, 0)
    if chunk is None:
        u_shape = jax.ShapeDtypeStruct((rows, ssm_w), F32)
        u_spec = pl.BlockSpec((tm, ssm_w), out_map)
    else:
        assert gb == 1 and rb % chunk == 0
        u_shape = jax.ShapeDtypeStruct((nb, chunk, nr // chunk, ssm_w), F32)
        u_spec = pl.BlockSpec((None, chunk, rb // chunk, ssm_w), lambda i, j: (i, 0, j, 0))
    return pl.pallas_call(
        functools.partial(_inproj_body, chunked=chunk is not None),
        out_shape=(u_shape,
                   jax.ShapeDtypeStruct((rows, attn_w), BF16),
                   jax.ShapeDtypeStruct((rows, kv_w), F32),
                   jax.ShapeDtypeStruct((rows, kv_w), F32)),
        grid=(nb // gb, nj),
        in_specs=[pl.BlockSpec((gb, rb, d), lambda i, j: (i, j, 0)),
                  pl.BlockSpec((gb, 2, 1, d), lambda i, j: (i, 0, 0, 0)),
                  _resident((1, 1, d)),
                  _resident((d, n_in))],
        out_specs=(u_spec,
                   pl.BlockSpec((tm, attn_w), out_map),
                   pl.BlockSpec((tm, kv_w), out_map),
                   pl.BlockSpec((tm, kv_w), out_map)),
        compiler_params=_cparams("arbitrary", "arbitrary"),
        name="in_projection",
    )(x3, mod4, g_pre.reshape(1, 1, d), w_in_bf)


def _sink_softmax(s, sk):
    m = jnp.maximum(jnp.max(s, axis=-1, keepdims=True), sk)
    p = jnp.exp(s - m)
    return p / (jnp.sum(p, axis=-1, keepdims=True) + jnp.exp(sk - m))


def _attn_prompt_body(sinks_ref, q_ref, kc_ref, vc_ref, kp_ref, vp_ref, bias_ref, band_ref, g_ref, o_ref):
    tq = q_ref.shape[0]
    blk = band_ref.shape[0]
    first_tile = pl.program_id(1) == 0
    kk = jnp.concatenate([kp_ref[...], kc_ref[...]], axis=0).astype(BF16)
    vv = jnp.concatenate([vp_ref[...], vc_ref[...]], axis=0).astype(BF16)
    band = band_ref[...] > 0.5
    col = lax.broadcasted_iota(jnp.int32, band.shape, 1)
    band_first = band & ((col >= blk) | jnp.logical_not(first_tile))
    for j in range(tq // blk):
        kw = kk[j * blk:(j + 2) * blk]
        vw = vv[j * blk:(j + 2) * blk]
        valid = band_first if j == 0 else band
        outs = []
        for h in range(N_HEADS):
            hk = h // GQA
            qh = q_ref[j * blk:(j + 1) * blk, h * HEAD_DIM:(h + 1) * HEAD_DIM]
            s = lax.dot_general(qh, kw[:, hk * HEAD_DIM:(hk + 1) * HEAD_DIM],
                                (((1,), (1,)), ((), ())), preferred_element_type=F32)
            s = jnp.where(valid, s + bias_ref[h], NEG_INF)
            p = _sink_softmax(s, sinks_ref[h]).astype(BF16)
            outs.append(jnp.dot(p, vw[:, hk * HEAD_DIM:(hk + 1) * HEAD_DIM], preferred_element_type=F32))
        att = jnp.concatenate(outs, axis=1)
        o_ref[j * blk:(j + 1) * blk, :] = _rms(att, g_ref[...]).astype(BF16)


def _attn_prompt(q, k, v, sinks, bias, band, g_attn, nb, seq):
    rows, attn_w = q.shape
    kv_w = k.shape[1]
    blk = band.shape[0]
    tq = ATTN_TILE
    nt = seq // tq
    per = tq // blk
    cur = lambda b, i: (b * nt + i, 0)
    prev = lambda b, i: (b * nt * per + jnp.maximum(i * per - 1, 0), 0)
    return pl.pallas_call(
        _attn_prompt_body,
        out_shape=jax.ShapeDtypeStruct((rows, attn_w), BF16),
        grid=(nb, nt),
        in_specs=[pl.BlockSpec(memory_space=pltpu.SMEM),
                  pl.BlockSpec((tq, attn_w), cur),
                  pl.BlockSpec((tq, kv_w), cur),
                  pl.BlockSpec((tq, kv_w), cur),
                  pl.BlockSpec((blk, kv_w), prev),
                  pl.BlockSpec((blk, kv_w), prev),
                  _resident(bias.shape),
                  _resident(band.shape),
                  _resident((1, attn_w))],
        out_specs=pl.BlockSpec((tq, attn_w), cur),
        compiler_params=_cparams("arbitrary", "arbitrary"),
        name="attention_prompt",
    )(sinks, q, k, v, k, v, bias, band, g_attn.reshape(1, attn_w))


def _attn_sample_body(sinks_ref, q_ref, kn_ref, vn_ref, ck_ref, cv_ref, bias_ref, band_ref, g_ref,
                      o_ref, ko_ref, vo_ref):
    t = q_ref.shape[1]
    kk = jnp.concatenate([ck_ref[...], kn_ref[...]], axis=1)
    vv = jnp.concatenate([cv_ref[...], vn_ref[...]], axis=1)
    ko_ref[...] = kk[:, t:, :]
    vo_ref[...] = vv[:, t:, :]
    kkb = kk.astype(BF16)
    vvb = vv.astype(BF16)
    valid = band_ref[...] > 0.5
    outs = []
    for h in range(N_HEADS):
        hk = h // GQA
        qh = q_ref[:, :, h * HEAD_DIM:(h + 1) * HEAD_DIM]
        s = jnp.einsum('bqd,bkd->bqk', qh, kkb[:, :, hk * HEAD_DIM:(hk + 1) * HEAD_DIM],
                       preferred_element_type=F32)
        s = jnp.where(valid, s + bias_ref[h], NEG_INF)
        p = _sink_softmax(s, sinks_ref[h]).astype(BF16)
        outs.append(jnp.einsum('bqk,bkd->bqd', p, vvb[:, :, hk * HEAD_DIM:(hk + 1) * HEAD_DIM],
                               preferred_element_type=F32))
    att = jnp.concatenate(outs, axis=2)
    o_ref[...] = _rms(att, g_ref[...]).astype(BF16)


def _attn_sample(q3, k3, v3, ck, cv, sinks, bias, band, g_attn, gb):
    nb, t, attn_w = q3.shape
    w, kv_w = ck.shape[1], ck.shape[2]
    blk3 = lambda last: pl.BlockSpec((gb, last[0], last[1]), lambda i: (i, 0, 0))
    return pl.pallas_call(
        _attn_sample_body,
        out_shape=(jax.ShapeDtypeStruct((nb, t, attn_w), BF16),
                   jax.ShapeDtypeStruct((nb, w, kv_w), F32),
                   jax.ShapeDtypeStruct((nb, w, kv_w), F32)),
        grid=(nb // gb,),
        in_specs=[pl.BlockSpec(memory_space=pltpu.SMEM),
                  blk3((t, attn_w)), blk3((t, kv_w)), blk3((t, kv_w)),
                  blk3((w, kv_w)), blk3((w, kv_w)),
                  _resident(bias.shape), _resident(band.shape), _resident((1, 1, attn_w))],
        out_specs=(blk3((t, attn_w)), blk3((w, kv_w)), blk3((w, kv_w))),
        compiler_params=_cparams("arbitrary"),
        name="attention_sample",
    )(sinks, q3, k3, v3, ck, cv, bias, band, g_attn.reshape(1, 1, attn_w))


def _ssm_tail(y, u, d_ref, wglu_ref, bglu_ref, g_ref):
    z = jax.nn.gelu(y + d_ref[...] * u)
    gate = jax.nn.sigmoid(jnp.dot(z.astype(BF16), wglu_ref[...], preferred_element_type=F32) + bglu_ref[...])
    return _rms(z * gate, g_ref[...]).astype(BF16)


def _ssm_chunk_body(u_ref, m_ref, e_ref, f_ref, w1_ref, w2_ref, yt_ref, hend_ref, ut_scr, s_scr):
    t, nc, ssm_w = u_ref.shape
    gb = m_ref.shape[0]
    n_levels = w1_ref.shape[1]
    two_p = e_ref.shape[1]
    ch = ssm_w // ut_scr.shape[1]

    @pl.when(pl.program_id(1) == 0)
    def _():
        for s in range(t):
            ut_scr[s] = u_ref[s].T.astype(BF16).reshape(ut_scr.shape[1], ch, nc)

    g0 = pl.program_id(1) * gb
    row = lax.broadcasted_iota(jnp.int32, (nc, two_p), 0)
    for gl in range(gb):
        r = ut_scr[:, g0 + gl].reshape(t * ch, nc)
        y = jnp.dot(m_ref[gl], r, preferred_element_type=F32)
        s_scr[...] = jnp.dot(e_ref[gl], r, preferred_element_type=F32)
        h = s_scr[...].T
        for lv in range(n_levels):
            sh = 1 << lv
            prev = jnp.where(row >= sh, pltpu.roll(h, sh, axis=0), 0.0)
            h = h + w1_ref[gl, lv] * prev + w2_ref[gl, lv] * pltpu.roll(prev, two_p // 2, axis=1)
        hend_ref[gl] = h[nc - 1:nc, :]
        s_scr[...] = jnp.where(row >= 1, pltpu.roll(h, 1, axis=0), 0.0).T
        y = y + jnp.dot(f_ref[gl], s_scr[...].astype(BF16), preferred_element_type=F32)
        yt_ref[:, gl * ch:(gl + 1) * ch, :] = y.reshape(t, ch, nc)


def _ssm_chunked(u2, kw):
    nb, t, nc, ssm_w = u2.shape
    ng, tc, _ = kw["m"].shape
    two_p = kw["e"].shape[1]
    n_levels = kw["w1"].shape[1]
    ch = tc // t
    gb = SSM_GROUP_BLOCK
    grp = lambda shape: pl.BlockSpec((gb,) + shape, lambda b, g: (g,) + (0,) * len(shape))
    return pl.pallas_call(
        _ssm_chunk_body,
        out_shape=(jax.ShapeDtypeStruct((nb, t, ssm_w, nc), F32),
                   jax.ShapeDtypeStruct((nb, ng, 1, two_p), F32)),
        grid=(nb, ng // gb),
        in_specs=[pl.BlockSpec((None, t, nc, ssm_w), lambda b, g: (b, 0, 0, 0), pipeline_mode=pl.Buffered(1)),
                  grp((tc, tc)), grp((two_p, tc)), grp((tc, two_p)),
                  grp((n_levels, 1, two_p)), grp((n_levels, 1, two_p))],
        out_specs=(pl.BlockSpec((None, t, gb * ch, nc), lambda b, g: (b, 0, g, 0)),
                   pl.BlockSpec((None, gb, 1, two_p), lambda b, g: (b, g, 0, 0))),
        scratch_shapes=[pltpu.VMEM((t, ng, ch, nc), BF16), pltpu.VMEM((two_p, nc), F32)],
        compiler_params=_cparams("arbitrary", "arbitrary"),
        name="ssm_chunked",
    )(u2, kw["m"], kw["e"], kw["f"], kw["w1"], kw["w2"])


def _ssm_tail_body(yt_ref, u_ref, d_ref, wglu_ref, bglu_ref, g_ref, o_ref):
    for i in range(yt_ref.shape[0]):
        o_ref[i] = _ssm_tail(yt_ref[i].T, u_ref[i], d_ref, wglu_ref, bglu_ref, g_ref)


def _ssm_tail_call(yt, u2, sw):
    nb, t, nc, ssm_w = u2.shape
    tb = SSM_TAIL_BLOCK
    return pl.pallas_call(
        _ssm_tail_body,
        out_shape=jax.ShapeDtypeStruct((nb, t, nc, ssm_w), BF16),
        grid=(nb, t // tb),
        in_specs=[pl.BlockSpec((None, tb, ssm_w, nc), lambda b, i: (b, i, 0, 0)),
                  pl.BlockSpec((None, tb, nc, ssm_w), lambda b, i: (b, i, 0, 0)),
                  _resident((1, ssm_w)), _resident((ssm_w, ssm_w)), _resident((1, ssm_w)), _resident((1, ssm_w))],
        out_specs=pl.BlockSpec((None, tb, nc, ssm_w), lambda b, i: (b, i, 0, 0)),
        compiler_params=_cparams("arbitrary", "arbitrary"),
        name="ssm_tail",
    )(yt, u2, sw["d"], sw["wglu"], sw["bglu"], sw["g"])


def _lag_matrix_body(z_ref, m_ref, *, t):
    gb, ch, _ = z_ref.shape
    width = t * ch
    for gl in range(gb):
        z = z_ref[gl]
        for r in range(t):
            off = (t - 1 - r) * ch
            m_ref[gl, r * ch:(r + 1) * ch, :] = z[:, off:off + width].astype(BF16)


def _lag_matrix(z, t):
    ng, ch, zw = z.shape
    gb = SSM_GROUP_BLOCK
    return pl.pallas_call(
        functools.partial(_lag_matrix_body, t=t),
        out_shape=jax.ShapeDtypeStruct((ng, t * ch, zw // 2), BF16),
        grid=(ng // gb,),
        in_specs=[pl.BlockSpec((gb, ch, zw), lambda g: (g, 0, 0))],
        out_specs=pl.BlockSpec((gb, t * ch, zw // 2), lambda g: (g, 0, 0)),
        compiler_params=_cparams("arbitrary"),
        name="ssm_lag_matrix",
    )(z)


def _ssm_chunk_weights(lam_re, lam_im, log_dt, b_re, b_im, c_re, c_im, t, n_chunks):
    ng, ns = lam_re.shape
    nc = b_re.shape[2]
    dt = jnp.exp(log_dt)[:, None]
    are, aim = lam_re * dt, lam_im * dt
    d = jnp.arange(t + 1, dtype=F32)[:, None, None]
    mag = jnp.exp(d * are)
    pre, pim = mag * jnp.cos(d * aim), mag * jnp.sin(d * aim)
    lre, lim = pre[1], pim[1]
    den = lam_re * lam_re + lam_im * lam_im
    fre = ((lre - 1.0) * lam_re + lim * lam_im) / den
    fim = (lim * lam_re - (lre - 1.0) * lam_im) / den
    bbr = fre[..., None] * b_re - fim[..., None] * b_im
    bbi = fre[..., None] * b_im + fim[..., None] * b_re
    pre_g, pim_g = jnp.transpose(pre, (1, 0, 2)), jnp.transpose(pim, (1, 0, 2))
    xr = c_re[:, None] * pre_g[:, :, None, :] - c_im[:, None] * pim_g[:, :, None, :]
    xi = c_re[:, None] * pim_g[:, :, None, :] + c_im[:, None] * pre_g[:, :, None, :]
    kern = (jnp.einsum('gdap,gpc->gdac', xr[:, :t], bbr, precision=HIGHEST)
            - jnp.einsum('gdap,gpc->gdac', xi[:, :t], bbi, precision=HIGHEST))
    lag_rows = jnp.transpose(kern[:, ::-1], (0, 2, 1, 3)).reshape(ng, nc, t * nc)
    m = _lag_matrix(jnp.concatenate([lag_rows, jnp.zeros_like(lag_rows)], axis=2), t)
    rev_re = jnp.transpose(pre_g[:, t - 1::-1][:, :t], (0, 2, 1))
    rev_im = jnp.transpose(pim_g[:, t - 1::-1][:, :t], (0, 2, 1))
    er = rev_re[..., None] * bbr[:, :, None, :] - rev_im[..., None] * bbi[:, :, None, :]
    ei = rev_re[..., None] * bbi[:, :, None, :] + rev_im[..., None] * bbr[:, :, None, :]
    e = jnp.concatenate([er.reshape(ng, ns, t * nc), ei.reshape(ng, ns, t * nc)], axis=1).astype(BF16)
    f = jnp.concatenate([xr[:, 1:].reshape(ng, t * nc, ns), -xi[:, 1:].reshape(ng, t * nc, ns)], axis=2).astype(BF16)
    wr, wi = pre[t], pim[t]
    w1, w2 = [], []
    for _ in range(max(1, (n_chunks - 1).bit_length())):
        w1.append(jnp.concatenate([wr, wr], axis=1))
        w2.append(jnp.concatenate([-wi, wi], axis=1))
        wr, wi = wr * wr - wi * wi, 2.0 * wr * wi
    w1 = jnp.stack(w1, axis=1)[:, :, None, :]
    w2 = jnp.stack(w2, axis=1)[:, :, None, :]
    return {"m": m, "e": e, "f": f, "w1": w1, "w2": w2}


def _ssm_sample_body(u_ref, h0re_ref, h0im_ref, bre_ref, bim_ref, cre_ref, cim_ref, lre_ref, lim_ref,
                     d_ref, wglu_ref, bglu_ref, g_ref, o_ref, hre_ref, him_ref):
    steps = u_ref.shape[0]
    half_in = bre_ref.shape[1]
    dot32 = functools.partial(jnp.dot, precision=HIGHEST, preferred_element_type=F32)
    ar = lre_ref[...]
    ai = lim_ref[...]
    hr = h0re_ref[...]
    hi = h0im_ref[...]
    for t in range(steps):
        u = u_ref[t]
        xr = jnp.concatenate([dot32(u[:, hf * half_in:(hf + 1) * half_in], bre_ref[hf]) for hf in range(2)], axis=1)
        xi = jnp.concatenate([dot32(u[:, hf * half_in:(hf + 1) * half_in], bim_ref[hf]) for hf in range(2)], axis=1)
        hr, hi = ar * hr - ai * hi + xr, ar * hi + ai * hr + xi
        half_st = hr.shape[1] // 2
        y = jnp.concatenate(
            [dot32(hr[:, hf * half_st:(hf + 1) * half_st], cre_ref[hf])
             + dot32(hi[:, hf * half_st:(hf + 1) * half_st], cim_ref[hf]) for hf in range(2)], axis=1)
        o_ref[t] = _ssm_tail(y, u, d_ref, wglu_ref, bglu_ref, g_ref)
    hre_ref[...] = hr
    him_ref[...] = hi


def _ssm_sample(u_tm, h0re, h0im, sw):
    steps, nb, ssm_w = u_tm.shape
    n_state = h0re.shape[1]
    return pl.pallas_call(
        _ssm_sample_body,
        out_shape=(jax.ShapeDtypeStruct((steps, nb, ssm_w), BF16),
                   jax.ShapeDtypeStruct((nb, n_state), F32),
                   jax.ShapeDtypeStruct((nb, n_state), F32)),
        compiler_params=pltpu.CompilerParams(vmem_limit_bytes=VMEM_LIMIT_BYTES),
        name="ssm_sample",
    )(u_tm, h0re, h0im, sw["bre"], sw["bim"], sw["cre"], sw["cim"], sw["lre"], sw["lim"],
      sw["d"], sw["wglu"], sw["bglu"], sw["g"])


def _ssm_weights(lam_re, lam_im, log_dt, b_re, b_im, c_re, c_im, d_skip, w_glu, b_glu, g_ssm):
    ng, ns = lam_re.shape
    nc = b_re.shape[2]
    dt = jnp.exp(log_dt)[:, None]
    mag = jnp.exp(lam_re * dt)
    lre = mag * jnp.cos(lam_im * dt)
    lim = mag * jnp.sin(lam_im * dt)
    den = lam_re * lam_re + lam_im * lam_im
    fre = ((lre - 1.0) * lam_re + lim * lam_im) / den
    fim = (lim * lam_re - (lre - 1.0) * lam_im) / den
    bbar_re = fre[..., None] * b_re - fim[..., None] * b_im
    bbar_im = fre[..., None] * b_im + fim[..., None] * b_re
    eye = jnp.eye(ng // 2, dtype=F32)

    def in_blocks(b):
        b2 = b.reshape(2, ng // 2, ns, nc)
        return jnp.einsum('hgpc,gk->hgckp', b2, eye).reshape(2, ng // 2 * nc, ng // 2 * ns)

    def out_blocks(c):
        c2 = c.reshape(2, ng // 2, nc, ns)
        return jnp.einsum('hgcp,gk->hgpkc', c2, eye).reshape(2, ng // 2 * ns, ng // 2 * nc)

    ssm_w = ng * nc
    return {
        "bre": in_blocks(bbar_re), "bim": in_blocks(bbar_im),
        "cre": out_blocks(c_re), "cim": out_blocks(-c_im),
        "lre": lre.reshape(1, ng * ns), "lim": lim.reshape(1, ng * ns),
        "d": d_skip.reshape(1, ssm_w), "wglu": w_glu.astype(BF16), "bglu": b_glu.reshape(1, ssm_w),
        "g": g_ssm.reshape(1, ssm_w),
    }


def _route(logits):
    lane = lax.broadcasted_iota(jnp.int32, logits.shape, 1)
    big = jnp.int32(10 ** 6)
    is_group = lane < N_EXPERT_GROUPS
    gl = jnp.where(is_group, logits, NEG_INF)
    gmax = jnp.max(gl, axis=-1, keepdims=True)
    gidx = jnp.min(jnp.where(is_group & (gl == gmax), lane, big), axis=-1, keepdims=True)
    g_p = 1.0 / jnp.sum(jnp.exp(gl - gmax), axis=-1, keepdims=True)
    lo = N_EXPERT_GROUPS + gidx * EXPERTS_PER_GROUP
    in_group = (lane >= lo) & (lane < lo + EXPERTS_PER_GROUP)
    el = jnp.where(in_group, logits, NEG_INF)
    ee = jnp.exp(el - jnp.max(el, axis=-1, keepdims=True))
    prob = ee / jnp.sum(ee, axis=-1, keepdims=True)
    p1 = jnp.max(jnp.where(in_group, prob, -1.0), axis=-1, keepdims=True)
    i1 = jnp.min(jnp.where(in_group & (prob == p1), lane, big), axis=-1, keepdims=True)
    rest = in_group & (lane != i1)
    p2 = jnp.max(jnp.where(rest, prob, -1.0), axis=-1, keepdims=True)
    i2 = jnp.min(jnp.where(rest & (prob == p2), lane, big), axis=-1, keepdims=True)
    tot = p1 + p2
    return jnp.where(lane == i1, g_p * p1 / tot, jnp.where(lane == i2, g_p * p2 / tot, 0.0)), gidx


def _split3(x):
    a = x.astype(BF16)
    r = x - a.astype(F32)
    b = r.astype(BF16)
    return a, b, (r - b.astype(F32)).astype(BF16)


def _moe_sorted(hn, gates, gidx, tri_ref, wg_ref, wu_ref, wd_ref, xs_scr, gs_scr, ys_scr):
    rows, d = hn.shape
    lane = lax.broadcasted_iota(jnp.int32, gates.shape, 1)
    member = lane == gidx
    csum = jnp.dot(tri_ref[...], member.astype(BF16), preferred_element_type=F32)
    rank = jnp.sum(jnp.where(member, csum, 0.0), axis=-1, keepdims=True)
    counts = [jnp.sum((gidx == grp).astype(jnp.int32)) for grp in range(N_EXPERT_GROUPS - 1)]
    offs = [jnp.int32(0)]
    for c in counts:
        offs.append(offs[-1] + c)
    offs.append(jnp.int32(rows))
    base = jnp.zeros_like(gidx)
    for grp in range(1, N_EXPERT_GROUPS):
        base = jnp.where(gidx == grp, offs[grp], base)
    pos = base.astype(F32) + rank - 1.0
    col = lax.broadcasted_iota(jnp.int32, (rows, rows), 1).astype(F32)
    row = lax.broadcasted_iota(jnp.int32, (rows, rows), 0).astype(F32)
    unsort = (col == pos).astype(BF16)
    pos_lanes = jnp.broadcast_to(pos, (rows, LANES)).T[0:1, :]
    sort = (row == pos_lanes).astype(BF16)
    xs_scr[...] = jnp.dot(sort, hn.astype(BF16), preferred_element_type=F32).astype(BF16)
    gs_scr[...] = sum(jnp.dot(sort, part, preferred_element_type=F32) for part in _split3(gates))
    ys_scr[...] = jnp.zeros_like(ys_scr)
    blk = MOE_ROW_BLOCK
    for k in range(rows // blk):
        for grp in range(N_EXPERT_GROUPS):
            @pl.when((offs[grp] < (k + 1) * blk) & (offs[grp + 1] > k * blk))
            def _():
                xb = xs_scr[k * blk:(k + 1) * blk, :]
                gb = gs_scr[k * blk:(k + 1) * blk, :]
                lane_b = lax.broadcasted_iota(jnp.int32, gb.shape, 1)
                acts = []
                for e in range(grp * EXPERTS_PER_GROUP, (grp + 1) * EXPERTS_PER_GROUP):
                    gate = jnp.sum(jnp.where(lane_b == N_EXPERT_GROUPS + e, gb, 0.0), axis=-1, keepdims=True)
                    hg = jnp.dot(xb, wg_ref[e], preferred_element_type=F32)
                    hu = jnp.dot(xb, wu_ref[e], preferred_element_type=F32)
                    acts.append((jax.nn.silu(hg) * hu * gate).astype(BF16))
                ys_scr[k * blk:(k + 1) * blk, :] += jnp.dot(jnp.concatenate(acts, axis=1), wd_ref[grp],
                                                            preferred_element_type=F32)
    return sum(jnp.dot(unsort, part, preferred_element_type=F32) for part in _split3(ys_scr[...]))


def _ffn_body(x_ref, ssm_ref, attn_ref, mod_ref, gpost_ref, gpre_ref, gffn_ref, wout_ref, wr_ref, br_ref,
              tri_ref, wg_ref, wu_ref, wd_ref, o_ref, xs_scr, gs_scr, ys_scr, *, chunked):
    x = x_ref[...]
    g, r, d = x.shape
    rows = g * r
    gt1, sh2, sc2, gt2 = mod_ref[:, 2], mod_ref[:, 3], mod_ref[:, 4], mod_ref[:, 5]
    if chunked:
        ssm = jnp.swapaxes(ssm_ref[...].astype(F32), 0, 1).reshape(rows, ssm_ref.shape[-1]).astype(BF16)
    else:
        ssm = ssm_ref[...]
    cat = jnp.concatenate([ssm, attn_ref[...]], axis=1)
    mixed = jnp.dot(cat, wout_ref[...], preferred_element_type=F32)
    x1 = x + gt1 * _rms(mixed, gpost_ref[0]).reshape(g, r, d)
    hn = (_rms(x1, gpre_ref[...]) * (1.0 + sc2) + sh2).reshape(rows, d)
    hb = hn.astype(BF16)
    hlo = (hn - hb.astype(F32)).astype(BF16)
    logits = (jnp.dot(hb, wr_ref[0], preferred_element_type=F32) + jnp.dot(hlo, wr_ref[0], preferred_element_type=F32)
              + jnp.dot(hb, wr_ref[1], preferred_element_type=F32) + br_ref[...])
    gates, gidx = _route(logits)
    ffn = _moe_sorted(hn, gates, gidx, tri_ref, wg_ref, wu_ref, wd_ref, xs_scr, gs_scr, ys_scr)
    o_ref[...] = x1 + gt2 * _rms(ffn, gffn_ref[0]).reshape(g, r, d)


def _ffn(x3, ssm_n, attn_n, mod4, fw, gb, rb, chunk=None):
    nb, nr, d = x3.shape
    assert gb == 1 or rb == nr
    half = ssm_n.shape[-1]
    nj = nr // rb
    tm = gb * rb
    row_map = lambda i, j: (i * nj + j, 0)
    vec = _resident((1, 1, d))
    if chunk is None:
        ssm_spec = pl.BlockSpec((tm, half), row_map)
    else:
        assert gb == 1 and rb % chunk == 0
        ssm_spec = pl.BlockSpec((None, chunk, rb // chunk, half), lambda i, j: (i, 0, j, 0))
    return pl.pallas_call(
        functools.partial(_ffn_body, chunked=chunk is not None),
        out_shape=jax.ShapeDtypeStruct((nb, nr, d), F32),
        grid=(nb // gb, nj),
        in_specs=[pl.BlockSpec((gb, rb, d), lambda i, j: (i, j, 0)),
                  ssm_spec,
                  pl.BlockSpec((tm, attn_n.shape[1]), row_map),
                  pl.BlockSpec((gb, N_MOD, 1, d), lambda i, j: (i, 0, 0, 0)),
                  vec, vec, vec,
                  _resident(fw["wout"].shape), _resident(fw["wr"].shape), _resident(fw["br"].shape),
                  _resident((tm, tm)),
                  _resident(fw["wg"].shape), _resident(fw["wu"].shape), _resident(fw["wd"].shape)],
        out_specs=pl.BlockSpec((gb, rb, d), lambda i, j: (i, j, 0)),
        scratch_shapes=[pltpu.VMEM((tm, d), BF16), pltpu.VMEM((tm, LANES), F32), pltpu.VMEM((tm, d), F32)],
        compiler_params=_cparams("arbitrary", "arbitrary"),
        name="outproj_moe",
    )(x3, ssm_n, attn_n, mod4, fw["gpost"], fw["gpre"], fw["gffn"], fw["wout"], fw["wr"], fw["br"],
      jnp.tril(jnp.ones((tm, tm), BF16)), fw["wg"], fw["wu"], fw["wd"])


def _ffn_weights(g_post_mix, g_pre_ffn, g_post_ffn, w_out, w_gr, b_gr, w_er, b_er, w_gate, w_up, w_down):
    d = w_out.shape[0]
    n_route = N_EXPERT_GROUPS + N_EXPERTS
    wr = jnp.concatenate([w_gr, w_er.reshape(d, N_EXPERTS)], axis=1)
    br = jnp.concatenate([b_gr, b_er.reshape(N_EXPERTS)])
    wr = jnp.pad(wr, ((0, 0), (0, LANES - n_route)))
    wr_hi = wr.astype(BF16)
    return {
        "gpost": g_post_mix.reshape(1, 1, d), "gpre": g_pre_ffn.reshape(1, 1, d), "gffn": g_post_ffn.reshape(1, 1, d),
        "wout": w_out.astype(BF16),
        "wr": jnp.stack([wr_hi, (wr - wr_hi.astype(F32)).astype(BF16)]),
        "br": jnp.pad(br, (0, LANES - n_route)).reshape(1, LANES),
        "wg": w_gate.astype(BF16), "wu": w_up.astype(BF16),
        "wd": w_down.astype(BF16).reshape(N_EXPERT_GROUPS, -1, d),
    }


def _rel_bucket(dist):
    max_exact = NUM_BUCKETS // 2
    dd = jnp.maximum(dist, 0)
    log_ratio = jnp.log(jnp.maximum(dd, 1).astype(F32) / max_exact) / math.log(MAX_DISTANCE / max_exact)
    large = jnp.minimum(max_exact + (log_ratio * (NUM_BUCKETS - max_exact)).astype(jnp.int32), NUM_BUCKETS - 1)
    return jnp.where(dd < max_exact, dd, large)


def _bias_and_band(dist, table):
    onehot = (_rel_bucket(dist)[:, :, None] == jnp.arange(NUM_BUCKETS)[None, None, :]).astype(F32)
    bias = jnp.einsum('qkb,bh->hqk', onehot, table.astype(F32), precision=HIGHEST)
    band = ((dist >= 0) & (dist <= WINDOW)).astype(F32)
    return bias, band


def _layer(xp, xs, ck, cv, h0re, h0im, mod_p, mod_s, table, p):
    nb, seq, d = xp.shape
    nbs, t, _ = xs.shape
    w = ck.shape[1]
    kv_w = N_KV_HEADS * HEAD_DIM
    w_in_bf = p["w_in"].astype(BF16)
    sw = _ssm_weights(p["ssm_lam_re"], p["ssm_lam_im"], p["ssm_log_dt"], p["ssm_b_re"], p["ssm_b_im"],
                      p["ssm_c_re"], p["ssm_c_im"], p["ssm_d"], p["w_glu"], p["b_glu"], p["g_ssm_out"])
    fw = _ffn_weights(p["g_post_mix"], p["g_pre_ffn"], p["g_post_ffn"], p["w_out"],
                      p["w_group_router"], p["b_group_router"], p["w_expert_router"], p["b_expert_router"],
                      p["w_exp_gate"], p["w_exp_up"], p["w_exp_down"])
    n_state = sw["lre"].shape[1]
    ng = p["ssm_lam_re"].shape[0]

    u2, q, k, v = _inproj(xp, mod_p, p["g_pre_mix"], w_in_bf, 1, TOKEN_TILE, chunk=SSM_CHUNK)
    blk = WINDOW
    dist_p = jnp.arange(blk)[:, None] + blk - jnp.arange(2 * blk)[None, :]
    bias_p, band_p = _bias_and_band(dist_p, table)
    attn_p = _attn_prompt(q, k, v, p["attn_sinks"], bias_p, band_p, p["g_attn_out"], nb, seq)
    kw = _ssm_chunk_weights(p["ssm_lam_re"], p["ssm_lam_im"], p["ssm_log_dt"], p["ssm_b_re"], p["ssm_b_im"],
                            p["ssm_c_re"], p["ssm_c_im"], SSM_CHUNK, seq // SSM_CHUNK)
    yt, hend = _ssm_chunked(u2, kw)
    ssm_p = _ssm_tail_call(yt, u2, sw)
    yp = _ffn(xp, ssm_p, attn_p, mod_p, fw, 1, TOKEN_TILE, chunk=SSM_CHUNK)
    wp = min(WINDOW, seq)
    k_p = k.reshape(nb, seq, kv_w)[:, seq - wp:].reshape(nb, wp, N_KV_HEADS, HEAD_DIM)
    v_p = v.reshape(nb, seq, kv_w)[:, seq - wp:].reshape(nb, wp, N_KV_HEADS, HEAD_DIM)
    n_p = n_state // ng
    hre_p = hend[:, :, 0, :n_p]
    him_p = hend[:, :, 0, n_p:]

    gs = TOKEN_TILE // t
    us, qs, ks, vs = _inproj(xs, mod_s, p["g_pre_mix"], w_in_bf, gs, t)
    dist_s = jnp.arange(t)[:, None] + w - jnp.arange(w + t)[None, :]
    bias_s, band_s = _bias_and_band(dist_s, table)
    attn_s, k_s, v_s = _attn_sample(qs.reshape(nbs, t, -1), ks.reshape(nbs, t, kv_w), vs.reshape(nbs, t, kv_w),
                                    ck.reshape(nbs, w, kv_w), cv.reshape(nbs, w, kv_w),
                                    p["attn_sinks"], bias_s, band_s, p["g_attn_out"], 16)
    u_tm = jnp.swapaxes(us.reshape(nbs, t, -1), 0, 1)
    ssm_tm, hre_s, him_s = _ssm_sample(u_tm, h0re.reshape(nbs, n_state), h0im.reshape(nbs, n_state), sw)
    ssm_s = jnp.swapaxes(ssm_tm, 0, 1).reshape(nbs * t, -1)
    ys = _ffn(xs, ssm_s, attn_s.reshape(nbs * t, -1), mod_s, fw, gs, t)
    k_s = k_s.reshape(nbs, w, N_KV_HEADS, HEAD_DIM)
    v_s = v_s.reshape(nbs, w, N_KV_HEADS, HEAD_DIM)
    hre_s = hre_s.reshape(nbs, ng, n_state // ng)
    him_s = him_s.reshape(nbs, ng, n_state // ng)
    return yp, ys, k_p, v_p, hre_p, him_p, k_s, v_s, hre_s, him_s


def kernel(x_prompt, x_sample, cache_k, cache_v, state_ssm_re, state_ssm_im, c_prompt, c_sample, rel_bias_table,
           w_ada, b_ada, g_pre_mix, g_post_mix, g_pre_ffn, g_post_ffn, w_in, ssm_lam_re, ssm_lam_im, ssm_log_dt,
           ssm_b_re, ssm_b_im, ssm_c_re, ssm_c_im, ssm_d, w_glu, b_glu, attn_sinks, g_ssm_out, g_attn_out, w_out,
           w_group_router, b_group_router, w_expert_router, b_expert_router, w_exp_gate, w_exp_up, w_exp_down):
    params = dict(
        w_ada=w_ada, b_ada=b_ada, g_pre_mix=g_pre_mix, g_post_mix=g_post_mix, g_pre_ffn=g_pre_ffn,
        g_post_ffn=g_post_ffn, w_in=w_in, ssm_lam_re=ssm_lam_re, ssm_lam_im=ssm_lam_im, ssm_log_dt=ssm_log_dt,
        ssm_b_re=ssm_b_re, ssm_b_im=ssm_b_im, ssm_c_re=ssm_c_re, ssm_c_im=ssm_c_im, ssm_d=ssm_d, w_glu=w_glu,
        b_glu=b_glu, attn_sinks=attn_sinks, g_ssm_out=g_ssm_out, g_attn_out=g_attn_out, w_out=w_out,
        w_group_router=w_group_router, b_group_router=b_group_router, w_expert_router=w_expert_router,
        b_expert_router=b_expert_router, w_exp_gate=w_exp_gate, w_exp_up=w_exp_up, w_exp_down=w_exp_down)
    depth = w_in.shape[0]
    nb, nbs = x_prompt.shape[0], x_sample.shape[0]
    d = x_prompt.shape[2]
    pad = (-nb) % 8
    yp, ys = x_prompt, x_sample
    outs = [[] for _ in range(8)]
    for li in range(depth):
        p = {name: val[li] for name, val in params.items()}
        c_all = jnp.concatenate([c_prompt, jnp.zeros((pad, d), F32), c_sample], axis=0)
        mod = _modulation(c_all, p["w_ada"], p["b_ada"])
        mod_p = mod[:nb].reshape(nb, N_MOD, 1, d)
        mod_s = mod[nb + pad:].reshape(nbs, N_MOD, 1, d)
        res = _layer(yp, ys, cache_k[li], cache_v[li], state_ssm_re[li], state_ssm_im[li], mod_p, mod_s,
                     rel_bias_table, p)
        yp, ys = res[0], res[1]
        for acc, val in zip(outs, res[2:]):
            acc.append(val)
    return (yp, ys) + tuple(jnp.stack(o) for o in outs)
```

```python
import functools
import math

import jax
import jax.numpy as jnp
from jax import lax
from jax.experimental import pallas as pl
from jax.experimental.pallas import tpu as pltpu

F32 = jnp.float32
BF16 = jnp.bfloat16
HIGHEST = lax.Precision.HIGHEST

EPS = 1e-6
NEG_INF = -1e30

SSM_GROUP = 16
SSM_STATE = 64
HEAD_DIM = 64
N_KV_HEADS = 2
GQA = 4
N_HEADS = N_KV_HEADS * GQA
WINDOW = 128
NUM_BUCKETS = 32
MAX_DISTANCE = 128
N_EXPERT_GROUPS = 4
EXPERTS_PER_GROUP = 4
N_EXPERTS = N_EXPERT_GROUPS * EXPERTS_PER_GROUP
N_MOD = 6

LANES = 128
VMEM_LIMIT_BYTES = 56 * 1024 * 1024

TOKEN_TILE = 512
ATTN_TILE = 512
SSM_CHUNK = 32
SSM_GROUP_BLOCK = 8
SSM_TAIL_BLOCK = 4
MOE_ROW_BLOCK = 128


def _cparams(*sem):
    return pltpu.CompilerParams(dimension_semantics=sem, vmem_limit_bytes=VMEM_LIMIT_BYTES)


def _resident(shape):
    zeros = (0,) * len(shape)
    return pl.BlockSpec(shape, lambda *_: zeros, pipeline_mode=pl.Buffered(1))


def _rms(x, g):
    return x * lax.rsqrt(jnp.mean(x * x, axis=-1, keepdims=True) + EPS) * g


def _mod_body(c_ref, w_ref, b_ref, o_ref):
    a = jax.nn.silu(c_ref[...])
    o_ref[...] = jnp.dot(a, w_ref[...], precision=HIGHEST, preferred_element_type=F32) + b_ref[...]


def _modulation(c, w_ada, b_ada):
    rows, d = c.shape
    n = w_ada.shape[1]
    tn = 1024
    return pl.pallas_call(
        _mod_body,
        out_shape=jax.ShapeDtypeStruct((rows, n), F32),
        grid=(n // tn,),
        in_specs=[pl.BlockSpec((rows, d), lambda j: (0, 0)),
                  pl.BlockSpec((d, tn), lambda j: (0, j)),
                  pl.BlockSpec((1, tn), lambda j: (0, j))],
        out_specs=pl.BlockSpec((rows, tn), lambda j: (0, j)),
        compiler_params=_cparams("arbitrary"),
        name="modulation",
    )(c, w_ada, b_ada.reshape(1, n))


def _inproj_body(x_ref, mod_ref, g_ref, w_ref, u_ref, q_ref, k_ref, v_ref, *, chunked):
    x = x_ref[...]
    g, r, d = x.shape
    hn = _rms(x, g_ref[...]) * (1.0 + mod_ref[:, 1]) + mod_ref[:, 0]
    hn = hn.reshape(g * r, d).astype(BF16)
    proj = jnp.dot(hn, w_ref[...], preferred_element_type=F32)
    ssm_w = u_ref.shape[-1]
    attn_w = q_ref.shape[1]
    kv_w = k_ref.shape[1]
    if chunked:
        t, nc, _ = u_ref.shape
        u_ref[...] = jnp.swapaxes(proj[:, :ssm_w].reshape(nc, t, ssm_w), 0, 1)
    else:
        u_ref[...] = proj[:, :ssm_w]
    q_ref[...] = (proj[:, ssm_w:ssm_w + attn_w] * (HEAD_DIM ** -0.5)).astype(BF16)
    k_ref[...] = proj[:, ssm_w + attn_w:ssm_w + attn_w + kv_w]
    v_ref[...] = proj[:, ssm_w + attn_w + kv_w:]


def _inproj(x3, mod4, g_pre, w_in_bf, gb, rb, chunk=None):
    nb, nr, d = x3.shape
    assert gb == 1 or rb == nr
    n_in = w_in_bf.shape[1]
    kv_w = N_KV_HEADS * HEAD_DIM
    attn_w = N_HEADS * HEAD_DIM
    ssm_w = n_in - attn_w - 2 * kv_w
    rows = nb * nr
    nj = nr // rb
    tm = gb * rb
    out_map = lambda i, j: (i * nj + j, 0)
    if chunk is None:
        u_shape = jax.ShapeDtypeStruct((rows, ssm_w), F32)
        u_spec = pl.BlockSpec((tm, ssm_w), out_map)
    else:
        assert gb == 1 and rb % chunk == 0
        u_shape = jax.ShapeDtypeStruct((nb, chunk, nr // chunk, ssm_w), F32)
        u_spec = pl.BlockSpec((None, chunk, rb // chunk, ssm_w), lambda i, j: (i, 0, j, 0))
    return pl.pallas_call(
        functools.partial(_inproj_body, chunked=chunk is not None),
        out_shape=(u_shape,
                   jax.ShapeDtypeStruct((rows, attn_w), BF16),
                   jax.ShapeDtypeStruct((rows, kv_w), F32),
                   jax.ShapeDtypeStruct((rows, kv_w), F32)),
        grid=(nb // gb, nj),
        in_specs=[pl.BlockSpec((gb, rb, d), lambda i, j: (i, j, 0)),
                  pl.BlockSpec((gb, 2, 1, d), lambda i, j: (i, 0, 0, 0)),
                  _resident((1, 1, d)),
                  _resident((d, n_in))],
        out_specs=(u_spec,
                   pl.BlockSpec((tm, attn_w), out_map),
                   pl.BlockSpec((tm, kv_w), out_map),
                   pl.BlockSpec((tm, kv_w), out_map)),
        compiler_params=_cparams("arbitrary", "arbitrary"),
        name="in_projection",
    )(x3, mod4, g_pre.reshape(1, 1, d), w_in_bf)


def _sink_softmax(s, sk):
    m = jnp.maximum(jnp.max(s, axis=-1, keepdims=True), sk)
    p = jnp.exp(s - m)
    return p / (jnp.sum(p, axis=-1, keepdims=True) + jnp.exp(sk - m))


def _attn_prompt_body(sinks_ref, q_ref, kc_ref, vc_ref, kp_ref, vp_ref, bias_ref, band_ref, g_ref, o_ref):
    tq = q_ref.shape[0]
    blk = band_ref.shape[0]
    first_tile = pl.program_id(1) == 0
    kk = jnp.concatenate([kp_ref[...], kc_ref[...]], axis=0).astype(BF16)
    vv = jnp.concatenate([vp_ref[...], vc_ref[...]], axis=0).astype(BF16)
    band = band_ref[...] > 0.5
    col = lax.broadcasted_iota(jnp.int32, band.shape, 1)
    band_first = band & ((col >= blk) | jnp.logical_not(first_tile))
    for j in range(tq // blk):
        kw = kk[j * blk:(j + 2) * blk]
        vw = vv[j * blk:(j + 2) * blk]
        valid = band_first if j == 0 else band
        outs = []
        for h in range(N_HEADS):
            hk = h // GQA
            qh = q_ref[j * blk:(j + 1) * blk, h * HEAD_DIM:(h + 1) * HEAD_DIM]
            s = lax.dot_general(qh, kw[:, hk * HEAD_DIM:(hk + 1) * HEAD_DIM],
                                (((1,), (1,)), ((), ())), preferred_element_type=F32)
            s = jnp.where(valid, s + bias_ref[h], NEG_INF)
            p = _sink_softmax(s, sinks_ref[h]).astype(BF16)
            outs.append(jnp.dot(p, vw[:, hk * HEAD_DIM:(hk + 1) * HEAD_DIM], preferred_element_type=F32))
        att = jnp.concatenate(outs, axis=1)
        o_ref[j * blk:(j + 1) * blk, :] = _rms(att, g_ref[...]).astype(BF16)


def _attn_prompt(q, k, v, sinks, bias, band, g_attn, nb, seq):
    rows, attn_w = q.shape
    kv_w = k.shape[1]
    blk = band.shape[0]
    tq = ATTN_TILE
    nt = seq // tq
    per = tq // blk
    cur = lambda b, i: (b * nt + i, 0)
    prev = lambda b, i: (b * nt * per + jnp.maximum(i * per - 1, 0), 0)
    return pl.pallas_call(
        _attn_prompt_body,
        out_shape=jax.ShapeDtypeStruct((rows, attn_w), BF16),
        grid=(nb, nt),
        in_specs=[pl.BlockSpec(memory_space=pltpu.SMEM),
                  pl.BlockSpec((tq, attn_w), cur),
                  pl.BlockSpec((tq, kv_w), cur),
                  pl.BlockSpec((tq, kv_w), cur),
                  pl.BlockSpec((blk, kv_w), prev),
                  pl.BlockSpec((blk, kv_w), prev),
                  _resident(bias.shape),
                  _resident(band.shape),
                  _resident((1, attn_w))],
        out_specs=pl.BlockSpec((tq, attn_w), cur),
        compiler_params=_cparams("arbitrary", "arbitrary"),
        name="attention_prompt",
    )(sinks, q, k, v, k, v, bias, band, g_attn.reshape(1, attn_w))


def _attn_sample_body(sinks_ref, q_ref, kn_ref, vn_ref, ck_ref, cv_ref, bias_ref, band_ref, g_ref,
                      o_ref, ko_ref, vo_ref):
    t = q_ref.shape[1]
    kk = jnp.concatenate([ck_ref[...], kn_ref[...]], axis=1)
    vv = jnp.concatenate([cv_ref[...], vn_ref[...]], axis=1)
    ko_ref[...] = kk[:, t:, :]
    vo_ref[...] = vv[:, t:, :]
    kkb = kk.astype(BF16)
    vvb = vv.astype(BF16)
    valid = band_ref[...] > 0.5
    outs = []
    for h in range(N_HEADS):
        hk = h // GQA
        qh = q_ref[:, :, h * HEAD_DIM:(h + 1) * HEAD_DIM]
        s = jnp.einsum('bqd,bkd->bqk', qh, kkb[:, :, hk * HEAD_DIM:(hk + 1) * HEAD_DIM],
                       preferred_element_type=F32)
        s = jnp.where(valid, s + bias_ref[h], NEG_INF)
        p = _sink_softmax(s, sinks_ref[h]).astype(BF16)
        outs.append(jnp.einsum('bqk,bkd->bqd', p, vvb[:, :, hk * HEAD_DIM:(hk + 1) * HEAD_DIM],
                               preferred_element_type=F32))
    att = jnp.concatenate(outs, axis=2)
    o_ref[...] = _rms(att, g_ref[...]).astype(BF16)


def _attn_sample(q3, k3, v3, ck, cv, sinks, bias, band, g_attn, gb):
    nb, t, attn_w = q3.shape
    w, kv_w = ck.shape[1], ck.shape[2]
    blk3 = lambda last: pl.BlockSpec((gb, last[0], last[1]), lambda i: (i, 0, 0))
    return pl.pallas_call(
        _attn_sample_body,
        out_shape=(jax.ShapeDtypeStruct((nb, t, attn_w), BF16),
                   jax.ShapeDtypeStruct((nb, w, kv_w), F32),
                   jax.ShapeDtypeStruct((nb, w, kv_w), F32)),
        grid=(nb // gb,),
        in_specs=[pl.BlockSpec(memory_space=pltpu.SMEM),
                  blk3((t, attn_w)), blk3((t, kv_w)), blk3((t, kv_w)),
                  blk3((w, kv_w)), blk3((w, kv_w)),
                  _resident(bias.shape), _resident(band.shape), _resident((1, 1, attn_w))],
        out_specs=(blk3((t, attn_w)), blk3((w, kv_w)), blk3((w, kv_w))),
        compiler_params=_cparams("arbitrary"),
        name="attention_sample",
    )(sinks, q3, k3, v3, ck, cv, bias, band, g_attn.reshape(1, 1, attn_w))


def _ssm_tail(y, u, d_ref, wglu_ref, bglu_ref, g_ref):
    z = jax.nn.gelu(y + d_ref[...] * u)
    gate = jax.nn.sigmoid(jnp.dot(z.astype(BF16), wglu_ref[...], preferred_element_type=F32) + bglu_ref[...])
    return _rms(z * gate, g_ref[...]).astype(BF16)


def _ssm_chunk_body(u_ref, m_ref, e_ref, f_ref, w1_ref, w2_ref, yt_ref, hend_ref, ut_scr, s_scr):
    t, nc, ssm_w = u_ref.shape
    gb = m_ref.shape[0]
    n_levels = w1_ref.shape[1]
    two_p = e_ref.shape[1]
    ch = ssm_w // ut_scr.shape[1]

    @pl.when(pl.program_id(1) == 0)
    def _():
        for s in range(t):
            ut_scr[s] = u_ref[s].T.astype(BF16).reshape(ut_scr.shape[1], ch, nc)

    g0 = pl.program_id(1) * gb
    row = lax.broadcasted_iota(jnp.int32, (nc, two_p), 0)
    for gl in range(gb):
        r = ut_scr[:, g0 + gl].reshape(t * ch, nc)
        y = jnp.dot(m_ref[gl], r, preferred_element_type=F32)
        s_scr[...] = jnp.dot(e_ref[gl], r, preferred_element_type=F32)
        h = s_scr[...].T
        for lv in range(n_levels):
            sh = 1 << lv
            prev = jnp.where(row >= sh, pltpu.roll(h, sh, axis=0), 0.0)
            h = h + w1_ref[gl, lv] * prev + w2_ref[gl, lv] * pltpu.roll(prev, two_p // 2, axis=1)
        hend_ref[gl] = h[nc - 1:nc, :]
        s_scr[...] = jnp.where(row >= 1, pltpu.roll(h, 1, axis=0), 0.0).T
        y = y + jnp.dot(f_ref[gl], s_scr[...].astype(BF16), preferred_element_type=F32)
        yt_ref[:, gl * ch:(gl + 1) * ch, :] = y.reshape(t, ch, nc)


def _ssm_chunked(u2, kw):
    nb, t, nc, ssm_w = u2.shape
    ng, tc, _ = kw["m"].shape
    two_p = kw["e"].shape[1]
    n_levels = kw["w1"].shape[1]
    ch = tc // t
    gb = SSM_GROUP_BLOCK
    grp = lambda shape: pl.BlockSpec((gb,) + shape, lambda b, g: (g,) + (0,) * len(shape))
    return pl.pallas_call(
        _ssm_chunk_body,
        out_shape=(jax.ShapeDtypeStruct((nb, t, ssm_w, nc), F32),
                   jax.ShapeDtypeStruct((nb, ng, 1, two_p), F32)),
        grid=(nb, ng // gb),
        in_specs=[pl.BlockSpec((None, t, nc, ssm_w), lambda b, g: (b, 0, 0, 0), pipeline_mode=pl.Buffered(1)),
                  grp((tc, tc)), grp((two_p, tc)), grp((tc, two_p)),
                  grp((n_levels, 1, two_p)), grp((n_levels, 1, two_p))],
        out_specs=(pl.BlockSpec((None, t, gb * ch, nc), lambda b, g: (b, 0, g, 0)),
                   pl.BlockSpec((None, gb, 1, two_p), lambda b, g: (b, g, 0, 0))),
        scratch_shapes=[pltpu.VMEM((t, ng, ch, nc), BF16), pltpu.VMEM((two_p, nc), F32)],
        compiler_params=_cparams("arbitrary", "arbitrary"),
        name="ssm_chunked",
    )(u2, kw["m"], kw["e"], kw["f"], kw["w1"], kw["w2"])


def _ssm_tail_body(yt_ref, u_ref, d_ref, wglu_ref, bglu_ref, g_ref, o_ref):
    for i in range(yt_ref.shape[0]):
        o_ref[i] = _ssm_tail(yt_ref[i].T, u_ref[i], d_ref, wglu_ref, bglu_ref, g_ref)


def _ssm_tail_call(yt, u2, sw):
    nb, t, nc, ssm_w = u2.shape
    tb = SSM_TAIL_BLOCK
    return pl.pallas_call(
        _ssm_tail_body,
        out_shape=jax.ShapeDtypeStruct((nb, t, nc, ssm_w), BF16),
        grid=(nb, t // tb),
        in_specs=[pl.BlockSpec((None, tb, ssm_w, nc), lambda b, i: (b, i, 0, 0)),
                  pl.BlockSpec((None, tb, nc, ssm_w), lambda b, i: (b, i, 0, 0)),
                  _resident((1, ssm_w)), _resident((ssm_w, ssm_w)), _resident((1, ssm_w)), _resident((1, ssm_w))],
        out_specs=pl.BlockSpec((None, tb, nc, ssm_w), lambda b, i: (b, i, 0, 0)),
        compiler_params=_cparams("arbitrary", "arbitrary"),
        name="ssm_tail",
    )(yt, u2, sw["d"], sw["wglu"], sw["bglu"], sw["g"])


def _lag_matrix_body(z_ref, m_ref, *, t):
    gb, ch, _ = z_ref.shape
    width = t * ch
    for gl in range(gb):
        z = z_ref[gl]
        for r in range(t):
            off = (t - 1 - r) * ch
            m_ref[gl, r * ch:(r + 1) * ch, :] = z[:, off:off + width].astype(BF16)


def _lag_matrix(z, t):
    ng, ch, zw = z.shape
    gb = SSM_GROUP_BLOCK
    return pl.pallas_call(
        functools.partial(_lag_matrix_body, t=t),
        out_shape=jax.ShapeDtypeStruct((ng, t * ch, zw // 2), BF16),
        grid=(ng // gb,),
        in_specs=[pl.BlockSpec((gb, ch, zw), lambda g: (g, 0, 0))],
        out_specs=pl.BlockSpec((gb, t * ch, zw // 2), lambda g: (g, 0, 0)),
        compiler_params=_cparams("arbitrary"),
        name="ssm_lag_matrix",
    )(z)


def _ssm_chunk_weights(lam_re, lam_im, log_dt, b_re, b_im, c_re, c_im, t, n_chunks):
    ng, ns = lam_re.shape
    nc = b_re.shape[2]
    dt = jnp.exp(log_dt)[:, None]
    are, aim = lam_re * dt, lam_im * dt
    d = jnp.arange(t + 1, dtype=F32)[:, None, None]
    mag = jnp.exp(d * are)
    pre, pim = mag * jnp.cos(d * aim), mag * jnp.sin(d * aim)
    lre, lim = pre[1], pim[1]
    den = lam_re * lam_re + lam_im * lam_im
    fre = ((lre - 1.0) * lam_re + lim * lam_im) / den
    fim = (lim * lam_re - (lre - 1.0) * lam_im) / den
    bbr = fre[..., None] * b_re - fim[..., None] * b_im
    bbi = fre[..., None] * b_im + fim[..., None] * b_re
    pre_g, pim_g = jnp.transpose(pre, (1, 0, 2)), jnp.transpose(pim, (1, 0, 2))
    xr = c_re[:, None] * pre_g[:, :, None, :] - c_im[:, None] * pim_g[:, :, None, :]
    xi = c_re[:, None] * pim_g[:, :, None, :] + c_im[:, None] * pre_g[:, :, None, :]
    kern = (jnp.einsum('gdap,gpc->gdac', xr[:, :t], bbr, precision=HIGHEST)
            - jnp.einsum('gdap,gpc->gdac', xi[:, :t], bbi, precision=HIGHEST))
    lag_rows = jnp.transpose(kern[:, ::-1], (0, 2, 1, 3)).reshape(ng, nc, t * nc)
    m = _lag_matrix(jnp.concatenate([lag_rows, jnp.zeros_like(lag_rows)], axis=2), t)
    rev_re = jnp.transpose(pre_g[:, t - 1::-1][:, :t], (0, 2, 1))
    rev_im = jnp.transpose(pim_g[:, t - 1::-1][:, :t], (0, 2, 1))
    er = rev_re[..., None] * bbr[:, :, None, :] - rev_im[..., None] * bbi[:, :, None, :]
    ei = rev_re[..., None] * bbi[:, :, None, :] + rev_im[..., None] * bbr[:, :, None, :]
    e = jnp.concatenate([er.reshape(ng, ns, t * nc), ei.reshape(ng, ns, t * nc)], axis=1).astype(BF16)
    f = jnp.concatenate([xr[:, 1:].reshape(ng, t * nc, ns), -xi[:, 1:].reshape(ng, t * nc, ns)], axis=2).astype(BF16)
    wr, wi = pre[t], pim[t]
    w1, w2 = [], []
    for _ in range(max(1, (n_chunks - 1).bit_length())):
        w1.append(jnp.concatenate([wr, wr], axis=1))
        w2.append(jnp.concatenate([-wi, wi], axis=1))
        wr, wi = wr * wr - wi * wi, 2.0 * wr * wi
    w1 = jnp.stack(w1, axis=1)[:, :, None, :]
    w2 = jnp.stack(w2, axis=1)[:, :, None, :]
    return {"m": m, "e": e, "f": f, "w1": w1, "w2": w2}


def _ssm_sample_body(u_ref, h0re_ref, h0im_ref, bre_ref, bim_ref, cre_ref, cim_ref, lre_ref, lim_ref,
                     d_ref, wglu_ref, bglu_ref, g_ref, o_ref, hre_ref, him_ref):
    steps = u_ref.shape[0]
    half_in = bre_ref.shape[1]
    dot32 = functools.partial(jnp.dot, precision=HIGHEST, preferred_element_type=F32)
    ar = lre_ref[...]
    ai = lim_ref[...]
    hr = h0re_ref[...]
    hi = h0im_ref[...]
    for t in range(steps):
        u = u_ref[t]
        xr = jnp.concatenate([dot32(u[:, hf * half_in:(hf + 1) * half_in], bre_ref[hf]) for hf in range(2)], axis=1)
        xi = jnp.concatenate([dot32(u[:, hf * half_in:(hf + 1) * half_in], bim_ref[hf]) for hf in range(2)], axis=1)
        hr, hi = ar * hr - ai * hi + xr, ar * hi + ai * hr + xi
        half_st = hr.shape[1] // 2
        y = jnp.concatenate(
            [dot32(hr[:, hf * half_st:(hf + 1) * half_st], cre_ref[hf])
             + dot32(hi[:, hf * half_st:(hf + 1) * half_st], cim_ref[hf]) for hf in range(2)], axis=1)
        o_ref[t] = _ssm_tail(y, u, d_ref, wglu_ref, bglu_ref, g_ref)
    hre_ref[...] = hr
    him_ref[...] = hi


def _ssm_sample(u_tm, h0re, h0im, sw):
    steps, nb, ssm_w = u_tm.shape
    n_state = h0re.shape[1]
    return pl.pallas_call(
        _ssm_sample_body,
        out_shape=(jax.ShapeDtypeStruct((steps, nb, ssm_w), BF16),
                   jax.ShapeDtypeStruct((nb, n_state), F32),
                   jax.ShapeDtypeStruct((nb, n_state), F32)),
        compiler_params=pltpu.CompilerParams(vmem_limit_bytes=VMEM_LIMIT_BYTES),
        name="ssm_sample",
    )(u_tm, h0re, h0im, sw["bre"], sw["bim"], sw["cre"], sw["cim"], sw["lre"], sw["lim"],
      sw["d"], sw["wglu"], sw["bglu"], sw["g"])


def _ssm_weights(lam_re, lam_im, log_dt, b_re, b_im, c_re, c_im, d_skip, w_glu, b_glu, g_ssm):
    ng, ns = lam_re.shape
    nc = b_re.shape[2]
    dt = jnp.exp(log_dt)[:, None]
    mag = jnp.exp(lam_re * dt)
    lre = mag * jnp.cos(lam_im * dt)
    lim = mag * jnp.sin(lam_im * dt)
    den = lam_re * lam_re + lam_im * lam_im
    fre = ((lre - 1.0) * lam_re + lim * lam_im) / den
    fim = (lim * lam_re - (lre - 1.0) * lam_im) / den
    bbar_re = fre[..., None] * b_re - fim[..., None] * b_im
    bbar_im = fre[..., None] * b_im + fim[..., None] * b_re
    eye = jnp.eye(ng // 2, dtype=F32)

    def in_blocks(b):
        b2 = b.reshape(2, ng // 2, ns, nc)
        return jnp.einsum('hgpc,gk->hgckp', b2, eye).reshape(2, ng // 2 * nc, ng // 2 * ns)

    def out_blocks(c):
        c2 = c.reshape(2, ng // 2, nc, ns)
        return jnp.einsum('hgcp,gk->hgpkc', c2, eye).reshape(2, ng // 2 * ns, ng // 2 * nc)

    ssm_w = ng * nc
    return {
        "bre": in_blocks(bbar_re), "bim": in_blocks(bbar_im),
        "cre": out_blocks(c_re), "cim": out_blocks(-c_im),
        "lre": lre.reshape(1, ng * ns), "lim": lim.reshape(1, ng * ns),
        "d": d_skip.reshape(1, ssm_w), "wglu": w_glu.astype(BF16), "bglu": b_glu.reshape(1, ssm_w),
        "g": g_ssm.reshape(1, ssm_w),
    }


def _route(logits):
    lane = lax.broadcasted_iota(jnp.int32, logits.shape, 1)
    big = jnp.int32(10 ** 6)
    is_group = lane < N_EXPERT_GROUPS
    gl = jnp.where(is_group, logits, NEG_INF)
    gmax = jnp.max(gl, axis=-1, keepdims=True)
    gidx = jnp.min(jnp.where(is_group & (gl == gmax), lane, big), axis=-1, keepdims=True)
    g_p = 1.0 / jnp.sum(jnp.exp(gl - gmax), axis=-1, keepdims=True)
    lo = N_EXPERT_GROUPS + gidx * EXPERTS_PER_GROUP
    in_group = (lane >= lo) & (lane < lo + EXPERTS_PER_GROUP)
    el = jnp.where(in_group, logits, NEG_INF)
    ee = jnp.exp(el - jnp.max(el, axis=-1, keepdims=True))
    prob = ee / jnp.sum(ee, axis=-1, keepdims=True)
    p1 = jnp.max(jnp.where(in_group, prob, -1.0), axis=-1, keepdims=True)
    i1 = jnp.min(jnp.where(in_group & (prob == p1), lane, big), axis=-1, keepdims=True)
    rest = in_group & (lane != i1)
    p2 = jnp.max(jnp.where(rest, prob, -1.0), axis=-1, keepdims=True)
    i2 = jnp.min(jnp.where(rest & (prob == p2), lane, big), axis=-1, keepdims=True)
    tot = p1 + p2
    return jnp.where(lane == i1, g_p * p1 / tot, jnp.where(lane == i2, g_p * p2 / tot, 0.0)), gidx


def _split3(x):
    a = x.astype(BF16)
    r = x - a.astype(F32)
    b = r.astype(BF16)
    return a, b, (r - b.astype(F32)).astype(BF16)


def _moe_sorted(hn, gates, gidx, tri_ref, wg_ref, wu_ref, wd_ref, xs_scr, gs_scr, ys_scr):
    rows, d = hn.shape
    lane = lax.broadcasted_iota(jnp.int32, gates.shape, 1)
    member = lane == gidx
    csum = jnp.dot(tri_ref[...], member.astype(BF16), preferred_element_type=F32)
    rank = jnp.sum(jnp.where(member, csum, 0.0), axis=-1, keepdims=True)
    counts = [jnp.sum((gidx == grp).astype(jnp.int32)) for grp in range(N_EXPERT_GROUPS - 1)]
    offs = [jnp.int32(0)]
    for c in counts:
        offs.append(offs[-1] + c)
    offs.append(jnp.int32(rows))
    base = jnp.zeros_like(gidx)
    for grp in range(1, N_EXPERT_GROUPS):
        base = jnp.where(gidx == grp, offs[grp], base)
    pos = base.astype(F32) + rank - 1.0
    col = lax.broadcasted_iota(jnp.int32, (rows, rows), 1).astype(F32)
    row = lax.broadcasted_iota(jnp.int32, (rows, rows), 0).astype(F32)
    unsort = (col == pos).astype(BF16)
    pos_lanes = jnp.broadcast_to(pos, (rows, LANES)).T[0:1, :]
    sort = (row == pos_lanes).astype(BF16)
    xs_scr[...] = jnp.dot(sort, hn.astype(BF16), preferred_element_type=F32).astype(BF16)
    gs_scr[...] = sum(jnp.dot(sort, part, preferred_element_type=F32) for part in _split3(gates))
    ys_scr[...] = jnp.zeros_like(ys_scr)
    blk = MOE_ROW_BLOCK
    n_blk = rows // blk
    cuts = [jnp.int32(k * blk) for k in range(1, n_blk)] + offs[1:N_EXPERT_GROUPS]
    assert len(cuts) == 6
    for a, b in ((0, 5), (1, 3), (2, 4), (1, 2), (3, 4), (0, 3), (2, 5), (0, 1), (2, 3), (4, 5), (1, 2), (3, 4)):
        cuts[a], cuts[b] = jnp.minimum(cuts[a], cuts[b]), jnp.maximum(cuts[a], cuts[b])
    cuts = [jnp.int32(0)] + cuts + [jnp.int32(rows)]
    lane_b = lax.broadcasted_iota(jnp.int32, (blk, LANES), 1)
    for j in range(len(cuts) - 1):
        start = cuts[j]
        live = (cuts[j + 1] > start).astype(F32)
        k = jnp.minimum(start // blk, n_blk - 1)
        grp = sum((offs[q] <= start).astype(jnp.int32) for q in range(1, N_EXPERT_GROUPS))
        sl = pl.ds(pl.multiple_of(k * blk, blk), blk)
        xb = xs_scr[sl, :]
        gb = gs_scr[sl, :] * live
        acts = []
        for i in range(EXPERTS_PER_GROUP):
            e = grp * EXPERTS_PER_GROUP + i
            gate = jnp.sum(jnp.where(lane_b == N_EXPERT_GROUPS + e, gb, 0.0), axis=-1, keepdims=True)
            hg = jnp.dot(xb, wg_ref[e], preferred_element_type=F32)
            hu = jnp.dot(xb, wu_ref[e], preferred_element_type=F32)
            acts.append((jax.nn.silu(hg) * hu * gate).astype(BF16))
        ys_scr[sl, :] += jnp.dot(jnp.concatenate(acts, axis=1), wd_ref[grp], preferred_element_type=F32)
    hi = ys_scr[...].astype(BF16)
    lo = (ys_scr[...] - hi.astype(F32)).astype(BF16)
    return jnp.dot(unsort, hi, preferred_element_type=F32) + jnp.dot(unsort, lo, preferred_element_type=F32)


def _ffn_body(x_ref, ssm_ref, attn_ref, mod_ref, gpost_ref, gpre_ref, gffn_ref, wout_ref, wr_ref, br_ref,
              tri_ref, wg_ref, wu_ref, wd_ref, o_ref, xs_scr, gs_scr, ys_scr, *, chunked):
    x = x_ref[...]
    g, r, d = x.shape
    rows = g * r
    gt1, sh2, sc2, gt2 = mod_ref[:, 2], mod_ref[:, 3], mod_ref[:, 4], mod_ref[:, 5]
    if chunked:
        ssm = jnp.swapaxes(ssm_ref[...].astype(F32), 0, 1).reshape(rows, ssm_ref.shape[-1]).astype(BF16)
    else:
        ssm = ssm_ref[...]
    cat = jnp.concatenate([ssm, attn_ref[...]], axis=1)
    mixed = jnp.dot(cat, wout_ref[...], preferred_element_type=F32)
    x1 = x + gt1 * _rms(mixed, gpost_ref[0]).reshape(g, r, d)
    hn = (_rms(x1, gpre_ref[...]) * (1.0 + sc2) + sh2).reshape(rows, d)
    hb = hn.astype(BF16)
    hlo = (hn - hb.astype(F32)).astype(BF16)
    logits = (jnp.dot(hb, wr_ref[0], preferred_element_type=F32) + jnp.dot(hlo, wr_ref[0], preferred_element_type=F32)
              + jnp.dot(hb, wr_ref[1], preferred_element_type=F32) + br_ref[...])
    gates, gidx = _route(logits)
    ffn = _moe_sorted(hn, gates, gidx, tri_ref, wg_ref, wu_ref, wd_ref, xs_scr, gs_scr, ys_scr)
    o_ref[...] = x1 + gt2 * _rms(ffn, gffn_ref[0]).reshape(g, r, d)


def _ffn(x3, ssm_n, attn_n, mod4, fw, gb, rb, chunk=None):
    nb, nr, d = x3.shape
    assert gb == 1 or rb == nr
    half = ssm_n.shape[-1]
    nj = nr // rb
    tm = gb * rb
    row_map = lambda i, j: (i * nj + j, 0)
    vec = _resident((1, 1, d))
    if chunk is None:
        ssm_spec = pl.BlockSpec((tm, half), row_map)
    else:
        assert gb == 1 and rb % chunk == 0
        ssm_spec = pl.BlockSpec((None, chunk, rb // chunk, half), lambda i, j: (i, 0, j, 0))
    return pl.pallas_call(
        functools.partial(_ffn_body, chunked=chunk is not None),
        out_shape=jax.ShapeDtypeStruct((nb, nr, d), F32),
        grid=(nb // gb, nj),
        in_specs=[pl.BlockSpec((gb, rb, d), lambda i, j: (i, j, 0)),
                  ssm_spec,
                  pl.BlockSpec((tm, attn_n.shape[1]), row_map),
                  pl.BlockSpec((gb, N_MOD, 1, d), lambda i, j: (i, 0, 0, 0)),
                  vec, vec, vec,
                  _resident(fw["wout"].shape), _resident(fw["wr"].shape), _resident(fw["br"].shape),
                  _resident((tm, tm)),
                  _resident(fw["wg"].shape), _resident(fw["wu"].shape), _resident(fw["wd"].shape)],
        out_specs=pl.BlockSpec((gb, rb, d), lambda i, j: (i, j, 0)),
        scratch_shapes=[pltpu.VMEM((tm, d), BF16), pltpu.VMEM((tm, LANES), F32), pltpu.VMEM((tm, d), F32)],
        compiler_params=_cparams("arbitrary", "arbitrary"),
        name="outproj_moe",
    )(x3, ssm_n, attn_n, mod4, fw["gpost"], fw["gpre"], fw["gffn"], fw["wout"], fw["wr"], fw["br"],
      jnp.tril(jnp.ones((tm, tm), BF16)), fw["wg"], fw["wu"], fw["wd"])


def _ffn_weights(g_post_mix, g_pre_ffn, g_post_ffn, w_out, w_gr, b_gr, w_er, b_er, w_gate, w_up, w_down):
    d = w_out.shape[0]
    n_route = N_EXPERT_GROUPS + N_EXPERTS
    wr = jnp.concatenate([w_gr, w_er.reshape(d, N_EXPERTS)], axis=1)
    br = jnp.concatenate([b_gr, b_er.reshape(N_EXPERTS)])
    wr = jnp.pad(wr, ((0, 0), (0, LANES - n_route)))
    wr_hi = wr.astype(BF16)
    return {
        "gpost": g_post_mix.reshape(1, 1, d), "gpre": g_pre_ffn.reshape(1, 1, d), "gffn": g_post_ffn.reshape(1, 1, d),
        "wout": w_out.astype(BF16),
        "wr": jnp.stack([wr_hi, (wr - wr_hi.astype(F32)).astype(BF16)]),
        "br": jnp.pad(br, (0, LANES - n_route)).reshape(1, LANES),
        "wg": w_gate.astype(BF16), "wu": w_up.astype(BF16),
        "wd": w_down.astype(BF16).reshape(N_EXPERT_GROUPS, -1, d),
    }


def _rel_bucket(dist):
    max_exact = NUM_BUCKETS // 2
    dd = jnp.maximum(dist, 0)
    log_ratio = jnp.log(jnp.maximum(dd, 1).astype(F32) / max_exact) / math.log(MAX_DISTANCE / max_exact)
    large = jnp.minimum(max_exact + (log_ratio * (NUM_BUCKETS - max_exact)).astype(jnp.int32), NUM_BUCKETS - 1)
    return jnp.where(dd < max_exact, dd, large)


def _bias_and_band(dist, table):
    onehot = (_rel_bucket(dist)[:, :, None] == jnp.arange(NUM_BUCKETS)[None, None, :]).astype(F32)
    bias = jnp.einsum('qkb,bh->hqk', onehot, table.astype(F32), precision=HIGHEST)
    band = ((dist >= 0) & (dist <= WINDOW)).astype(F32)
    return bias, band


def _layer(xp, xs, ck, cv, h0re, h0im, mod_p, mod_s, table, p):
    nb, seq, d = xp.shape
    nbs, t, _ = xs.shape
    w = ck.shape[1]
    kv_w = N_KV_HEADS * HEAD_DIM
    w_in_bf = p["w_in"].astype(BF16)
    sw = _ssm_weights(p["ssm_lam_re"], p["ssm_lam_im"], p["ssm_log_dt"], p["ssm_b_re"], p["ssm_b_im"],
                      p["ssm_c_re"], p["ssm_c_im"], p["ssm_d"], p["w_glu"], p["b_glu"], p["g_ssm_out"])
    fw = _ffn_weights(p["g_post_mix"], p["g_pre_ffn"], p["g_post_ffn"], p["w_out"],
                      p["w_group_router"], p["b_group_router"], p["w_expert_router"], p["b_expert_router"],
                      p["w_exp_gate"], p["w_exp_up"], p["w_exp_down"])
    n_state = sw["lre"].shape[1]
    ng = p["ssm_lam_re"].shape[0]

    u2, q, k, v = _inproj(xp, mod_p, p["g_pre_mix"], w_in_bf, 1, TOKEN_TILE, chunk=SSM_CHUNK)
    blk = WINDOW
    dist_p = jnp.arange(blk)[:, None] + blk - jnp.arange(2 * blk)[None, :]
    bias_p, band_p = _bias_and_band(dist_p, table)
    attn_p = _attn_prompt(q, k, v, p["attn_sinks"], bias_p, band_p, p["g_attn_out"], nb, seq)
    kw = _ssm_chunk_weights(p["ssm_lam_re"], p["ssm_lam_im"], p["ssm_log_dt"], p["ssm_b_re"], p["ssm_b_im"],
                            p["ssm_c_re"], p["ssm_c_im"], SSM_CHUNK, seq // SSM_CHUNK)
    yt, hend = _ssm_chunked(u2, kw)
    ssm_p = _ssm_tail_call(yt, u2, sw)
    yp = _ffn(xp, ssm_p, attn_p, mod_p, fw, 1, TOKEN_TILE, chunk=SSM_CHUNK)
    wp = min(WINDOW, seq)
    k_p = k.reshape(nb, seq, kv_w)[:, seq - wp:].reshape(nb, wp, N_KV_HEADS, HEAD_DIM)
    v_p = v.reshape(nb, seq, kv_w)[:, seq - wp:].reshape(nb, wp, N_KV_HEADS, HEAD_DIM)
    n_p = n_state // ng
    hre_p = hend[:, :, 0, :n_p]
    him_p = hend[:, :, 0, n_p:]

    gs = TOKEN_TILE // t
    us, qs, ks, vs = _inproj(xs, mod_s, p["g_pre_mix"], w_in_bf, gs, t)
    dist_s = jnp.arange(t)[:, None] + w - jnp.arange(w + t)[None, :]
    bias_s, band_s = _bias_and_band(dist_s, table)
    attn_s, k_s, v_s = _attn_sample(qs.reshape(nbs, t, -1), ks.reshape(nbs, t, kv_w), vs.reshape(nbs, t, kv_w),
                                    ck.reshape(nbs, w, kv_w), cv.reshape(nbs, w, kv_w),
                                    p["attn_sinks"], bias_s, band_s, p["g_attn_out"], 16)
    u_tm = jnp.swapaxes(us.reshape(nbs, t, -1), 0, 1)
    ssm_tm, hre_s, him_s = _ssm_sample(u_tm, h0re.reshape(nbs, n_state), h0im.reshape(nbs, n_state), sw)
    ssm_s = jnp.swapaxes(ssm_tm, 0, 1).reshape(nbs * t, -1)
    ys = _ffn(xs, ssm_s, attn_s.reshape(nbs * t, -1), mod_s, fw, gs, t)
    k_s = k_s.reshape(nbs, w, N_KV_HEADS, HEAD_DIM)
    v_s = v_s.reshape(nbs, w, N_KV_HEADS, HEAD_DIM)
    hre_s = hre_s.reshape(nbs, ng, n_state // ng)
    him_s = him_s.reshape(nbs, ng, n_state // ng)
    return yp, ys, k_p, v_p, hre_p, him_p, k_s, v_s, hre_s, him_s


def kernel(x_prompt, x_sample, cache_k, cache_v, state_ssm_re, state_ssm_im, c_prompt, c_sample, rel_bias_table,
           w_ada, b_ada, g_pre_mix, g_post_mix, g_pre_ffn, g_post_ffn, w_in, ssm_lam_re, ssm_lam_im, ssm_log_dt,
           ssm_b_re, ssm_b_im, ssm_c_re, ssm_c_im, ssm_d, w_glu, b_glu, attn_sinks, g_ssm_out, g_attn_out, w_out,
           w_group_router, b_group_router, w_expert_router, b_expert_router, w_exp_gate, w_exp_up, w_exp_down):
    params = dict(
        w_ada=w_ada, b_ada=b_ada, g_pre_mix=g_pre_mix, g_post_mix=g_post_mix, g_pre_ffn=g_pre_ffn,
        g_post_ffn=g_post_ffn, w_in=w_in, ssm_lam_re=ssm_lam_re, ssm_lam_im=ssm_lam_im, ssm_log_dt=ssm_log_dt,
        ssm_b_re=ssm_b_re, ssm_b_im=ssm_b_im, ssm_c_re=ssm_c_re, ssm_c_im=ssm_c_im, ssm_d=ssm_d, w_glu=w_glu,
        b_glu=b_glu, attn_sinks=attn_sinks, g_ssm_out=g_ssm_out, g_attn_out=g_attn_out, w_out=w_out,
        w_group_router=w_group_router, b_group_router=b_group_router, w_expert_router=w_expert_router,
        b_expert_router=b_expert_router, w_exp_gate=w_exp_gate, w_exp_up=w_exp_up, w_exp_down=w_exp_down)
    depth = w_in.shape[0]
    nb, nbs = x_prompt.shape[0], x_sample.shape[0]
    d = x_prompt.shape[2]
    pad = (-nb) % 8
    yp, ys = x_prompt, x_sample
    outs = [[] for _ in range(8)]
    for li in range(depth):
        p = {name: val[li] for name, val in params.items()}
        c_all = jnp.concatenate([c_prompt, jnp.zeros((pad, d), F32), c_sample], axis=0)
        mod = _modulation(c_all, p["w_ada"], p["b_ada"])
        mod_p = mod[:nb].reshape(nb, N_MOD, 1, d)
        mod_s = mod[nb + pad:].reshape(nbs, N_MOD, 1, d)
        res = _layer(yp, ys, cache_k[li], cache_v[li], state_ssm_re[li], state_ssm_im[li], mod_p, mod_s,
                     rel_bias_table, p)
        yp, ys = res[0], res[1]
        for acc, val in zip(outs, res[2:]):
            acc.append(val)
    return (yp, ys) + tuple(jnp.stack(o) for o in outs)
```

```python
import functools
import math

import jax
import jax.numpy as jnp
from jax import lax
from jax.experimental import pallas as pl
from jax.experimental.pallas import tpu as pltpu

F32 = jnp.float32
BF16 = jnp.bfloat16
HIGHEST = lax.Precision.HIGHEST

EPS = 1e-6
NEG_INF = -1e30

SSM_GROUP = 16
SSM_STATE = 64
HEAD_DIM = 64
N_KV_HEADS = 2
GQA = 4
N_HEADS = N_KV_HEADS * GQA
WINDOW = 128
NUM_BUCKETS = 32
MAX_DISTANCE = 128
N_EXPERT_GROUPS = 4
EXPERTS_PER_GROUP = 4
N_EXPERTS = N_EXPERT_GROUPS * EXPERTS_PER_GROUP
N_MOD = 6

LANES = 128
VMEM_LIMIT_BYTES = 56 * 1024 * 1024

TOKEN_TILE = 512
ATTN_TILE = 512
SSM_CHUNK = 32
SSM_GROUP_BLOCK = 8
SSM_TAIL_BLOCK = 4
MOE_ROW_BLOCK = 128


def _cparams(*sem):
    return pltpu.CompilerParams(dimension_semantics=sem, vmem_limit_bytes=VMEM_LIMIT_BYTES)


def _resident(shape):
    zeros = (0,) * len(shape)
    return pl.BlockSpec(shape, lambda *_: zeros, pipeline_mode=pl.Buffered(1))


def _rms(x, g):
    return x * lax.rsqrt(jnp.mean(x * x, axis=-1, keepdims=True) + EPS) * g


def _mod_body(c_ref, w_ref, b_ref, o_ref):
    a = jax.nn.silu(c_ref[...])
    o_ref[...] = jnp.dot(a, w_ref[...], precision=HIGHEST, preferred_element_type=F32) + b_ref[...]


def _modulation(c, w_ada, b_ada):
    rows, d = c.shape
    n = w_ada.shape[1]
    tn = 1024
    return pl.pallas_call(
        _mod_body,
        out_shape=jax.ShapeDtypeStruct((rows, n), F32),
        grid=(n // tn,),
        in_specs=[pl.BlockSpec((rows, d), lambda j: (0, 0)),
                  pl.BlockSpec((d, tn), lambda j: (0, j)),
                  pl.BlockSpec((1, tn), lambda j: (0, j))],
        out_specs=pl.BlockSpec((rows, tn), lambda j: (0, j)),
        compiler_params=_cparams("arbitrary"),
        name="modulation",
    )(c, w_ada, b_ada.reshape(1, n))


def _inproj_body(x_ref, mod_ref, g_ref, w_ref, u_ref, q_ref, k_ref, v_ref, *, chunked):
    x = x_ref[...]
    g, r, d = x.shape
    hn = _rms(x, g_ref[...]) * (1.0 + mod_ref[:, 1]) + mod_ref[:, 0]
    hn = hn.reshape(g * r, d).astype(BF16)
    proj = jnp.dot(hn, w_ref[...], preferred_element_type=F32)
    ssm_w = u_ref.shape[-1]
    attn_w = q_ref.shape[1]
    kv_w = k_ref.shape[1]
    if chunked:
        t, nc, _ = u_ref.shape
        u_ref[...] = jnp.swapaxes(proj[:, :ssm_w].reshape(nc, t, ssm_w), 0, 1)
    else:
        u_ref[...] = proj[:, :ssm_w]
    q_ref[...] = (proj[:, ssm_w:ssm_w + attn_w] * (HEAD_DIM ** -0.5)).astype(BF16)
    k_ref[...] = proj[:, ssm_w + attn_w:ssm_w + attn_w + kv_w]
    v_ref[...] = proj[:, ssm_w + attn_w + kv_w:]


def _inproj(x3, mod4, g_pre, w_in_bf, gb, rb, chunk=None):
    nb, nr, d = x3.shape
    assert gb == 1 or rb == nr
    n_in = w_in_bf.shape[1]
    kv_w = N_KV_HEADS * HEAD_DIM
    attn_w = N_HEADS * HEAD_DIM
    ssm_w = n_in - attn_w - 2 * kv_w
    rows = nb * nr
    nj = nr // rb
    tm = gb * rb
    out_map = lambda i, j: (i * nj + j, 0)
    if chunk is None:
        u_shape = jax.ShapeDtypeStruct((rows, ssm_w), F32)
        u_spec = pl.BlockSpec((tm, ssm_w), out_map)
    else:
        assert gb == 1 and rb % chunk == 0
        u_shape = jax.ShapeDtypeStruct((nb, chunk, nr // chunk, ssm_w), F32)
        u_spec = pl.BlockSpec((None, chunk, rb // chunk, ssm_w), lambda i, j: (i, 0, j, 0))
    return pl.pallas_call(
        functools.partial(_inproj_body, chunked=chunk is not None),
        out_shape=(u_shape,
                   jax.ShapeDtypeStruct((rows, attn_w), BF16),
                   jax.ShapeDtypeStruct((rows, kv_w), F32),
                   jax.ShapeDtypeStruct((rows, kv_w), F32)),
        grid=(nb // gb, nj),
        in_specs=[pl.BlockSpec((gb, rb, d), lambda i, j: (i, j, 0)),
                  pl.BlockSpec((gb, 2, 1, d), lambda i, j: (i, 0, 0, 0)),
                  _resident((1, 1, d)),
                  _resident((d, n_in))],
        out_specs=(u_spec,
                   pl.BlockSpec((tm, attn_w), out_map),
                   pl.BlockSpec((tm, kv_w), out_map),
                   pl.BlockSpec((tm, kv_w), out_map)),
        compiler_params=_cparams("arbitrary", "arbitrary"),
        name="in_projection",
    )(x3, mod4, g_pre.reshape(1, 1, d), w_in_bf)


def _sink_softmax(s, sk):
    m = jnp.maximum(jnp.max(s, axis=-1, keepdims=True), sk)
    p = jnp.exp(s - m)
    return p / (jnp.sum(p, axis=-1, keepdims=True) + jnp.exp(sk - m))


def _attn_prompt_body(sinks_ref, q_ref, kc_ref, vc_ref, kp_ref, vp_ref, bias_ref, g_ref, o_ref, s_scr, p_scr):
    tq = q_ref.shape[0]
    blk = bias_ref.shape[2] // 2
    first_tile = pl.program_id(1) == 0
    kk = jnp.concatenate([kp_ref[...], kc_ref[...]], axis=0).astype(BF16)
    vv = jnp.concatenate([vp_ref[...], vc_ref[...]], axis=0).astype(BF16)
    shape = (GQA * blk, 2 * blk)
    col = lax.broadcasted_iota(jnp.int32, shape, 1)
    row = lax.broadcasted_iota(jnp.int32, (GQA * blk, 1), 0)
    no_prev = first_tile & (col < blk)
    sink_cols = []
    for hk in range(N_KV_HEADS):
        sk = jnp.full((GQA * blk, 1), sinks_ref[hk * GQA], F32)
        for gq in range(1, GQA):
            sk = jnp.where(row >= gq * blk, sinks_ref[hk * GQA + gq], sk)
        sink_cols.append(sk)
    pairs = [(j, hk) for j in range(tq // blk) for hk in range(N_KV_HEADS)]
    for idx, (j, hk) in enumerate(pairs):
        qg = jnp.concatenate([q_ref[j * blk:(j + 1) * blk, h * HEAD_DIM:(h + 1) * HEAD_DIM]
                              for h in range(hk * GQA, (hk + 1) * GQA)], axis=0)
        s = lax.dot_general(qg, kk[j * blk:(j + 2) * blk, hk * HEAD_DIM:(hk + 1) * HEAD_DIM],
                            (((1,), (1,)), ((), ())), preferred_element_type=F32) + bias_ref[hk]
        s_scr[idx] = jnp.where(no_prev, NEG_INF, s) if j == 0 else s
    dens = []
    for idx, (j, hk) in enumerate(pairs):
        s = s_scr[idx]
        m = jnp.maximum(jnp.max(s, axis=-1, keepdims=True), sink_cols[hk])
        p = jnp.exp(s - m)
        dens.append(jnp.sum(p, axis=-1, keepdims=True) + jnp.exp(sink_cols[hk] - m))
        p_scr[idx] = p.astype(BF16)
    outs = {}
    for idx, (j, hk) in enumerate(pairs):
        o = jnp.dot(p_scr[idx], vv[j * blk:(j + 2) * blk, hk * HEAD_DIM:(hk + 1) * HEAD_DIM],
                    preferred_element_type=F32) / dens[idx]
        for gq in range(GQA):
            outs[j, hk * GQA + gq] = o[gq * blk:(gq + 1) * blk]
    for j in range(tq // blk):
        att = jnp.concatenate([outs[j, h] for h in range(N_HEADS)], axis=1)
        o_ref[j * blk:(j + 1) * blk, :] = _rms(att, g_ref[...]).astype(BF16)


def _attn_prompt(q, k, v, sinks, bias, band, g_attn, nb, seq):
    rows, attn_w = q.shape
    kv_w = k.shape[1]
    blk = band.shape[0]
    bias = jnp.where(band > 0.5, bias, NEG_INF).reshape(N_KV_HEADS, GQA * blk, 2 * blk)
    tq = ATTN_TILE
    nt = seq // tq
    per = tq // blk
    cur = lambda b, i: (b * nt + i, 0)
    prev = lambda b, i: (b * nt * per + jnp.maximum(i * per - 1, 0), 0)
    return pl.pallas_call(
        _attn_prompt_body,
        out_shape=jax.ShapeDtypeStruct((rows, attn_w), BF16),
        grid=(nb, nt),
        in_specs=[pl.BlockSpec(memory_space=pltpu.SMEM),
                  pl.BlockSpec((tq, attn_w), cur),
                  pl.BlockSpec((tq, kv_w), cur),
                  pl.BlockSpec((tq, kv_w), cur),
                  pl.BlockSpec((blk, kv_w), prev),
                  pl.BlockSpec((blk, kv_w), prev),
                  _resident(bias.shape),
                  _resident((1, attn_w))],
        out_specs=pl.BlockSpec((tq, attn_w), cur),
        scratch_shapes=[pltpu.VMEM((per * N_KV_HEADS, GQA * blk, 2 * blk), F32),
                        pltpu.VMEM((per * N_KV_HEADS, GQA * blk, 2 * blk), BF16)],
        compiler_params=_cparams("arbitrary", "arbitrary"),
        name="attention_prompt",
    )(sinks, q, k, v, k, v, bias, g_attn.reshape(1, attn_w))


def _attn_sample_body(sinks_ref, q_ref, kn_ref, vn_ref, ck_ref, cv_ref, bias_ref, band_ref, g_ref,
                      o_ref, ko_ref, vo_ref):
    t = q_ref.shape[1]
    kk = jnp.concatenate([ck_ref[...], kn_ref[...]], axis=1)
    vv = jnp.concatenate([cv_ref[...], vn_ref[...]], axis=1)
    ko_ref[...] = kk[:, t:, :]
    vo_ref[...] = vv[:, t:, :]
    kkb = kk.astype(BF16)
    vvb = vv.astype(BF16)
    valid = band_ref[...] > 0.5
    outs = []
    for h in range(N_HEADS):
        hk = h // GQA
        qh = q_ref[:, :, h * HEAD_DIM:(h + 1) * HEAD_DIM]
        s = jnp.einsum('bqd,bkd->bqk', qh, kkb[:, :, hk * HEAD_DIM:(hk + 1) * HEAD_DIM],
                       preferred_element_type=F32)
        s = jnp.where(valid, s + bias_ref[h], NEG_INF)
        p = _sink_softmax(s, sinks_ref[h]).astype(BF16)
        outs.append(jnp.einsum('bqk,bkd->bqd', p, vvb[:, :, hk * HEAD_DIM:(hk + 1) * HEAD_DIM],
                               preferred_element_type=F32))
    att = jnp.concatenate(outs, axis=2)
    o_ref[...] = _rms(att, g_ref[...]).astype(BF16)


def _attn_sample(q3, k3, v3, ck, cv, sinks, bias, band, g_attn, gb):
    nb, t, attn_w = q3.shape
    w, kv_w = ck.shape[1], ck.shape[2]
    blk3 = lambda last: pl.BlockSpec((gb, last[0], last[1]), lambda i: (i, 0, 0))
    return pl.pallas_call(
        _attn_sample_body,
        out_shape=(jax.ShapeDtypeStruct((nb, t, attn_w), BF16),
                   jax.ShapeDtypeStruct((nb, w, kv_w), F32),
                   jax.ShapeDtypeStruct((nb, w, kv_w), F32)),
        grid=(nb // gb,),
        in_specs=[pl.BlockSpec(memory_space=pltpu.SMEM),
                  blk3((t, attn_w)), blk3((t, kv_w)), blk3((t, kv_w)),
                  blk3((w, kv_w)), blk3((w, kv_w)),
                  _resident(bias.shape), _resident(band.shape), _resident((1, 1, attn_w))],
        out_specs=(blk3((t, attn_w)), blk3((w, kv_w)), blk3((w, kv_w))),
        compiler_params=_cparams("arbitrary"),
        name="attention_sample",
    )(sinks, q3, k3, v3, ck, cv, bias, band, g_attn.reshape(1, 1, attn_w))


def _ssm_tail(y, u, d_ref, wglu_ref, bglu_ref, g_ref):
    z = jax.nn.gelu(y + d_ref[...] * u)
    gate = jax.nn.sigmoid(jnp.dot(z.astype(BF16), wglu_ref[...], preferred_element_type=F32) + bglu_ref[...])
    return _rms(z * gate, g_ref[...]).astype(BF16)


def _ssm_chunk_body(u_ref, m_ref, e_ref, f_ref, w1_ref, w2_ref, yt_ref, hend_ref, ut_scr, s_scr):
    t, nc, ssm_w = u_ref.shape
    gb = m_ref.shape[0]
    n_levels = w1_ref.shape[1]
    two_p = e_ref.shape[1]
    ch = ssm_w // ut_scr.shape[1]

    @pl.when(pl.program_id(1) == 0)
    def _():
        for s in range(t):
            ut_scr[s] = u_ref[s].T.astype(BF16).reshape(ut_scr.shape[1], ch, nc)

    g0 = pl.program_id(1) * gb
    row = lax.broadcasted_iota(jnp.int32, (nc, two_p), 0)
    for gl in range(gb):
        r = ut_scr[:, g0 + gl].reshape(t * ch, nc)
        y = jnp.dot(m_ref[gl], r, preferred_element_type=F32)
        s_scr[...] = jnp.dot(e_ref[gl], r, preferred_element_type=F32)
        h = s_scr[...].T
        for lv in range(n_levels):
            sh = 1 << lv
            prev = jnp.where(row >= sh, pltpu.roll(h, sh, axis=0), 0.0)
            h = h + w1_ref[gl, lv] * prev + w2_ref[gl, lv] * pltpu.roll(prev, two_p // 2, axis=1)
        hend_ref[gl] = h[nc - 1:nc, :]
        s_scr[...] = jnp.where(row >= 1, pltpu.roll(h, 1, axis=0), 0.0).T
        y = y + jnp.dot(f_ref[gl], s_scr[...].astype(BF16), preferred_element_type=F32)
        yt_ref[:, gl * ch:(gl + 1) * ch, :] = y.reshape(t, ch, nc)


def _ssm_chunked(u2, kw):
    nb, t, nc, ssm_w = u2.shape
    ng, tc, _ = kw["m"].shape
    two_p = kw["e"].shape[1]
    n_levels = kw["w1"].shape[1]
    ch = tc // t
    gb = SSM_GROUP_BLOCK
    grp = lambda shape: pl.BlockSpec((gb,) + shape, lambda b, g: (g,) + (0,) * len(shape))
    return pl.pallas_call(
        _ssm_chunk_body,
        out_shape=(jax.ShapeDtypeStruct((nb, t, ssm_w, nc), F32),
                   jax.ShapeDtypeStruct((nb, ng, 1, two_p), F32)),
        grid=(nb, ng // gb),
        in_specs=[pl.BlockSpec((None, t, nc, ssm_w), lambda b, g: (b, 0, 0, 0), pipeline_mode=pl.Buffered(1)),
                  grp((tc, tc)), grp((two_p, tc)), grp((tc, two_p)),
                  grp((n_levels, 1, two_p)), grp((n_levels, 1, two_p))],
        out_specs=(pl.BlockSpec((None, t, gb * ch, nc), lambda b, g: (b, 0, g, 0)),
                   pl.BlockSpec((None, gb, 1, two_p), lambda b, g: (b, g, 0, 0))),
        scratch_shapes=[pltpu.VMEM((t, ng, ch, nc), BF16), pltpu.VMEM((two_p, nc), F32)],
        compiler_params=_cparams("arbitrary", "arbitrary"),
        name="ssm_chunked",
    )(u2, kw["m"], kw["e"], kw["f"], kw["w1"], kw["w2"])


def _ssm_tail_body(yt_ref, u_ref, d_ref, wglu_ref, bglu_ref, g_ref, o_ref):
    for i in range(yt_ref.shape[0]):
        o_ref[i] = _ssm_tail(yt_ref[i].T, u_ref[i], d_ref, wglu_ref, bglu_ref, g_ref)


def _ssm_tail_call(yt, u2, sw):
    nb, t, nc, ssm_w = u2.shape
    tb = SSM_TAIL_BLOCK
    return pl.pallas_call(
        _ssm_tail_body,
        out_shape=jax.ShapeDtypeStruct((nb, t, nc, ssm_w), BF16),
        grid=(nb, t // tb),
        in_specs=[pl.BlockSpec((None, tb, ssm_w, nc), lambda b, i: (b, i, 0, 0)),
                  pl.BlockSpec((None, tb, nc, ssm_w), lambda b, i: (b, i, 0, 0)),
                  _resident((1, ssm_w)), _resident((ssm_w, ssm_w)), _resident((1, ssm_w)), _resident((1, ssm_w))],
        out_specs=pl.BlockSpec((None, tb, nc, ssm_w), lambda b, i: (b, i, 0, 0)),
        compiler_params=_cparams("arbitrary", "arbitrary"),
        name="ssm_tail",
    )(yt, u2, sw["d"], sw["wglu"], sw["bglu"], sw["g"])


def _lag_matrix_body(z_ref, m_ref, *, t):
    gb, ch, _ = z_ref.shape
    width = t * ch
    for gl in range(gb):
        z = z_ref[gl]
        for r in range(t):
            off = (t - 1 - r) * ch
            m_ref[gl, r * ch:(r + 1) * ch, :] = z[:, off:off + width].astype(BF16)


def _lag_matrix(z, t):
    ng, ch, zw = z.shape
    gb = SSM_GROUP_BLOCK
    return pl.pallas_call(
        functools.partial(_lag_matrix_body, t=t),
        out_shape=jax.ShapeDtypeStruct((ng, t * ch, zw // 2), BF16),
        grid=(ng // gb,),
        in_specs=[pl.BlockSpec((gb, ch, zw), lambda g: (g, 0, 0))],
        out_specs=pl.BlockSpec((gb, t * ch, zw // 2), lambda g: (g, 0, 0)),
        compiler_params=_cparams("arbitrary"),
        name="ssm_lag_matrix",
    )(z)


def _ssm_chunk_weights(lam_re, lam_im, log_dt, b_re, b_im, c_re, c_im, t, n_chunks):
    ng, ns = lam_re.shape
    nc = b_re.shape[2]
    dt = jnp.exp(log_dt)[:, None]
    are, aim = lam_re * dt, lam_im * dt
    d = jnp.arange(t + 1, dtype=F32)[:, None, None]
    mag = jnp.exp(d * are)
    pre, pim = mag * jnp.cos(d * aim), mag * jnp.sin(d * aim)
    lre, lim = pre[1], pim[1]
    den = lam_re * lam_re + lam_im * lam_im
    fre = ((lre - 1.0) * lam_re + lim * lam_im) / den
    fim = (lim * lam_re - (lre - 1.0) * lam_im) / den
    bbr = fre[..., None] * b_re - fim[..., None] * b_im
    bbi = fre[..., None] * b_im + fim[..., None] * b_re
    pre_g, pim_g = jnp.transpose(pre, (1, 0, 2)), jnp.transpose(pim, (1, 0, 2))
    xr = c_re[:, None] * pre_g[:, :, None, :] - c_im[:, None] * pim_g[:, :, None, :]
    xi = c_re[:, None] * pim_g[:, :, None, :] + c_im[:, None] * pre_g[:, :, None, :]
    kern = (jnp.einsum('gdap,gpc->gdac', xr[:, :t], bbr, precision=HIGHEST)
            - jnp.einsum('gdap,gpc->gdac', xi[:, :t], bbi, precision=HIGHEST))
    lag_rows = jnp.transpose(kern[:, ::-1], (0, 2, 1, 3)).reshape(ng, nc, t * nc)
    m = _lag_matrix(jnp.concatenate([lag_rows, jnp.zeros_like(lag_rows)], axis=2), t)
    rev_re = jnp.transpose(pre_g[:, t - 1::-1][:, :t], (0, 2, 1))
    rev_im = jnp.transpose(pim_g[:, t - 1::-1][:, :t], (0, 2, 1))
    er = rev_re[..., None] * bbr[:, :, None, :] - rev_im[..., None] * bbi[:, :, None, :]
    ei = rev_re[..., None] * bbi[:, :, None, :] + rev_im[..., None] * bbr[:, :, None, :]
    e = jnp.concatenate([er.reshape(ng, ns, t * nc), ei.reshape(ng, ns, t * nc)], axis=1).astype(BF16)
    f = jnp.concatenate([xr[:, 1:].reshape(ng, t * nc, ns), -xi[:, 1:].reshape(ng, t * nc, ns)], axis=2).astype(BF16)
    wr, wi = pre[t], pim[t]
    w1, w2 = [], []
    for _ in range(max(1, (n_chunks - 1).bit_length())):
        w1.append(jnp.concatenate([wr, wr], axis=1))
        w2.append(jnp.concatenate([-wi, wi], axis=1))
        wr, wi = wr * wr - wi * wi, 2.0 * wr * wi
    w1 = jnp.stack(w1, axis=1)[:, :, None, :]
    w2 = jnp.stack(w2, axis=1)[:, :, None, :]
    return {"m": m, "e": e, "f": f, "w1": w1, "w2": w2}


def _ssm_sample_body(u_ref, h0re_ref, h0im_ref, bre_ref, bim_ref, cre_ref, cim_ref, lre_ref, lim_ref,
                     d_ref, wglu_ref, bglu_ref, g_ref, o_ref, hre_ref, him_ref):
    steps = u_ref.shape[0]
    half_in = bre_ref.shape[1]
    dot32 = functools.partial(jnp.dot, precision=HIGHEST, preferred_element_type=F32)
    ar = lre_ref[...]
    ai = lim_ref[...]
    hr = h0re_ref[...]
    hi = h0im_ref[...]
    for t in range(steps):
        u = u_ref[t]
        xr = jnp.concatenate([dot32(u[:, hf * half_in:(hf + 1) * half_in], bre_ref[hf]) for hf in range(2)], axis=1)
        xi = jnp.concatenate([dot32(u[:, hf * half_in:(hf + 1) * half_in], bim_ref[hf]) for hf in range(2)], axis=1)
        hr, hi = ar * hr - ai * hi + xr, ar * hi + ai * hr + xi
        half_st = hr.shape[1] // 2
        y = jnp.concatenate(
            [dot32(hr[:, hf * half_st:(hf + 1) * half_st], cre_ref[hf])
             + dot32(hi[:, hf * half_st:(hf + 1) * half_st], cim_ref[hf]) for hf in range(2)], axis=1)
        o_ref[t] = _ssm_tail(y, u, d_ref, wglu_ref, bglu_ref, g_ref)
    hre_ref[...] = hr
    him_ref[...] = hi


def _ssm_sample(u_tm, h0re, h0im, sw):
    steps, nb, ssm_w = u_tm.shape
    n_state = h0re.shape[1]
    return pl.pallas_call(
        _ssm_sample_body,
        out_shape=(jax.ShapeDtypeStruct((steps, nb, ssm_w), BF16),
                   jax.ShapeDtypeStruct((nb, n_state), F32),
                   jax.ShapeDtypeStruct((nb, n_state), F32)),
        compiler_params=pltpu.CompilerParams(vmem_limit_bytes=VMEM_LIMIT_BYTES),
        name="ssm_sample",
    )(u_tm, h0re, h0im, sw["bre"], sw["bim"], sw["cre"], sw["cim"], sw["lre"], sw["lim"],
      sw["d"], sw["wglu"], sw["bglu"], sw["g"])


def _ssm_weights(lam_re, lam_im, log_dt, b_re, b_im, c_re, c_im, d_skip, w_glu, b_glu, g_ssm):
    ng, ns = lam_re.shape
    nc = b_re.shape[2]
    dt = jnp.exp(log_dt)[:, None]
    mag = jnp.exp(lam_re * dt)
    lre = mag * jnp.cos(lam_im * dt)
    lim = mag * jnp.sin(lam_im * dt)
    den = lam_re * lam_re + lam_im * lam_im
    fre = ((lre - 1.0) * lam_re + lim * lam_im) / den
    fim = (lim * lam_re - (lre - 1.0) * lam_im) / den
    bbar_re = fre[..., None] * b_re - fim[..., None] * b_im
    bbar_im = fre[..., None] * b_im + fim[..., None] * b_re
    eye = jnp.eye(ng // 2, dtype=F32)

    def in_blocks(b):
        b2 = b.reshape(2, ng // 2, ns, nc)
        return jnp.einsum('hgpc,gk->hgckp', b2, eye).reshape(2, ng // 2 * nc, ng // 2 * ns)

    def out_blocks(c):
        c2 = c.reshape(2, ng // 2, nc, ns)
        return jnp.einsum('hgcp,gk->hgpkc', c2, eye).reshape(2, ng // 2 * ns, ng // 2 * nc)

    ssm_w = ng * nc
    return {
        "bre": in_blocks(bbar_re), "bim": in_blocks(bbar_im),
        "cre": out_blocks(c_re), "cim": out_blocks(-c_im),
        "lre": lre.reshape(1, ng * ns), "lim": lim.reshape(1, ng * ns),
        "d": d_skip.reshape(1, ssm_w), "wglu": w_glu.astype(BF16), "bglu": b_glu.reshape(1, ssm_w),
        "g": g_ssm.reshape(1, ssm_w),
    }


def _route(logits):
    lane = lax.broadcasted_iota(jnp.int32, logits.shape, 1)
    big = jnp.int32(10 ** 6)
    is_group = lane < N_EXPERT_GROUPS
    gl = jnp.where(is_group, logits, NEG_INF)
    gmax = jnp.max(gl, axis=-1, keepdims=True)
    gidx = jnp.min(jnp.where(is_group & (gl == gmax), lane, big), axis=-1, keepdims=True)
    g_p = 1.0 / jnp.sum(jnp.exp(gl - gmax), axis=-1, keepdims=True)
    lo = N_EXPERT_GROUPS + gidx * EXPERTS_PER_GROUP
    in_group = (lane >= lo) & (lane < lo + EXPERTS_PER_GROUP)
    el = jnp.where(in_group, logits, NEG_INF)
    ee = jnp.exp(el - jnp.max(el, axis=-1, keepdims=True))
    prob = ee / jnp.sum(ee, axis=-1, keepdims=True)
    p1 = jnp.max(jnp.where(in_group, prob, -1.0), axis=-1, keepdims=True)
    i1 = jnp.min(jnp.where(in_group & (prob == p1), lane, big), axis=-1, keepdims=True)
    rest = in_group & (lane != i1)
    p2 = jnp.max(jnp.where(rest, prob, -1.0), axis=-1, keepdims=True)
    i2 = jnp.min(jnp.where(rest & (prob == p2), lane, big), axis=-1, keepdims=True)
    tot = p1 + p2
    return jnp.where(lane == i1, g_p * p1 / tot, jnp.where(lane == i2, g_p * p2 / tot, 0.0)), gidx


def _split3(x):
    a = x.astype(BF16)
    r = x - a.astype(F32)
    b = r.astype(BF16)
    return a, b, (r - b.astype(F32)).astype(BF16)


def _moe_sorted(hn, gates, gidx, tri_ref, wg_ref, wu_ref, wd_ref, xs_scr, gs_scr, ys_scr):
    rows, d = hn.shape
    lane = lax.broadcasted_iota(jnp.int32, gates.shape, 1)
    member = lane == gidx
    csum = jnp.dot(tri_ref[...], member.astype(BF16), preferred_element_type=F32)
    rank = jnp.sum(jnp.where(member, csum, 0.0), axis=-1, keepdims=True)
    counts = [jnp.sum((gidx == grp).astype(jnp.int32)) for grp in range(N_EXPERT_GROUPS - 1)]
    offs = [jnp.int32(0)]
    for c in counts:
        offs.append(offs[-1] + c)
    offs.append(jnp.int32(rows))
    base = jnp.zeros_like(gidx)
    for grp in range(1, N_EXPERT_GROUPS):
        base = jnp.where(gidx == grp, offs[grp], base)
    pos = base.astype(F32) + rank - 1.0
    col = lax.broadcasted_iota(jnp.int32, (rows, rows), 1).astype(F32)
    row = lax.broadcasted_iota(jnp.int32, (rows, rows), 0).astype(F32)
    unsort = (col == pos).astype(BF16)
    pos_lanes = jnp.broadcast_to(pos, (rows, LANES)).T[0:1, :]
    sort = (row == pos_lanes).astype(BF16)
    xs_scr[...] = jnp.dot(sort, hn.astype(BF16), preferred_element_type=F32).astype(BF16)
    gs_scr[...] = sum(jnp.dot(sort, part, preferred_element_type=F32) for part in _split3(gates))
    ys_scr[...] = jnp.zeros_like(ys_scr)
    blk = MOE_ROW_BLOCK
    n_blk = rows // blk
    cuts = [jnp.int32(k * blk) for k in range(1, n_blk)] + offs[1:N_EXPERT_GROUPS]
    assert len(cuts) == 6
    for a, b in ((0, 5), (1, 3), (2, 4), (1, 2), (3, 4), (0, 3), (2, 5), (0, 1), (2, 3), (4, 5), (1, 2), (3, 4)):
        cuts[a], cuts[b] = jnp.minimum(cuts[a], cuts[b]), jnp.maximum(cuts[a], cuts[b])
    cuts = [jnp.int32(0)] + cuts + [jnp.int32(rows)]
    lane_b = lax.broadcasted_iota(jnp.int32, (blk, LANES), 1)
    for j in range(len(cuts) - 1):
        start = cuts[j]
        live = (cuts[j + 1] > start).astype(F32)
        k = jnp.minimum(start // blk, n_blk - 1)
        grp = sum((offs[q] <= start).astype(jnp.int32) for q in range(1, N_EXPERT_GROUPS))
        sl = pl.ds(pl.multiple_of(k * blk, blk), blk)
        xb = xs_scr[sl, :]
        gb = gs_scr[sl, :] * live
        acts = []
        for i in range(EXPERTS_PER_GROUP):
            e = grp * EXPERTS_PER_GROUP + i
            gate = jnp.sum(jnp.where(lane_b == N_EXPERT_GROUPS + e, gb, 0.0), axis=-1, keepdims=True)
            hg = jnp.dot(xb, wg_ref[e], preferred_element_type=F32)
            hu = jnp.dot(xb, wu_ref[e], preferred_element_type=F32)
            acts.append((jax.nn.silu(hg) * hu * gate).astype(BF16))
        ys_scr[sl, :] += jnp.dot(jnp.concatenate(acts, axis=1), wd_ref[grp], preferred_element_type=F32)
    hi = ys_scr[...].astype(BF16)
    lo = (ys_scr[...] - hi.astype(F32)).astype(BF16)
    return jnp.dot(unsort, hi, preferred_element_type=F32) + jnp.dot(unsort, lo, preferred_element_type=F32)


def _ffn_body(x_ref, ssm_ref, attn_ref, mod_ref, gpost_ref, gpre_ref, gffn_ref, wout_ref, wr_ref, br_ref,
              tri_ref, wg_ref, wu_ref, wd_ref, o_ref, xs_scr, gs_scr, ys_scr, *, chunked):
    x = x_ref[...]
    g, r, d = x.shape
    rows = g * r
    gt1, sh2, sc2, gt2 = mod_ref[:, 2], mod_ref[:, 3], mod_ref[:, 4], mod_ref[:, 5]
    if chunked:
        ssm = jnp.swapaxes(ssm_ref[...].astype(F32), 0, 1).reshape(rows, ssm_ref.shape[-1]).astype(BF16)
    else:
        ssm = ssm_ref[...]
    cat = jnp.concatenate([ssm, attn_ref[...]], axis=1)
    mixed = jnp.dot(cat, wout_ref[...], preferred_element_type=F32)
    x1 = x + gt1 * _rms(mixed, gpost_ref[0]).reshape(g, r, d)
    hn = (_rms(x1, gpre_ref[...]) * (1.0 + sc2) + sh2).reshape(rows, d)
    hb = hn.astype(BF16)
    hlo = (hn - hb.astype(F32)).astype(BF16)
    logits = (jnp.dot(hb, wr_ref[0], preferred_element_type=F32) + jnp.dot(hlo, wr_ref[0], preferred_element_type=F32)
              + jnp.dot(hb, wr_ref[1], preferred_element_type=F32) + br_ref[...])
    gates, gidx = _route(logits)
    ffn = _moe_sorted(hn, gates, gidx, tri_ref, wg_ref, wu_ref, wd_ref, xs_scr, gs_scr, ys_scr)
    o_ref[...] = x1 + gt2 * _rms(ffn, gffn_ref[0]).reshape(g, r, d)


def _ffn(x3, ssm_n, attn_n, mod4, fw, gb, rb, chunk=None):
    nb, nr, d = x3.shape
    assert gb == 1 or rb == nr
    half = ssm_n.shape[-1]
    nj = nr // rb
    tm = gb * rb
    row_map = lambda i, j: (i * nj + j, 0)
    vec = _resident((1, 1, d))
    if chunk is None:
        ssm_spec = pl.BlockSpec((tm, half), row_map)
    else:
        assert gb == 1 and rb % chunk == 0
        ssm_spec = pl.BlockSpec((None, chunk, rb // chunk, half), lambda i, j: (i, 0, j, 0))
    return pl.pallas_call(
        functools.partial(_ffn_body, chunked=chunk is not None),
        out_shape=jax.ShapeDtypeStruct((nb, nr, d), F32),
        grid=(nb // gb, nj),
        in_specs=[pl.BlockSpec((gb, rb, d), lambda i, j: (i, j, 0)),
                  ssm_spec,
                  pl.BlockSpec((tm, attn_n.shape[1]), row_map),
                  pl.BlockSpec((gb, N_MOD, 1, d), lambda i, j: (i, 0, 0, 0)),
                  vec, vec, vec,
                  _resident(fw["wout"].shape), _resident(fw["wr"].shape), _resident(fw["br"].shape),
                  _resident((tm, tm)),
                  _resident(fw["wg"].shape), _resident(fw["wu"].shape), _resident(fw["wd"].shape)],
        out_specs=pl.BlockSpec((gb, rb, d), lambda i, j: (i, j, 0)),
        scratch_shapes=[pltpu.VMEM((tm, d), BF16), pltpu.VMEM((tm, LANES), F32), pltpu.VMEM((tm, d), F32)],
        compiler_params=_cparams("arbitrary", "arbitrary"),
        name="outproj_moe",
    )(x3, ssm_n, attn_n, mod4, fw["gpost"], fw["gpre"], fw["gffn"], fw["wout"], fw["wr"], fw["br"],
      jnp.tril(jnp.ones((tm, tm), BF16)), fw["wg"], fw["wu"], fw["wd"])


def _ffn_weights(g_post_mix, g_pre_ffn, g_post_ffn, w_out, w_gr, b_gr, w_er, b_er, w_gate, w_up, w_down):
    d = w_out.shape[0]
    n_route = N_EXPERT_GROUPS + N_EXPERTS
    wr = jnp.concatenate([w_gr, w_er.reshape(d, N_EXPERTS)], axis=1)
    br = jnp.concatenate([b_gr, b_er.reshape(N_EXPERTS)])
    wr = jnp.pad(wr, ((0, 0), (0, LANES - n_route)))
    wr_hi = wr.astype(BF16)
    return {
        "gpost": g_post_mix.reshape(1, 1, d), "gpre": g_pre_ffn.reshape(1, 1, d), "gffn": g_post_ffn.reshape(1, 1, d),
        "wout": w_out.astype(BF16),
        "wr": jnp.stack([wr_hi, (wr - wr_hi.astype(F32)).astype(BF16)]),
        "br": jnp.pad(br, (0, LANES - n_route)).reshape(1, LANES),
        "wg": w_gate.astype(BF16), "wu": w_up.astype(BF16),
        "wd": w_down.astype(BF16).reshape(N_EXPERT_GROUPS, -1, d),
    }


def _rel_bucket(dist):
    max_exact = NUM_BUCKETS // 2
    dd = jnp.maximum(dist, 0)
    log_ratio = jnp.log(jnp.maximum(dd, 1).astype(F32) / max_exact) / math.log(MAX_DISTANCE / max_exact)
    large = jnp.minimum(max_exact + (log_ratio * (NUM_BUCKETS - max_exact)).astype(jnp.int32), NUM_BUCKETS - 1)
    return jnp.where(dd < max_exact, dd, large)


def _bias_and_band(dist, table):
    onehot = (_rel_bucket(dist)[:, :, None] == jnp.arange(NUM_BUCKETS)[None, None, :]).astype(F32)
    bias = jnp.einsum('qkb,bh->hqk', onehot, table.astype(F32), precision=HIGHEST)
    band = ((dist >= 0) & (dist <= WINDOW)).astype(F32)
    return bias, band


def _layer(xp, xs, ck, cv, h0re, h0im, mod_p, mod_s, table, p):
    nb, seq, d = xp.shape
    nbs, t, _ = xs.shape
    w = ck.shape[1]
    kv_w = N_KV_HEADS * HEAD_DIM
    w_in_bf = p["w_in"].astype(BF16)
    sw = _ssm_weights(p["ssm_lam_re"], p["ssm_lam_im"], p["ssm_log_dt"], p["ssm_b_re"], p["ssm_b_im"],
                      p["ssm_c_re"], p["ssm_c_im"], p["ssm_d"], p["w_glu"], p["b_glu"], p["g_ssm_out"])
    fw = _ffn_weights(p["g_post_mix"], p["g_pre_ffn"], p["g_post_ffn"], p["w_out"],
                      p["w_group_router"], p["b_group_router"], p["w_expert_router"], p["b_expert_router"],
                      p["w_exp_gate"], p["w_exp_up"], p["w_exp_down"])
    n_state = sw["lre"].shape[1]
    ng = p["ssm_lam_re"].shape[0]

    u2, q, k, v = _inproj(xp, mod_p, p["g_pre_mix"], w_in_bf, 1, TOKEN_TILE, chunk=SSM_CHUNK)
    blk = WINDOW
    dist_p = jnp.arange(blk)[:, None] + blk - jnp.arange(2 * blk)[None, :]
    bias_p, band_p = _bias_and_band(dist_p, table)
    attn_p = _attn_prompt(q, k, v, p["attn_sinks"], bias_p, band_p, p["g_attn_out"], nb, seq)
    kw = _ssm_chunk_weights(p["ssm_lam_re"], p["ssm_lam_im"], p["ssm_log_dt"], p["ssm_b_re"], p["ssm_b_im"],
                            p["ssm_c_re"], p["ssm_c_im"], SSM_CHUNK, seq // SSM_CHUNK)
    yt, hend = _ssm_chunked(u2, kw)
    ssm_p = _ssm_tail_call(yt, u2, sw)
    yp = _ffn(xp, ssm_p, attn_p, mod_p, fw, 1, TOKEN_TILE, chunk=SSM_CHUNK)
    wp = min(WINDOW, seq)
    k_p = k.reshape(nb, seq, kv_w)[:, seq - wp:].reshape(nb, wp, N_KV_HEADS, HEAD_DIM)
    v_p = v.reshape(nb, seq, kv_w)[:, seq - wp:].reshape(nb, wp, N_KV_HEADS, HEAD_DIM)
    n_p = n_state // ng
    hre_p = hend[:, :, 0, :n_p]
    him_p = hend[:, :, 0, n_p:]

    gs = TOKEN_TILE // t
    us, qs, ks, vs = _inproj(xs, mod_s, p["g_pre_mix"], w_in_bf, gs, t)
    dist_s = jnp.arange(t)[:, None] + w - jnp.arange(w + t)[None, :]
    bias_s, band_s = _bias_and_band(dist_s, table)
    attn_s, k_s, v_s = _attn_sample(qs.reshape(nbs, t, -1), ks.reshape(nbs, t, kv_w), vs.reshape(nbs, t, kv_w),
                                    ck.reshape(nbs, w, kv_w), cv.reshape(nbs, w, kv_w),
                                    p["attn_sinks"], bias_s, band_s, p["g_attn_out"], 16)
    u_tm = jnp.swapaxes(us.reshape(nbs, t, -1), 0, 1)
    ssm_tm, hre_s, him_s = _ssm_sample(u_tm, h0re.reshape(nbs, n_state), h0im.reshape(nbs, n_state), sw)
    ssm_s = jnp.swapaxes(ssm_tm, 0, 1).reshape(nbs * t, -1)
    ys = _ffn(xs, ssm_s, attn_s.reshape(nbs * t, -1), mod_s, fw, gs, t)
    k_s = k_s.reshape(nbs, w, N_KV_HEADS, HEAD_DIM)
    v_s = v_s.reshape(nbs, w, N_KV_HEADS, HEAD_DIM)
    hre_s = hre_s.reshape(nbs, ng, n_state // ng)
    him_s = him_s.reshape(nbs, ng, n_state // ng)
    return yp, ys, k_p, v_p, hre_p, him_p, k_s, v_s, hre_s, him_s


def kernel(x_prompt, x_sample, cache_k, cache_v, state_ssm_re, state_ssm_im, c_prompt, c_sample, rel_bias_table,
           w_ada, b_ada, g_pre_mix, g_post_mix, g_pre_ffn, g_post_ffn, w_in, ssm_lam_re, ssm_lam_im, ssm_log_dt,
           ssm_b_re, ssm_b_im, ssm_c_re, ssm_c_im, ssm_d, w_glu, b_glu, attn_sinks, g_ssm_out, g_attn_out, w_out,
           w_group_router, b_group_router, w_expert_router, b_expert_router, w_exp_gate, w_exp_up, w_exp_down):
    params = dict(
        w_ada=w_ada, b_ada=b_ada, g_pre_mix=g_pre_mix, g_post_mix=g_post_mix, g_pre_ffn=g_pre_ffn,
        g_post_ffn=g_post_ffn, w_in=w_in, ssm_lam_re=ssm_lam_re, ssm_lam_im=ssm_lam_im, ssm_log_dt=ssm_log_dt,
        ssm_b_re=ssm_b_re, ssm_b_im=ssm_b_im, ssm_c_re=ssm_c_re, ssm_c_im=ssm_c_im, ssm_d=ssm_d, w_glu=w_glu,
        b_glu=b_glu, attn_sinks=attn_sinks, g_ssm_out=g_ssm_out, g_attn_out=g_attn_out, w_out=w_out,
        w_group_router=w_group_router, b_group_router=b_group_router, w_expert_router=w_expert_router,
        b_expert_router=b_expert_router, w_exp_gate=w_exp_gate, w_exp_up=w_exp_up, w_exp_down=w_exp_down)
    depth = w_in.shape[0]
    nb, nbs = x_prompt.shape[0], x_sample.shape[0]
    d = x_prompt.shape[2]
    pad = (-nb) % 8
    yp, ys = x_prompt, x_sample
    outs = [[] for _ in range(8)]
    for li in range(depth):
        p = {name: val[li] for name, val in params.items()}
        c_all = jnp.concatenate([c_prompt, jnp.zeros((pad, d), F32), c_sample], axis=0)
        mod = _modulation(c_all, p["w_ada"], p["b_ada"])
        mod_p = mod[:nb].reshape(nb, N_MOD, 1, d)
        mod_s = mod[nb + pad:].reshape(nbs, N_MOD, 1, d)
        res = _layer(yp, ys, cache_k[li], cache_v[li], state_ssm_re[li], state_ssm_im[li], mod_p, mod_s,
                     rel_bias_table, p)
        yp, ys = res[0], res[1]
        for acc, val in zip(outs, res[2:]):
            acc.append(val)
    return (yp, ys) + tuple(jnp.stack(o) for o in outs)
```

```python
import functools
import math

import jax
import jax.numpy as jnp
from jax import lax
from jax.experimental import pallas as pl
from jax.experimental.pallas import tpu as pltpu

F32 = jnp.float32
BF16 = jnp.bfloat16
HIGHEST = lax.Precision.HIGHEST

EPS = 1e-6
NEG_INF = -1e30

SSM_GROUP = 16
SSM_STATE = 64
HEAD_DIM = 64
N_KV_HEADS = 2
GQA = 4
N_HEADS = N_KV_HEADS * GQA
WINDOW = 128
NUM_BUCKETS = 32
MAX_DISTANCE = 128
N_EXPERT_GROUPS = 4
EXPERTS_PER_GROUP = 4
N_EXPERTS = N_EXPERT_GROUPS * EXPERTS_PER_GROUP
N_MOD = 6

LANES = 128
VMEM_LIMIT_BYTES = 56 * 1024 * 1024

TOKEN_TILE = 512
ATTN_TILE = 512
SSM_CHUNK = 32
SSM_GROUP_BLOCK = 8
SSM_TAIL_BLOCK = 4
MOE_ROW_BLOCK = 128


def _cparams(*sem):
    return pltpu.CompilerParams(dimension_semantics=sem, vmem_limit_bytes=VMEM_LIMIT_BYTES)


def _resident(shape):
    zeros = (0,) * len(shape)
    return pl.BlockSpec(shape, lambda *_: zeros, pipeline_mode=pl.Buffered(1))


def _rms(x, g):
    return x * lax.rsqrt(jnp.mean(x * x, axis=-1, keepdims=True) + EPS) * g


def _mod_body(c_ref, w_ref, b_ref, o_ref):
    a = jax.nn.silu(c_ref[...])
    o_ref[...] = jnp.dot(a, w_ref[...], precision=HIGHEST, preferred_element_type=F32) + b_ref[...]


def _modulation(c, w_ada, b_ada):
    rows, d = c.shape
    n = w_ada.shape[1]
    tn = 1024
    return pl.pallas_call(
        _mod_body,
        out_shape=jax.ShapeDtypeStruct((rows, n), F32),
        grid=(n // tn,),
        in_specs=[pl.BlockSpec((rows, d), lambda j: (0, 0)),
                  pl.BlockSpec((d, tn), lambda j: (0, j)),
                  pl.BlockSpec((1, tn), lambda j: (0, j))],
        out_specs=pl.BlockSpec((rows, tn), lambda j: (0, j)),
        compiler_params=_cparams("arbitrary"),
        name="modulation",
    )(c, w_ada, b_ada.reshape(1, n))


def _inproj_body(x_ref, mod_ref, g_ref, w_ref, u_ref, q_ref, k_ref, v_ref, *, chunked):
    x = x_ref[...]
    g, r, d = x.shape
    hn = _rms(x, g_ref[...]) * (1.0 + mod_ref[:, 1]) + mod_ref[:, 0]
    hn = hn.reshape(g * r, d).astype(BF16)
    proj = jnp.dot(hn, w_ref[...], preferred_element_type=F32)
    ssm_w = u_ref.shape[-1]
    attn_w = q_ref.shape[1]
    kv_w = k_ref.shape[1]
    if chunked:
        t, nc, _ = u_ref.shape
        u_ref[...] = jnp.swapaxes(proj[:, :ssm_w].reshape(nc, t, ssm_w), 0, 1)
    else:
        u_ref[...] = proj[:, :ssm_w]
    q_ref[...] = (proj[:, ssm_w:ssm_w + attn_w] * (HEAD_DIM ** -0.5)).astype(BF16)
    k_ref[...] = proj[:, ssm_w + attn_w:ssm_w + attn_w + kv_w]
    v_ref[...] = proj[:, ssm_w + attn_w + kv_w:]


def _inproj(x3, mod4, g_pre, w_in_bf, gb, rb, chunk=None):
    nb, nr, d = x3.shape
    assert gb == 1 or rb == nr
    n_in = w_in_bf.shape[1]
    kv_w = N_KV_HEADS * HEAD_DIM
    attn_w = N_HEADS * HEAD_DIM
    ssm_w = n_in - attn_w - 2 * kv_w
    rows = nb * nr
    nj = nr // rb
    tm = gb * rb
    out_map = lambda i, j: (i * nj + j, 0)
    if chunk is None:
        u_shape = jax.ShapeDtypeStruct((rows, ssm_w), F32)
        u_spec = pl.BlockSpec((tm, ssm_w), out_map)
    else:
        assert gb == 1 and rb % chunk == 0
        u_shape = jax.ShapeDtypeStruct((nb, chunk, nr // chunk, ssm_w), F32)
        u_spec = pl.BlockSpec((None, chunk, rb // chunk, ssm_w), lambda i, j: (i, 0, j, 0))
    return pl.pallas_call(
        functools.partial(_inproj_body, chunked=chunk is not None),
        out_shape=(u_shape,
                   jax.ShapeDtypeStruct((rows, attn_w), BF16),
                   jax.ShapeDtypeStruct((rows, kv_w), F32),
                   jax.ShapeDtypeStruct((rows, kv_w), F32)),
        grid=(nb // gb, nj),
        in_specs=[pl.BlockSpec((gb, rb, d), lambda i, j: (i, j, 0)),
                  pl.BlockSpec((gb, 2, 1, d), lambda i, j: (i, 0, 0, 0)),
                  _resident((1, 1, d)),
                  _resident((d, n_in))],
        out_specs=(u_spec,
                   pl.BlockSpec((tm, attn_w), out_map),
                   pl.BlockSpec((tm, kv_w), out_map),
                   pl.BlockSpec((tm, kv_w), out_map)),
        compiler_params=_cparams("arbitrary", "arbitrary"),
        name="in_projection",
    )(x3, mod4, g_pre.reshape(1, 1, d), w_in_bf)


def _sink_softmax(s, sk):
    m = jnp.maximum(jnp.max(s, axis=-1, keepdims=True), sk)
    p = jnp.exp(s - m)
    return p / (jnp.sum(p, axis=-1, keepdims=True) + jnp.exp(sk - m))


def _attn_prompt_body(sinks_ref, q_ref, kc_ref, vc_ref, kp_ref, vp_ref, bias_ref, g_ref, o_ref, s_scr, p_scr):
    tq = q_ref.shape[0]
    blk = bias_ref.shape[2] // 2
    first_tile = pl.program_id(1) == 0
    kk = jnp.concatenate([kp_ref[...], kc_ref[...]], axis=0).astype(BF16)
    vv = jnp.concatenate([vp_ref[...], vc_ref[...]], axis=0).astype(BF16)
    shape = (GQA * blk, 2 * blk)
    col = lax.broadcasted_iota(jnp.int32, shape, 1)
    row = lax.broadcasted_iota(jnp.int32, (GQA * blk, 1), 0)
    no_prev = first_tile & (col < blk)
    sink_cols = []
    for hk in range(N_KV_HEADS):
        sk = jnp.full((GQA * blk, 1), sinks_ref[hk * GQA], F32)
        for gq in range(1, GQA):
            sk = jnp.where(row >= gq * blk, sinks_ref[hk * GQA + gq], sk)
        sink_cols.append(sk)
    pairs = [(j, hk) for j in range(tq // blk) for hk in range(N_KV_HEADS)]
    for idx, (j, hk) in enumerate(pairs):
        qg = jnp.concatenate([q_ref[j * blk:(j + 1) * blk, h * HEAD_DIM:(h + 1) * HEAD_DIM]
                              for h in range(hk * GQA, (hk + 1) * GQA)], axis=0)
        s = lax.dot_general(qg, kk[j * blk:(j + 2) * blk, hk * HEAD_DIM:(hk + 1) * HEAD_DIM],
                            (((1,), (1,)), ((), ())), preferred_element_type=F32) + bias_ref[hk]
        s_scr[idx] = jnp.where(no_prev, NEG_INF, s) if j == 0 else s
    dens = []
    for idx, (j, hk) in enumerate(pairs):
        s = s_scr[idx]
        m = jnp.maximum(jnp.max(s, axis=-1, keepdims=True), sink_cols[hk])
        p = jnp.exp(s - m)
        dens.append(jnp.sum(p, axis=-1, keepdims=True) + jnp.exp(sink_cols[hk] - m))
        p_scr[idx] = p.astype(BF16)
    outs = {}
    for idx, (j, hk) in enumerate(pairs):
        o = jnp.dot(p_scr[idx], vv[j * blk:(j + 2) * blk, hk * HEAD_DIM:(hk + 1) * HEAD_DIM],
                    preferred_element_type=F32) / dens[idx]
        for gq in range(GQA):
            outs[j, hk * GQA + gq] = o[gq * blk:(gq + 1) * blk]
    for j in range(tq // blk):
        att = jnp.concatenate([outs[j, h] for h in range(N_HEADS)], axis=1)
        o_ref[j * blk:(j + 1) * blk, :] = _rms(att, g_ref[...]).astype(BF16)


def _attn_prompt(q, k, v, sinks, bias, band, g_attn, nb, seq):
    rows, attn_w = q.shape
    kv_w = k.shape[1]
    blk = band.shape[0]
    bias = jnp.where(band > 0.5, bias, NEG_INF).reshape(N_KV_HEADS, GQA * blk, 2 * blk)
    tq = ATTN_TILE
    nt = seq // tq
    per = tq // blk
    cur = lambda b, i: (b * nt + i, 0)
    prev = lambda b, i: (b * nt * per + jnp.maximum(i * per - 1, 0), 0)
    return pl.pallas_call(
        _attn_prompt_body,
        out_shape=jax.ShapeDtypeStruct((rows, attn_w), BF16),
        grid=(nb, nt),
        in_specs=[pl.BlockSpec(memory_space=pltpu.SMEM),
                  pl.BlockSpec((tq, attn_w), cur),
                  pl.BlockSpec((tq, kv_w), cur),
                  pl.BlockSpec((tq, kv_w), cur),
                  pl.BlockSpec((blk, kv_w), prev),
                  pl.BlockSpec((blk, kv_w), prev),
                  _resident(bias.shape),
                  _resident((1, attn_w))],
        out_specs=pl.BlockSpec((tq, attn_w), cur),
        scratch_shapes=[pltpu.VMEM((per * N_KV_HEADS, GQA * blk, 2 * blk), F32),
                        pltpu.VMEM((per * N_KV_HEADS, GQA * blk, 2 * blk), BF16)],
        compiler_params=_cparams("arbitrary", "arbitrary"),
        name="attention_prompt",
    )(sinks, q, k, v, k, v, bias, g_attn.reshape(1, attn_w))


def _attn_sample_body(sinks_ref, q_ref, kn_ref, vn_ref, ck_ref, cv_ref, bias_ref, band_ref, g_ref,
                      o_ref, ko_ref, vo_ref):
    t = q_ref.shape[1]
    kk = jnp.concatenate([ck_ref[...], kn_ref[...]], axis=1)
    vv = jnp.concatenate([cv_ref[...], vn_ref[...]], axis=1)
    ko_ref[...] = kk[:, t:, :]
    vo_ref[...] = vv[:, t:, :]
    kkb = kk.astype(BF16)
    vvb = vv.astype(BF16)
    valid = band_ref[...] > 0.5
    outs = []
    for h in range(N_HEADS):
        hk = h // GQA
        qh = q_ref[:, :, h * HEAD_DIM:(h + 1) * HEAD_DIM]
        s = jnp.einsum('bqd,bkd->bqk', qh, kkb[:, :, hk * HEAD_DIM:(hk + 1) * HEAD_DIM],
                       preferred_element_type=F32)
        s = jnp.where(valid, s + bias_ref[h], NEG_INF)
        p = _sink_softmax(s, sinks_ref[h]).astype(BF16)
        outs.append(jnp.einsum('bqk,bkd->bqd', p, vvb[:, :, hk * HEAD_DIM:(hk + 1) * HEAD_DIM],
                               preferred_element_type=F32))
    att = jnp.concatenate(outs, axis=2)
    o_ref[...] = _rms(att, g_ref[...]).astype(BF16)


def _attn_sample(q3, k3, v3, ck, cv, sinks, bias, band, g_attn, gb):
    nb, t, attn_w = q3.shape
    w, kv_w = ck.shape[1], ck.shape[2]
    blk3 = lambda last: pl.BlockSpec((gb, last[0], last[1]), lambda i: (i, 0, 0))
    return pl.pallas_call(
        _attn_sample_body,
        out_shape=(jax.ShapeDtypeStruct((nb, t, attn_w), BF16),
                   jax.ShapeDtypeStruct((nb, w, kv_w), F32),
                   jax.ShapeDtypeStruct((nb, w, kv_w), F32)),
        grid=(nb // gb,),
        in_specs=[pl.BlockSpec(memory_space=pltpu.SMEM),
                  blk3((t, attn_w)), blk3((t, kv_w)), blk3((t, kv_w)),
                  blk3((w, kv_w)), blk3((w, kv_w)),
                  _resident(bias.shape), _resident(band.shape), _resident((1, 1, attn_w))],
        out_specs=(blk3((t, attn_w)), blk3((w, kv_w)), blk3((w, kv_w))),
        compiler_params=_cparams("arbitrary"),
        name="attention_sample",
    )(sinks, q3, k3, v3, ck, cv, bias, band, g_attn.reshape(1, 1, attn_w))


def _ssm_tail(y, u, d_ref, wglu_ref, bglu_ref, g_ref):
    z = jax.nn.gelu(y + d_ref[...] * u)
    gate = jax.nn.sigmoid(jnp.dot(z.astype(BF16), wglu_ref[...], preferred_element_type=F32) + bglu_ref[...])
    return _rms(z * gate, g_ref[...]).astype(BF16)


def _ssm_chunk_body(ut_ref, m_ref, e_ref, f_ref, w1_ref, w2_ref, yt_ref, hend_ref, s_scr):
    t, gb, ch, nc = ut_ref.shape
    n_levels = w1_ref.shape[1]
    two_p = e_ref.shape[1]
    row = lax.broadcasted_iota(jnp.int32, (nc, two_p), 0)
    rs = [ut_ref[:, gl].reshape(t * ch, nc) for gl in range(gb)]
    hs = []
    for gl in range(gb):
        s_scr[gl] = jnp.dot(e_ref[gl], rs[gl], preferred_element_type=F32)
        hs.append(s_scr[gl].T)

    def conv(gl):
        yt_ref[:, gl * ch:(gl + 1) * ch, :] = jnp.dot(m_ref[gl], rs[gl], preferred_element_type=F32).reshape(t, ch, nc)

    for lv in range(n_levels):
        sh = 1 << lv
        for gl in range(gb):
            prev = jnp.where(row >= sh, pltpu.roll(hs[gl], sh, axis=0), 0.0)
            hs[gl] = hs[gl] + w1_ref[gl, lv] * prev + w2_ref[gl, lv] * pltpu.roll(prev, two_p // 2, axis=1)
        for gl in range(lv * gb // n_levels, (lv + 1) * gb // n_levels):
            conv(gl)
    for gl in range(gb):
        hend_ref[gl] = hs[gl][nc - 1:nc, :]
        s_scr[gl] = jnp.where(row >= 1, pltpu.roll(hs[gl], 1, axis=0), 0.0).T
        yt_ref[:, gl * ch:(gl + 1) * ch, :] += jnp.dot(f_ref[gl], s_scr[gl].astype(BF16),
                                                       preferred_element_type=F32).reshape(t, ch, nc)


def _ssm_transpose_body(u_ref, o_ref):
    for s in range(u_ref.shape[0]):
        o_ref[s] = u_ref[s].T.astype(BF16).reshape(o_ref.shape[1:])


def _ssm_transpose(u2, ch):
    nb, t, nc, ssm_w = u2.shape
    tb = SSM_TAIL_BLOCK
    return pl.pallas_call(
        _ssm_transpose_body,
        out_shape=jax.ShapeDtypeStruct((nb, t, ssm_w // ch, ch, nc), BF16),
        grid=(nb, t // tb),
        in_specs=[pl.BlockSpec((None, tb, nc, ssm_w), lambda b, i: (b, i, 0, 0))],
        out_specs=pl.BlockSpec((None, tb, ssm_w // ch, ch, nc), lambda b, i: (b, i, 0, 0, 0)),
        compiler_params=_cparams("arbitrary", "arbitrary"),
        name="ssm_transpose",
    )(u2)


def _ssm_chunked(ut, kw):
    nb, t, ng, ch, nc = ut.shape
    tc = t * ch
    two_p = kw["e"].shape[1]
    n_levels = kw["w1"].shape[1]
    gb = SSM_GROUP_BLOCK
    grp = lambda shape: pl.BlockSpec((gb,) + shape, lambda b, g: (g,) + (0,) * len(shape))
    return pl.pallas_call(
        _ssm_chunk_body,
        out_shape=(jax.ShapeDtypeStruct((nb, t, ng * ch, nc), F32),
                   jax.ShapeDtypeStruct((nb, ng, 1, two_p), F32)),
        grid=(nb, ng // gb),
        in_specs=[pl.BlockSpec((None, t, gb, ch, nc), lambda b, g: (b, 0, g, 0, 0)),
                  grp((tc, tc)), grp((two_p, tc)), grp((tc, two_p)),
                  grp((n_levels, 1, two_p)), grp((n_levels, 1, two_p))],
        out_specs=(pl.BlockSpec((None, t, gb * ch, nc), lambda b, g: (b, 0, g, 0)),
                   pl.BlockSpec((None, gb, 1, two_p), lambda b, g: (b, g, 0, 0))),
        scratch_shapes=[pltpu.VMEM((gb, two_p, nc), F32)],
        compiler_params=_cparams("arbitrary", "arbitrary"),
        name="ssm_chunked",
    )(ut, kw["m"], kw["e"], kw["f"], kw["w1"], kw["w2"])


def _ssm_tail_body(yt_ref, u_ref, d_ref, wglu_ref, bglu_ref, g_ref, o_ref):
    for i in range(yt_ref.shape[0]):
        o_ref[i] = _ssm_tail(yt_ref[i].T, u_ref[i], d_ref, wglu_ref, bglu_ref, g_ref)


def _ssm_tail_call(yt, u2, sw):
    nb, t, nc, ssm_w = u2.shape
    tb = SSM_TAIL_BLOCK
    return pl.pallas_call(
        _ssm_tail_body,
        out_shape=jax.ShapeDtypeStruct((nb, t, nc, ssm_w), BF16),
        grid=(nb, t // tb),
        in_specs=[pl.BlockSpec((None, tb, ssm_w, nc), lambda b, i: (b, i, 0, 0)),
                  pl.BlockSpec((None, tb, nc, ssm_w), lambda b, i: (b, i, 0, 0)),
                  _resident((1, ssm_w)), _resident((ssm_w, ssm_w)), _resident((1, ssm_w)), _resident((1, ssm_w))],
        out_specs=pl.BlockSpec((None, tb, nc, ssm_w), lambda b, i: (b, i, 0, 0)),
        compiler_params=_cparams("arbitrary", "arbitrary"),
        name="ssm_tail",
    )(yt, u2, sw["d"], sw["wglu"], sw["bglu"], sw["g"])


def _lag_matrix_body(z_ref, m_ref, *, t):
    gb, ch, _ = z_ref.shape
    width = t * ch
    for gl in range(gb):
        z = z_ref[gl]
        for r in range(t):
            off = (t - 1 - r) * ch
            m_ref[gl, r * ch:(r + 1) * ch, :] = z[:, off:off + width].astype(BF16)


def _lag_matrix(z, t):
    ng, ch, zw = z.shape
    gb = SSM_GROUP_BLOCK
    return pl.pallas_call(
        functools.partial(_lag_matrix_body, t=t),
        out_shape=jax.ShapeDtypeStruct((ng, t * ch, zw // 2), BF16),
        grid=(ng // gb,),
        in_specs=[pl.BlockSpec((gb, ch, zw), lambda g: (g, 0, 0))],
        out_specs=pl.BlockSpec((gb, t * ch, zw // 2), lambda g: (g, 0, 0)),
        compiler_params=_cparams("arbitrary"),
        name="ssm_lag_matrix",
    )(z)


def _ssm_chunk_weights(lam_re, lam_im, log_dt, b_re, b_im, c_re, c_im, t, n_chunks):
    ng, ns = lam_re.shape
    nc = b_re.shape[2]
    dt = jnp.exp(log_dt)[:, None]
    are, aim = lam_re * dt, lam_im * dt
    d = jnp.arange(t + 1, dtype=F32)[:, None, None]
    mag = jnp.exp(d * are)
    pre, pim = mag * jnp.cos(d * aim), mag * jnp.sin(d * aim)
    lre, lim = pre[1], pim[1]
    den = lam_re * lam_re + lam_im * lam_im
    fre = ((lre - 1.0) * lam_re + lim * lam_im) / den
    fim = (lim * lam_re - (lre - 1.0) * lam_im) / den
    bbr = fre[..., None] * b_re - fim[..., None] * b_im
    bbi = fre[..., None] * b_im + fim[..., None] * b_re
    pre_g, pim_g = jnp.transpose(pre, (1, 0, 2)), jnp.transpose(pim, (1, 0, 2))
    xr = c_re[:, None] * pre_g[:, :, None, :] - c_im[:, None] * pim_g[:, :, None, :]
    xi = c_re[:, None] * pim_g[:, :, None, :] + c_im[:, None] * pre_g[:, :, None, :]
    kern = jnp.einsum('gdap,gpc->gdac', jnp.concatenate([xr[:, :t], -xi[:, :t]], axis=3),
                      jnp.concatenate([bbr, bbi], axis=1), precision=HIGHEST)
    lag_rows = jnp.transpose(kern[:, ::-1], (0, 2, 1, 3)).reshape(ng, nc, t * nc)
    m = _lag_matrix(jnp.concatenate([lag_rows, jnp.zeros_like(lag_rows)], axis=2), t)
    rev_re = jnp.transpose(pre_g[:, t - 1::-1][:, :t], (0, 2, 1))
    rev_im = jnp.transpose(pim_g[:, t - 1::-1][:, :t], (0, 2, 1))
    er = rev_re[..., None] * bbr[:, :, None, :] - rev_im[..., None] * bbi[:, :, None, :]
    ei = rev_re[..., None] * bbi[:, :, None, :] + rev_im[..., None] * bbr[:, :, None, :]
    e = jnp.concatenate([er.reshape(ng, ns, t * nc), ei.reshape(ng, ns, t * nc)], axis=1).astype(BF16)
    f = jnp.concatenate([xr[:, 1:].reshape(ng, t * nc, ns), -xi[:, 1:].reshape(ng, t * nc, ns)], axis=2).astype(BF16)
    wr, wi = pre[t], pim[t]
    w1, w2 = [], []
    for _ in range(max(1, (n_chunks - 1).bit_length())):
        w1.append(jnp.concatenate([wr, wr], axis=1))
        w2.append(jnp.concatenate([-wi, wi], axis=1))
        wr, wi = wr * wr - wi * wi, 2.0 * wr * wi
    w1 = jnp.stack(w1, axis=1)[:, :, None, :]
    w2 = jnp.stack(w2, axis=1)[:, :, None, :]
    return {"m": m, "e": e, "f": f, "w1": w1, "w2": w2}


def _ssm_sample_body(u_ref, h0re_ref, h0im_ref, bre_ref, bim_ref, cre_ref, cim_ref, lre_ref, lim_ref,
                     d_ref, wglu_ref, bglu_ref, g_ref, o_ref, hre_ref, him_ref):
    steps = u_ref.shape[0]
    half_in = bre_ref.shape[2]
    dot = functools.partial(jnp.dot, preferred_element_type=F32)
    ar = lre_ref[...]
    ai = lim_ref[...]
    hr = h0re_ref[...]
    hi = h0im_ref[...]
    for t in range(steps):
        u = u_ref[t]
        u_hi = u.astype(BF16)
        u_lo = (u - u_hi.astype(F32)).astype(BF16)

        def bu(b_ref):
            cols = []
            for hf in range(2):
                lanes = slice(hf * half_in, (hf + 1) * half_in)
                cols.append(dot(u_hi[:, lanes], b_ref[0, hf]) + dot(u_lo[:, lanes], b_ref[0, hf])
                            + dot(u_hi[:, lanes], b_ref[1, hf]))
            return jnp.concatenate(cols, axis=1)

        hr, hi = ar * hr - ai * hi + bu(bre_ref), ar * hi + ai * hr + bu(bim_ref)
        half_st = hr.shape[1] // 2
        hrb, hib = hr.astype(BF16), hi.astype(BF16)
        y = jnp.concatenate(
            [dot(hrb[:, hf * half_st:(hf + 1) * half_st], cre_ref[hf])
             + dot(hib[:, hf * half_st:(hf + 1) * half_st], cim_ref[hf]) for hf in range(2)], axis=1)
        o_ref[t] = _ssm_tail(y, u, d_ref, wglu_ref, bglu_ref, g_ref)
    hre_ref[...] = hr
    him_ref[...] = hi


def _ssm_sample(u_tm, h0re, h0im, sw):
    steps, nb, ssm_w = u_tm.shape
    n_state = h0re.shape[1]
    return pl.pallas_call(
        _ssm_sample_body,
        out_shape=(jax.ShapeDtypeStruct((steps, nb, ssm_w), BF16),
                   jax.ShapeDtypeStruct((nb, n_state), F32),
                   jax.ShapeDtypeStruct((nb, n_state), F32)),
        compiler_params=pltpu.CompilerParams(vmem_limit_bytes=VMEM_LIMIT_BYTES),
        name="ssm_sample",
    )(u_tm, h0re, h0im, _hi_lo(sw["bre"]), _hi_lo(sw["bim"]), sw["cre"].astype(BF16), sw["cim"].astype(BF16),
      sw["lre"], sw["lim"], sw["d"], sw["wglu"], sw["bglu"], sw["g"])


def _hi_lo(w):
    hi = w.astype(BF16)
    return jnp.stack([hi, (w - hi.astype(F32)).astype(BF16)])


def _ssm_weights(lam_re, lam_im, log_dt, b_re, b_im, c_re, c_im, d_skip, w_glu, b_glu, g_ssm):
    ng, ns = lam_re.shape
    nc = b_re.shape[2]
    dt = jnp.exp(log_dt)[:, None]
    mag = jnp.exp(lam_re * dt)
    lre = mag * jnp.cos(lam_im * dt)
    lim = mag * jnp.sin(lam_im * dt)
    den = lam_re * lam_re + lam_im * lam_im
    fre = ((lre - 1.0) * lam_re + lim * lam_im) / den
    fim = (lim * lam_re - (lre - 1.0) * lam_im) / den
    bbar_re = fre[..., None] * b_re - fim[..., None] * b_im
    bbar_im = fre[..., None] * b_im + fim[..., None] * b_re
    eye = jnp.eye(ng // 2, dtype=F32)

    def in_blocks(b):
        b2 = b.reshape(2, ng // 2, ns, nc)
        return jnp.einsum('hgpc,gk->hgckp', b2, eye).reshape(2, ng // 2 * nc, ng // 2 * ns)

    def out_blocks(c):
        c2 = c.reshape(2, ng // 2, nc, ns)
        return jnp.einsum('hgcp,gk->hgpkc', c2, eye).reshape(2, ng // 2 * ns, ng // 2 * nc)

    ssm_w = ng * nc
    return {
        "bre": in_blocks(bbar_re), "bim": in_blocks(bbar_im),
        "cre": out_blocks(c_re), "cim": out_blocks(-c_im),
        "lre": lre.reshape(1, ng * ns), "lim": lim.reshape(1, ng * ns),
        "d": d_skip.reshape(1, ssm_w), "wglu": w_glu.astype(BF16), "bglu": b_glu.reshape(1, ssm_w),
        "g": g_ssm.reshape(1, ssm_w),
    }


def _route(logits):
    lane = lax.broadcasted_iota(jnp.int32, logits.shape, 1)
    big = jnp.int32(10 ** 6)
    is_group = lane < N_EXPERT_GROUPS
    gl = jnp.where(is_group, logits, NEG_INF)
    gmax = jnp.max(gl, axis=-1, keepdims=True)
    gidx = jnp.min(jnp.where(is_group & (gl == gmax), lane, big), axis=-1, keepdims=True)
    g_p = 1.0 / jnp.sum(jnp.exp(gl - gmax), axis=-1, keepdims=True)
    lo = N_EXPERT_GROUPS + gidx * EXPERTS_PER_GROUP
    in_group = (lane >= lo) & (lane < lo + EXPERTS_PER_GROUP)
    el = jnp.where(in_group, logits, NEG_INF)
    ee = jnp.exp(el - jnp.max(el, axis=-1, keepdims=True))
    prob = ee / jnp.sum(ee, axis=-1, keepdims=True)
    p1 = jnp.max(jnp.where(in_group, prob, -1.0), axis=-1, keepdims=True)
    i1 = jnp.min(jnp.where(in_group & (prob == p1), lane, big), axis=-1, keepdims=True)
    rest = in_group & (lane != i1)
    p2 = jnp.max(jnp.where(rest, prob, -1.0), axis=-1, keepdims=True)
    i2 = jnp.min(jnp.where(rest & (prob == p2), lane, big), axis=-1, keepdims=True)
    tot = p1 + p2
    return jnp.where(lane == i1, g_p * p1 / tot, jnp.where(lane == i2, g_p * p2 / tot, 0.0)), gidx


def _split3(x):
    a = x.astype(BF16)
    r = x - a.astype(F32)
    b = r.astype(BF16)
    return a, b, (r - b.astype(F32)).astype(BF16)


def _moe_sorted(hn, gates, gidx, tri_ref, wg_ref, wu_ref, wd_ref, xs_scr, gs_scr, ys_scr):
    rows, d = hn.shape
    lane = lax.broadcasted_iota(jnp.int32, gates.shape, 1)
    member = lane == gidx
    csum = jnp.dot(tri_ref[...], member.astype(BF16), preferred_element_type=F32)
    rank = jnp.sum(jnp.where(member, csum, 0.0), axis=-1, keepdims=True)
    counts = [jnp.sum((gidx == grp).astype(jnp.int32)) for grp in range(N_EXPERT_GROUPS - 1)]
    offs = [jnp.int32(0)]
    for c in counts:
        offs.append(offs[-1] + c)
    offs.append(jnp.int32(rows))
    base = jnp.zeros_like(gidx)
    for grp in range(1, N_EXPERT_GROUPS):
        base = jnp.where(gidx == grp, offs[grp], base)
    pos = base.astype(F32) + rank - 1.0
    col = lax.broadcasted_iota(jnp.int32, (rows, rows), 1).astype(F32)
    row = lax.broadcasted_iota(jnp.int32, (rows, rows), 0).astype(F32)
    unsort = (col == pos).astype(BF16)
    pos_lanes = jnp.broadcast_to(pos, (rows, LANES)).T[0:1, :]
    sort = (row == pos_lanes).astype(BF16)
    xs_scr[...] = jnp.dot(sort, hn.astype(BF16), preferred_element_type=F32).astype(BF16)
    gs_scr[...] = sum(jnp.dot(sort, part, preferred_element_type=F32) for part in _split3(gates))
    ys_scr[...] = jnp.zeros_like(ys_scr)
    blk = MOE_ROW_BLOCK
    n_blk = rows // blk
    cuts = [jnp.int32(k * blk) for k in range(1, n_blk)] + offs[1:N_EXPERT_GROUPS]
    assert len(cuts) == 6
    for a, b in ((0, 5), (1, 3), (2, 4), (1, 2), (3, 4), (0, 3), (2, 5), (0, 1), (2, 3), (4, 5), (1, 2), (3, 4)):
        cuts[a], cuts[b] = jnp.minimum(cuts[a], cuts[b]), jnp.maximum(cuts[a], cuts[b])
    cuts = [jnp.int32(0)] + cuts + [jnp.int32(rows)]
    lane_b = lax.broadcasted_iota(jnp.int32, (blk, LANES), 1)
    for j in range(len(cuts) - 1):
        start = cuts[j]
        live = (cuts[j + 1] > start).astype(F32)
        k = jnp.minimum(start // blk, n_blk - 1)
        grp = sum((offs[q] <= start).astype(jnp.int32) for q in range(1, N_EXPERT_GROUPS))
        sl = pl.ds(pl.multiple_of(k * blk, blk), blk)
        xb = xs_scr[sl, :]
        gb = gs_scr[sl, :] * live
        acts = []
        for i in range(EXPERTS_PER_GROUP):
            e = grp * EXPERTS_PER_GROUP + i
            gate = jnp.sum(jnp.where(lane_b == N_EXPERT_GROUPS + e, gb, 0.0), axis=-1, keepdims=True)
            hg = jnp.dot(xb, wg_ref[e], preferred_element_type=F32)
            hu = jnp.dot(xb, wu_ref[e], preferred_element_type=F32)
            acts.append((jax.nn.silu(hg) * hu * gate).astype(BF16))
        ys_scr[sl, :] += jnp.dot(jnp.concatenate(acts, axis=1), wd_ref[grp], preferred_element_type=F32)
    hi = ys_scr[...].astype(BF16)
    lo = (ys_scr[...] - hi.astype(F32)).astype(BF16)
    return jnp.dot(unsort, hi, preferred_element_type=F32) + jnp.dot(unsort, lo, preferred_element_type=F32)


def _ffn_body(x_ref, ssm_ref, attn_ref, mod_ref, gpost_ref, gpre_ref, gffn_ref, wout_ref, wr_ref, br_ref,
              tri_ref, wg_ref, wu_ref, wd_ref, o_ref, xs_scr, gs_scr, ys_scr, *, chunked):
    x = x_ref[...]
    g, r, d = x.shape
    rows = g * r
    gt1, sh2, sc2, gt2 = mod_ref[:, 2], mod_ref[:, 3], mod_ref[:, 4], mod_ref[:, 5]
    if chunked:
        ssm = jnp.swapaxes(ssm_ref[...].astype(F32), 0, 1).reshape(rows, ssm_ref.shape[-1]).astype(BF16)
    else:
        ssm = ssm_ref[...]
    cat = jnp.concatenate([ssm, attn_ref[...]], axis=1)
    mixed = jnp.dot(cat, wout_ref[...], preferred_element_type=F32)
    x1 = x + gt1 * _rms(mixed, gpost_ref[0]).reshape(g, r, d)
    hn = (_rms(x1, gpre_ref[...]) * (1.0 + sc2) + sh2).reshape(rows, d)
    hb = hn.astype(BF16)
    hlo = (hn - hb.astype(F32)).astype(BF16)
    logits = (jnp.dot(hb, wr_ref[0], preferred_element_type=F32) + jnp.dot(hlo, wr_ref[0], preferred_element_type=F32)
              + jnp.dot(hb, wr_ref[1], preferred_element_type=F32) + br_ref[...])
    gates, gidx = _route(logits)
    ffn = _moe_sorted(hn, gates, gidx, tri_ref, wg_ref, wu_ref, wd_ref, xs_scr, gs_scr, ys_scr)
    o_ref[...] = x1 + gt2 * _rms(ffn, gffn_ref[0]).reshape(g, r, d)


def _ffn(x3, ssm_n, attn_n, mod4, fw, gb, rb, chunk=None):
    nb, nr, d = x3.shape
    assert gb == 1 or rb == nr
    half = ssm_n.shape[-1]
    nj = nr // rb
    tm = gb * rb
    row_map = lambda i, j: (i * nj + j, 0)
    vec = _resident((1, 1, d))
    if chunk is None:
        ssm_spec = pl.BlockSpec((tm, half), row_map)
    else:
        assert gb == 1 and rb % chunk == 0
        ssm_spec = pl.BlockSpec((None, chunk, rb // chunk, half), lambda i, j: (i, 0, j, 0))
    return pl.pallas_call(
        functools.partial(_ffn_body, chunked=chunk is not None),
        out_shape=jax.ShapeDtypeStruct((nb, nr, d), F32),
        grid=(nb // gb, nj),
        in_specs=[pl.BlockSpec((gb, rb, d), lambda i, j: (i, j, 0)),
                  ssm_spec,
                  pl.BlockSpec((tm, attn_n.shape[1]), row_map),
                  pl.BlockSpec((gb, N_MOD, 1, d), lambda i, j: (i, 0, 0, 0)),
                  vec, vec, vec,
                  _resident(fw["wout"].shape), _resident(fw["wr"].shape), _resident(fw["br"].shape),
                  _resident((tm, tm)),
                  _resident(fw["wg"].shape), _resident(fw["wu"].shape), _resident(fw["wd"].shape)],
        out_specs=pl.BlockSpec((gb, rb, d), lambda i, j: (i, j, 0)),
        scratch_shapes=[pltpu.VMEM((tm, d), BF16), pltpu.VMEM((tm, LANES), F32), pltpu.VMEM((tm, d), F32)],
        compiler_params=_cparams("arbitrary", "arbitrary"),
        name="outproj_moe",
    )(x3, ssm_n, attn_n, mod4, fw["gpost"], fw["gpre"], fw["gffn"], fw["wout"], fw["wr"], fw["br"],
      jnp.tril(jnp.ones((tm, tm), BF16)), fw["wg"], fw["wu"], fw["wd"])


def _ffn_weights(g_post_mix, g_pre_ffn, g_post_ffn, w_out, w_gr, b_gr, w_er, b_er, w_gate, w_up, w_down):
    d = w_out.shape[0]
    n_route = N_EXPERT_GROUPS + N_EXPERTS
    wr = jnp.concatenate([w_gr, w_er.reshape(d, N_EXPERTS)], axis=1)
    br = jnp.concatenate([b_gr, b_er.reshape(N_EXPERTS)])
    wr = jnp.pad(wr, ((0, 0), (0, LANES - n_route)))
    return {
        "gpost": g_post_mix.reshape(1, 1, d), "gpre": g_pre_ffn.reshape(1, 1, d), "gffn": g_post_ffn.reshape(1, 1, d),
        "wout": w_out.astype(BF16),
        "wr": _hi_lo(wr),
        "br": jnp.pad(br, (0, LANES - n_route)).reshape(1, LANES),
        "wg": w_gate.astype(BF16), "wu": w_up.astype(BF16),
        "wd": w_down.astype(BF16).reshape(N_EXPERT_GROUPS, -1, d),
    }


def _rel_bucket(dist):
    max_exact = NUM_BUCKETS // 2
    dd = jnp.maximum(dist, 0)
    log_ratio = jnp.log(jnp.maximum(dd, 1).astype(F32) / max_exact) / math.log(MAX_DISTANCE / max_exact)
    large = jnp.minimum(max_exact + (log_ratio * (NUM_BUCKETS - max_exact)).astype(jnp.int32), NUM_BUCKETS - 1)
    return jnp.where(dd < max_exact, dd, large)


def _bias_and_band(dist, table):
    onehot = (_rel_bucket(dist)[:, :, None] == jnp.arange(NUM_BUCKETS)[None, None, :]).astype(F32)
    bias = jnp.einsum('qkb,bh->hqk', onehot, table.astype(F32), precision=HIGHEST)
    band = ((dist >= 0) & (dist <= WINDOW)).astype(F32)
    return bias, band


def _layer(xp, xs, ck, cv, h0re, h0im, mod_p, mod_s, table, p):
    nb, seq, d = xp.shape
    nbs, t, _ = xs.shape
    w = ck.shape[1]
    kv_w = N_KV_HEADS * HEAD_DIM
    w_in_bf = p["w_in"].astype(BF16)
    sw = _ssm_weights(p["ssm_lam_re"], p["ssm_lam_im"], p["ssm_log_dt"], p["ssm_b_re"], p["ssm_b_im"],
                      p["ssm_c_re"], p["ssm_c_im"], p["ssm_d"], p["w_glu"], p["b_glu"], p["g_ssm_out"])
    fw = _ffn_weights(p["g_post_mix"], p["g_pre_ffn"], p["g_post_ffn"], p["w_out"],
                      p["w_group_router"], p["b_group_router"], p["w_expert_router"], p["b_expert_router"],
                      p["w_exp_gate"], p["w_exp_up"], p["w_exp_down"])
    n_state = sw["lre"].shape[1]
    ng = p["ssm_lam_re"].shape[0]

    u2, q, k, v = _inproj(xp, mod_p, p["g_pre_mix"], w_in_bf, 1, TOKEN_TILE, chunk=SSM_CHUNK)
    blk = WINDOW
    dist_p = jnp.arange(blk)[:, None] + blk - jnp.arange(2 * blk)[None, :]
    bias_p, band_p = _bias_and_band(dist_p, table)
    attn_p = _attn_prompt(q, k, v, p["attn_sinks"], bias_p, band_p, p["g_attn_out"], nb, seq)
    kw = _ssm_chunk_weights(p["ssm_lam_re"], p["ssm_lam_im"], p["ssm_log_dt"], p["ssm_b_re"], p["ssm_b_im"],
                            p["ssm_c_re"], p["ssm_c_im"], SSM_CHUNK, seq // SSM_CHUNK)
    yt, hend = _ssm_chunked(_ssm_transpose(u2, p["ssm_b_re"].shape[2]), kw)
    ssm_p = _ssm_tail_call(yt, u2, sw)
    yp = _ffn(xp, ssm_p, attn_p, mod_p, fw, 1, TOKEN_TILE, chunk=SSM_CHUNK)
    wp = min(WINDOW, seq)
    k_p = k.reshape(nb, seq, kv_w)[:, seq - wp:].reshape(nb, wp, N_KV_HEADS, HEAD_DIM)
    v_p = v.reshape(nb, seq, kv_w)[:, seq - wp:].reshape(nb, wp, N_KV_HEADS, HEAD_DIM)
    n_p = n_state // ng
    hre_p = hend[:, :, 0, :n_p]
    him_p = hend[:, :, 0, n_p:]

    gs = TOKEN_TILE // t
    us, qs, ks, vs = _inproj(xs, mod_s, p["g_pre_mix"], w_in_bf, gs, t)
    dist_s = jnp.arange(t)[:, None] + w - jnp.arange(w + t)[None, :]
    bias_s, band_s = _bias_and_band(dist_s, table)
    attn_s, k_s, v_s = _attn_sample(qs.reshape(nbs, t, -1), ks.reshape(nbs, t, kv_w), vs.reshape(nbs, t, kv_w),
                                    ck.reshape(nbs, w, kv_w), cv.reshape(nbs, w, kv_w),
                                    p["attn_sinks"], bias_s, band_s, p["g_attn_out"], 16)
    u_tm = jnp.swapaxes(us.reshape(nbs, t, -1), 0, 1)
    ssm_tm, hre_s, him_s = _ssm_sample(u_tm, h0re.reshape(nbs, n_state), h0im.reshape(nbs, n_state), sw)
    ssm_s = jnp.swapaxes(ssm_tm, 0, 1).reshape(nbs * t, -1)
    ys = _ffn(xs, ssm_s, attn_s.reshape(nbs * t, -1), mod_s, fw, gs, t)
    k_s = k_s.reshape(nbs, w, N_KV_HEADS, HEAD_DIM)
    v_s = v_s.reshape(nbs, w, N_KV_HEADS, HEAD_DIM)
    hre_s = hre_s.reshape(nbs, ng, n_state // ng)
    him_s = him_s.reshape(nbs, ng, n_state // ng)
    return yp, ys, k_p, v_p, hre_p, him_p, k_s, v_s, hre_s, him_s


def kernel(x_prompt, x_sample, cache_k, cache_v, state_ssm_re, state_ssm_im, c_prompt, c_sample, rel_bias_table,
           w_ada, b_ada, g_pre_mix, g_post_mix, g_pre_ffn, g_post_ffn, w_in, ssm_lam_re, ssm_lam_im, ssm_log_dt,
           ssm_b_re, ssm_b_im, ssm_c_re, ssm_c_im, ssm_d, w_glu, b_glu, attn_sinks, g_ssm_out, g_attn_out, w_out,
           w_group_router, b_group_router, w_expert_router, b_expert_router, w_exp_gate, w_exp_up, w_exp_down):
    params = dict(
        w_ada=w_ada, b_ada=b_ada, g_pre_mix=g_pre_mix, g_post_mix=g_post_mix, g_pre_ffn=g_pre_ffn,
        g_post_ffn=g_post_ffn, w_in=w_in, ssm_lam_re=ssm_lam_re, ssm_lam_im=ssm_lam_im, ssm_log_dt=ssm_log_dt,
        ssm_b_re=ssm_b_re, ssm_b_im=ssm_b_im, ssm_c_re=ssm_c_re, ssm_c_im=ssm_c_im, ssm_d=ssm_d, w_glu=w_glu,
        b_glu=b_glu, attn_sinks=attn_sinks, g_ssm_out=g_ssm_out, g_attn_out=g_attn_out, w_out=w_out,
        w_group_router=w_group_router, b_group_router=b_group_router, w_expert_router=w_expert_router,
        b_expert_router=b_expert_router, w_exp_gate=w_exp_gate, w_exp_up=w_exp_up, w_exp_down=w_exp_down)
    depth = w_in.shape[0]
    nb, nbs = x_prompt.shape[0], x_sample.shape[0]
    d = x_prompt.shape[2]
    pad = (-nb) % 8
    yp, ys = x_prompt, x_sample
    outs = [[] for _ in range(8)]
    for li in range(depth):
        p = {name: val[li] for name, val in params.items()}
        c_all = jnp.concatenate([c_prompt, jnp.zeros((pad, d), F32), c_sample], axis=0)
        mod = _modulation(c_all, p["w_ada"], p["b_ada"])
        mod_p = mod[:nb].reshape(nb, N_MOD, 1, d)
        mod_s = mod[nb + pad:].reshape(nbs, N_MOD, 1, d)
        res = _layer(yp, ys, cache_k[li], cache_v[li], state_ssm_re[li], state_ssm_im[li], mod_p, mod_s,
                     rel_bias_table, p)
        yp, ys = res[0], res[1]
        for acc, val in zip(outs, res[2:]):
            acc.append(val)
    return (yp, ys) + tuple(jnp.stack(o) for o in outs)
```

```python
import functools
import math

import jax
import jax.numpy as jnp
from jax import lax
from jax.experimental import pallas as pl
from jax.experimental.pallas import tpu as pltpu

F32 = jnp.float32
BF16 = jnp.bfloat16
HIGHEST = lax.Precision.HIGHEST

EPS = 1e-6
NEG_INF = -1e30

SSM_GROUP = 16
SSM_STATE = 64
HEAD_DIM = 64
N_KV_HEADS = 2
GQA = 4
N_HEADS = N_KV_HEADS * GQA
WINDOW = 128
NUM_BUCKETS = 32
MAX_DISTANCE = 128
N_EXPERT_GROUPS = 4
EXPERTS_PER_GROUP = 4
N_EXPERTS = N_EXPERT_GROUPS * EXPERTS_PER_GROUP
N_MOD = 6

LANES = 128
VMEM_LIMIT_BYTES = 56 * 1024 * 1024
FFN_VMEM_LIMIT_BYTES = 62 * 1024 * 1024

TOKEN_TILE = 512
ATTN_TILE = 512
SSM_CHUNK = 32
SSM_GROUP_BLOCK = 8
SSM_TAIL_BLOCK = 4
MOE_ROW_BLOCK = 128
N_CUTS = 8


def _cparams(*sem):
    return pltpu.CompilerParams(dimension_semantics=sem, vmem_limit_bytes=VMEM_LIMIT_BYTES)


def _resident(shape):
    zeros = (0,) * len(shape)
    return pl.BlockSpec(shape, lambda *_: zeros, pipeline_mode=pl.Buffered(1))


def _rms(x, g):
    return x * lax.rsqrt(jnp.mean(x * x, axis=-1, keepdims=True) + EPS) * g


def _mod_body(c_ref, w_ref, b_ref, o_ref):
    a = jax.nn.silu(c_ref[...])
    o_ref[...] = jnp.dot(a, w_ref[...], precision=HIGHEST, preferred_element_type=F32) + b_ref[...]


def _modulation(c, w_ada, b_ada):
    rows, d = c.shape
    n = w_ada.shape[1]
    tn = 1024
    return pl.pallas_call(
        _mod_body,
        out_shape=jax.ShapeDtypeStruct((rows, n), F32),
        grid=(n // tn,),
        in_specs=[pl.BlockSpec((rows, d), lambda j: (0, 0)),
                  pl.BlockSpec((d, tn), lambda j: (0, j)),
                  pl.BlockSpec((1, tn), lambda j: (0, j))],
        out_specs=pl.BlockSpec((rows, tn), lambda j: (0, j)),
        compiler_params=_cparams("arbitrary"),
        name="modulation",
    )(c, w_ada, b_ada.reshape(1, n))


def _inproj_body(x_ref, mod_ref, g_ref, w_ref, u_ref, q_ref, k_ref, v_ref, *, chunked):
    x = x_ref[...]
    g, r, d = x.shape
    hn = _rms(x, g_ref[...]) * (1.0 + mod_ref[:, 1]) + mod_ref[:, 0]
    hn = hn.reshape(g * r, d).astype(BF16)
    proj = jnp.dot(hn, w_ref[...], preferred_element_type=F32)
    ssm_w = u_ref.shape[-1]
    attn_w = q_ref.shape[1]
    kv_w = k_ref.shape[1]
    if chunked:
        t, nc, _ = u_ref.shape
        u_ref[...] = jnp.swapaxes(proj[:, :ssm_w].reshape(nc, t, ssm_w), 0, 1)
    else:
        u_ref[...] = proj[:, :ssm_w]
    q_ref[...] = (proj[:, ssm_w:ssm_w + attn_w] * (HEAD_DIM ** -0.5)).astype(BF16)
    k_ref[...] = proj[:, ssm_w + attn_w:ssm_w + attn_w + kv_w]
    v_ref[...] = proj[:, ssm_w + attn_w + kv_w:]


def _inproj(x3, mod4, g_pre, w_in_bf, gb, rb, chunk=None):
    nb, nr, d = x3.shape
    assert gb == 1 or rb == nr
    n_in = w_in_bf.shape[1]
    kv_w = N_KV_HEADS * HEAD_DIM
    attn_w = N_HEADS * HEAD_DIM
    ssm_w = n_in - attn_w - 2 * kv_w
    rows = nb * nr
    nj = nr // rb
    tm = gb * rb
    out_map = lambda i, j: (i * nj + j, 0)
    if chunk is None:
        u_shape = jax.ShapeDtypeStruct((rows, ssm_w), F32)
        u_spec = pl.BlockSpec((tm, ssm_w), out_map)
    else:
        assert gb == 1 and rb % chunk == 0
        u_shape = jax.ShapeDtypeStruct((nb, chunk, nr // chunk, ssm_w), F32)
        u_spec = pl.BlockSpec((None, chunk, rb // chunk, ssm_w), lambda i, j: (i, 0, j, 0))
    return pl.pallas_call(
        functools.partial(_inproj_body, chunked=chunk is not None),
        out_shape=(u_shape,
                   jax.ShapeDtypeStruct((rows, attn_w), BF16),
                   jax.ShapeDtypeStruct((rows, kv_w), F32),
                   jax.ShapeDtypeStruct((rows, kv_w), F32)),
        grid=(nb // gb, nj),
        in_specs=[pl.BlockSpec((gb, rb, d), lambda i, j: (i, j, 0)),
                  pl.BlockSpec((gb, 2, 1, d), lambda i, j: (i, 0, 0, 0)),
                  _resident((1, 1, d)),
                  _resident((d, n_in))],
        out_specs=(u_spec,
                   pl.BlockSpec((tm, attn_w), out_map),
                   pl.BlockSpec((tm, kv_w), out_map),
                   pl.BlockSpec((tm, kv_w), out_map)),
        compiler_params=_cparams("arbitrary", "arbitrary"),
        name="in_projection",
    )(x3, mod4, g_pre.reshape(1, 1, d), w_in_bf)


def _sink_softmax(s, sk):
    m = jnp.maximum(jnp.max(s, axis=-1, keepdims=True), sk)
    p = jnp.exp(s - m)
    return p / (jnp.sum(p, axis=-1, keepdims=True) + jnp.exp(sk - m))


def _attn_prompt_body(sinks_ref, q_ref, kc_ref, vc_ref, kp_ref, vp_ref, bias_ref, g_ref, o_ref, s_scr, p_scr):
    tq = q_ref.shape[0]
    blk = bias_ref.shape[2] // 2
    first_tile = pl.program_id(1) == 0
    kk = jnp.concatenate([kp_ref[...], kc_ref[...]], axis=0).astype(BF16)
    vv = jnp.concatenate([vp_ref[...], vc_ref[...]], axis=0).astype(BF16)
    shape = (GQA * blk, 2 * blk)
    col = lax.broadcasted_iota(jnp.int32, shape, 1)
    row = lax.broadcasted_iota(jnp.int32, (GQA * blk, 1), 0)
    no_prev = first_tile & (col < blk)
    sink_cols = []
    for hk in range(N_KV_HEADS):
        sk = jnp.full((GQA * blk, 1), sinks_ref[hk * GQA], F32)
        for gq in range(1, GQA):
            sk = jnp.where(row >= gq * blk, sinks_ref[hk * GQA + gq], sk)
        sink_cols.append(sk)
    pairs = [(j, hk) for j in range(tq // blk) for hk in range(N_KV_HEADS)]
    for idx, (j, hk) in enumerate(pairs):
        qg = jnp.concatenate([q_ref[j * blk:(j + 1) * blk, h * HEAD_DIM:(h + 1) * HEAD_DIM]
                              for h in range(hk * GQA, (hk + 1) * GQA)], axis=0)
        s = lax.dot_general(qg, kk[j * blk:(j + 2) * blk, hk * HEAD_DIM:(hk + 1) * HEAD_DIM],
                            (((1,), (1,)), ((), ())), preferred_element_type=F32) + bias_ref[hk]
        s_scr[idx] = jnp.where(no_prev, NEG_INF, s) if j == 0 else s
    dens = []
    for idx, (j, hk) in enumerate(pairs):
        s = s_scr[idx]
        m = jnp.maximum(jnp.max(s, axis=-1, keepdims=True), sink_cols[hk])
        p = jnp.exp(s - m)
        dens.append(jnp.sum(p, axis=-1, keepdims=True) + jnp.exp(sink_cols[hk] - m))
        p_scr[idx] = p.astype(BF16)
    outs = {}
    for idx, (j, hk) in enumerate(pairs):
        o = jnp.dot(p_scr[idx], vv[j * blk:(j + 2) * blk, hk * HEAD_DIM:(hk + 1) * HEAD_DIM],
                    preferred_element_type=F32) / dens[idx]
        for gq in range(GQA):
            outs[j, hk * GQA + gq] = o[gq * blk:(gq + 1) * blk]
    for j in range(tq // blk):
        att = jnp.concatenate([outs[j, h] for h in range(N_HEADS)], axis=1)
        o_ref[j * blk:(j + 1) * blk, :] = _rms(att, g_ref[...]).astype(BF16)


def _attn_prompt(q, k, v, sinks, bias, band, g_attn, nb, seq):
    rows, attn_w = q.shape
    kv_w = k.shape[1]
    blk = band.shape[0]
    bias = jnp.where(band > 0.5, bias, NEG_INF).reshape(N_KV_HEADS, GQA * blk, 2 * blk)
    tq = ATTN_TILE
    nt = seq // tq
    per = tq // blk
    cur = lambda b, i: (b * nt + i, 0)
    prev = lambda b, i: (b * nt * per + jnp.maximum(i * per - 1, 0), 0)
    return pl.pallas_call(
        _attn_prompt_body,
        out_shape=jax.ShapeDtypeStruct((rows, attn_w), BF16),
        grid=(nb, nt),
        in_specs=[pl.BlockSpec(memory_space=pltpu.SMEM),
                  pl.BlockSpec((tq, attn_w), cur),
                  pl.BlockSpec((tq, kv_w), cur),
                  pl.BlockSpec((tq, kv_w), cur),
                  pl.BlockSpec((blk, kv_w), prev),
                  pl.BlockSpec((blk, kv_w), prev),
                  _resident(bias.shape),
                  _resident((1, attn_w))],
        out_specs=pl.BlockSpec((tq, attn_w), cur),
        scratch_shapes=[pltpu.VMEM((per * N_KV_HEADS, GQA * blk, 2 * blk), F32),
                        pltpu.VMEM((per * N_KV_HEADS, GQA * blk, 2 * blk), BF16)],
        compiler_params=_cparams("arbitrary", "arbitrary"),
        name="attention_prompt",
    )(sinks, q, k, v, k, v, bias, g_attn.reshape(1, attn_w))


def _attn_sample_body(sinks_ref, q_ref, kn_ref, vn_ref, ck_ref, cv_ref, bias_ref, band_ref, g_ref,
                      o_ref, ko_ref, vo_ref):
    t = q_ref.shape[1]
    kk = jnp.concatenate([ck_ref[...], kn_ref[...]], axis=1)
    vv = jnp.concatenate([cv_ref[...], vn_ref[...]], axis=1)
    ko_ref[...] = kk[:, t:, :]
    vo_ref[...] = vv[:, t:, :]
    kkb = kk.astype(BF16)
    vvb = vv.astype(BF16)
    valid = band_ref[...] > 0.5
    outs = []
    for h in range(N_HEADS):
        hk = h // GQA
        qh = q_ref[:, :, h * HEAD_DIM:(h + 1) * HEAD_DIM]
        s = jnp.einsum('bqd,bkd->bqk', qh, kkb[:, :, hk * HEAD_DIM:(hk + 1) * HEAD_DIM],
                       preferred_element_type=F32)
        s = jnp.where(valid, s + bias_ref[h], NEG_INF)
        p = _sink_softmax(s, sinks_ref[h]).astype(BF16)
        outs.append(jnp.einsum('bqk,bkd->bqd', p, vvb[:, :, hk * HEAD_DIM:(hk + 1) * HEAD_DIM],
                               preferred_element_type=F32))
    att = jnp.concatenate(outs, axis=2)
    o_ref[...] = _rms(att, g_ref[...]).astype(BF16)


def _attn_sample(q3, k3, v3, ck, cv, sinks, bias, band, g_attn, gb):
    nb, t, attn_w = q3.shape
    w, kv_w = ck.shape[1], ck.shape[2]
    blk3 = lambda last: pl.BlockSpec((gb, last[0], last[1]), lambda i: (i, 0, 0))
    return pl.pallas_call(
        _attn_sample_body,
        out_shape=(jax.ShapeDtypeStruct((nb, t, attn_w), BF16),
                   jax.ShapeDtypeStruct((nb, w, kv_w), F32),
                   jax.ShapeDtypeStruct((nb, w, kv_w), F32)),
        grid=(nb // gb,),
        in_specs=[pl.BlockSpec(memory_space=pltpu.SMEM),
                  blk3((t, attn_w)), blk3((t, kv_w)), blk3((t, kv_w)),
                  blk3((w, kv_w)), blk3((w, kv_w)),
                  _resident(bias.shape), _resident(band.shape), _resident((1, 1, attn_w))],
        out_specs=(blk3((t, attn_w)), blk3((w, kv_w)), blk3((w, kv_w))),
        compiler_params=_cparams("arbitrary"),
        name="attention_sample",
    )(sinks, q3, k3, v3, ck, cv, bias, band, g_attn.reshape(1, 1, attn_w))


def _ssm_tail(y, u, d_ref, wglu_ref, bglu_ref, g_ref):
    z = jax.nn.gelu(y + d_ref[...] * u)
    gate = jax.nn.sigmoid(jnp.dot(z.astype(BF16), wglu_ref[...], preferred_element_type=F32) + bglu_ref[...])
    return _rms(z * gate, g_ref[...]).astype(BF16)


def _ssm_chunk_body(ut_ref, m_ref, e_ref, f_ref, w1_ref, w2_ref, yt_ref, hend_ref, s_scr):
    t, gb, ch, nc = ut_ref.shape
    n_levels = w1_ref.shape[1]
    two_p = e_ref.shape[1]
    row = lax.broadcasted_iota(jnp.int32, (nc, two_p), 0)
    rs = [ut_ref[:, gl].reshape(t * ch, nc) for gl in range(gb)]
    hs = []
    for gl in range(gb):
        s_scr[gl] = jnp.dot(e_ref[gl], rs[gl], preferred_element_type=F32)
        hs.append(s_scr[gl].T)

    def conv(gl):
        yt_ref[:, gl * ch:(gl + 1) * ch, :] = jnp.dot(m_ref[gl], rs[gl], preferred_element_type=F32).reshape(t, ch, nc)

    for lv in range(n_levels):
        sh = 1 << lv
        for gl in range(gb):
            prev = jnp.where(row >= sh, pltpu.roll(hs[gl], sh, axis=0), 0.0)
            hs[gl] = hs[gl] + w1_ref[gl, lv] * prev + w2_ref[gl, lv] * pltpu.roll(prev, two_p // 2, axis=1)
        for gl in range(lv * gb // n_levels, (lv + 1) * gb // n_levels):
            conv(gl)
    for gl in range(gb):
        hend_ref[gl] = hs[gl][nc - 1:nc, :]
        s_scr[gl] = jnp.where(row >= 1, pltpu.roll(hs[gl], 1, axis=0), 0.0).T
        yt_ref[:, gl * ch:(gl + 1) * ch, :] += jnp.dot(f_ref[gl], s_scr[gl].astype(BF16),
                                                       preferred_element_type=F32).reshape(t, ch, nc)


def _ssm_transpose_body(u_ref, o_ref):
    for s in range(u_ref.shape[0]):
        o_ref[s] = u_ref[s].T.astype(BF16).reshape(o_ref.shape[1:])


def _ssm_transpose(u2, ch):
    nb, t, nc, ssm_w = u2.shape
    tb = SSM_TAIL_BLOCK
    return pl.pallas_call(
        _ssm_transpose_body,
        out_shape=jax.ShapeDtypeStruct((nb, t, ssm_w // ch, ch, nc), BF16),
        grid=(nb, t // tb),
        in_specs=[pl.BlockSpec((None, tb, nc, ssm_w), lambda b, i: (b, i, 0, 0))],
        out_specs=pl.BlockSpec((None, tb, ssm_w // ch, ch, nc), lambda b, i: (b, i, 0, 0, 0)),
        compiler_params=_cparams("arbitrary", "arbitrary"),
        name="ssm_transpose",
    )(u2)


def _ssm_chunked(ut, kw):
    nb, t, ng, ch, nc = ut.shape
    tc = t * ch
    two_p = kw["e"].shape[1]
    n_levels = kw["w1"].shape[1]
    gb = SSM_GROUP_BLOCK
    grp = lambda shape: pl.BlockSpec((gb,) + shape, lambda b, g: (g,) + (0,) * len(shape))
    return pl.pallas_call(
        _ssm_chunk_body,
        out_shape=(jax.ShapeDtypeStruct((nb, t, ng * ch, nc), F32),
                   jax.ShapeDtypeStruct((nb, ng, 1, two_p), F32)),
        grid=(nb, ng // gb),
        in_specs=[pl.BlockSpec((None, t, gb, ch, nc), lambda b, g: (b, 0, g, 0, 0)),
                  grp((tc, tc)), grp((two_p, tc)), grp((tc, two_p)),
                  grp((n_levels, 1, two_p)), grp((n_levels, 1, two_p))],
        out_specs=(pl.BlockSpec((None, t, gb * ch, nc), lambda b, g: (b, 0, g, 0)),
                   pl.BlockSpec((None, gb, 1, two_p), lambda b, g: (b, g, 0, 0))),
        scratch_shapes=[pltpu.VMEM((gb, two_p, nc), F32)],
        compiler_params=_cparams("arbitrary", "arbitrary"),
        name="ssm_chunked",
    )(ut, kw["m"], kw["e"], kw["f"], kw["w1"], kw["w2"])


def _ssm_tail_body(yt_ref, u_ref, d_ref, wglu_ref, bglu_ref, g_ref, o_ref):
    for i in range(yt_ref.shape[0]):
        o_ref[i] = _ssm_tail(yt_ref[i].T, u_ref[i], d_ref, wglu_ref, bglu_ref, g_ref)


def _ssm_tail_call(yt, u2, sw):
    nb, t, nc, ssm_w = u2.shape
    tb = SSM_TAIL_BLOCK
    return pl.pallas_call(
        _ssm_tail_body,
        out_shape=jax.ShapeDtypeStruct((nb, t, nc, ssm_w), BF16),
        grid=(nb, t // tb),
        in_specs=[pl.BlockSpec((None, tb, ssm_w, nc), lambda b, i: (b, i, 0, 0)),
                  pl.BlockSpec((None, tb, nc, ssm_w), lambda b, i: (b, i, 0, 0)),
                  _resident((1, ssm_w)), _resident((ssm_w, ssm_w)), _resident((1, ssm_w)), _resident((1, ssm_w))],
        out_specs=pl.BlockSpec((None, tb, nc, ssm_w), lambda b, i: (b, i, 0, 0)),
        compiler_params=_cparams("arbitrary", "arbitrary"),
        name="ssm_tail",
    )(yt, u2, sw["d"], sw["wglu"], sw["bglu"], sw["g"])


def _lag_matrix_body(z_ref, m_ref, *, t):
    gb, ch, _ = z_ref.shape
    width = t * ch
    for gl in range(gb):
        z = z_ref[gl]
        for r in range(t):
            off = (t - 1 - r) * ch
            m_ref[gl, r * ch:(r + 1) * ch, :] = z[:, off:off + width].astype(BF16)


def _lag_matrix(z, t):
    ng, ch, zw = z.shape
    gb = SSM_GROUP_BLOCK
    return pl.pallas_call(
        functools.partial(_lag_matrix_body, t=t),
        out_shape=jax.ShapeDtypeStruct((ng, t * ch, zw // 2), BF16),
        grid=(ng // gb,),
        in_specs=[pl.BlockSpec((gb, ch, zw), lambda g: (g, 0, 0))],
        out_specs=pl.BlockSpec((gb, t * ch, zw // 2), lambda g: (g, 0, 0)),
        compiler_params=_cparams("arbitrary"),
        name="ssm_lag_matrix",
    )(z)


def _ssm_chunk_weights(lam_re, lam_im, log_dt, b_re, b_im, c_re, c_im, t, n_chunks):
    ng, ns = lam_re.shape
    nc = b_re.shape[2]
    dt = jnp.exp(log_dt)[:, None]
    are, aim = lam_re * dt, lam_im * dt
    d = jnp.arange(t + 1, dtype=F32)[:, None, None]
    mag = jnp.exp(d * are)
    pre, pim = mag * jnp.cos(d * aim), mag * jnp.sin(d * aim)
    lre, lim = pre[1], pim[1]
    den = lam_re * lam_re + lam_im * lam_im
    fre = ((lre - 1.0) * lam_re + lim * lam_im) / den
    fim = (lim * lam_re - (lre - 1.0) * lam_im) / den
    bbr = fre[..., None] * b_re - fim[..., None] * b_im
    bbi = fre[..., None] * b_im + fim[..., None] * b_re
    pre_g, pim_g = jnp.transpose(pre, (1, 0, 2)), jnp.transpose(pim, (1, 0, 2))
    xr = c_re[:, None] * pre_g[:, :, None, :] - c_im[:, None] * pim_g[:, :, None, :]
    xi = c_re[:, None] * pim_g[:, :, None, :] + c_im[:, None] * pre_g[:, :, None, :]
    kern = jnp.einsum('gdap,gpc->gdac', jnp.concatenate([xr[:, :t], -xi[:, :t]], axis=3),
                      jnp.concatenate([bbr, bbi], axis=1), precision=HIGHEST)
    lag_rows = jnp.transpose(kern[:, ::-1], (0, 2, 1, 3)).reshape(ng, nc, t * nc)
    m = _lag_matrix(jnp.concatenate([lag_rows, jnp.zeros_like(lag_rows)], axis=2), t)
    rev_re = jnp.transpose(pre_g[:, t - 1::-1][:, :t], (0, 2, 1))
    rev_im = jnp.transpose(pim_g[:, t - 1::-1][:, :t], (0, 2, 1))
    er = rev_re[..., None] * bbr[:, :, None, :] - rev_im[..., None] * bbi[:, :, None, :]
    ei = rev_re[..., None] * bbi[:, :, None, :] + rev_im[..., None] * bbr[:, :, None, :]
    e = jnp.concatenate([er.reshape(ng, ns, t * nc), ei.reshape(ng, ns, t * nc)], axis=1).astype(BF16)
    f = jnp.concatenate([xr[:, 1:].reshape(ng, t * nc, ns), -xi[:, 1:].reshape(ng, t * nc, ns)], axis=2).astype(BF16)
    wr, wi = pre[t], pim[t]
    w1, w2 = [], []
    for _ in range(max(1, (n_chunks - 1).bit_length())):
        w1.append(jnp.concatenate([wr, wr], axis=1))
        w2.append(jnp.concatenate([-wi, wi], axis=1))
        wr, wi = wr * wr - wi * wi, 2.0 * wr * wi
    w1 = jnp.stack(w1, axis=1)[:, :, None, :]
    w2 = jnp.stack(w2, axis=1)[:, :, None, :]
    return {"m": m, "e": e, "f": f, "w1": w1, "w2": w2}


def _ssm_sample_body(u_ref, h0re_ref, h0im_ref, bre_ref, bim_ref, cre_ref, cim_ref, lre_ref, lim_ref,
                     d_ref, wglu_ref, bglu_ref, g_ref, o_ref, hre_ref, him_ref):
    steps = u_ref.shape[0]
    half_in = bre_ref.shape[2]
    dot = functools.partial(jnp.dot, preferred_element_type=F32)
    ar = lre_ref[...]
    ai = lim_ref[...]
    hr = h0re_ref[...]
    hi = h0im_ref[...]
    for t in range(steps):
        u = u_ref[t]
        u_hi = u.astype(BF16)
        u_lo = (u - u_hi.astype(F32)).astype(BF16)

        def bu(b_ref):
            cols = []
            for hf in range(2):
                lanes = slice(hf * half_in, (hf + 1) * half_in)
                cols.append(dot(u_hi[:, lanes], b_ref[0, hf]) + dot(u_lo[:, lanes], b_ref[0, hf])
                            + dot(u_hi[:, lanes], b_ref[1, hf]))
            return jnp.concatenate(cols, axis=1)

        hr, hi = ar * hr - ai * hi + bu(bre_ref), ar * hi + ai * hr + bu(bim_ref)
        half_st = hr.shape[1] // 2
        hrb, hib = hr.astype(BF16), hi.astype(BF16)
        y = jnp.concatenate(
            [dot(hrb[:, hf * half_st:(hf + 1) * half_st], cre_ref[hf])
             + dot(hib[:, hf * half_st:(hf + 1) * half_st], cim_ref[hf]) for hf in range(2)], axis=1)
        o_ref[t] = _ssm_tail(y, u, d_ref, wglu_ref, bglu_ref, g_ref)
    hre_ref[...] = hr
    him_ref[...] = hi


def _ssm_sample(u_tm, h0re, h0im, sw):
    steps, nb, ssm_w = u_tm.shape
    n_state = h0re.shape[1]
    return pl.pallas_call(
        _ssm_sample_body,
        out_shape=(jax.ShapeDtypeStruct((steps, nb, ssm_w), BF16),
                   jax.ShapeDtypeStruct((nb, n_state), F32),
                   jax.ShapeDtypeStruct((nb, n_state), F32)),
        compiler_params=pltpu.CompilerParams(vmem_limit_bytes=VMEM_LIMIT_BYTES),
        name="ssm_sample",
    )(u_tm, h0re, h0im, _hi_lo(sw["bre"]), _hi_lo(sw["bim"]), sw["cre"].astype(BF16), sw["cim"].astype(BF16),
      sw["lre"], sw["lim"], sw["d"], sw["wglu"], sw["bglu"], sw["g"])


def _hi_lo(w):
    hi = w.astype(BF16)
    return jnp.stack([hi, (w - hi.astype(F32)).astype(BF16)])


def _ssm_weights(lam_re, lam_im, log_dt, b_re, b_im, c_re, c_im, d_skip, w_glu, b_glu, g_ssm):
    ng, ns = lam_re.shape
    nc = b_re.shape[2]
    dt = jnp.exp(log_dt)[:, None]
    mag = jnp.exp(lam_re * dt)
    lre = mag * jnp.cos(lam_im * dt)
    lim = mag * jnp.sin(lam_im * dt)
    den = lam_re * lam_re + lam_im * lam_im
    fre = ((lre - 1.0) * lam_re + lim * lam_im) / den
    fim = (lim * lam_re - (lre - 1.0) * lam_im) / den
    bbar_re = fre[..., None] * b_re - fim[..., None] * b_im
    bbar_im = fre[..., None] * b_im + fim[..., None] * b_re
    eye = jnp.eye(ng // 2, dtype=F32)

    def in_blocks(b):
        b2 = b.reshape(2, ng // 2, ns, nc)
        return jnp.einsum('hgpc,gk->hgckp', b2, eye).reshape(2, ng // 2 * nc, ng // 2 * ns)

    def out_blocks(c):
        c2 = c.reshape(2, ng // 2, nc, ns)
        return jnp.einsum('hgcp,gk->hgpkc', c2, eye).reshape(2, ng // 2 * ns, ng // 2 * nc)

    ssm_w = ng * nc
    return {
        "bre": in_blocks(bbar_re), "bim": in_blocks(bbar_im),
        "cre": out_blocks(c_re), "cim": out_blocks(-c_im),
        "lre": lre.reshape(1, ng * ns), "lim": lim.reshape(1, ng * ns),
        "d": d_skip.reshape(1, ssm_w), "wglu": w_glu.astype(BF16), "bglu": b_glu.reshape(1, ssm_w),
        "g": g_ssm.reshape(1, ssm_w),
    }


def _route(logits):
    lane = lax.broadcasted_iota(jnp.int32, logits.shape, 1)
    big = jnp.int32(10 ** 6)
    is_group = lane < N_EXPERT_GROUPS
    gl = jnp.where(is_group, logits, NEG_INF)
    gmax = jnp.max(gl, axis=-1, keepdims=True)
    gidx = jnp.min(jnp.where(is_group & (gl == gmax), lane, big), axis=-1, keepdims=True)
    g_p = 1.0 / jnp.sum(jnp.exp(gl - gmax), axis=-1, keepdims=True)
    lo = N_EXPERT_GROUPS + gidx * EXPERTS_PER_GROUP
    in_group = (lane >= lo) & (lane < lo + EXPERTS_PER_GROUP)
    el = jnp.where(in_group, logits, NEG_INF)
    ee = jnp.exp(el - jnp.max(el, axis=-1, keepdims=True))
    prob = ee / jnp.sum(ee, axis=-1, keepdims=True)
    p1 = jnp.max(jnp.where(in_group, prob, -1.0), axis=-1, keepdims=True)
    i1 = jnp.min(jnp.where(in_group & (prob == p1), lane, big), axis=-1, keepdims=True)
    rest = in_group & (lane != i1)
    p2 = jnp.max(jnp.where(rest, prob, -1.0), axis=-1, keepdims=True)
    i2 = jnp.min(jnp.where(rest & (prob == p2), lane, big), axis=-1, keepdims=True)
    tot = p1 + p2
    return jnp.where(lane == i1, g_p * p1 / tot, jnp.where(lane == i2, g_p * p2 / tot, 0.0)), gidx


def _split3(x):
    a = x.astype(BF16)
    r = x - a.astype(F32)
    b = r.astype(BF16)
    return a, b, (r - b.astype(F32)).astype(BF16)


def _moe_positions(gidx, tri_ref):
    rows = gidx.shape[0]
    lane = lax.broadcasted_iota(jnp.int32, (rows, LANES), 1)
    member = lane == gidx
    csum = jnp.dot(tri_ref[...], member.astype(BF16), preferred_element_type=F32)
    rank = jnp.sum(jnp.where(member, csum, 0.0), axis=-1, keepdims=True)
    counts = [jnp.sum((gidx == grp).astype(jnp.int32)) for grp in range(N_EXPERT_GROUPS - 1)]
    offs = [jnp.int32(0)]
    for c in counts:
        offs.append(offs[-1] + c)
    offs.append(jnp.int32(rows))
    base = jnp.zeros_like(gidx)
    for grp in range(1, N_EXPERT_GROUPS):
        base = jnp.where(gidx == grp, offs[grp], base)
    pos = base.astype(F32) + rank - 1.0
    blk = MOE_ROW_BLOCK
    cuts = [jnp.int32(k * blk) for k in range(1, rows // blk)] + offs[1:N_EXPERT_GROUPS]
    assert len(cuts) + 2 == N_CUTS
    for a, b in ((0, 5), (1, 3), (2, 4), (1, 2), (3, 4), (0, 3), (2, 5), (0, 1), (2, 3), (4, 5), (1, 2), (3, 4)):
        cuts[a], cuts[b] = jnp.minimum(cuts[a], cuts[b]), jnp.maximum(cuts[a], cuts[b])
    return pos, [jnp.int32(0)] + cuts + [jnp.int32(rows)], offs


def _moe_permutations(pos):
    rows = pos.shape[0]
    col = lax.broadcasted_iota(jnp.int32, (rows, rows), 1).astype(F32)
    row = lax.broadcasted_iota(jnp.int32, (rows, rows), 0).astype(F32)
    unsort = (col == pos).astype(BF16)
    pos_lanes = jnp.broadcast_to(pos, (rows, LANES)).T[0:1, :]
    return (row == pos_lanes).astype(BF16), unsort


def _moe_apply(xs_ref, gs_ref, unsort, cuts, offs, wg_ref, wu_ref, wd_ref, ys_scr, interleave=()):
    rows = ys_scr.shape[0]
    blk = MOE_ROW_BLOCK
    n_blk = rows // blk
    ys_scr[...] = jnp.zeros_like(ys_scr)
    lane_b = lax.broadcasted_iota(jnp.int32, (blk, LANES), 1)
    for j in range(len(cuts) - 1):
        start = cuts[j]
        live = (cuts[j + 1] > start).astype(F32)
        k = jnp.minimum(start // blk, n_blk - 1)
        grp = sum((offs[q] <= start).astype(jnp.int32) for q in range(1, N_EXPERT_GROUPS))
        sl = pl.ds(pl.multiple_of(k * blk, blk), blk)
        xb = xs_ref[sl, :]
        gb = gs_ref[sl, :] * live
        acts = []
        for i in range(EXPERTS_PER_GROUP):
            e = grp * EXPERTS_PER_GROUP + i
            gate = jnp.sum(jnp.where(lane_b == N_EXPERT_GROUPS + e, gb, 0.0), axis=-1, keepdims=True)
            hg = jnp.dot(xb, wg_ref[e], preferred_element_type=F32)
            hu = jnp.dot(xb, wu_ref[e], preferred_element_type=F32)
            acts.append((jax.nn.silu(hg) * hu * gate).astype(BF16))
        ys_scr[sl, :] += jnp.dot(jnp.concatenate(acts, axis=1), wd_ref[grp], preferred_element_type=F32)
        if j < len(interleave):
            interleave[j]()
    hi = ys_scr[...].astype(BF16)
    lo = (ys_scr[...] - hi.astype(F32)).astype(BF16)
    return jnp.dot(unsort, hi, preferred_element_type=F32) + jnp.dot(unsort, lo, preferred_element_type=F32)


def _ffn_body(x_ref, ssm_ref, attn_ref, mod_ref, gpost_ref, gpre_ref, gffn_ref, wout_ref, wr_ref, br_ref,
              tri_ref, wg_ref, wu_ref, wd_ref, o_ref,
              xs_scr, gs_scr, un_scr, x1_scr, gt2_scr, ys_scr, sc_smem, *, chunked):
    step = pl.program_id(0)
    slot = lax.rem(step, 2)
    prev = 1 - slot
    g, r, d = x_ref.shape
    rows = g * r

    @pl.when(step == 0)
    def _():
        xs_scr[1] = jnp.zeros(xs_scr.shape[1:], xs_scr.dtype)
        gs_scr[1] = jnp.zeros(gs_scr.shape[1:], gs_scr.dtype)
        un_scr[1] = jnp.zeros(un_scr.shape[1:], un_scr.dtype)
        x1_scr[1] = jnp.zeros(x1_scr.shape[1:], x1_scr.dtype)
        gt2_scr[1] = jnp.zeros(gt2_scr.shape[1:], gt2_scr.dtype)
        for q in range(N_CUTS + N_EXPERT_GROUPS + 1):
            sc_smem[1, q] = jnp.int32(0)

    st = {}

    def stage_outproj():
        x = x_ref[...]
        if chunked:
            ssm = jnp.swapaxes(ssm_ref[...].astype(F32), 0, 1).reshape(rows, ssm_ref.shape[-1]).astype(BF16)
        else:
            ssm = ssm_ref[...]
        cat = jnp.concatenate([ssm, attn_ref[...]], axis=1)
        mixed = jnp.dot(cat, wout_ref[...], preferred_element_type=F32)
        st["x1"] = x + mod_ref[:, 2] * _rms(mixed, gpost_ref[0]).reshape(g, r, d)

    def stage_logits():
        hn = (_rms(st["x1"], gpre_ref[...]) * (1.0 + mod_ref[:, 4]) + mod_ref[:, 3]).reshape(rows, d)
        hb = hn.astype(BF16)
        hlo = (hn - hb.astype(F32)).astype(BF16)
        st["hb"] = hb
        st["logits"] = (jnp.dot(hb, wr_ref[0], preferred_element_type=F32)
                        + jnp.dot(hlo, wr_ref[0], preferred_element_type=F32)
                        + jnp.dot(hb, wr_ref[1], preferred_element_type=F32) + br_ref[...])

    def stage_route():
        st["gates"], st["gidx"] = _route(st["logits"])

    def stage_positions():
        st["pos"], st["cuts"], st["offs"] = _moe_positions(st["gidx"], tri_ref)

    def stage_sort():
        sort, st["unsort"] = _moe_permutations(st["pos"])
        st["xs"] = jnp.dot(sort, st["hb"], preferred_element_type=F32).astype(BF16)
        st["gs"] = sum(jnp.dot(sort, part, preferred_element_type=F32) for part in _split3(st["gates"]))

    cuts = [sc_smem[prev, q] for q in range(N_CUTS)]
    offs = [sc_smem[prev, N_CUTS + q] for q in range(N_EXPERT_GROUPS + 1)]
    ffn = _moe_apply(xs_scr.at[prev], gs_scr.at[prev], un_scr[prev], cuts, offs, wg_ref, wu_ref, wd_ref, ys_scr,
                     interleave=(stage_outproj, stage_logits, stage_route, stage_positions, stage_sort))
    o_ref[...] = x1_scr[prev].reshape(g, r, d) + gt2_scr[prev] * _rms(ffn, gffn_ref[0]).reshape(g, r, d)

    xs_scr[slot] = st["xs"]
    gs_scr[slot] = st["gs"]
    un_scr[slot] = st["unsort"]
    x1_scr[slot] = st["x1"].reshape(rows, d)
    gt2_scr[slot] = mod_ref[:, 5]
    for q, val in enumerate(st["cuts"] + st["offs"]):
        sc_smem[slot, q] = val


def _ffn(x3, ssm_n, attn_n, mod4, fw, gb, rb, chunk=None):
    nb, nr, d = x3.shape
    assert gb == 1 or rb == nr
    half = ssm_n.shape[-1]
    nj = nr // rb
    tm = gb * rb
    n_tiles = (nb // gb) * nj
    cur = lambda s: jnp.minimum(s, n_tiles - 1)
    vec = _resident((1, 1, d))
    if chunk is None:
        ssm_spec = pl.BlockSpec((tm, half), lambda s: (cur(s), 0))
    else:
        assert gb == 1 and rb % chunk == 0
        ssm_spec = pl.BlockSpec((None, chunk, rb // chunk, half), lambda s: (cur(s) // nj, 0, cur(s) % nj, 0))
    return pl.pallas_call(
        functools.partial(_ffn_body, chunked=chunk is not None),
        out_shape=jax.ShapeDtypeStruct((nb, nr, d), F32),
        grid=(n_tiles + 1,),
        in_specs=[pl.BlockSpec((gb, rb, d), lambda s: (cur(s) // nj, cur(s) % nj, 0)),
                  ssm_spec,
                  pl.BlockSpec((tm, attn_n.shape[1]), lambda s: (cur(s), 0)),
                  pl.BlockSpec((gb, N_MOD, 1, d), lambda s: (cur(s) // nj, 0, 0, 0)),
                  vec, vec, vec,
                  _resident(fw["wout"].shape), _resident(fw["wr"].shape), _resident(fw["br"].shape),
                  _resident((tm, tm)),
                  _resident(fw["wg"].shape), _resident(fw["wu"].shape), _resident(fw["wd"].shape)],
        out_specs=pl.BlockSpec((gb, rb, d), lambda s: (jnp.maximum(s - 1, 0) // nj, jnp.maximum(s - 1, 0) % nj, 0)),
        scratch_shapes=[pltpu.VMEM((2, tm, d), BF16), pltpu.VMEM((2, tm, LANES), F32), pltpu.VMEM((2, tm, tm), BF16),
                        pltpu.VMEM((2, tm, d), F32), pltpu.VMEM((2, gb, 1, d), F32), pltpu.VMEM((tm, d), F32),
                        pltpu.SMEM((2, N_CUTS + N_EXPERT_GROUPS + 1), jnp.int32)],
        compiler_params=pltpu.CompilerParams(dimension_semantics=("arbitrary",), vmem_limit_bytes=FFN_VMEM_LIMIT_BYTES),
        name="outproj_moe",
    )(x3, ssm_n, attn_n, mod4, fw["gpost"], fw["gpre"], fw["gffn"], fw["wout"], fw["wr"], fw["br"],
      jnp.tril(jnp.ones((tm, tm), BF16)), fw["wg"], fw["wu"], fw["wd"])


def _ffn_weights(g_post_mix, g_pre_ffn, g_post_ffn, w_out, w_gr, b_gr, w_er, b_er, w_gate, w_up, w_down):
    d = w_out.shape[0]
    n_route = N_EXPERT_GROUPS + N_EXPERTS
    wr = jnp.concatenate([w_gr, w_er.reshape(d, N_EXPERTS)], axis=1)
    br = jnp.concatenate([b_gr, b_er.reshape(N_EXPERTS)])
    wr = jnp.pad(wr, ((0, 0), (0, LANES - n_route)))
    return {
        "gpost": g_post_mix.reshape(1, 1, d), "gpre": g_pre_ffn.reshape(1, 1, d), "gffn": g_post_ffn.reshape(1, 1, d),
        "wout": w_out.astype(BF16),
        "wr": _hi_lo(wr),
        "br": jnp.pad(br, (0, LANES - n_route)).reshape(1, LANES),
        "wg": w_gate.astype(BF16), "wu": w_up.astype(BF16),
        "wd": w_down.astype(BF16).reshape(N_EXPERT_GROUPS, -1, d),
    }


def _rel_bucket(dist):
    max_exact = NUM_BUCKETS // 2
    dd = jnp.maximum(dist, 0)
    log_ratio = jnp.log(jnp.maximum(dd, 1).astype(F32) / max_exact) / math.log(MAX_DISTANCE / max_exact)
    large = jnp.minimum(max_exact + (log_ratio * (NUM_BUCKETS - max_exact)).astype(jnp.int32), NUM_BUCKETS - 1)
    return jnp.where(dd < max_exact, dd, large)


def _bias_and_band(dist, table):
    onehot = (_rel_bucket(dist)[:, :, None] == jnp.arange(NUM_BUCKETS)[None, None, :]).astype(F32)
    bias = jnp.einsum('qkb,bh->hqk', onehot, table.astype(F32), precision=HIGHEST)
    band = ((dist >= 0) & (dist <= WINDOW)).astype(F32)
    return bias, band


def _layer(xp, xs, ck, cv, h0re, h0im, mod_p, mod_s, table, p):
    nb, seq, d = xp.shape
    nbs, t, _ = xs.shape
    w = ck.shape[1]
    kv_w = N_KV_HEADS * HEAD_DIM
    w_in_bf = p["w_in"].astype(BF16)
    sw = _ssm_weights(p["ssm_lam_re"], p["ssm_lam_im"], p["ssm_log_dt"], p["ssm_b_re"], p["ssm_b_im"],
                      p["ssm_c_re"], p["ssm_c_im"], p["ssm_d"], p["w_glu"], p["b_glu"], p["g_ssm_out"])
    fw = _ffn_weights(p["g_post_mix"], p["g_pre_ffn"], p["g_post_ffn"], p["w_out"],
                      p["w_group_router"], p["b_group_router"], p["w_expert_router"], p["b_expert_router"],
                      p["w_exp_gate"], p["w_exp_up"], p["w_exp_down"])
    n_state = sw["lre"].shape[1]
    ng = p["ssm_lam_re"].shape[0]

    u2, q, k, v = _inproj(xp, mod_p, p["g_pre_mix"], w_in_bf, 1, TOKEN_TILE, chunk=SSM_CHUNK)
    blk = WINDOW
    dist_p = jnp.arange(blk)[:, None] + blk - jnp.arange(2 * blk)[None, :]
    bias_p, band_p = _bias_and_band(dist_p, table)
    attn_p = _attn_prompt(q, k, v, p["attn_sinks"], bias_p, band_p, p["g_attn_out"], nb, seq)
    kw = _ssm_chunk_weights(p["ssm_lam_re"], p["ssm_lam_im"], p["ssm_log_dt"], p["ssm_b_re"], p["ssm_b_im"],
                            p["ssm_c_re"], p["ssm_c_im"], SSM_CHUNK, seq // SSM_CHUNK)
    yt, hend = _ssm_chunked(_ssm_transpose(u2, p["ssm_b_re"].shape[2]), kw)
    ssm_p = _ssm_tail_call(yt, u2, sw)
    yp = _ffn(xp, ssm_p, attn_p, mod_p, fw, 1, TOKEN_TILE, chunk=SSM_CHUNK)
    wp = min(WINDOW, seq)
    k_p = k.reshape(nb, seq, kv_w)[:, seq - wp:].reshape(nb, wp, N_KV_HEADS, HEAD_DIM)
    v_p = v.reshape(nb, seq, kv_w)[:, seq - wp:].reshape(nb, wp, N_KV_HEADS, HEAD_DIM)
    n_p = n_state // ng
    hre_p = hend[:, :, 0, :n_p]
    him_p = hend[:, :, 0, n_p:]

    gs = TOKEN_TILE // t
    us, qs, ks, vs = _inproj(xs, mod_s, p["g_pre_mix"], w_in_bf, gs, t)
    dist_s = jnp.arange(t)[:, None] + w - jnp.arange(w + t)[None, :]
    bias_s, band_s = _bias_and_band(dist_s, table)
    attn_s, k_s, v_s = _attn_sample(qs.reshape(nbs, t, -1), ks.reshape(nbs, t, kv_w), vs.reshape(nbs, t, kv_w),
                                    ck.reshape(nbs, w, kv_w), cv.reshape(nbs, w, kv_w),
                                    p["attn_sinks"], bias_s, band_s, p["g_attn_out"], 16)
    u_tm = jnp.swapaxes(us.reshape(nbs, t, -1), 0, 1)
    ssm_tm, hre_s, him_s = _ssm_sample(u_tm, h0re.reshape(nbs, n_state), h0im.reshape(nbs, n_state), sw)
    ssm_s = jnp.swapaxes(ssm_tm, 0, 1).reshape(nbs * t, -1)
    ys = _ffn(xs, ssm_s, attn_s.reshape(nbs * t, -1), mod_s, fw, gs, t)
    k_s = k_s.reshape(nbs, w, N_KV_HEADS, HEAD_DIM)
    v_s = v_s.reshape(nbs, w, N_KV_HEADS, HEAD_DIM)
    hre_s = hre_s.reshape(nbs, ng, n_state // ng)
    him_s = him_s.reshape(nbs, ng, n_state // ng)
    return yp, ys, k_p, v_p, hre_p, him_p, k_s, v_s, hre_s, him_s


def kernel(x_prompt, x_sample, cache_k, cache_v, state_ssm_re, state_ssm_im, c_prompt, c_sample, rel_bias_table,
           w_ada, b_ada, g_pre_mix, g_post_mix, g_pre_ffn, g_post_ffn, w_in, ssm_lam_re, ssm_lam_im, ssm_log_dt,
           ssm_b_re, ssm_b_im, ssm_c_re, ssm_c_im, ssm_d, w_glu, b_glu, attn_sinks, g_ssm_out, g_attn_out, w_out,
           w_group_router, b_group_router, w_expert_router, b_expert_router, w_exp_gate, w_exp_up, w_exp_down):
    params = dict(
        w_ada=w_ada, b_ada=b_ada, g_pre_mix=g_pre_mix, g_post_mix=g_post_mix, g_pre_ffn=g_pre_ffn,
        g_post_ffn=g_post_ffn, w_in=w_in, ssm_lam_re=ssm_lam_re, ssm_lam_im=ssm_lam_im, ssm_log_dt=ssm_log_dt,
        ssm_b_re=ssm_b_re, ssm_b_im=ssm_b_im, ssm_c_re=ssm_c_re, ssm_c_im=ssm_c_im, ssm_d=ssm_d, w_glu=w_glu,
        b_glu=b_glu, attn_sinks=attn_sinks, g_ssm_out=g_ssm_out, g_attn_out=g_attn_out, w_out=w_out,
        w_group_router=w_group_router, b_group_router=b_group_router, w_expert_router=w_expert_router,
        b_expert_router=b_expert_router, w_exp_gate=w_exp_gate, w_exp_up=w_exp_up, w_exp_down=w_exp_down)
    depth = w_in.shape[0]
    nb, nbs = x_prompt.shape[0], x_sample.shape[0]
    d = x_prompt.shape[2]
    pad = (-nb) % 8
    yp, ys = x_prompt, x_sample
    outs = [[] for _ in range(8)]
    for li in range(depth):
        p = {name: val[li] for name, val in params.items()}
        c_all = jnp.concatenate([c_prompt, jnp.zeros((pad, d), F32), c_sample], axis=0)
        mod = _modulation(c_all, p["w_ada"], p["b_ada"])
        mod_p = mod[:nb].reshape(nb, N_MOD, 1, d)
        mod_s = mod[nb + pad:].reshape(nbs, N_MOD, 1, d)
        res = _layer(yp, ys, cache_k[li], cache_v[li], state_ssm_re[li], state_ssm_im[li], mod_p, mod_s,
                     rel_bias_table, p)
        yp, ys = res[0], res[1]
        for acc, val in zip(outs, res[2:]):
            acc.append(val)
    return (yp, ys) + tuple(jnp.stack(o) for o in outs)
```

```python
import functools
import math

import jax
import jax.numpy as jnp
from jax import lax
from jax.experimental import pallas as pl
from jax.experimental.pallas import tpu as pltpu

F32 = jnp.float32
BF16 = jnp.bfloat16
HIGHEST = lax.Precision.HIGHEST

EPS = 1e-6
NEG_INF = -1e30

SSM_GROUP = 16
SSM_STATE = 64
HEAD_DIM = 64
N_KV_HEADS = 2
GQA = 4
N_HEADS = N_KV_HEADS * GQA
WINDOW = 128
NUM_BUCKETS = 32
MAX_DISTANCE = 128
N_EXPERT_GROUPS = 4
EXPERTS_PER_GROUP = 4
N_EXPERTS = N_EXPERT_GROUPS * EXPERTS_PER_GROUP
N_MOD = 6

LANES = 128
VMEM_LIMIT_BYTES = 56 * 1024 * 1024

TOKEN_TILE = 512
INPROJ_TILE = 1024
ATTN_TILE = 512
SSM_CHUNK = 32
SSM_GROUP_BLOCK = 8
SSM_TAIL_BLOCK = 4
MOE_ROW_BLOCK = 128


def _cparams(*sem):
    return pltpu.CompilerParams(dimension_semantics=sem, vmem_limit_bytes=VMEM_LIMIT_BYTES)


def _resident(shape):
    zeros = (0,) * len(shape)
    return pl.BlockSpec(shape, lambda *_: zeros, pipeline_mode=pl.Buffered(1))


def _rms(x, g):
    return x * lax.rsqrt(jnp.mean(x * x, axis=-1, keepdims=True) + EPS) * g


def _mod_body(c_ref, w_ref, b_ref, o_ref):
    a = jax.nn.silu(c_ref[...])
    w = w_ref[...]
    a_hi, w_hi = a.astype(BF16), w.astype(BF16)
    a_lo, w_lo = (a - a_hi.astype(F32)).astype(BF16), (w - w_hi.astype(F32)).astype(BF16)
    dot = functools.partial(jnp.dot, preferred_element_type=F32)
    o_ref[...] = dot(a_hi, w_hi) + dot(a_lo, w_hi) + dot(a_hi, w_lo) + b_ref[...]


def _modulation(c, w_ada, b_ada):
    rows, d = c.shape
    n = w_ada.shape[1]
    tn = 1024
    return pl.pallas_call(
        _mod_body,
        out_shape=jax.ShapeDtypeStruct((rows, n), F32),
        grid=(n // tn,),
        in_specs=[pl.BlockSpec((rows, d), lambda j: (0, 0)),
                  pl.BlockSpec((d, tn), lambda j: (0, j)),
                  pl.BlockSpec((1, tn), lambda j: (0, j))],
        out_specs=pl.BlockSpec((rows, tn), lambda j: (0, j)),
        compiler_params=_cparams("arbitrary"),
        name="modulation",
    )(c, w_ada, b_ada.reshape(1, n))


def _inproj_body(x_ref, mod_ref, g_ref, w_ref, u_ref, q_ref, k_ref, v_ref, *, chunked):
    x = x_ref[...]
    g, r, d = x.shape
    hn = _rms(x, g_ref[...]) * (1.0 + mod_ref[:, 1]) + mod_ref[:, 0]
    hn = hn.reshape(g * r, d).astype(BF16)
    proj = jnp.dot(hn, w_ref[...], preferred_element_type=F32)
    ssm_w = u_ref.shape[-1]
    attn_w = q_ref.shape[1]
    kv_w = k_ref.shape[1]
    if chunked:
        t, nc, _ = u_ref.shape
        u_ref[...] = jnp.swapaxes(proj[:, :ssm_w].reshape(nc, t, ssm_w), 0, 1)
    else:
        u_ref[...] = proj[:, :ssm_w]
    q_ref[...] = (proj[:, ssm_w:ssm_w + attn_w] * (HEAD_DIM ** -0.5)).astype(BF16)
    k_ref[...] = proj[:, ssm_w + attn_w:ssm_w + attn_w + kv_w]
    v_ref[...] = proj[:, ssm_w + attn_w + kv_w:]


def _inproj(x3, mod4, g_pre, w_in_bf, gb, rb, chunk=None):
    nb, nr, d = x3.shape
    assert gb == 1 or rb == nr
    n_in = w_in_bf.shape[1]
    kv_w = N_KV_HEADS * HEAD_DIM
    attn_w = N_HEADS * HEAD_DIM
    ssm_w = n_in - attn_w - 2 * kv_w
    rows = nb * nr
    nj = nr // rb
    tm = gb * rb
    out_map = lambda i, j: (i * nj + j, 0)
    if chunk is None:
        u_shape = jax.ShapeDtypeStruct((rows, ssm_w), F32)
        u_spec = pl.BlockSpec((tm, ssm_w), out_map)
    else:
        assert gb == 1 and rb % chunk == 0
        u_shape = jax.ShapeDtypeStruct((nb, chunk, nr // chunk, ssm_w), F32)
        u_spec = pl.BlockSpec((None, chunk, rb // chunk, ssm_w), lambda i, j: (i, 0, j, 0))
    return pl.pallas_call(
        functools.partial(_inproj_body, chunked=chunk is not None),
        out_shape=(u_shape,
                   jax.ShapeDtypeStruct((rows, attn_w), BF16),
                   jax.ShapeDtypeStruct((rows, kv_w), F32),
                   jax.ShapeDtypeStruct((rows, kv_w), F32)),
        grid=(nb // gb, nj),
        in_specs=[pl.BlockSpec((gb, rb, d), lambda i, j: (i, j, 0)),
                  pl.BlockSpec((gb, 2, 1, d), lambda i, j: (i, 0, 0, 0)),
                  _resident((1, 1, d)),
                  _resident((d, n_in))],
        out_specs=(u_spec,
                   pl.BlockSpec((tm, attn_w), out_map),
                   pl.BlockSpec((tm, kv_w), out_map),
                   pl.BlockSpec((tm, kv_w), out_map)),
        compiler_params=_cparams("arbitrary", "arbitrary"),
        name="in_projection",
    )(x3, mod4, g_pre.reshape(1, 1, d), w_in_bf)


def _attn_prompt_body(sinks_ref, q_ref, kc_ref, vc_ref, kp_ref, vp_ref, bias_ref, g_ref, o_ref, s_scr, p_scr):
    tq = q_ref.shape[0]
    blk = bias_ref.shape[2] // 2
    first_tile = pl.program_id(1) == 0
    kk = jnp.concatenate([kp_ref[...], kc_ref[...]], axis=0).astype(BF16)
    vv = jnp.concatenate([vp_ref[...], vc_ref[...]], axis=0).astype(BF16)
    shape = (GQA * blk, 2 * blk)
    col = lax.broadcasted_iota(jnp.int32, shape, 1)
    row = lax.broadcasted_iota(jnp.int32, (GQA * blk, 1), 0)
    no_prev = first_tile & (col < blk)
    sink_cols = []
    for hk in range(N_KV_HEADS):
        sk = jnp.full((GQA * blk, 1), sinks_ref[hk * GQA], F32)
        for gq in range(1, GQA):
            sk = jnp.where(row >= gq * blk, sinks_ref[hk * GQA + gq], sk)
        sink_cols.append(sk)
    pairs = [(j, hk) for j in range(tq // blk) for hk in range(N_KV_HEADS)]
    for idx, (j, hk) in enumerate(pairs):
        qg = jnp.concatenate([q_ref[j * blk:(j + 1) * blk, h * HEAD_DIM:(h + 1) * HEAD_DIM]
                              for h in range(hk * GQA, (hk + 1) * GQA)], axis=0)
        s = lax.dot_general(qg, kk[j * blk:(j + 2) * blk, hk * HEAD_DIM:(hk + 1) * HEAD_DIM],
                            (((1,), (1,)), ((), ())), preferred_element_type=F32) + bias_ref[hk]
        s_scr[idx] = jnp.where(no_prev, NEG_INF, s) if j == 0 else s
    dens = []
    for idx, (j, hk) in enumerate(pairs):
        s = s_scr[idx]
        m = jnp.maximum(jnp.max(s, axis=-1, keepdims=True), sink_cols[hk])
        p = jnp.exp(s - m)
        dens.append(jnp.sum(p, axis=-1, keepdims=True) + jnp.exp(sink_cols[hk] - m))
        p_scr[idx] = p.astype(BF16)
    outs = {}
    for idx, (j, hk) in enumerate(pairs):
        o = jnp.dot(p_scr[idx], vv[j * blk:(j + 2) * blk, hk * HEAD_DIM:(hk + 1) * HEAD_DIM],
                    preferred_element_type=F32) / dens[idx]
        for gq in range(GQA):
            outs[j, hk * GQA + gq] = o[gq * blk:(gq + 1) * blk]
    for j in range(tq // blk):
        att = jnp.concatenate([outs[j, h] for h in range(N_HEADS)], axis=1)
        o_ref[j * blk:(j + 1) * blk, :] = _rms(att, g_ref[...]).astype(BF16)


def _attn_prompt(q, k, v, sinks, bias, band, g_attn, nb, seq):
    rows, attn_w = q.shape
    kv_w = k.shape[1]
    blk = band.shape[0]
    bias = jnp.where(band > 0.5, bias, NEG_INF).reshape(N_KV_HEADS, GQA * blk, 2 * blk)
    tq = ATTN_TILE
    nt = seq // tq
    per = tq // blk
    cur = lambda b, i: (b * nt + i, 0)
    prev = lambda b, i: (b * nt * per + jnp.maximum(i * per - 1, 0), 0)
    return pl.pallas_call(
        _attn_prompt_body,
        out_shape=jax.ShapeDtypeStruct((rows, attn_w), BF16),
        grid=(nb, nt),
        in_specs=[pl.BlockSpec(memory_space=pltpu.SMEM),
                  pl.BlockSpec((tq, attn_w), cur),
                  pl.BlockSpec((tq, kv_w), cur),
                  pl.BlockSpec((tq, kv_w), cur),
                  pl.BlockSpec((blk, kv_w), prev),
                  pl.BlockSpec((blk, kv_w), prev),
                  _resident(bias.shape),
                  _resident((1, attn_w))],
        out_specs=pl.BlockSpec((tq, attn_w), cur),
        scratch_shapes=[pltpu.VMEM((per * N_KV_HEADS, GQA * blk, 2 * blk), F32),
                        pltpu.VMEM((per * N_KV_HEADS, GQA * blk, 2 * blk), BF16)],
        compiler_params=_cparams("arbitrary", "arbitrary"),
        name="attention_prompt",
    )(sinks, q, k, v, k, v, bias, g_attn.reshape(1, attn_w))


def _attn_sample_body(sinks_ref, q_ref, kn_ref, vn_ref, ck_ref, cv_ref, bias_ref, g_ref, o_ref, ko_ref, vo_ref):
    t = q_ref.shape[1]
    kk = jnp.concatenate([ck_ref[...], kn_ref[...]], axis=1)
    vv = jnp.concatenate([cv_ref[...], vn_ref[...]], axis=1)
    ko_ref[...] = kk[:, t:, :]
    vo_ref[...] = vv[:, t:, :]
    kkb = kk.astype(BF16)
    vvb = vv.astype(BF16)
    row = lax.broadcasted_iota(jnp.int32, (GQA * t, 1), 0)
    outs = [None] * N_HEADS
    for hk in range(N_KV_HEADS):
        sk = jnp.full((GQA * t, 1), sinks_ref[hk * GQA], F32)
        for gq in range(1, GQA):
            sk = jnp.where(row >= gq * t, sinks_ref[hk * GQA + gq], sk)
        qg = jnp.concatenate([q_ref[:, :, h * HEAD_DIM:(h + 1) * HEAD_DIM]
                              for h in range(hk * GQA, (hk + 1) * GQA)], axis=1)
        s = jnp.einsum('bqd,bkd->bqk', qg, kkb[:, :, hk * HEAD_DIM:(hk + 1) * HEAD_DIM],
                       preferred_element_type=F32) + bias_ref[hk]
        m = jnp.maximum(jnp.max(s, axis=-1, keepdims=True), sk)
        p = jnp.exp(s - m)
        den = jnp.sum(p, axis=-1, keepdims=True) + jnp.exp(sk - m)
        o = jnp.einsum('bqk,bkd->bqd', p.astype(BF16), vvb[:, :, hk * HEAD_DIM:(hk + 1) * HEAD_DIM],
                       preferred_element_type=F32) / den
        for gq in range(GQA):
            outs[hk * GQA + gq] = o[:, gq * t:(gq + 1) * t, :]
    att = jnp.concatenate(outs, axis=2)
    o_ref[...] = _rms(att, g_ref[...]).astype(BF16)


def _attn_sample(q3, k3, v3, ck, cv, sinks, bias, band, g_attn, gb):
    nb, t, attn_w = q3.shape
    w, kv_w = ck.shape[1], ck.shape[2]
    bias = jnp.where(band > 0.5, bias, NEG_INF).reshape(N_KV_HEADS, GQA * t, w + t)
    blk3 = lambda last: pl.BlockSpec((gb, last[0], last[1]), lambda i: (i, 0, 0))
    return pl.pallas_call(
        _attn_sample_body,
        out_shape=(jax.ShapeDtypeStruct((nb, t, attn_w), BF16),
                   jax.ShapeDtypeStruct((nb, w, kv_w), F32),
                   jax.ShapeDtypeStruct((nb, w, kv_w), F32)),
        grid=(nb // gb,),
        in_specs=[pl.BlockSpec(memory_space=pltpu.SMEM),
                  blk3((t, attn_w)), blk3((t, kv_w)), blk3((t, kv_w)),
                  blk3((w, kv_w)), blk3((w, kv_w)),
                  _resident(bias.shape), _resident((1, 1, attn_w))],
        out_specs=(blk3((t, attn_w)), blk3((w, kv_w)), blk3((w, kv_w))),
        compiler_params=_cparams("arbitrary"),
        name="attention_sample",
    )(sinks, q3, k3, v3, ck, cv, bias, g_attn.reshape(1, 1, attn_w))


def _ssm_tail(y, u, d_ref, wglu_ref, bglu_ref, g_ref):
    z = jax.nn.gelu(y + d_ref[...] * u)
    gate = jax.nn.sigmoid(jnp.dot(z.astype(BF16), wglu_ref[...], preferred_element_type=F32) + bglu_ref[...])
    return _rms(z * gate, g_ref[...]).astype(BF16)


def _ssm_chunk_body(ut_ref, m_ref, e_ref, f_ref, w1_ref, w2_ref, yt_ref, hend_ref, s_scr):
    t, gb, ch, nc = ut_ref.shape
    n_levels = w1_ref.shape[1]
    two_p = e_ref.shape[1]
    row = lax.broadcasted_iota(jnp.int32, (nc, two_p), 0)
    rs = [ut_ref[:, gl].reshape(t * ch, nc) for gl in range(gb)]
    hs = []
    for gl in range(gb):
        s_scr[gl] = jnp.dot(e_ref[gl], rs[gl], preferred_element_type=F32)
        hs.append(s_scr[gl].T)

    def conv(gl):
        yt_ref[:, gl * ch:(gl + 1) * ch, :] = jnp.dot(m_ref[gl], rs[gl], preferred_element_type=F32).reshape(t, ch, nc)

    for lv in range(n_levels):
        sh = 1 << lv
        for gl in range(gb):
            prev = jnp.where(row >= sh, pltpu.roll(hs[gl], sh, axis=0), 0.0)
            hs[gl] = hs[gl] + w1_ref[gl, lv] * prev + w2_ref[gl, lv] * pltpu.roll(prev, two_p // 2, axis=1)
        for gl in range(lv * gb // n_levels, (lv + 1) * gb // n_levels):
            conv(gl)
    for gl in range(gb):
        hend_ref[gl] = hs[gl][nc - 1:nc, :]
        s_scr[gl] = jnp.where(row >= 1, pltpu.roll(hs[gl], 1, axis=0), 0.0).T
        yt_ref[:, gl * ch:(gl + 1) * ch, :] += jnp.dot(f_ref[gl], s_scr[gl].astype(BF16),
                                                       preferred_element_type=F32).reshape(t, ch, nc)


def _ssm_transpose_body(u_ref, o_ref):
    for s in range(u_ref.shape[0]):
        o_ref[s] = u_ref[s].T.astype(BF16).reshape(o_ref.shape[1:])


def _ssm_transpose(u2, ch):
    nb, t, nc, ssm_w = u2.shape
    tb = SSM_TAIL_BLOCK
    return pl.pallas_call(
        _ssm_transpose_body,
        out_shape=jax.ShapeDtypeStruct((nb, t, ssm_w // ch, ch, nc), BF16),
        grid=(nb, t // tb),
        in_specs=[pl.BlockSpec((None, tb, nc, ssm_w), lambda b, i: (b, i, 0, 0))],
        out_specs=pl.BlockSpec((None, tb, ssm_w // ch, ch, nc), lambda b, i: (b, i, 0, 0, 0)),
        compiler_params=_cparams("arbitrary", "arbitrary"),
        name="ssm_transpose",
    )(u2)


def _ssm_chunked(ut, kw):
    nb, t, ng, ch, nc = ut.shape
    tc = t * ch
    two_p = kw["e"].shape[1]
    n_levels = kw["w1"].shape[1]
    gb = SSM_GROUP_BLOCK
    grp = lambda shape: pl.BlockSpec((gb,) + shape, lambda b, g: (g,) + (0,) * len(shape))
    return pl.pallas_call(
        _ssm_chunk_body,
        out_shape=(jax.ShapeDtypeStruct((nb, t, ng * ch, nc), F32),
                   jax.ShapeDtypeStruct((nb, ng, 1, two_p), F32)),
        grid=(nb, ng // gb),
        in_specs=[pl.BlockSpec((None, t, gb, ch, nc), lambda b, g: (b, 0, g, 0, 0)),
                  grp((tc, tc)), grp((two_p, tc)), grp((tc, two_p)),
                  grp((n_levels, 1, two_p)), grp((n_levels, 1, two_p))],
        out_specs=(pl.BlockSpec((None, t, gb * ch, nc), lambda b, g: (b, 0, g, 0)),
                   pl.BlockSpec((None, gb, 1, two_p), lambda b, g: (b, g, 0, 0))),
        scratch_shapes=[pltpu.VMEM((gb, two_p, nc), F32)],
        compiler_params=_cparams("arbitrary", "arbitrary"),
        name="ssm_chunked",
    )(ut, kw["m"], kw["e"], kw["f"], kw["w1"], kw["w2"])


def _ssm_tail_body(yt_ref, u_ref, d_ref, wglu_ref, bglu_ref, g_ref, o_ref):
    for i in range(yt_ref.shape[0]):
        o_ref[i] = _ssm_tail(yt_ref[i].T, u_ref[i], d_ref, wglu_ref, bglu_ref, g_ref)


def _ssm_tail_call(yt, u2, sw):
    nb, t, nc, ssm_w = u2.shape
    tb = SSM_TAIL_BLOCK
    return pl.pallas_call(
        _ssm_tail_body,
        out_shape=jax.ShapeDtypeStruct((nb, t, nc, ssm_w), BF16),
        grid=(nb, t // tb),
        in_specs=[pl.BlockSpec((None, tb, ssm_w, nc), lambda b, i: (b, i, 0, 0)),
                  pl.BlockSpec((None, tb, nc, ssm_w), lambda b, i: (b, i, 0, 0)),
                  _resident((1, ssm_w)), _resident((ssm_w, ssm_w)), _resident((1, ssm_w)), _resident((1, ssm_w))],
        out_specs=pl.BlockSpec((None, tb, nc, ssm_w), lambda b, i: (b, i, 0, 0)),
        compiler_params=_cparams("arbitrary", "arbitrary"),
        name="ssm_tail",
    )(yt, u2, sw["d"], sw["wglu"], sw["bglu"], sw["g"])


def _lag_matrix_body(z_ref, m_ref, *, t):
    gb, ch, _ = z_ref.shape
    width = t * ch
    for gl in range(gb):
        z = z_ref[gl]
        for r in range(t):
            off = (t - 1 - r) * ch
            m_ref[gl, r * ch:(r + 1) * ch, :] = z[:, off:off + width].astype(BF16)


def _lag_matrix(z, t):
    ng, ch, zw = z.shape
    gb = SSM_GROUP_BLOCK
    return pl.pallas_call(
        functools.partial(_lag_matrix_body, t=t),
        out_shape=jax.ShapeDtypeStruct((ng, t * ch, zw // 2), BF16),
        grid=(ng // gb,),
        in_specs=[pl.BlockSpec((gb, ch, zw), lambda g: (g, 0, 0))],
        out_specs=pl.BlockSpec((gb, t * ch, zw // 2), lambda g: (g, 0, 0)),
        compiler_params=_cparams("arbitrary"),
        name="ssm_lag_matrix",
    )(z)


def _ssm_chunk_weights(lam_re, lam_im, log_dt, b_re, b_im, c_re, c_im, t, n_chunks):
    ng, ns = lam_re.shape
    nc = b_re.shape[2]
    dt = jnp.exp(log_dt)[:, None]
    are, aim = lam_re * dt, lam_im * dt
    d = jnp.arange(t + 1, dtype=F32)[:, None, None]
    mag = jnp.exp(d * are)
    pre, pim = mag * jnp.cos(d * aim), mag * jnp.sin(d * aim)
    lre, lim = pre[1], pim[1]
    den = lam_re * lam_re + lam_im * lam_im
    fre = ((lre - 1.0) * lam_re + lim * lam_im) / den
    fim = (lim * lam_re - (lre - 1.0) * lam_im) / den
    bbr = fre[..., None] * b_re - fim[..., None] * b_im
    bbi = fre[..., None] * b_im + fim[..., None] * b_re
    pre_g, pim_g = jnp.transpose(pre, (1, 0, 2)), jnp.transpose(pim, (1, 0, 2))
    xr = c_re[:, None] * pre_g[:, :, None, :] - c_im[:, None] * pim_g[:, :, None, :]
    xi = c_re[:, None] * pim_g[:, :, None, :] + c_im[:, None] * pre_g[:, :, None, :]
    kern = jnp.einsum('gdap,gpc->gdac', jnp.concatenate([xr[:, :t], -xi[:, :t]], axis=3),
                      jnp.concatenate([bbr, bbi], axis=1), precision=HIGHEST)
    lag_rows = jnp.transpose(kern[:, ::-1], (0, 2, 1, 3)).reshape(ng, nc, t * nc)
    m = _lag_matrix(jnp.concatenate([lag_rows, jnp.zeros_like(lag_rows)], axis=2), t)
    rev_re = jnp.transpose(pre_g[:, t - 1::-1][:, :t], (0, 2, 1))
    rev_im = jnp.transpose(pim_g[:, t - 1::-1][:, :t], (0, 2, 1))
    er = rev_re[..., None] * bbr[:, :, None, :] - rev_im[..., None] * bbi[:, :, None, :]
    ei = rev_re[..., None] * bbi[:, :, None, :] + rev_im[..., None] * bbr[:, :, None, :]
    e = jnp.concatenate([er.reshape(ng, ns, t * nc), ei.reshape(ng, ns, t * nc)], axis=1).astype(BF16)
    f = jnp.concatenate([xr[:, 1:].reshape(ng, t * nc, ns), -xi[:, 1:].reshape(ng, t * nc, ns)], axis=2).astype(BF16)
    wr, wi = pre[t], pim[t]
    w1, w2 = [], []
    for _ in range(max(1, (n_chunks - 1).bit_length())):
        w1.append(jnp.concatenate([wr, wr], axis=1))
        w2.append(jnp.concatenate([-wi, wi], axis=1))
        wr, wi = wr * wr - wi * wi, 2.0 * wr * wi
    w1 = jnp.stack(w1, axis=1)[:, :, None, :]
    w2 = jnp.stack(w2, axis=1)[:, :, None, :]
    return {"m": m, "e": e, "f": f, "w1": w1, "w2": w2}


def _ssm_sample_body(u_ref, h0re_ref, h0im_ref, bre_ref, bim_ref, cre_ref, cim_ref, lre_ref, lim_ref,
                     d_ref, wglu_ref, bglu_ref, g_ref, o_ref, hre_ref, him_ref):
    steps = u_ref.shape[0]
    half_in = bre_ref.shape[2]
    dot = functools.partial(jnp.dot, preferred_element_type=F32)
    ar = lre_ref[...]
    ai = lim_ref[...]
    hr = h0re_ref[...]
    hi = h0im_ref[...]
    for t in range(steps):
        u = u_ref[t]
        u_hi = u.astype(BF16)
        u_lo = (u - u_hi.astype(F32)).astype(BF16)

        def bu(b_ref):
            cols = []
            for hf in range(2):
                lanes = slice(hf * half_in, (hf + 1) * half_in)
                cols.append(dot(u_hi[:, lanes], b_ref[0, hf]) + dot(u_lo[:, lanes], b_ref[0, hf])
                            + dot(u_hi[:, lanes], b_ref[1, hf]))
            return jnp.concatenate(cols, axis=1)

        hr, hi = ar * hr - ai * hi + bu(bre_ref), ar * hi + ai * hr + bu(bim_ref)
        half_st = hr.shape[1] // 2
        hrb, hib = hr.astype(BF16), hi.astype(BF16)
        y = jnp.concatenate(
            [dot(hrb[:, hf * half_st:(hf + 1) * half_st], cre_ref[hf])
             + dot(hib[:, hf * half_st:(hf + 1) * half_st], cim_ref[hf]) for hf in range(2)], axis=1)
        o_ref[t] = _ssm_tail(y, u, d_ref, wglu_ref, bglu_ref, g_ref)
    hre_ref[...] = hr
    him_ref[...] = hi


def _ssm_sample(u_tm, h0re, h0im, sw):
    steps, nb, ssm_w = u_tm.shape
    n_state = h0re.shape[1]
    return pl.pallas_call(
        _ssm_sample_body,
        out_shape=(jax.ShapeDtypeStruct((steps, nb, ssm_w), BF16),
                   jax.ShapeDtypeStruct((nb, n_state), F32),
                   jax.ShapeDtypeStruct((nb, n_state), F32)),
        compiler_params=pltpu.CompilerParams(vmem_limit_bytes=VMEM_LIMIT_BYTES),
        name="ssm_sample",
    )(u_tm, h0re, h0im, _hi_lo(sw["bre"]), _hi_lo(sw["bim"]), sw["cre"].astype(BF16), sw["cim"].astype(BF16),
      sw["lre"], sw["lim"], sw["d"], sw["wglu"], sw["bglu"], sw["g"])


def _hi_lo(w):
    hi = w.astype(BF16)
    return jnp.stack([hi, (w - hi.astype(F32)).astype(BF16)])


def _ssm_weights(lam_re, lam_im, log_dt, b_re, b_im, c_re, c_im, d_skip, w_glu, b_glu, g_ssm):
    ng, ns = lam_re.shape
    nc = b_re.shape[2]
    dt = jnp.exp(log_dt)[:, None]
    mag = jnp.exp(lam_re * dt)
    lre = mag * jnp.cos(lam_im * dt)
    lim = mag * jnp.sin(lam_im * dt)
    den = lam_re * lam_re + lam_im * lam_im
    fre = ((lre - 1.0) * lam_re + lim * lam_im) / den
    fim = (lim * lam_re - (lre - 1.0) * lam_im) / den
    bbar_re = fre[..., None] * b_re - fim[..., None] * b_im
    bbar_im = fre[..., None] * b_im + fim[..., None] * b_re
    eye = jnp.eye(ng // 2, dtype=F32)

    def in_blocks(b):
        b2 = b.reshape(2, ng // 2, ns, nc)
        return jnp.einsum('hgpc,gk->hgckp', b2, eye).reshape(2, ng // 2 * nc, ng // 2 * ns)

    def out_blocks(c):
        c2 = c.reshape(2, ng // 2, nc, ns)
        return jnp.einsum('hgcp,gk->hgpkc', c2, eye).reshape(2, ng // 2 * ns, ng // 2 * nc)

    ssm_w = ng * nc
    return {
        "bre": in_blocks(bbar_re), "bim": in_blocks(bbar_im),
        "cre": out_blocks(c_re), "cim": out_blocks(-c_im),
        "lre": lre.reshape(1, ng * ns), "lim": lim.reshape(1, ng * ns),
        "d": d_skip.reshape(1, ssm_w), "wglu": w_glu.astype(BF16), "bglu": b_glu.reshape(1, ssm_w),
        "g": g_ssm.reshape(1, ssm_w),
    }


def _route(logits):
    lane = lax.broadcasted_iota(jnp.int32, logits.shape, 1)
    big = jnp.int32(10 ** 6)
    is_group = lane < N_EXPERT_GROUPS
    gl = jnp.where(is_group, logits, NEG_INF)
    gmax = jnp.max(gl, axis=-1, keepdims=True)
    gidx = jnp.min(jnp.where(is_group & (gl == gmax), lane, big), axis=-1, keepdims=True)
    g_p = 1.0 / jnp.sum(jnp.exp(gl - gmax), axis=-1, keepdims=True)
    lo = N_EXPERT_GROUPS + gidx * EXPERTS_PER_GROUP
    in_group = (lane >= lo) & (lane < lo + EXPERTS_PER_GROUP)
    el = jnp.where(in_group, logits, NEG_INF)
    ee = jnp.exp(el - jnp.max(el, axis=-1, keepdims=True))
    prob = ee / jnp.sum(ee, axis=-1, keepdims=True)
    p1 = jnp.max(jnp.where(in_group, prob, -1.0), axis=-1, keepdims=True)
    i1 = jnp.min(jnp.where(in_group & (prob == p1), lane, big), axis=-1, keepdims=True)
    rest = in_group & (lane != i1)
    p2 = jnp.max(jnp.where(rest, prob, -1.0), axis=-1, keepdims=True)
    i2 = jnp.min(jnp.where(rest & (prob == p2), lane, big), axis=-1, keepdims=True)
    tot = p1 + p2
    return jnp.where(lane == i1, g_p * p1 / tot, jnp.where(lane == i2, g_p * p2 / tot, 0.0)), gidx


def _split3(x):
    a = x.astype(BF16)
    r = x - a.astype(F32)
    b = r.astype(BF16)
    return a, b, (r - b.astype(F32)).astype(BF16)


def _moe_sorted(hn, gates, gidx, tri_ref, wg_ref, wu_ref, wd_ref, xs_scr, gs_scr, ys_scr):
    rows, d = hn.shape
    lane = lax.broadcasted_iota(jnp.int32, gates.shape, 1)
    member = lane == gidx
    csum = jnp.dot(tri_ref[...], member.astype(BF16), preferred_element_type=F32)
    rank = jnp.sum(jnp.where(member, csum, 0.0), axis=-1, keepdims=True)
    counts = [jnp.sum((gidx == grp).astype(jnp.int32)) for grp in range(N_EXPERT_GROUPS - 1)]
    offs = [jnp.int32(0)]
    for c in counts:
        offs.append(offs[-1] + c)
    offs.append(jnp.int32(rows))
    base = jnp.zeros_like(gidx)
    for grp in range(1, N_EXPERT_GROUPS):
        base = jnp.where(gidx == grp, offs[grp], base)
    pos = base.astype(F32) + rank - 1.0
    col = lax.broadcasted_iota(jnp.int32, (rows, rows), 1).astype(F32)
    row = lax.broadcasted_iota(jnp.int32, (rows, rows), 0).astype(F32)
    unsort = (col == pos).astype(BF16)
    pos_lanes = jnp.broadcast_to(pos, (rows, LANES)).T[0:1, :]
    sort = (row == pos_lanes).astype(BF16)
    xs_scr[...] = jnp.dot(sort, hn.astype(BF16), preferred_element_type=F32).astype(BF16)
    gs_scr[...] = sum(jnp.dot(sort, part, preferred_element_type=F32) for part in _split3(gates))
    ys_scr[...] = jnp.zeros_like(ys_scr)
    blk = MOE_ROW_BLOCK
    n_blk = rows // blk
    cuts = [jnp.int32(k * blk) for k in range(1, n_blk)] + offs[1:N_EXPERT_GROUPS]
    assert len(cuts) == 6
    for a, b in ((0, 5), (1, 3), (2, 4), (1, 2), (3, 4), (0, 3), (2, 5), (0, 1), (2, 3), (4, 5), (1, 2), (3, 4)):
        cuts[a], cuts[b] = jnp.minimum(cuts[a], cuts[b]), jnp.maximum(cuts[a], cuts[b])
    cuts = [jnp.int32(0)] + cuts + [jnp.int32(rows)]
    lane_b = lax.broadcasted_iota(jnp.int32, (blk, LANES), 1)
    for j in range(len(cuts) - 1):
        start = cuts[j]
        live = (cuts[j + 1] > start).astype(F32)
        k = jnp.minimum(start // blk, n_blk - 1)
        grp = sum((offs[q] <= start).astype(jnp.int32) for q in range(1, N_EXPERT_GROUPS))
        sl = pl.ds(pl.multiple_of(k * blk, blk), blk)
        xb = xs_scr[sl, :]
        gb = gs_scr[sl, :] * live
        acts = []
        for i in range(EXPERTS_PER_GROUP):
            e = grp * EXPERTS_PER_GROUP + i
            gate = jnp.sum(jnp.where(lane_b == N_EXPERT_GROUPS + e, gb, 0.0), axis=-1, keepdims=True)
            hg = jnp.dot(xb, wg_ref[e], preferred_element_type=F32)
            hu = jnp.dot(xb, wu_ref[e], preferred_element_type=F32)
            acts.append((jax.nn.silu(hg) * hu * gate).astype(BF16))
        ys_scr[sl, :] += jnp.dot(jnp.concatenate(acts, axis=1), wd_ref[grp], preferred_element_type=F32)
    hi = ys_scr[...].astype(BF16)
    lo = (ys_scr[...] - hi.astype(F32)).astype(BF16)
    return jnp.dot(unsort, hi, preferred_element_type=F32) + jnp.dot(unsort, lo, preferred_element_type=F32)


def _ffn_body(x_ref, ssm_ref, attn_ref, mod_ref, gpost_ref, gpre_ref, gffn_ref, wout_ref, wr_ref, br_ref,
              tri_ref, wg_ref, wu_ref, wd_ref, o_ref, xs_scr, gs_scr, ys_scr, *, chunked):
    x = x_ref[...]
    g, r, d = x.shape
    rows = g * r
    gt1, sh2, sc2, gt2 = mod_ref[:, 2], mod_ref[:, 3], mod_ref[:, 4], mod_ref[:, 5]
    if chunked:
        ssm = jnp.swapaxes(ssm_ref[...].astype(F32), 0, 1).reshape(rows, ssm_ref.shape[-1]).astype(BF16)
    else:
        ssm = ssm_ref[...]
    cat = jnp.concatenate([ssm, attn_ref[...]], axis=1)
    mixed = jnp.dot(cat, wout_ref[...], preferred_element_type=F32)
    x1 = x + gt1 * _rms(mixed, gpost_ref[0]).reshape(g, r, d)
    hn = (_rms(x1, gpre_ref[...]) * (1.0 + sc2) + sh2).reshape(rows, d)
    hb = hn.astype(BF16)
    hlo = (hn - hb.astype(F32)).astype(BF16)
    logits = (jnp.dot(hb, wr_ref[0], preferred_element_type=F32) + jnp.dot(hlo, wr_ref[0], preferred_element_type=F32)
              + jnp.dot(hb, wr_ref[1], preferred_element_type=F32) + br_ref[...])
    gates, gidx = _route(logits)
    ffn = _moe_sorted(hn, gates, gidx, tri_ref, wg_ref, wu_ref, wd_ref, xs_scr, gs_scr, ys_scr)
    o_ref[...] = x1 + gt2 * _rms(ffn, gffn_ref[0]).reshape(g, r, d)


def _ffn(x3, ssm_n, attn_n, mod4, fw, gb, rb, chunk=None):
    nb, nr, d = x3.shape
    assert gb == 1 or rb == nr
    half = ssm_n.shape[-1]
    nj = nr // rb
    tm = gb * rb
    row_map = lambda i, j: (i * nj + j, 0)
    vec = _resident((1, 1, d))
    if chunk is None:
        ssm_spec = pl.BlockSpec((tm, half), row_map)
    else:
        assert gb == 1 and rb % chunk == 0
        ssm_spec = pl.BlockSpec((None, chunk, rb // chunk, half), lambda i, j: (i, 0, j, 0))
    return pl.pallas_call(
        functools.partial(_ffn_body, chunked=chunk is not None),
        out_shape=jax.ShapeDtypeStruct((nb, nr, d), F32),
        grid=(nb // gb, nj),
        in_specs=[pl.BlockSpec((gb, rb, d), lambda i, j: (i, j, 0)),
                  ssm_spec,
                  pl.BlockSpec((tm, attn_n.shape[1]), row_map),
                  pl.BlockSpec((gb, N_MOD, 1, d), lambda i, j: (i, 0, 0, 0)),
                  vec, vec, vec,
                  _resident(fw["wout"].shape), _resident(fw["wr"].shape), _resident(fw["br"].shape),
                  _resident((tm, tm)),
                  _resident(fw["wg"].shape), _resident(fw["wu"].shape), _resident(fw["wd"].shape)],
        out_specs=pl.BlockSpec((gb, rb, d), lambda i, j: (i, j, 0)),
        scratch_shapes=[pltpu.VMEM((tm, d), BF16), pltpu.VMEM((tm, LANES), F32), pltpu.VMEM((tm, d), F32)],
        compiler_params=_cparams("arbitrary", "arbitrary"),
        name="outproj_moe",
    )(x3, ssm_n, attn_n, mod4, fw["gpost"], fw["gpre"], fw["gffn"], fw["wout"], fw["wr"], fw["br"],
      jnp.tril(jnp.ones((tm, tm), BF16)), fw["wg"], fw["wu"], fw["wd"])


def _ffn_weights(g_post_mix, g_pre_ffn, g_post_ffn, w_out, w_gr, b_gr, w_er, b_er, w_gate, w_up, w_down):
    d = w_out.shape[0]
    n_route = N_EXPERT_GROUPS + N_EXPERTS
    wr = jnp.concatenate([w_gr, w_er.reshape(d, N_EXPERTS)], axis=1)
    br = jnp.concatenate([b_gr, b_er.reshape(N_EXPERTS)])
    wr = jnp.pad(wr, ((0, 0), (0, LANES - n_route)))
    return {
        "gpost": g_post_mix.reshape(1, 1, d), "gpre": g_pre_ffn.reshape(1, 1, d), "gffn": g_post_ffn.reshape(1, 1, d),
        "wout": w_out.astype(BF16),
        "wr": _hi_lo(wr),
        "br": jnp.pad(br, (0, LANES - n_route)).reshape(1, LANES),
        "wg": w_gate.astype(BF16), "wu": w_up.astype(BF16),
        "wd": w_down.astype(BF16).reshape(N_EXPERT_GROUPS, -1, d),
    }


def _rel_bucket(dist):
    max_exact = NUM_BUCKETS // 2
    dd = jnp.maximum(dist, 0)
    log_ratio = jnp.log(jnp.maximum(dd, 1).astype(F32) / max_exact) / math.log(MAX_DISTANCE / max_exact)
    large = jnp.minimum(max_exact + (log_ratio * (NUM_BUCKETS - max_exact)).astype(jnp.int32), NUM_BUCKETS - 1)
    return jnp.where(dd < max_exact, dd, large)


def _bias_and_band(dist, table):
    onehot = (_rel_bucket(dist)[:, :, None] == jnp.arange(NUM_BUCKETS)[None, None, :]).astype(F32)
    bias = jnp.einsum('qkb,bh->hqk', onehot, table.astype(F32), precision=HIGHEST)
    band = ((dist >= 0) & (dist <= WINDOW)).astype(F32)
    return bias, band


def _layer(xp, xs, ck, cv, h0re, h0im, mod_p, mod_s, table, p):
    nb, seq, d = xp.shape
    nbs, t, _ = xs.shape
    w = ck.shape[1]
    kv_w = N_KV_HEADS * HEAD_DIM
    w_in_bf = p["w_in"].astype(BF16)
    sw = _ssm_weights(p["ssm_lam_re"], p["ssm_lam_im"], p["ssm_log_dt"], p["ssm_b_re"], p["ssm_b_im"],
                      p["ssm_c_re"], p["ssm_c_im"], p["ssm_d"], p["w_glu"], p["b_glu"], p["g_ssm_out"])
    fw = _ffn_weights(p["g_post_mix"], p["g_pre_ffn"], p["g_post_ffn"], p["w_out"],
                      p["w_group_router"], p["b_group_router"], p["w_expert_router"], p["b_expert_router"],
                      p["w_exp_gate"], p["w_exp_up"], p["w_exp_down"])
    n_state = sw["lre"].shape[1]
    ng = p["ssm_lam_re"].shape[0]

    u2, q, k, v = _inproj(xp, mod_p, p["g_pre_mix"], w_in_bf, 1, INPROJ_TILE, chunk=SSM_CHUNK)
    blk = WINDOW
    dist_p = jnp.arange(blk)[:, None] + blk - jnp.arange(2 * blk)[None, :]
    bias_p, band_p = _bias_and_band(dist_p, table)
    attn_p = _attn_prompt(q, k, v, p["attn_sinks"], bias_p, band_p, p["g_attn_out"], nb, seq)
    kw = _ssm_chunk_weights(p["ssm_lam_re"], p["ssm_lam_im"], p["ssm_log_dt"], p["ssm_b_re"], p["ssm_b_im"],
                            p["ssm_c_re"], p["ssm_c_im"], SSM_CHUNK, seq // SSM_CHUNK)
    yt, hend = _ssm_chunked(_ssm_transpose(u2, p["ssm_b_re"].shape[2]), kw)
    ssm_p = _ssm_tail_call(yt, u2, sw)
    yp = _ffn(xp, ssm_p, attn_p, mod_p, fw, 1, TOKEN_TILE, chunk=SSM_CHUNK)
    wp = min(WINDOW, seq)
    k_p = k.reshape(nb, seq, kv_w)[:, seq - wp:].reshape(nb, wp, N_KV_HEADS, HEAD_DIM)
    v_p = v.reshape(nb, seq, kv_w)[:, seq - wp:].reshape(nb, wp, N_KV_HEADS, HEAD_DIM)
    n_p = n_state // ng
    hre_p = hend[:, :, 0, :n_p]
    him_p = hend[:, :, 0, n_p:]

    gs = TOKEN_TILE // t
    us, qs, ks, vs = _inproj(xs, mod_s, p["g_pre_mix"], w_in_bf, gs, t)
    dist_s = jnp.arange(t)[:, None] + w - jnp.arange(w + t)[None, :]
    bias_s, band_s = _bias_and_band(dist_s, table)
    attn_s, k_s, v_s = _attn_sample(qs.reshape(nbs, t, -1), ks.reshape(nbs, t, kv_w), vs.reshape(nbs, t, kv_w),
                                    ck.reshape(nbs, w, kv_w), cv.reshape(nbs, w, kv_w),
                                    p["attn_sinks"], bias_s, band_s, p["g_attn_out"], 16)
    u_tm = jnp.swapaxes(us.reshape(nbs, t, -1), 0, 1)
    ssm_tm, hre_s, him_s = _ssm_sample(u_tm, h0re.reshape(nbs, n_state), h0im.reshape(nbs, n_state), sw)
    ssm_s = jnp.swapaxes(ssm_tm, 0, 1).reshape(nbs * t, -1)
    ys = _ffn(xs, ssm_s, attn_s.reshape(nbs * t, -1), mod_s, fw, gs, t)
    k_s = k_s.reshape(nbs, w, N_KV_HEADS, HEAD_DIM)
    v_s = v_s.reshape(nbs, w, N_KV_HEADS, HEAD_DIM)
    hre_s = hre_s.reshape(nbs, ng, n_state // ng)
    him_s = him_s.reshape(nbs, ng, n_state // ng)
    return yp, ys, k_p, v_p, hre_p, him_p, k_s, v_s, hre_s, him_s


def kernel(x_prompt, x_sample, cache_k, cache_v, state_ssm_re, state_ssm_im, c_prompt, c_sample, rel_bias_table,
           w_ada, b_ada, g_pre_mix, g_post_mix, g_pre_ffn, g_post_ffn, w_in, ssm_lam_re, ssm_lam_im, ssm_log_dt,
           ssm_b_re, ssm_b_im, ssm_c_re, ssm_c_im, ssm_d, w_glu, b_glu, attn_sinks, g_ssm_out, g_attn_out, w_out,
           w_group_router, b_group_router, w_expert_router, b_expert_router, w_exp_gate, w_exp_up, w_exp_down):
    params = dict(
        w_ada=w_ada, b_ada=b_ada, g_pre_mix=g_pre_mix, g_post_mix=g_post_mix, g_pre_ffn=g_pre_ffn,
        g_post_ffn=g_post_ffn, w_in=w_in, ssm_lam_re=ssm_lam_re, ssm_lam_im=ssm_lam_im, ssm_log_dt=ssm_log_dt,
        ssm_b_re=ssm_b_re, ssm_b_im=ssm_b_im, ssm_c_re=ssm_c_re, ssm_c_im=ssm_c_im, ssm_d=ssm_d, w_glu=w_glu,
        b_glu=b_glu, attn_sinks=attn_sinks, g_ssm_out=g_ssm_out, g_attn_out=g_attn_out, w_out=w_out,
        w_group_router=w_group_router, b_group_router=b_group_router, w_expert_router=w_expert_router,
        b_expert_router=b_expert_router, w_exp_gate=w_exp_gate, w_exp_up=w_exp_up, w_exp_down=w_exp_down)
    depth = w_in.shape[0]
    nb, nbs = x_prompt.shape[0], x_sample.shape[0]
    d = x_prompt.shape[2]
    pad = (-nb) % 8
    yp, ys = x_prompt, x_sample
    outs = [[] for _ in range(8)]
    for li in range(depth):
        p = {name: val[li] for name, val in params.items()}
        c_all = jnp.concatenate([c_prompt, jnp.zeros((pad, d), F32), c_sample], axis=0)
        mod = _modulation(c_all, p["w_ada"], p["b_ada"])
        mod_p = mod[:nb].reshape(nb, N_MOD, 1, d)
        mod_s = mod[nb + pad:].reshape(nbs, N_MOD, 1, d)
        res = _layer(yp, ys, cache_k[li], cache_v[li], state_ssm_re[li], state_ssm_im[li], mod_p, mod_s,
                     rel_bias_table, p)
        yp, ys = res[0], res[1]
        for acc, val in zip(outs, res[2:]):
            acc.append(val)
    return (yp, ys) + tuple(jnp.stack(o) for o in outs)
```

```python
import functools
import math

import jax
import jax.numpy as jnp
from jax import lax
from jax.experimental import pallas as pl
from jax.experimental.pallas import tpu as pltpu

F32 = jnp.float32
BF16 = jnp.bfloat16
HIGHEST = lax.Precision.HIGHEST

EPS = 1e-6
NEG_INF = -1e30

SSM_GROUP = 16
SSM_STATE = 64
HEAD_DIM = 64
N_KV_HEADS = 2
GQA = 4
N_HEADS = N_KV_HEADS * GQA
WINDOW = 128
NUM_BUCKETS = 32
MAX_DISTANCE = 128
N_EXPERT_GROUPS = 4
EXPERTS_PER_GROUP = 4
N_EXPERTS = N_EXPERT_GROUPS * EXPERTS_PER_GROUP
N_MOD = 6

LANES = 128
SUBLANES = 8
VMEM_LIMIT_BYTES = 56 * 1024 * 1024

TOKEN_TILE = 512
INPROJ_TILE = 1024
ATTN_TILE = 512
SSM_CHUNK = 32
SSM_GROUP_BLOCK = 8
SSM_TAIL_BLOCK = 4
MOE_ROW_BLOCK = 128
ROUTE_ROWS = 32


def _cparams(*sem):
    return pltpu.CompilerParams(dimension_semantics=sem, vmem_limit_bytes=VMEM_LIMIT_BYTES)


def _resident(shape):
    zeros = (0,) * len(shape)
    return pl.BlockSpec(shape, lambda *_: zeros, pipeline_mode=pl.Buffered(1))


def _rms(x, g):
    return x * lax.rsqrt(jnp.mean(x * x, axis=-1, keepdims=True) + EPS) * g


def _mod_body(c_ref, w_ref, b_ref, o_ref):
    a = jax.nn.silu(c_ref[...])
    w = w_ref[...]
    a_hi, w_hi = a.astype(BF16), w.astype(BF16)
    a_lo, w_lo = (a - a_hi.astype(F32)).astype(BF16), (w - w_hi.astype(F32)).astype(BF16)
    dot = functools.partial(jnp.dot, preferred_element_type=F32)
    o_ref[...] = dot(a_hi, w_hi) + dot(a_lo, w_hi) + dot(a_hi, w_lo) + b_ref[...]


def _modulation(c, w_ada, b_ada):
    rows, d = c.shape
    n = w_ada.shape[1]
    tn = 1024
    return pl.pallas_call(
        _mod_body,
        out_shape=jax.ShapeDtypeStruct((rows, n), F32),
        grid=(n // tn,),
        in_specs=[pl.BlockSpec((rows, d), lambda j: (0, 0)),
                  pl.BlockSpec((d, tn), lambda j: (0, j)),
                  pl.BlockSpec((1, tn), lambda j: (0, j))],
        out_specs=pl.BlockSpec((rows, tn), lambda j: (0, j)),
        compiler_params=_cparams("arbitrary"),
        name="modulation",
    )(c, w_ada, b_ada.reshape(1, n))


def _inproj_body(x_ref, mod_ref, g_ref, w_ref, u_ref, q_ref, k_ref, v_ref, *, chunked):
    x = x_ref[...]
    g, r, d = x.shape
    hn = _rms(x, g_ref[...]) * (1.0 + mod_ref[:, 1]) + mod_ref[:, 0]
    hn = hn.reshape(g * r, d).astype(BF16)
    proj = jnp.dot(hn, w_ref[...], preferred_element_type=F32)
    ssm_w = u_ref.shape[-1]
    attn_w = q_ref.shape[1]
    kv_w = k_ref.shape[1]
    if chunked:
        t, nc, _ = u_ref.shape
        u_ref[...] = jnp.swapaxes(proj[:, :ssm_w].reshape(nc, t, ssm_w), 0, 1)
    else:
        u_ref[...] = proj[:, :ssm_w]
    q_ref[...] = (proj[:, ssm_w:ssm_w + attn_w] * (HEAD_DIM ** -0.5)).astype(BF16)
    k_ref[...] = proj[:, ssm_w + attn_w:ssm_w + attn_w + kv_w]
    v_ref[...] = proj[:, ssm_w + attn_w + kv_w:]


def _inproj(x3, mod4, g_pre, w_in_bf, gb, rb, chunk=None):
    nb, nr, d = x3.shape
    assert gb == 1 or rb == nr
    n_in = w_in_bf.shape[1]
    kv_w = N_KV_HEADS * HEAD_DIM
    attn_w = N_HEADS * HEAD_DIM
    ssm_w = n_in - attn_w - 2 * kv_w
    rows = nb * nr
    nj = nr // rb
    tm = gb * rb
    out_map = lambda i, j: (i * nj + j, 0)
    if chunk is None:
        u_shape = jax.ShapeDtypeStruct((rows, ssm_w), F32)
        u_spec = pl.BlockSpec((tm, ssm_w), out_map)
    else:
        assert gb == 1 and rb % chunk == 0
        u_shape = jax.ShapeDtypeStruct((nb, chunk, nr // chunk, ssm_w), F32)
        u_spec = pl.BlockSpec((None, chunk, rb // chunk, ssm_w), lambda i, j: (i, 0, j, 0))
    return pl.pallas_call(
        functools.partial(_inproj_body, chunked=chunk is not None),
        out_shape=(u_shape,
                   jax.ShapeDtypeStruct((rows, attn_w), BF16),
                   jax.ShapeDtypeStruct((rows, kv_w), F32),
                   jax.ShapeDtypeStruct((rows, kv_w), F32)),
        grid=(nb // gb, nj),
        in_specs=[pl.BlockSpec((gb, rb, d), lambda i, j: (i, j, 0)),
                  pl.BlockSpec((gb, 2, 1, d), lambda i, j: (i, 0, 0, 0)),
                  _resident((1, 1, d)),
                  _resident((d, n_in))],
        out_specs=(u_spec,
                   pl.BlockSpec((tm, attn_w), out_map),
                   pl.BlockSpec((tm, kv_w), out_map),
                   pl.BlockSpec((tm, kv_w), out_map)),
        compiler_params=_cparams("arbitrary", "arbitrary"),
        name="in_projection",
    )(x3, mod4, g_pre.reshape(1, 1, d), w_in_bf)


def _attn_prompt_body(sinks_ref, q_ref, kc_ref, vc_ref, kp_ref, vp_ref, bias_ref, g_ref, o_ref, s_scr, p_scr):
    tq = q_ref.shape[0]
    blk = bias_ref.shape[2] // 2
    first_tile = pl.program_id(1) == 0
    kk = jnp.concatenate([kp_ref[...], kc_ref[...]], axis=0).astype(BF16)
    vv = jnp.concatenate([vp_ref[...], vc_ref[...]], axis=0).astype(BF16)
    shape = (GQA * blk, 2 * blk)
    col = lax.broadcasted_iota(jnp.int32, shape, 1)
    row = lax.broadcasted_iota(jnp.int32, (GQA * blk, 1), 0)
    no_prev = first_tile & (col < blk)
    sink_cols = []
    for hk in range(N_KV_HEADS):
        sk = jnp.full((GQA * blk, 1), sinks_ref[hk * GQA], F32)
        for gq in range(1, GQA):
            sk = jnp.where(row >= gq * blk, sinks_ref[hk * GQA + gq], sk)
        sink_cols.append(sk)
    pairs = [(j, hk) for j in range(tq // blk) for hk in range(N_KV_HEADS)]
    for idx, (j, hk) in enumerate(pairs):
        qg = jnp.concatenate([q_ref[j * blk:(j + 1) * blk, h * HEAD_DIM:(h + 1) * HEAD_DIM]
                              for h in range(hk * GQA, (hk + 1) * GQA)], axis=0)
        s = lax.dot_general(qg, kk[j * blk:(j + 2) * blk, hk * HEAD_DIM:(hk + 1) * HEAD_DIM],
                            (((1,), (1,)), ((), ())), preferred_element_type=F32) + bias_ref[hk]
        s_scr[idx] = jnp.where(no_prev, NEG_INF, s) if j == 0 else s
    dens = []
    for idx, (j, hk) in enumerate(pairs):
        s = s_scr[idx]
        m = jnp.maximum(jnp.max(s, axis=-1, keepdims=True), sink_cols[hk])
        p = jnp.exp(s - m)
        dens.append(jnp.sum(p, axis=-1, keepdims=True) + jnp.exp(sink_cols[hk] - m))
        p_scr[idx] = p.astype(BF16)
    outs = {}
    for idx, (j, hk) in enumerate(pairs):
        o = jnp.dot(p_scr[idx], vv[j * blk:(j + 2) * blk, hk * HEAD_DIM:(hk + 1) * HEAD_DIM],
                    preferred_element_type=F32) / dens[idx]
        for gq in range(GQA):
            outs[j, hk * GQA + gq] = o[gq * blk:(gq + 1) * blk]
    for j in range(tq // blk):
        att = jnp.concatenate([outs[j, h] for h in range(N_HEADS)], axis=1)
        o_ref[j * blk:(j + 1) * blk, :] = _rms(att, g_ref[...]).astype(BF16)


def _attn_prompt(q, k, v, sinks, bias, band, g_attn, nb, seq):
    rows, attn_w = q.shape
    kv_w = k.shape[1]
    blk = band.shape[0]
    bias = jnp.where(band > 0.5, bias, NEG_INF).reshape(N_KV_HEADS, GQA * blk, 2 * blk)
    tq = ATTN_TILE
    nt = seq // tq
    per = tq // blk
    cur = lambda b, i: (b * nt + i, 0)
    prev = lambda b, i: (b * nt * per + jnp.maximum(i * per - 1, 0), 0)
    return pl.pallas_call(
        _attn_prompt_body,
        out_shape=jax.ShapeDtypeStruct((rows, attn_w), BF16),
        grid=(nb, nt),
        in_specs=[pl.BlockSpec(memory_space=pltpu.SMEM),
                  pl.BlockSpec((tq, attn_w), cur),
                  pl.BlockSpec((tq, kv_w), cur),
                  pl.BlockSpec((tq, kv_w), cur),
                  pl.BlockSpec((blk, kv_w), prev),
                  pl.BlockSpec((blk, kv_w), prev),
                  _resident(bias.shape),
                  _resident((1, attn_w))],
        out_specs=pl.BlockSpec((tq, attn_w), cur),
        scratch_shapes=[pltpu.VMEM((per * N_KV_HEADS, GQA * blk, 2 * blk), F32),
                        pltpu.VMEM((per * N_KV_HEADS, GQA * blk, 2 * blk), BF16)],
        compiler_params=_cparams("arbitrary", "arbitrary"),
        name="attention_prompt",
    )(sinks, q, k, v, k, v, bias, g_attn.reshape(1, attn_w))


def _attn_sample_body(sinks_ref, q_ref, kn_ref, vn_ref, ck_ref, cv_ref, bias_ref, g_ref, o_ref, ko_ref, vo_ref):
    t = q_ref.shape[1]
    kk = jnp.concatenate([ck_ref[...], kn_ref[...]], axis=1)
    vv = jnp.concatenate([cv_ref[...], vn_ref[...]], axis=1)
    ko_ref[...] = kk[:, t:, :]
    vo_ref[...] = vv[:, t:, :]
    kkb = kk.astype(BF16)
    vvb = vv.astype(BF16)
    row = lax.broadcasted_iota(jnp.int32, (GQA * t, 1), 0)
    outs = [None] * N_HEADS
    for hk in range(N_KV_HEADS):
        sk = jnp.full((GQA * t, 1), sinks_ref[hk * GQA], F32)
        for gq in range(1, GQA):
            sk = jnp.where(row >= gq * t, sinks_ref[hk * GQA + gq], sk)
        qg = jnp.concatenate([q_ref[:, :, h * HEAD_DIM:(h + 1) * HEAD_DIM]
                              for h in range(hk * GQA, (hk + 1) * GQA)], axis=1)
        s = jnp.einsum('bqd,bkd->bqk', qg, kkb[:, :, hk * HEAD_DIM:(hk + 1) * HEAD_DIM],
                       preferred_element_type=F32) + bias_ref[hk]
        m = jnp.maximum(jnp.max(s, axis=-1, keepdims=True), sk)
        p = jnp.exp(s - m)
        den = jnp.sum(p, axis=-1, keepdims=True) + jnp.exp(sk - m)
        o = jnp.einsum('bqk,bkd->bqd', p.astype(BF16), vvb[:, :, hk * HEAD_DIM:(hk + 1) * HEAD_DIM],
                       preferred_element_type=F32) / den
        for gq in range(GQA):
            outs[hk * GQA + gq] = o[:, gq * t:(gq + 1) * t, :]
    att = jnp.concatenate(outs, axis=2)
    o_ref[...] = _rms(att, g_ref[...]).astype(BF16)


def _attn_sample(q3, k3, v3, ck, cv, sinks, bias, band, g_attn, gb):
    nb, t, attn_w = q3.shape
    w, kv_w = ck.shape[1], ck.shape[2]
    bias = jnp.where(band > 0.5, bias, NEG_INF).reshape(N_KV_HEADS, GQA * t, w + t)
    blk3 = lambda last: pl.BlockSpec((gb, last[0], last[1]), lambda i: (i, 0, 0))
    return pl.pallas_call(
        _attn_sample_body,
        out_shape=(jax.ShapeDtypeStruct((nb, t, attn_w), BF16),
                   jax.ShapeDtypeStruct((nb, w, kv_w), F32),
                   jax.ShapeDtypeStruct((nb, w, kv_w), F32)),
        grid=(nb // gb,),
        in_specs=[pl.BlockSpec(memory_space=pltpu.SMEM),
                  blk3((t, attn_w)), blk3((t, kv_w)), blk3((t, kv_w)),
                  blk3((w, kv_w)), blk3((w, kv_w)),
                  _resident(bias.shape), _resident((1, 1, attn_w))],
        out_specs=(blk3((t, attn_w)), blk3((w, kv_w)), blk3((w, kv_w))),
        compiler_params=_cparams("arbitrary"),
        name="attention_sample",
    )(sinks, q3, k3, v3, ck, cv, bias, g_attn.reshape(1, 1, attn_w))


def _ssm_tail(y, u, d_ref, wglu_ref, bglu_ref, g_ref):
    z = jax.nn.gelu(y + d_ref[...] * u)
    gate = jax.nn.sigmoid(jnp.dot(z.astype(BF16), wglu_ref[...], preferred_element_type=F32) + bglu_ref[...])
    return _rms(z * gate, g_ref[...]).astype(BF16)


def _ssm_chunk_body(ut_ref, m_ref, e_ref, f_ref, w1_ref, w2_ref, yt_ref, hend_ref, s_scr):
    t, gb, ch, nc = ut_ref.shape
    n_levels = w1_ref.shape[1]
    two_p = e_ref.shape[1]
    row = lax.broadcasted_iota(jnp.int32, (nc, two_p), 0)
    rs = [ut_ref[:, gl].reshape(t * ch, nc) for gl in range(gb)]
    hs = []
    for gl in range(gb):
        s_scr[gl] = jnp.dot(e_ref[gl], rs[gl], preferred_element_type=F32)
        hs.append(s_scr[gl].T)

    def conv(gl):
        yt_ref[:, gl * ch:(gl + 1) * ch, :] = jnp.dot(m_ref[gl], rs[gl], preferred_element_type=F32).reshape(t, ch, nc)

    for lv in range(n_levels):
        sh = 1 << lv
        for gl in range(gb):
            prev = jnp.where(row >= sh, pltpu.roll(hs[gl], sh, axis=0), 0.0)
            hs[gl] = hs[gl] + w1_ref[gl, lv] * prev + w2_ref[gl, lv] * pltpu.roll(prev, two_p // 2, axis=1)
        for gl in range(lv * gb // n_levels, (lv + 1) * gb // n_levels):
            conv(gl)
    for gl in range(gb):
        hend_ref[gl] = hs[gl][nc - 1:nc, :]
        s_scr[gl] = jnp.where(row >= 1, pltpu.roll(hs[gl], 1, axis=0), 0.0).T
        yt_ref[:, gl * ch:(gl + 1) * ch, :] += jnp.dot(f_ref[gl], s_scr[gl].astype(BF16),
                                                       preferred_element_type=F32).reshape(t, ch, nc)


def _ssm_transpose_body(u_ref, o_ref):
    for s in range(u_ref.shape[0]):
        o_ref[s] = u_ref[s].T.astype(BF16).reshape(o_ref.shape[1:])


def _ssm_transpose(u2, ch):
    nb, t, nc, ssm_w = u2.shape
    tb = SSM_TAIL_BLOCK
    return pl.pallas_call(
        _ssm_transpose_body,
        out_shape=jax.ShapeDtypeStruct((nb, t, ssm_w // ch, ch, nc), BF16),
        grid=(nb, t // tb),
        in_specs=[pl.BlockSpec((None, tb, nc, ssm_w), lambda b, i: (b, i, 0, 0))],
        out_specs=pl.BlockSpec((None, tb, ssm_w // ch, ch, nc), lambda b, i: (b, i, 0, 0, 0)),
        compiler_params=_cparams("arbitrary", "arbitrary"),
        name="ssm_transpose",
    )(u2)


def _ssm_chunked(ut, kw):
    nb, t, ng, ch, nc = ut.shape
    tc = t * ch
    two_p = kw["e"].shape[1]
    n_levels = kw["w1"].shape[1]
    gb = SSM_GROUP_BLOCK
    grp = lambda shape: pl.BlockSpec((gb,) + shape, lambda b, g: (g,) + (0,) * len(shape))
    return pl.pallas_call(
        _ssm_chunk_body,
        out_shape=(jax.ShapeDtypeStruct((nb, t, ng * ch, nc), F32),
                   jax.ShapeDtypeStruct((nb, ng, 1, two_p), F32)),
        grid=(nb, ng // gb),
        in_specs=[pl.BlockSpec((None, t, gb, ch, nc), lambda b, g: (b, 0, g, 0, 0)),
                  grp((tc, tc)), grp((two_p, tc)), grp((tc, two_p)),
                  grp((n_levels, 1, two_p)), grp((n_levels, 1, two_p))],
        out_specs=(pl.BlockSpec((None, t, gb * ch, nc), lambda b, g: (b, 0, g, 0)),
                   pl.BlockSpec((None, gb, 1, two_p), lambda b, g: (b, g, 0, 0))),
        scratch_shapes=[pltpu.VMEM((gb, two_p, nc), F32)],
        compiler_params=_cparams("arbitrary", "arbitrary"),
        name="ssm_chunked",
    )(ut, kw["m"], kw["e"], kw["f"], kw["w1"], kw["w2"])


def _ssm_tail_body(yt_ref, u_ref, d_ref, wglu_ref, bglu_ref, g_ref, o_ref):
    for i in range(yt_ref.shape[0]):
        o_ref[i] = _ssm_tail(yt_ref[i].T, u_ref[i], d_ref, wglu_ref, bglu_ref, g_ref)


def _ssm_tail_call(yt, u2, sw):
    nb, t, nc, ssm_w = u2.shape
    tb = SSM_TAIL_BLOCK
    return pl.pallas_call(
        _ssm_tail_body,
        out_shape=jax.ShapeDtypeStruct((nb, t, nc, ssm_w), BF16),
        grid=(nb, t // tb),
        in_specs=[pl.BlockSpec((None, tb, ssm_w, nc), lambda b, i: (b, i, 0, 0)),
                  pl.BlockSpec((None, tb, nc, ssm_w), lambda b, i: (b, i, 0, 0)),
                  _resident((1, ssm_w)), _resident((ssm_w, ssm_w)), _resident((1, ssm_w)), _resident((1, ssm_w))],
        out_specs=pl.BlockSpec((None, tb, nc, ssm_w), lambda b, i: (b, i, 0, 0)),
        compiler_params=_cparams("arbitrary", "arbitrary"),
        name="ssm_tail",
    )(yt, u2, sw["d"], sw["wglu"], sw["bglu"], sw["g"])


def _lag_matrix_body(z_ref, m_ref, *, t):
    gb, ch, _ = z_ref.shape
    width = t * ch
    for gl in range(gb):
        z = z_ref[gl]
        for r in range(t):
            off = (t - 1 - r) * ch
            m_ref[gl, r * ch:(r + 1) * ch, :] = z[:, off:off + width].astype(BF16)


def _lag_matrix(z, t):
    ng, ch, zw = z.shape
    gb = SSM_GROUP_BLOCK
    return pl.pallas_call(
        functools.partial(_lag_matrix_body, t=t),
        out_shape=jax.ShapeDtypeStruct((ng, t * ch, zw // 2), BF16),
        grid=(ng // gb,),
        in_specs=[pl.BlockSpec((gb, ch, zw), lambda g: (g, 0, 0))],
        out_specs=pl.BlockSpec((gb, t * ch, zw // 2), lambda g: (g, 0, 0)),
        compiler_params=_cparams("arbitrary"),
        name="ssm_lag_matrix",
    )(z)


def _ssm_chunk_weights(lam_re, lam_im, log_dt, b_re, b_im, c_re, c_im, t, n_chunks):
    ng, ns = lam_re.shape
    nc = b_re.shape[2]
    dt = jnp.exp(log_dt)[:, None]
    are, aim = lam_re * dt, lam_im * dt
    d = jnp.arange(t + 1, dtype=F32)[:, None, None]
    mag = jnp.exp(d * are)
    pre, pim = mag * jnp.cos(d * aim), mag * jnp.sin(d * aim)
    lre, lim = pre[1], pim[1]
    den = lam_re * lam_re + lam_im * lam_im
    fre = ((lre - 1.0) * lam_re + lim * lam_im) / den
    fim = (lim * lam_re - (lre - 1.0) * lam_im) / den
    bbr = fre[..., None] * b_re - fim[..., None] * b_im
    bbi = fre[..., None] * b_im + fim[..., None] * b_re
    pre_g, pim_g = jnp.transpose(pre, (1, 0, 2)), jnp.transpose(pim, (1, 0, 2))
    xr = c_re[:, None] * pre_g[:, :, None, :] - c_im[:, None] * pim_g[:, :, None, :]
    xi = c_re[:, None] * pim_g[:, :, None, :] + c_im[:, None] * pre_g[:, :, None, :]
    kern = jnp.einsum('gdap,gpc->gdac', jnp.concatenate([xr[:, :t], -xi[:, :t]], axis=3),
                      jnp.concatenate([bbr, bbi], axis=1), precision=HIGHEST)
    lag_rows = jnp.transpose(kern[:, ::-1], (0, 2, 1, 3)).reshape(ng, nc, t * nc)
    m = _lag_matrix(jnp.concatenate([lag_rows, jnp.zeros_like(lag_rows)], axis=2), t)
    rev_re = jnp.transpose(pre_g[:, t - 1::-1][:, :t], (0, 2, 1))
    rev_im = jnp.transpose(pim_g[:, t - 1::-1][:, :t], (0, 2, 1))
    er = rev_re[..., None] * bbr[:, :, None, :] - rev_im[..., None] * bbi[:, :, None, :]
    ei = rev_re[..., None] * bbi[:, :, None, :] + rev_im[..., None] * bbr[:, :, None, :]
    e = jnp.concatenate([er.reshape(ng, ns, t * nc), ei.reshape(ng, ns, t * nc)], axis=1).astype(BF16)
    f = jnp.concatenate([xr[:, 1:].reshape(ng, t * nc, ns), -xi[:, 1:].reshape(ng, t * nc, ns)], axis=2).astype(BF16)
    wr, wi = pre[t], pim[t]
    w1, w2 = [], []
    for _ in range(max(1, (n_chunks - 1).bit_length())):
        w1.append(jnp.concatenate([wr, wr], axis=1))
        w2.append(jnp.concatenate([-wi, wi], axis=1))
        wr, wi = wr * wr - wi * wi, 2.0 * wr * wi
    w1 = jnp.stack(w1, axis=1)[:, :, None, :]
    w2 = jnp.stack(w2, axis=1)[:, :, None, :]
    return {"m": m, "e": e, "f": f, "w1": w1, "w2": w2}


def _ssm_sample_body(u_ref, h0re_ref, h0im_ref, bre_ref, bim_ref, cre_ref, cim_ref, lre_ref, lim_ref,
                     d_ref, wglu_ref, bglu_ref, g_ref, o_ref, hre_ref, him_ref):
    steps = u_ref.shape[0]
    half_in = bre_ref.shape[2]
    dot = functools.partial(jnp.dot, preferred_element_type=F32)
    ar = lre_ref[...]
    ai = lim_ref[...]
    hr = h0re_ref[...]
    hi = h0im_ref[...]
    for t in range(steps):
        u = u_ref[t]
        u_hi = u.astype(BF16)
        u_lo = (u - u_hi.astype(F32)).astype(BF16)

        def bu(b_ref):
            cols = []
            for hf in range(2):
                lanes = slice(hf * half_in, (hf + 1) * half_in)
                cols.append(dot(u_hi[:, lanes], b_ref[0, hf]) + dot(u_lo[:, lanes], b_ref[0, hf])
                            + dot(u_hi[:, lanes], b_ref[1, hf]))
            return jnp.concatenate(cols, axis=1)

        hr, hi = ar * hr - ai * hi + bu(bre_ref), ar * hi + ai * hr + bu(bim_ref)
        half_st = hr.shape[1] // 2
        hrb, hib = hr.astype(BF16), hi.astype(BF16)
        y = jnp.concatenate(
            [dot(hrb[:, hf * half_st:(hf + 1) * half_st], cre_ref[hf])
             + dot(hib[:, hf * half_st:(hf + 1) * half_st], cim_ref[hf]) for hf in range(2)], axis=1)
        o_ref[t] = _ssm_tail(y, u, d_ref, wglu_ref, bglu_ref, g_ref)
    hre_ref[...] = hr
    him_ref[...] = hi


def _ssm_sample(u_tm, h0re, h0im, sw):
    steps, nb, ssm_w = u_tm.shape
    n_state = h0re.shape[1]
    return pl.pallas_call(
        _ssm_sample_body,
        out_shape=(jax.ShapeDtypeStruct((steps, nb, ssm_w), BF16),
                   jax.ShapeDtypeStruct((nb, n_state), F32),
                   jax.ShapeDtypeStruct((nb, n_state), F32)),
        compiler_params=pltpu.CompilerParams(vmem_limit_bytes=VMEM_LIMIT_BYTES),
        name="ssm_sample",
    )(u_tm, h0re, h0im, _hi_lo(sw["bre"]), _hi_lo(sw["bim"]), sw["cre"].astype(BF16), sw["cim"].astype(BF16),
      sw["lre"], sw["lim"], sw["d"], sw["wglu"], sw["bglu"], sw["g"])


def _hi_lo(w):
    hi = w.astype(BF16)
    return jnp.stack([hi, (w - hi.astype(F32)).astype(BF16)])


def _ssm_weights(lam_re, lam_im, log_dt, b_re, b_im, c_re, c_im, d_skip, w_glu, b_glu, g_ssm):
    ng, ns = lam_re.shape
    nc = b_re.shape[2]
    dt = jnp.exp(log_dt)[:, None]
    mag = jnp.exp(lam_re * dt)
    lre = mag * jnp.cos(lam_im * dt)
    lim = mag * jnp.sin(lam_im * dt)
    den = lam_re * lam_re + lam_im * lam_im
    fre = ((lre - 1.0) * lam_re + lim * lam_im) / den
    fim = (lim * lam_re - (lre - 1.0) * lam_im) / den
    bbar_re = fre[..., None] * b_re - fim[..., None] * b_im
    bbar_im = fre[..., None] * b_im + fim[..., None] * b_re
    eye = jnp.eye(ng // 2, dtype=F32)

    def in_blocks(b):
        b2 = b.reshape(2, ng // 2, ns, nc)
        return jnp.einsum('hgpc,gk->hgckp', b2, eye).reshape(2, ng // 2 * nc, ng // 2 * ns)

    def out_blocks(c):
        c2 = c.reshape(2, ng // 2, nc, ns)
        return jnp.einsum('hgcp,gk->hgpkc', c2, eye).reshape(2, ng // 2 * ns, ng // 2 * nc)

    ssm_w = ng * nc
    return {
        "bre": in_blocks(bbar_re), "bim": in_blocks(bbar_im),
        "cre": out_blocks(c_re), "cim": out_blocks(-c_im),
        "lre": lre.reshape(1, ng * ns), "lim": lim.reshape(1, ng * ns),
        "d": d_skip.reshape(1, ssm_w), "wglu": w_glu.astype(BF16), "bglu": b_glu.reshape(1, ssm_w),
        "g": g_ssm.reshape(1, ssm_w),
    }


def _route(logits_t):
    big = jnp.int32(10 ** 6)
    gl = logits_t[:N_EXPERT_GROUPS]
    grow = lax.broadcasted_iota(jnp.int32, gl.shape, 0)
    gmax = jnp.max(gl, axis=0, keepdims=True)
    gidx = jnp.min(jnp.where(gl == gmax, grow, big), axis=0, keepdims=True)
    g_p = 1.0 / jnp.sum(jnp.exp(gl - gmax), axis=0, keepdims=True)
    ex = logits_t[N_EXPERT_GROUPS:N_EXPERT_GROUPS + N_EXPERTS]
    erow = lax.broadcasted_iota(jnp.int32, ex.shape, 0)
    lo = gidx * EXPERTS_PER_GROUP
    in_group = (erow >= lo) & (erow < lo + EXPERTS_PER_GROUP)
    el = jnp.where(in_group, ex, NEG_INF)
    ee = jnp.exp(el - jnp.max(el, axis=0, keepdims=True))
    prob = ee / jnp.sum(ee, axis=0, keepdims=True)
    p1 = jnp.max(jnp.where(in_group, prob, -1.0), axis=0, keepdims=True)
    i1 = jnp.min(jnp.where(in_group & (prob == p1), erow, big), axis=0, keepdims=True)
    rest = in_group & (erow != i1)
    p2 = jnp.max(jnp.where(rest, prob, -1.0), axis=0, keepdims=True)
    i2 = jnp.min(jnp.where(rest & (prob == p2), erow, big), axis=0, keepdims=True)
    tot = p1 + p2
    return jnp.where(erow == i1, g_p * p1 / tot, jnp.where(erow == i2, g_p * p2 / tot, 0.0)), gidx


def _split3(x):
    a = x.astype(BF16)
    r = x - a.astype(F32)
    b = r.astype(BF16)
    return a, b, (r - b.astype(F32)).astype(BF16)


def _moe_sorted(hb, gates_t, gidx, tri_ref, wg_ref, wu_ref, wd_ref, xs_scr, gs_scr, ys_scr):
    rows, d = hb.shape
    grow = lax.broadcasted_iota(jnp.int32, (SUBLANES, rows), 0)
    member = grow == gidx
    csum = jnp.dot(member.astype(BF16), tri_ref[...], preferred_element_type=F32)
    rank = jnp.sum(jnp.where(member, csum, 0.0), axis=0, keepdims=True)
    counts = [jnp.sum((gidx == grp).astype(jnp.int32)) for grp in range(N_EXPERT_GROUPS - 1)]
    offs = [jnp.int32(0)]
    for c in counts:
        offs.append(offs[-1] + c)
    offs.append(jnp.int32(rows))
    base = jnp.zeros_like(gidx)
    for grp in range(1, N_EXPERT_GROUPS):
        base = jnp.where(gidx == grp, offs[grp], base)
    pos_lanes = base.astype(F32) + rank - 1.0
    pos = jnp.broadcast_to(pos_lanes, (SUBLANES, rows)).T[:, 0:1]
    col = lax.broadcasted_iota(jnp.int32, (rows, rows), 1).astype(F32)
    row = lax.broadcasted_iota(jnp.int32, (rows, rows), 0).astype(F32)
    unsort = (col == pos).astype(BF16)
    sort = (row == pos_lanes).astype(BF16)
    xs_scr[...] = jnp.dot(sort, hb, preferred_element_type=F32).astype(BF16)
    pad = jnp.zeros((LANES - N_EXPERT_GROUPS - N_EXPERTS, rows), F32)
    gates_pad = jnp.concatenate([jnp.zeros((N_EXPERT_GROUPS, rows), F32), gates_t, pad], axis=0)
    gs_t = sum(jnp.dot(part, unsort, preferred_element_type=F32) for part in _split3(gates_pad))
    gs_scr[...] = gs_t.T
    ys_scr[...] = jnp.zeros_like(ys_scr)
    blk = MOE_ROW_BLOCK
    n_blk = rows // blk
    cuts = [jnp.int32(k * blk) for k in range(1, n_blk)] + offs[1:N_EXPERT_GROUPS]
    assert len(cuts) == 6
    for a, b in ((0, 5), (1, 3), (2, 4), (1, 2), (3, 4), (0, 3), (2, 5), (0, 1), (2, 3), (4, 5), (1, 2), (3, 4)):
        cuts[a], cuts[b] = jnp.minimum(cuts[a], cuts[b]), jnp.maximum(cuts[a], cuts[b])
    cuts = [jnp.int32(0)] + cuts + [jnp.int32(rows)]
    lane_b = lax.broadcasted_iota(jnp.int32, (blk, LANES), 1)
    for j in range(len(cuts) - 1):
        start = cuts[j]
        live = (cuts[j + 1] > start).astype(F32)
        k = jnp.minimum(start // blk, n_blk - 1)
        grp = sum((offs[q] <= start).astype(jnp.int32) for q in range(1, N_EXPERT_GROUPS))
        sl = pl.ds(pl.multiple_of(k * blk, blk), blk)
        xb = xs_scr[sl, :]
        gb = gs_scr[sl, :] * live
        acts = []
        for i in range(EXPERTS_PER_GROUP):
            e = grp * EXPERTS_PER_GROUP + i
            gate = jnp.sum(jnp.where(lane_b == N_EXPERT_GROUPS + e, gb, 0.0), axis=-1, keepdims=True)
            hg = jnp.dot(xb, wg_ref[e], preferred_element_type=F32)
            hu = jnp.dot(xb, wu_ref[e], preferred_element_type=F32)
            acts.append((jax.nn.silu(hg) * hu * gate).astype(BF16))
        ys_scr[sl, :] += jnp.dot(jnp.concatenate(acts, axis=1), wd_ref[grp], preferred_element_type=F32)
    hi = ys_scr[...].astype(BF16)
    lo = (ys_scr[...] - hi.astype(F32)).astype(BF16)
    return jnp.dot(unsort, hi, preferred_element_type=F32) + jnp.dot(unsort, lo, preferred_element_type=F32)


def _ffn_body(x_ref, ssm_ref, attn_ref, mod_ref, gpost_ref, gpre_ref, gffn_ref, wout_ref, wr_ref, br_ref,
              tri_ref, wg_ref, wu_ref, wd_ref, o_ref, xs_scr, gs_scr, ys_scr, *, chunked):
    x = x_ref[...]
    g, r, d = x.shape
    rows = g * r
    gt1, sh2, sc2, gt2 = mod_ref[:, 2], mod_ref[:, 3], mod_ref[:, 4], mod_ref[:, 5]
    if chunked:
        ssm = jnp.swapaxes(ssm_ref[...].astype(F32), 0, 1).reshape(rows, ssm_ref.shape[-1]).astype(BF16)
    else:
        ssm = ssm_ref[...]
    cat = jnp.concatenate([ssm, attn_ref[...]], axis=1)
    mixed = jnp.dot(cat, wout_ref[...], preferred_element_type=F32)
    x1 = x + gt1 * _rms(mixed, gpost_ref[0]).reshape(g, r, d)
    hn = (_rms(x1, gpre_ref[...]) * (1.0 + sc2) + sh2).reshape(rows, d)
    hb = hn.astype(BF16)
    hlo = (hn - hb.astype(F32)).astype(BF16)
    nt = functools.partial(lax.dot_general, dimension_numbers=(((1,), (1,)), ((), ())), preferred_element_type=F32)
    logits_t = nt(wr_ref[0], hb) + nt(wr_ref[0], hlo) + nt(wr_ref[1], hb) + br_ref[...]
    gates_t, gidx = _route(logits_t)
    ffn = _moe_sorted(hb, gates_t, gidx, tri_ref, wg_ref, wu_ref, wd_ref, xs_scr, gs_scr, ys_scr)
    o_ref[...] = x1 + gt2 * _rms(ffn, gffn_ref[0]).reshape(g, r, d)


def _ffn(x3, ssm_n, attn_n, mod4, fw, gb, rb, chunk=None):
    nb, nr, d = x3.shape
    assert gb == 1 or rb == nr
    half = ssm_n.shape[-1]
    nj = nr // rb
    tm = gb * rb
    row_map = lambda i, j: (i * nj + j, 0)
    vec = _resident((1, 1, d))
    if chunk is None:
        ssm_spec = pl.BlockSpec((tm, half), row_map)
    else:
        assert gb == 1 and rb % chunk == 0
        ssm_spec = pl.BlockSpec((None, chunk, rb // chunk, half), lambda i, j: (i, 0, j, 0))
    return pl.pallas_call(
        functools.partial(_ffn_body, chunked=chunk is not None),
        out_shape=jax.ShapeDtypeStruct((nb, nr, d), F32),
        grid=(nb // gb, nj),
        in_specs=[pl.BlockSpec((gb, rb, d), lambda i, j: (i, j, 0)),
                  ssm_spec,
                  pl.BlockSpec((tm, attn_n.shape[1]), row_map),
                  pl.BlockSpec((gb, N_MOD, 1, d), lambda i, j: (i, 0, 0, 0)),
                  vec, vec, vec,
                  _resident(fw["wout"].shape), _resident(fw["wr"].shape), _resident(fw["br"].shape),
                  _resident((tm, tm)),
                  _resident(fw["wg"].shape), _resident(fw["wu"].shape), _resident(fw["wd"].shape)],
        out_specs=pl.BlockSpec((gb, rb, d), lambda i, j: (i, j, 0)),
        scratch_shapes=[pltpu.VMEM((tm, d), BF16), pltpu.VMEM((tm, LANES), F32), pltpu.VMEM((tm, d), F32)],
        compiler_params=_cparams("arbitrary", "arbitrary"),
        name="outproj_moe",
    )(x3, ssm_n, attn_n, mod4, fw["gpost"], fw["gpre"], fw["gffn"], fw["wout"], fw["wr"], fw["br"],
      jnp.triu(jnp.ones((tm, tm), BF16)), fw["wg"], fw["wu"], fw["wd"])


def _ffn_weights(g_post_mix, g_pre_ffn, g_post_ffn, w_out, w_gr, b_gr, w_er, b_er, w_gate, w_up, w_down):
    d = w_out.shape[0]
    n_route = N_EXPERT_GROUPS + N_EXPERTS
    wr = jnp.concatenate([w_gr, w_er.reshape(d, N_EXPERTS)], axis=1)
    br = jnp.concatenate([b_gr, b_er.reshape(N_EXPERTS)])
    wr = jnp.pad(wr, ((0, 0), (0, ROUTE_ROWS - n_route))).T
    return {
        "gpost": g_post_mix.reshape(1, 1, d), "gpre": g_pre_ffn.reshape(1, 1, d), "gffn": g_post_ffn.reshape(1, 1, d),
        "wout": w_out.astype(BF16),
        "wr": _hi_lo(wr),
        "br": jnp.pad(br, (0, ROUTE_ROWS - n_route)).reshape(ROUTE_ROWS, 1),
        "wg": w_gate.astype(BF16), "wu": w_up.astype(BF16),
        "wd": w_down.astype(BF16).reshape(N_EXPERT_GROUPS, -1, d),
    }


def _rel_bucket(dist):
    max_exact = NUM_BUCKETS // 2
    dd = jnp.maximum(dist, 0)
    log_ratio = jnp.log(jnp.maximum(dd, 1).astype(F32) / max_exact) / math.log(MAX_DISTANCE / max_exact)
    large = jnp.minimum(max_exact + (log_ratio * (NUM_BUCKETS - max_exact)).astype(jnp.int32), NUM_BUCKETS - 1)
    return jnp.where(dd < max_exact, dd, large)


def _bias_and_band(dist, table):
    onehot = (_rel_bucket(dist)[:, :, None] == jnp.arange(NUM_BUCKETS)[None, None, :]).astype(F32)
    bias = jnp.einsum('qkb,bh->hqk', onehot, table.astype(F32), precision=HIGHEST)
    band = ((dist >= 0) & (dist <= WINDOW)).astype(F32)
    return bias, band


def _layer(xp, xs, ck, cv, h0re, h0im, mod_p, mod_s, table, p):
    nb, seq, d = xp.shape
    nbs, t, _ = xs.shape
    w = ck.shape[1]
    kv_w = N_KV_HEADS * HEAD_DIM
    w_in_bf = p["w_in"].astype(BF16)
    sw = _ssm_weights(p["ssm_lam_re"], p["ssm_lam_im"], p["ssm_log_dt"], p["ssm_b_re"], p["ssm_b_im"],
                      p["ssm_c_re"], p["ssm_c_im"], p["ssm_d"], p["w_glu"], p["b_glu"], p["g_ssm_out"])
    fw = _ffn_weights(p["g_post_mix"], p["g_pre_ffn"], p["g_post_ffn"], p["w_out"],
                      p["w_group_router"], p["b_group_router"], p["w_expert_router"], p["b_expert_router"],
                      p["w_exp_gate"], p["w_exp_up"], p["w_exp_down"])
    n_state = sw["lre"].shape[1]
    ng = p["ssm_lam_re"].shape[0]

    u2, q, k, v = _inproj(xp, mod_p, p["g_pre_mix"], w_in_bf, 1, INPROJ_TILE, chunk=SSM_CHUNK)
    blk = WINDOW
    dist_p = jnp.arange(blk)[:, None] + blk - jnp.arange(2 * blk)[None, :]
    bias_p, band_p = _bias_and_band(dist_p, table)
    attn_p = _attn_prompt(q, k, v, p["attn_sinks"], bias_p, band_p, p["g_attn_out"], nb, seq)
    kw = _ssm_chunk_weights(p["ssm_lam_re"], p["ssm_lam_im"], p["ssm_log_dt"], p["ssm_b_re"], p["ssm_b_im"],
                            p["ssm_c_re"], p["ssm_c_im"], SSM_CHUNK, seq // SSM_CHUNK)
    yt, hend = _ssm_chunked(_ssm_transpose(u2, p["ssm_b_re"].shape[2]), kw)
    ssm_p = _ssm_tail_call(yt, u2, sw)
    yp = _ffn(xp, ssm_p, attn_p, mod_p, fw, 1, TOKEN_TILE, chunk=SSM_CHUNK)
    wp = min(WINDOW, seq)
    k_p = k.reshape(nb, seq, kv_w)[:, seq - wp:].reshape(nb, wp, N_KV_HEADS, HEAD_DIM)
    v_p = v.reshape(nb, seq, kv_w)[:, seq - wp:].reshape(nb, wp, N_KV_HEADS, HEAD_DIM)
    n_p = n_state // ng
    hre_p = hend[:, :, 0, :n_p]
    him_p = hend[:, :, 0, n_p:]

    gs = TOKEN_TILE // t
    us, qs, ks, vs = _inproj(xs, mod_s, p["g_pre_mix"], w_in_bf, gs, t)
    dist_s = jnp.arange(t)[:, None] + w - jnp.arange(w + t)[None, :]
    bias_s, band_s = _bias_and_band(dist_s, table)
    attn_s, k_s, v_s = _attn_sample(qs.reshape(nbs, t, -1), ks.reshape(nbs, t, kv_w), vs.reshape(nbs, t, kv_w),
                                    ck.reshape(nbs, w, kv_w), cv.reshape(nbs, w, kv_w),
                                    p["attn_sinks"], bias_s, band_s, p["g_attn_out"], 16)
    u_tm = jnp.swapaxes(us.reshape(nbs, t, -1), 0, 1)
    ssm_tm, hre_s, him_s = _ssm_sample(u_tm, h0re.reshape(nbs, n_state), h0im.reshape(nbs, n_state), sw)
    ssm_s = jnp.swapaxes(ssm_tm, 0, 1).reshape(nbs * t, -1)
    ys = _ffn(xs, ssm_s, attn_s.reshape(nbs * t, -1), mod_s, fw, gs, t)
    k_s = k_s.reshape(nbs, w, N_KV_HEADS, HEAD_DIM)
    v_s = v_s.reshape(nbs, w, N_KV_HEADS, HEAD_DIM)
    hre_s = hre_s.reshape(nbs, ng, n_state // ng)
    him_s = him_s.reshape(nbs, ng, n_state // ng)
    return yp, ys, k_p, v_p, hre_p, him_p, k_s, v_s, hre_s, him_s


def kernel(x_prompt, x_sample, cache_k, cache_v, state_ssm_re, state_ssm_im, c_prompt, c_sample, rel_bias_table,
           w_ada, b_ada, g_pre_mix, g_post_mix, g_pre_ffn, g_post_ffn, w_in, ssm_lam_re, ssm_lam_im, ssm_log_dt,
           ssm_b_re, ssm_b_im, ssm_c_re, ssm_c_im, ssm_d, w_glu, b_glu, attn_sinks, g_ssm_out, g_attn_out, w_out,
           w_group_router, b_group_router, w_expert_router, b_expert_router, w_exp_gate, w_exp_up, w_exp_down):
    params = dict(
        w_ada=w_ada, b_ada=b_ada, g_pre_mix=g_pre_mix, g_post_mix=g_post_mix, g_pre_ffn=g_pre_ffn,
        g_post_ffn=g_post_ffn, w_in=w_in, ssm_lam_re=ssm_lam_re, ssm_lam_im=ssm_lam_im, ssm_log_dt=ssm_log_dt,
        ssm_b_re=ssm_b_re, ssm_b_im=ssm_b_im, ssm_c_re=ssm_c_re, ssm_c_im=ssm_c_im, ssm_d=ssm_d, w_glu=w_glu,
        b_glu=b_glu, attn_sinks=attn_sinks, g_ssm_out=g_ssm_out, g_attn_out=g_attn_out, w_out=w_out,
        w_group_router=w_group_router, b_group_router=b_group_router, w_expert_router=w_expert_router,
        b_expert_router=b_expert_router, w_exp_gate=w_exp_gate, w_exp_up=w_exp_up, w_exp_down=w_exp_down)
    depth = w_in.shape[0]
    nb, nbs = x_prompt.shape[0], x_sample.shape[0]
    d = x_prompt.shape[2]
    pad = (-nb) % 8
    yp, ys = x_prompt, x_sample
    outs = [[] for _ in range(8)]
    for li in range(depth):
        p = {name: val[li] for name, val in params.items()}
        c_all = jnp.concatenate([c_prompt, jnp.zeros((pad, d), F32), c_sample], axis=0)
        mod = _modulation(c_all, p["w_ada"], p["b_ada"])
        mod_p = mod[:nb].reshape(nb, N_MOD, 1, d)
        mod_s = mod[nb + pad:].reshape(nbs, N_MOD, 1, d)
        res = _layer(yp, ys, cache_k[li], cache_v[li], state_ssm_re[li], state_ssm_im[li], mod_p, mod_s,
                     rel_bias_table, p)
        yp, ys = res[0], res[1]
        for acc, val in zip(outs, res[2:]):
            acc.append(val)
    return (yp, ys) + tuple(jnp.stack(o) for o in outs)
```

```python
import functools
import math

import jax
import jax.numpy as jnp
from jax import lax
from jax.experimental import pallas as pl
from jax.experimental.pallas import tpu as pltpu

F32 = jnp.float32
BF16 = jnp.bfloat16
HIGHEST = lax.Precision.HIGHEST

EPS = 1e-6
NEG_INF = -1e30

SSM_GROUP = 16
SSM_STATE = 64
HEAD_DIM = 64
N_KV_HEADS = 2
GQA = 4
N_HEADS = N_KV_HEADS * GQA
WINDOW = 128
NUM_BUCKETS = 32
MAX_DISTANCE = 128
N_EXPERT_GROUPS = 4
EXPERTS_PER_GROUP = 4
N_EXPERTS = N_EXPERT_GROUPS * EXPERTS_PER_GROUP
N_MOD = 6

LANES = 128
SUBLANES = 8
VMEM_LIMIT_BYTES = 56 * 1024 * 1024

TOKEN_TILE = 512
INPROJ_TILE = 1024
INPROJ_SUB = 256
ATTN_TILE = 512
SSM_CHUNK = 32
SSM_GROUP_BLOCK = 8
SSM_TAIL_BLOCK = 8
MOE_ROW_BLOCK = 128
ROUTE_ROWS = 32


def _cparams(*sem):
    return pltpu.CompilerParams(dimension_semantics=sem, vmem_limit_bytes=VMEM_LIMIT_BYTES)


def _resident(shape):
    zeros = (0,) * len(shape)
    return pl.BlockSpec(shape, lambda *_: zeros, pipeline_mode=pl.Buffered(1))


def _rms(x, g):
    return x * lax.rsqrt(jnp.mean(x * x, axis=-1, keepdims=True) + EPS) * g


def _mod_body(c_ref, w_ref, b_ref, o_ref):
    a = jax.nn.silu(c_ref[...])
    w = w_ref[...]
    a_hi, w_hi = a.astype(BF16), w.astype(BF16)
    a_lo, w_lo = (a - a_hi.astype(F32)).astype(BF16), (w - w_hi.astype(F32)).astype(BF16)
    dot = functools.partial(jnp.dot, preferred_element_type=F32)
    o_ref[...] = dot(a_hi, w_hi) + dot(a_lo, w_hi) + dot(a_hi, w_lo) + b_ref[...]


def _modulation(c, w_ada, b_ada):
    rows, d = c.shape
    n = w_ada.shape[1]
    tn = 1024
    return pl.pallas_call(
        _mod_body,
        out_shape=jax.ShapeDtypeStruct((rows, n), F32),
        grid=(n // tn,),
        in_specs=[pl.BlockSpec((rows, d), lambda j: (0, 0)),
                  pl.BlockSpec((d, tn), lambda j: (0, j)),
                  pl.BlockSpec((1, tn), lambda j: (0, j))],
        out_specs=pl.BlockSpec((rows, tn), lambda j: (0, j)),
        compiler_params=_cparams("arbitrary"),
        name="modulation",
    )(c, w_ada, b_ada.reshape(1, n))


def _inproj_body(x_ref, mod_ref, g_ref, w_ref, u_ref, q_ref, k_ref, v_ref, *, chunked):
    g, r, d = x_ref.shape
    ssm_w = u_ref.shape[-1]
    attn_w = q_ref.shape[1]
    kv_w = k_ref.shape[1]
    rows = g * r
    sub = min(rows, INPROJ_SUB)
    hns = []
    for c in range(rows // sub):
        if g == 1:
            x = x_ref[:, c * sub:(c + 1) * sub, :]
            hn = _rms(x, g_ref[...]) * (1.0 + mod_ref[:, 1]) + mod_ref[:, 0]
        else:
            gs = sub // r
            x = x_ref[c * gs:(c + 1) * gs]
            hn = _rms(x, g_ref[...]) * (1.0 + mod_ref[c * gs:(c + 1) * gs, 1]) + mod_ref[c * gs:(c + 1) * gs, 0]
        hns.append(hn.reshape(sub, d).astype(BF16))
    for c, hn in enumerate(hns):
        rs = slice(c * sub, (c + 1) * sub)
        proj = jnp.dot(hn, w_ref[...], preferred_element_type=F32)
        if chunked:
            t = u_ref.shape[0]
            nc = sub // t
            u_ref[:, c * nc:(c + 1) * nc, :] = jnp.swapaxes(proj[:, :ssm_w].reshape(nc, t, ssm_w), 0, 1)
        else:
            u_ref[rs, :] = proj[:, :ssm_w]
        q_ref[rs, :] = (proj[:, ssm_w:ssm_w + attn_w] * (HEAD_DIM ** -0.5)).astype(BF16)
        k_ref[rs, :] = proj[:, ssm_w + attn_w:ssm_w + attn_w + kv_w]
        v_ref[rs, :] = proj[:, ssm_w + attn_w + kv_w:]


def _inproj(x3, mod4, g_pre, w_in_bf, gb, rb, chunk=None):
    nb, nr, d = x3.shape
    assert gb == 1 or rb == nr
    n_in = w_in_bf.shape[1]
    kv_w = N_KV_HEADS * HEAD_DIM
    attn_w = N_HEADS * HEAD_DIM
    ssm_w = n_in - attn_w - 2 * kv_w
    rows = nb * nr
    nj = nr // rb
    tm = gb * rb
    out_map = lambda i, j: (i * nj + j, 0)
    if chunk is None:
        u_shape = jax.ShapeDtypeStruct((rows, ssm_w), F32)
        u_spec = pl.BlockSpec((tm, ssm_w), out_map)
    else:
        assert gb == 1 and rb % chunk == 0
        u_shape = jax.ShapeDtypeStruct((nb, chunk, nr // chunk, ssm_w), F32)
        u_spec = pl.BlockSpec((None, chunk, rb // chunk, ssm_w), lambda i, j: (i, 0, j, 0))
    return pl.pallas_call(
        functools.partial(_inproj_body, chunked=chunk is not None),
        out_shape=(u_shape,
                   jax.ShapeDtypeStruct((rows, attn_w), BF16),
                   jax.ShapeDtypeStruct((rows, kv_w), F32),
                   jax.ShapeDtypeStruct((rows, kv_w), F32)),
        grid=(nb // gb, nj),
        in_specs=[pl.BlockSpec((gb, rb, d), lambda i, j: (i, j, 0)),
                  pl.BlockSpec((gb, 2, 1, d), lambda i, j: (i, 0, 0, 0)),
                  _resident((1, 1, d)),
                  _resident((d, n_in))],
        out_specs=(u_spec,
                   pl.BlockSpec((tm, attn_w), out_map),
                   pl.BlockSpec((tm, kv_w), out_map),
                   pl.BlockSpec((tm, kv_w), out_map)),
        compiler_params=_cparams("arbitrary", "arbitrary"),
        name="in_projection",
    )(x3, mod4, g_pre.reshape(1, 1, d), w_in_bf)


def _attn_prompt_body(sinks_ref, q_ref, kc_ref, vc_ref, kp_ref, vp_ref, bias_ref, g_ref, o_ref, s_scr, p_scr):
    tq = q_ref.shape[0]
    blk = bias_ref.shape[2] // 2
    first_tile = pl.program_id(1) == 0
    kk = jnp.concatenate([kp_ref[...], kc_ref[...]], axis=0).astype(BF16)
    vv = jnp.concatenate([vp_ref[...], vc_ref[...]], axis=0).astype(BF16)
    shape = (GQA * blk, 2 * blk)
    col = lax.broadcasted_iota(jnp.int32, shape, 1)
    row = lax.broadcasted_iota(jnp.int32, (GQA * blk, 1), 0)
    no_prev = first_tile & (col < blk)
    sink_cols = []
    for hk in range(N_KV_HEADS):
        sk = jnp.full((GQA * blk, 1), sinks_ref[hk * GQA], F32)
        for gq in range(1, GQA):
            sk = jnp.where(row >= gq * blk, sinks_ref[hk * GQA + gq], sk)
        sink_cols.append(sk)
    pairs = [(j, hk) for j in range(tq // blk) for hk in range(N_KV_HEADS)]
    for idx, (j, hk) in enumerate(pairs):
        qg = jnp.concatenate([q_ref[j * blk:(j + 1) * blk, h * HEAD_DIM:(h + 1) * HEAD_DIM]
                              for h in range(hk * GQA, (hk + 1) * GQA)], axis=0)
        s = lax.dot_general(qg, kk[j * blk:(j + 2) * blk, hk * HEAD_DIM:(hk + 1) * HEAD_DIM],
                            (((1,), (1,)), ((), ())), preferred_element_type=F32) + bias_ref[hk]
        s_scr[idx] = jnp.where(no_prev, NEG_INF, s) if j == 0 else s
    dens = []
    for idx, (j, hk) in enumerate(pairs):
        s = s_scr[idx]
        m = jnp.maximum(jnp.max(s, axis=-1, keepdims=True), sink_cols[hk])
        p = jnp.exp(s - m)
        dens.append(jnp.sum(p, axis=-1, keepdims=True) + jnp.exp(sink_cols[hk] - m))
        p_scr[idx] = p.astype(BF16)
    outs = {}
    for idx, (j, hk) in enumerate(pairs):
        o = jnp.dot(p_scr[idx], vv[j * blk:(j + 2) * blk, hk * HEAD_DIM:(hk + 1) * HEAD_DIM],
                    preferred_element_type=F32) / dens[idx]
        for gq in range(GQA):
            outs[j, hk * GQA + gq] = o[gq * blk:(gq + 1) * blk]
    for j in range(tq // blk):
        att = jnp.concatenate([outs[j, h] for h in range(N_HEADS)], axis=1)
        o_ref[j * blk:(j + 1) * blk, :] = _rms(att, g_ref[...]).astype(BF16)


def _attn_prompt(q, k, v, sinks, bias, band, g_attn, nb, seq):
    rows, attn_w = q.shape
    kv_w = k.shape[1]
    blk = band.shape[0]
    bias = jnp.where(band > 0.5, bias, NEG_INF).reshape(N_KV_HEADS, GQA * blk, 2 * blk)
    tq = ATTN_TILE
    nt = seq // tq
    per = tq // blk
    cur = lambda b, i: (b * nt + i, 0)
    prev = lambda b, i: (b * nt * per + jnp.maximum(i * per - 1, 0), 0)
    return pl.pallas_call(
        _attn_prompt_body,
        out_shape=jax.ShapeDtypeStruct((rows, attn_w), BF16),
        grid=(nb, nt),
        in_specs=[pl.BlockSpec(memory_space=pltpu.SMEM),
                  pl.BlockSpec((tq, attn_w), cur),
                  pl.BlockSpec((tq, kv_w), cur),
                  pl.BlockSpec((tq, kv_w), cur),
                  pl.BlockSpec((blk, kv_w), prev),
                  pl.BlockSpec((blk, kv_w), prev),
                  _resident(bias.shape),
                  _resident((1, attn_w))],
        out_specs=pl.BlockSpec((tq, attn_w), cur),
        scratch_shapes=[pltpu.VMEM((per * N_KV_HEADS, GQA * blk, 2 * blk), F32),
                        pltpu.VMEM((per * N_KV_HEADS, GQA * blk, 2 * blk), BF16)],
        compiler_params=_cparams("arbitrary", "arbitrary"),
        name="attention_prompt",
    )(sinks, q, k, v, k, v, bias, g_attn.reshape(1, attn_w))


def _attn_sample_body(sinks_ref, q_ref, kn_ref, vn_ref, ck_ref, cv_ref, bias_ref, g_ref, o_ref, ko_ref, vo_ref):
    t = q_ref.shape[1]
    kk = jnp.concatenate([ck_ref[...], kn_ref[...]], axis=1)
    vv = jnp.concatenate([cv_ref[...], vn_ref[...]], axis=1)
    ko_ref[...] = kk[:, t:, :]
    vo_ref[...] = vv[:, t:, :]
    kkb = kk.astype(BF16)
    vvb = vv.astype(BF16)
    row = lax.broadcasted_iota(jnp.int32, (GQA * t, 1), 0)
    outs = [None] * N_HEADS
    for hk in range(N_KV_HEADS):
        sk = jnp.full((GQA * t, 1), sinks_ref[hk * GQA], F32)
        for gq in range(1, GQA):
            sk = jnp.where(row >= gq * t, sinks_ref[hk * GQA + gq], sk)
        qg = jnp.concatenate([q_ref[:, :, h * HEAD_DIM:(h + 1) * HEAD_DIM]
                              for h in range(hk * GQA, (hk + 1) * GQA)], axis=1)
        s = jnp.einsum('bqd,bkd->bqk', qg, kkb[:, :, hk * HEAD_DIM:(hk + 1) * HEAD_DIM],
                       preferred_element_type=F32) + bias_ref[hk]
        m = jnp.maximum(jnp.max(s, axis=-1, keepdims=True), sk)
        p = jnp.exp(s - m)
        den = jnp.sum(p, axis=-1, keepdims=True) + jnp.exp(sk - m)
        o = jnp.einsum('bqk,bkd->bqd', p.astype(BF16), vvb[:, :, hk * HEAD_DIM:(hk + 1) * HEAD_DIM],
                       preferred_element_type=F32) / den
        for gq in range(GQA):
            outs[hk * GQA + gq] = o[:, gq * t:(gq + 1) * t, :]
    att = jnp.concatenate(outs, axis=2)
    o_ref[...] = _rms(att, g_ref[...]).astype(BF16)


def _attn_sample(q3, k3, v3, ck, cv, sinks, bias, band, g_attn, gb):
    nb, t, attn_w = q3.shape
    w, kv_w = ck.shape[1], ck.shape[2]
    bias = jnp.where(band > 0.5, bias, NEG_INF).reshape(N_KV_HEADS, GQA * t, w + t)
    blk3 = lambda last: pl.BlockSpec((gb, last[0], last[1]), lambda i: (i, 0, 0))
    return pl.pallas_call(
        _attn_sample_body,
        out_shape=(jax.ShapeDtypeStruct((nb, t, attn_w), BF16),
                   jax.ShapeDtypeStruct((nb, w, kv_w), F32),
                   jax.ShapeDtypeStruct((nb, w, kv_w), F32)),
        grid=(nb // gb,),
        in_specs=[pl.BlockSpec(memory_space=pltpu.SMEM),
                  blk3((t, attn_w)), blk3((t, kv_w)), blk3((t, kv_w)),
                  blk3((w, kv_w)), blk3((w, kv_w)),
                  _resident(bias.shape), _resident((1, 1, attn_w))],
        out_specs=(blk3((t, attn_w)), blk3((w, kv_w)), blk3((w, kv_w))),
        compiler_params=_cparams("arbitrary"),
        name="attention_sample",
    )(sinks, q3, k3, v3, ck, cv, bias, g_attn.reshape(1, 1, attn_w))


def _ssm_tail(y, u, d_ref, wglu_ref, bglu_ref, g_ref):
    z = jax.nn.gelu(y + d_ref[...] * u)
    gate = jax.nn.sigmoid(jnp.dot(z.astype(BF16), wglu_ref[...], preferred_element_type=F32) + bglu_ref[...])
    return _rms(z * gate, g_ref[...]).astype(BF16)


def _ssm_chunk_body(ut_ref, m_ref, e_ref, f_ref, w1_ref, w2_ref, yt_ref, hend_ref, s_scr):
    t, gb, ch, nc = ut_ref.shape
    n_levels = w1_ref.shape[1]
    two_p = e_ref.shape[1]
    row = lax.broadcasted_iota(jnp.int32, (nc, two_p), 0)
    rs = [ut_ref[:, gl].reshape(t * ch, nc) for gl in range(gb)]
    hs = []
    for gl in range(gb):
        s_scr[gl] = jnp.dot(e_ref[gl], rs[gl], preferred_element_type=F32)
        hs.append(s_scr[gl].T)

    def conv(gl):
        yt_ref[:, gl * ch:(gl + 1) * ch, :] = jnp.dot(m_ref[gl], rs[gl], preferred_element_type=F32).reshape(t, ch, nc)

    for lv in range(n_levels):
        sh = 1 << lv
        for gl in range(gb):
            prev = jnp.where(row >= sh, pltpu.roll(hs[gl], sh, axis=0), 0.0)
            hs[gl] = hs[gl] + w1_ref[gl, lv] * prev + w2_ref[gl, lv] * pltpu.roll(prev, two_p // 2, axis=1)
        for gl in range(lv * gb // n_levels, (lv + 1) * gb // n_levels):
            conv(gl)
    for gl in range(gb):
        hend_ref[gl] = hs[gl][nc - 1:nc, :]
        s_scr[gl] = jnp.where(row >= 1, pltpu.roll(hs[gl], 1, axis=0), 0.0).T
        yt_ref[:, gl * ch:(gl + 1) * ch, :] += jnp.dot(f_ref[gl], s_scr[gl].astype(BF16),
                                                       preferred_element_type=F32).reshape(t, ch, nc)


def _ssm_transpose_body(u_ref, o_ref):
    for s in range(u_ref.shape[0]):
        o_ref[s] = u_ref[s].T.astype(BF16).reshape(o_ref.shape[1:])


def _ssm_transpose(u2, ch):
    nb, t, nc, ssm_w = u2.shape
    tb = SSM_TAIL_BLOCK
    return pl.pallas_call(
        _ssm_transpose_body,
        out_shape=jax.ShapeDtypeStruct((nb, t, ssm_w // ch, ch, nc), BF16),
        grid=(nb, t // tb),
        in_specs=[pl.BlockSpec((None, tb, nc, ssm_w), lambda b, i: (b, i, 0, 0))],
        out_specs=pl.BlockSpec((None, tb, ssm_w // ch, ch, nc), lambda b, i: (b, i, 0, 0, 0)),
        compiler_params=_cparams("arbitrary", "arbitrary"),
        name="ssm_transpose",
    )(u2)


def _ssm_chunked(ut, kw):
    nb, t, ng, ch, nc = ut.shape
    tc = t * ch
    two_p = kw["e"].shape[1]
    n_levels = kw["w1"].shape[1]
    gb = SSM_GROUP_BLOCK
    grp = lambda shape: pl.BlockSpec((gb,) + shape, lambda b, g: (g,) + (0,) * len(shape))
    return pl.pallas_call(
        _ssm_chunk_body,
        out_shape=(jax.ShapeDtypeStruct((nb, t, ng * ch, nc), F32),
                   jax.ShapeDtypeStruct((nb, ng, 1, two_p), F32)),
        grid=(nb, ng // gb),
        in_specs=[pl.BlockSpec((None, t, gb, ch, nc), lambda b, g: (b, 0, g, 0, 0)),
                  grp((tc, tc)), grp((two_p, tc)), grp((tc, two_p)),
                  grp((n_levels, 1, two_p)), grp((n_levels, 1, two_p))],
        out_specs=(pl.BlockSpec((None, t, gb * ch, nc), lambda b, g: (b, 0, g, 0)),
                   pl.BlockSpec((None, gb, 1, two_p), lambda b, g: (b, g, 0, 0))),
        scratch_shapes=[pltpu.VMEM((gb, two_p, nc), F32)],
        compiler_params=_cparams("arbitrary", "arbitrary"),
        name="ssm_chunked",
    )(ut, kw["m"], kw["e"], kw["f"], kw["w1"], kw["w2"])


def _ssm_tail_body(yt_ref, u_ref, d_ref, wglu_ref, bglu_ref, g_ref, o_ref):
    for i in range(yt_ref.shape[0]):
        o_ref[i] = _ssm_tail(yt_ref[i].T, u_ref[i], d_ref, wglu_ref, bglu_ref, g_ref)


def _ssm_tail_call(yt, u2, sw):
    nb, t, nc, ssm_w = u2.shape
    tb = SSM_TAIL_BLOCK
    return pl.pallas_call(
        _ssm_tail_body,
        out_shape=jax.ShapeDtypeStruct((nb, t, nc, ssm_w), BF16),
        grid=(nb, t // tb),
        in_specs=[pl.BlockSpec((None, tb, ssm_w, nc), lambda b, i: (b, i, 0, 0)),
                  pl.BlockSpec((None, tb, nc, ssm_w), lambda b, i: (b, i, 0, 0)),
                  _resident((1, ssm_w)), _resident((ssm_w, ssm_w)), _resident((1, ssm_w)), _resident((1, ssm_w))],
        out_specs=pl.BlockSpec((None, tb, nc, ssm_w), lambda b, i: (b, i, 0, 0)),
        compiler_params=_cparams("arbitrary", "arbitrary"),
        name="ssm_tail",
    )(yt, u2, sw["d"], sw["wglu"], sw["bglu"], sw["g"])


def _lag_matrix_body(z_ref, m_ref, *, t):
    gb, ch, _ = z_ref.shape
    width = t * ch
    for gl in range(gb):
        z = z_ref[gl]
        for r in range(t):
            off = (t - 1 - r) * ch
            m_ref[gl, r * ch:(r + 1) * ch, :] = z[:, off:off + width].astype(BF16)


def _lag_matrix(z, t):
    ng, ch, zw = z.shape
    gb = SSM_GROUP_BLOCK
    return pl.pallas_call(
        functools.partial(_lag_matrix_body, t=t),
        out_shape=jax.ShapeDtypeStruct((ng, t * ch, zw // 2), BF16),
        grid=(ng // gb,),
        in_specs=[pl.BlockSpec((gb, ch, zw), lambda g: (g, 0, 0))],
        out_specs=pl.BlockSpec((gb, t * ch, zw // 2), lambda g: (g, 0, 0)),
        compiler_params=_cparams("arbitrary"),
        name="ssm_lag_matrix",
    )(z)


def _ssm_chunk_weights(lam_re, lam_im, log_dt, b_re, b_im, c_re, c_im, t, n_chunks):
    ng, ns = lam_re.shape
    nc = b_re.shape[2]
    dt = jnp.exp(log_dt)[:, None]
    are, aim = lam_re * dt, lam_im * dt
    d = jnp.arange(t + 1, dtype=F32)[:, None, None]
    mag = jnp.exp(d * are)
    pre, pim = mag * jnp.cos(d * aim), mag * jnp.sin(d * aim)
    lre, lim = pre[1], pim[1]
    den = lam_re * lam_re + lam_im * lam_im
    fre = ((lre - 1.0) * lam_re + lim * lam_im) / den
    fim = (lim * lam_re - (lre - 1.0) * lam_im) / den
    bbr = fre[..., None] * b_re - fim[..., None] * b_im
    bbi = fre[..., None] * b_im + fim[..., None] * b_re
    pre_g, pim_g = jnp.transpose(pre, (1, 0, 2)), jnp.transpose(pim, (1, 0, 2))
    xr = c_re[:, None] * pre_g[:, :, None, :] - c_im[:, None] * pim_g[:, :, None, :]
    xi = c_re[:, None] * pim_g[:, :, None, :] + c_im[:, None] * pre_g[:, :, None, :]
    kern = jnp.einsum('gdap,gpc->gdac', jnp.concatenate([xr[:, :t], -xi[:, :t]], axis=3),
                      jnp.concatenate([bbr, bbi], axis=1), precision=HIGHEST)
    lag_rows = jnp.transpose(kern[:, ::-1], (0, 2, 1, 3)).reshape(ng, nc, t * nc)
    m = _lag_matrix(jnp.concatenate([lag_rows, jnp.zeros_like(lag_rows)], axis=2), t)
    rev_re = jnp.transpose(pre_g[:, t - 1::-1][:, :t], (0, 2, 1))
    rev_im = jnp.transpose(pim_g[:, t - 1::-1][:, :t], (0, 2, 1))
    er = rev_re[..., None] * bbr[:, :, None, :] - rev_im[..., None] * bbi[:, :, None, :]
    ei = rev_re[..., None] * bbi[:, :, None, :] + rev_im[..., None] * bbr[:, :, None, :]
    e = jnp.concatenate([er.reshape(ng, ns, t * nc), ei.reshape(ng, ns, t * nc)], axis=1).astype(BF16)
    f = jnp.concatenate([xr[:, 1:].reshape(ng, t * nc, ns), -xi[:, 1:].reshape(ng, t * nc, ns)], axis=2).astype(BF16)
    wr, wi = pre[t], pim[t]
    w1, w2 = [], []
    for _ in range(max(1, (n_chunks - 1).bit_length())):
        w1.append(jnp.concatenate([wr, wr], axis=1))
        w2.append(jnp.concatenate([-wi, wi], axis=1))
        wr, wi = wr * wr - wi * wi, 2.0 * wr * wi
    w1 = jnp.stack(w1, axis=1)[:, :, None, :]
    w2 = jnp.stack(w2, axis=1)[:, :, None, :]
    return {"m": m, "e": e, "f": f, "w1": w1, "w2": w2}


def _ssm_sample_body(u_ref, h0re_ref, h0im_ref, bre_ref, bim_ref, cre_ref, cim_ref, lre_ref, lim_ref,
                     d_ref, wglu_ref, bglu_ref, g_ref, o_ref, hre_ref, him_ref):
    steps = u_ref.shape[0]
    half_in = bre_ref.shape[2]
    dot = functools.partial(jnp.dot, preferred_element_type=F32)
    ar = lre_ref[...]
    ai = lim_ref[...]
    hr = h0re_ref[...]
    hi = h0im_ref[...]
    for t in range(steps):
        u = u_ref[t]
        u_hi = u.astype(BF16)
        u_lo = (u - u_hi.astype(F32)).astype(BF16)

        def bu(b_ref):
            cols = []
            for hf in range(2):
                lanes = slice(hf * half_in, (hf + 1) * half_in)
                cols.append(dot(u_hi[:, lanes], b_ref[0, hf]) + dot(u_lo[:, lanes], b_ref[0, hf])
                            + dot(u_hi[:, lanes], b_ref[1, hf]))
            return jnp.concatenate(cols, axis=1)

        hr, hi = ar * hr - ai * hi + bu(bre_ref), ar * hi + ai * hr + bu(bim_ref)
        half_st = hr.shape[1] // 2
        hrb, hib = hr.astype(BF16), hi.astype(BF16)
        y = jnp.concatenate(
            [dot(hrb[:, hf * half_st:(hf + 1) * half_st], cre_ref[hf])
             + dot(hib[:, hf * half_st:(hf + 1) * half_st], cim_ref[hf]) for hf in range(2)], axis=1)
        o_ref[t] = _ssm_tail(y, u, d_ref, wglu_ref, bglu_ref, g_ref)
    hre_ref[...] = hr
    him_ref[...] = hi


def _ssm_sample(u_tm, h0re, h0im, sw):
    steps, nb, ssm_w = u_tm.shape
    n_state = h0re.shape[1]
    return pl.pallas_call(
        _ssm_sample_body,
        out_shape=(jax.ShapeDtypeStruct((steps, nb, ssm_w), BF16),
                   jax.ShapeDtypeStruct((nb, n_state), F32),
                   jax.ShapeDtypeStruct((nb, n_state), F32)),
        compiler_params=pltpu.CompilerParams(vmem_limit_bytes=VMEM_LIMIT_BYTES),
        name="ssm_sample",
    )(u_tm, h0re, h0im, _hi_lo(sw["bre"]), _hi_lo(sw["bim"]), sw["cre"].astype(BF16), sw["cim"].astype(BF16),
      sw["lre"], sw["lim"], sw["d"], sw["wglu"], sw["bglu"], sw["g"])


def _hi_lo(w):
    hi = w.astype(BF16)
    return jnp.stack([hi, (w - hi.astype(F32)).astype(BF16)])


def _ssm_weights(lam_re, lam_im, log_dt, b_re, b_im, c_re, c_im, d_skip, w_glu, b_glu, g_ssm):
    ng, ns = lam_re.shape
    nc = b_re.shape[2]
    dt = jnp.exp(log_dt)[:, None]
    mag = jnp.exp(lam_re * dt)
    lre = mag * jnp.cos(lam_im * dt)
    lim = mag * jnp.sin(lam_im * dt)
    den = lam_re * lam_re + lam_im * lam_im
    fre = ((lre - 1.0) * lam_re + lim * lam_im) / den
    fim = (lim * lam_re - (lre - 1.0) * lam_im) / den
    bbar_re = fre[..., None] * b_re - fim[..., None] * b_im
    bbar_im = fre[..., None] * b_im + fim[..., None] * b_re
    eye = jnp.eye(ng // 2, dtype=F32)

    def in_blocks(b):
        b2 = b.reshape(2, ng // 2, ns, nc)
        return jnp.einsum('hgpc,gk->hgckp', b2, eye).reshape(2, ng // 2 * nc, ng // 2 * ns)

    def out_blocks(c):
        c2 = c.reshape(2, ng // 2, nc, ns)
        return jnp.einsum('hgcp,gk->hgpkc', c2, eye).reshape(2, ng // 2 * ns, ng // 2 * nc)

    ssm_w = ng * nc
    return {
        "bre": in_blocks(bbar_re), "bim": in_blocks(bbar_im),
        "cre": out_blocks(c_re), "cim": out_blocks(-c_im),
        "lre": lre.reshape(1, ng * ns), "lim": lim.reshape(1, ng * ns),
        "d": d_skip.reshape(1, ssm_w), "wglu": w_glu.astype(BF16), "bglu": b_glu.reshape(1, ssm_w),
        "g": g_ssm.reshape(1, ssm_w),
    }


def _route(logits_t):
    big = jnp.int32(10 ** 6)
    gl = logits_t[:N_EXPERT_GROUPS]
    grow = lax.broadcasted_iota(jnp.int32, gl.shape, 0)
    gmax = jnp.max(gl, axis=0, keepdims=True)
    gidx = jnp.min(jnp.where(gl == gmax, grow, big), axis=0, keepdims=True)
    g_p = 1.0 / jnp.sum(jnp.exp(gl - gmax), axis=0, keepdims=True)
    ex = logits_t[N_EXPERT_GROUPS:N_EXPERT_GROUPS + N_EXPERTS]
    erow = lax.broadcasted_iota(jnp.int32, ex.shape, 0)
    lo = gidx * EXPERTS_PER_GROUP
    in_group = (erow >= lo) & (erow < lo + EXPERTS_PER_GROUP)
    el = jnp.where(in_group, ex, NEG_INF)
    ee = jnp.exp(el - jnp.max(el, axis=0, keepdims=True))
    prob = ee / jnp.sum(ee, axis=0, keepdims=True)
    p1 = jnp.max(jnp.where(in_group, prob, -1.0), axis=0, keepdims=True)
    i1 = jnp.min(jnp.where(in_group & (prob == p1), erow, big), axis=0, keepdims=True)
    rest = in_group & (erow != i1)
    p2 = jnp.max(jnp.where(rest, prob, -1.0), axis=0, keepdims=True)
    i2 = jnp.min(jnp.where(rest & (prob == p2), erow, big), axis=0, keepdims=True)
    tot = p1 + p2
    return jnp.where(erow == i1, g_p * p1 / tot, jnp.where(erow == i2, g_p * p2 / tot, 0.0)), gidx


def _split3(x):
    a = x.astype(BF16)
    r = x - a.astype(F32)
    b = r.astype(BF16)
    return a, b, (r - b.astype(F32)).astype(BF16)


def _moe_sorted(hb, gates_t, gidx, tri_ref, wg_ref, wu_ref, wd_ref, xs_scr, gs_scr, ys_scr):
    rows, d = hb.shape
    grow = lax.broadcasted_iota(jnp.int32, (SUBLANES, rows), 0)
    member = grow == gidx
    csum = jnp.dot(member.astype(BF16), tri_ref[...], preferred_element_type=F32)
    rank = jnp.sum(jnp.where(member, csum, 0.0), axis=0, keepdims=True)
    counts = [jnp.sum((gidx == grp).astype(jnp.int32)) for grp in range(N_EXPERT_GROUPS - 1)]
    offs = [jnp.int32(0)]
    for c in counts:
        offs.append(offs[-1] + c)
    offs.append(jnp.int32(rows))
    base = jnp.zeros_like(gidx)
    for grp in range(1, N_EXPERT_GROUPS):
        base = jnp.where(gidx == grp, offs[grp], base)
    pos_lanes = base.astype(F32) + rank - 1.0
    pos = jnp.broadcast_to(pos_lanes, (SUBLANES, rows)).T[:, 0:1]
    col = lax.broadcasted_iota(jnp.int32, (rows, rows), 1).astype(F32)
    row = lax.broadcasted_iota(jnp.int32, (rows, rows), 0).astype(F32)
    unsort = (col == pos).astype(BF16)
    sort = (row == pos_lanes).astype(BF16)
    xs_scr[...] = jnp.dot(sort, hb, preferred_element_type=F32).astype(BF16)
    pad = jnp.zeros((LANES - N_EXPERT_GROUPS - N_EXPERTS, rows), F32)
    gates_pad = jnp.concatenate([jnp.zeros((N_EXPERT_GROUPS, rows), F32), gates_t, pad], axis=0)
    gs_t = sum(jnp.dot(part, unsort, preferred_element_type=F32) for part in _split3(gates_pad))
    gs_scr[...] = gs_t.T
    ys_scr[...] = jnp.zeros_like(ys_scr)
    blk = MOE_ROW_BLOCK
    n_blk = rows // blk
    cuts = [jnp.int32(k * blk) for k in range(1, n_blk)] + offs[1:N_EXPERT_GROUPS]
    assert len(cuts) == 6
    for a, b in ((0, 5), (1, 3), (2, 4), (1, 2), (3, 4), (0, 3), (2, 5), (0, 1), (2, 3), (4, 5), (1, 2), (3, 4)):
        cuts[a], cuts[b] = jnp.minimum(cuts[a], cuts[b]), jnp.maximum(cuts[a], cuts[b])
    cuts = [jnp.int32(0)] + cuts + [jnp.int32(rows)]
    lane_b = lax.broadcasted_iota(jnp.int32, (blk, LANES), 1)
    for j in range(len(cuts) - 1):
        start = cuts[j]
        live = (cuts[j + 1] > start).astype(F32)
        k = jnp.minimum(start // blk, n_blk - 1)
        grp = sum((offs[q] <= start).astype(jnp.int32) for q in range(1, N_EXPERT_GROUPS))
        sl = pl.ds(pl.multiple_of(k * blk, blk), blk)
        xb = xs_scr[sl, :]
        gb = gs_scr[sl, :] * live
        acts = []
        for i in range(EXPERTS_PER_GROUP):
            e = grp * EXPERTS_PER_GROUP + i
            gate = jnp.sum(jnp.where(lane_b == N_EXPERT_GROUPS + e, gb, 0.0), axis=-1, keepdims=True)
            hg = jnp.dot(xb, wg_ref[e], preferred_element_type=F32)
            hu = jnp.dot(xb, wu_ref[e], preferred_element_type=F32)
            acts.append((jax.nn.silu(hg) * hu * gate).astype(BF16))
        ys_scr[sl, :] += jnp.dot(jnp.concatenate(acts, axis=1), wd_ref[grp], preferred_element_type=F32)
    hi = ys_scr[...].astype(BF16)
    lo = (ys_scr[...] - hi.astype(F32)).astype(BF16)
    return jnp.dot(unsort, hi, preferred_element_type=F32) + jnp.dot(unsort, lo, preferred_element_type=F32)


def _ffn_body(x_ref, ssm_ref, attn_ref, mod_ref, gpost_ref, gpre_ref, gffn_ref, wout_ref, wr_ref, br_ref,
              tri_ref, wg_ref, wu_ref, wd_ref, o_ref, xs_scr, gs_scr, ys_scr, *, chunked):
    x = x_ref[...]
    g, r, d = x.shape
    rows = g * r
    gt1, sh2, sc2, gt2 = mod_ref[:, 2], mod_ref[:, 3], mod_ref[:, 4], mod_ref[:, 5]
    if chunked:
        ssm = jnp.swapaxes(ssm_ref[...].astype(F32), 0, 1).reshape(rows, ssm_ref.shape[-1]).astype(BF16)
    else:
        ssm = ssm_ref[...]
    cat = jnp.concatenate([ssm, attn_ref[...]], axis=1)
    mixed = jnp.dot(cat, wout_ref[...], preferred_element_type=F32)
    x1 = x + gt1 * _rms(mixed, gpost_ref[0]).reshape(g, r, d)
    hn = (_rms(x1, gpre_ref[...]) * (1.0 + sc2) + sh2).reshape(rows, d)
    hb = hn.astype(BF16)
    hlo = (hn - hb.astype(F32)).astype(BF16)
    nt = functools.partial(lax.dot_general, dimension_numbers=(((1,), (1,)), ((), ())), preferred_element_type=F32)
    logits_t = nt(wr_ref[0], hb) + nt(wr_ref[0], hlo) + nt(wr_ref[1], hb) + br_ref[...]
    gates_t, gidx = _route(logits_t)
    ffn = _moe_sorted(hb, gates_t, gidx, tri_ref, wg_ref, wu_ref, wd_ref, xs_scr, gs_scr, ys_scr)
    o_ref[...] = x1 + gt2 * _rms(ffn, gffn_ref[0]).reshape(g, r, d)


def _ffn(x3, ssm_n, attn_n, mod4, fw, gb, rb, chunk=None):
    nb, nr, d = x3.shape
    assert gb == 1 or rb == nr
    half = ssm_n.shape[-1]
    nj = nr // rb
    tm = gb * rb
    row_map = lambda i, j: (i * nj + j, 0)
    vec = _resident((1, 1, d))
    if chunk is None:
        ssm_spec = pl.BlockSpec((tm, half), row_map)
    else:
        assert gb == 1 and rb % chunk == 0
        ssm_spec = pl.BlockSpec((None, chunk, rb // chunk, half), lambda i, j: (i, 0, j, 0))
    return pl.pallas_call(
        functools.partial(_ffn_body, chunked=chunk is not None),
        out_shape=jax.ShapeDtypeStruct((nb, nr, d), F32),
        grid=(nb // gb, nj),
        in_specs=[pl.BlockSpec((gb, rb, d), lambda i, j: (i, j, 0)),
                  ssm_spec,
                  pl.BlockSpec((tm, attn_n.shape[1]), row_map),
                  pl.BlockSpec((gb, N_MOD, 1, d), lambda i, j: (i, 0, 0, 0)),
                  vec, vec, vec,
                  _resident(fw["wout"].shape), _resident(fw["wr"].shape), _resident(fw["br"].shape),
                  _resident((tm, tm)),
                  _resident(fw["wg"].shape), _resident(fw["wu"].shape), _resident(fw["wd"].shape)],
        out_specs=pl.BlockSpec((gb, rb, d), lambda i, j: (i, j, 0)),
        scratch_shapes=[pltpu.VMEM((tm, d), BF16), pltpu.VMEM((tm, LANES), F32), pltpu.VMEM((tm, d), F32)],
        compiler_params=_cparams("arbitrary", "arbitrary"),
        name="outproj_moe",
    )(x3, ssm_n, attn_n, mod4, fw["gpost"], fw["gpre"], fw["gffn"], fw["wout"], fw["wr"], fw["br"],
      jnp.triu(jnp.ones((tm, tm), BF16)), fw["wg"], fw["wu"], fw["wd"])


def _ffn_weights(g_post_mix, g_pre_ffn, g_post_ffn, w_out, w_gr, b_gr, w_er, b_er, w_gate, w_up, w_down):
    d = w_out.shape[0]
    n_route = N_EXPERT_GROUPS + N_EXPERTS
    wr = jnp.concatenate([w_gr, w_er.reshape(d, N_EXPERTS)], axis=1)
    br = jnp.concatenate([b_gr, b_er.reshape(N_EXPERTS)])
    wr = jnp.pad(wr, ((0, 0), (0, ROUTE_ROWS - n_route))).T
    return {
        "gpost": g_post_mix.reshape(1, 1, d), "gpre": g_pre_ffn.reshape(1, 1, d), "gffn": g_post_ffn.reshape(1, 1, d),
        "wout": w_out.astype(BF16),
        "wr": _hi_lo(wr),
        "br": jnp.pad(br, (0, ROUTE_ROWS - n_route)).reshape(ROUTE_ROWS, 1),
        "wg": w_gate.astype(BF16), "wu": w_up.astype(BF16),
        "wd": w_down.astype(BF16).reshape(N_EXPERT_GROUPS, -1, d),
    }


def _rel_bucket(dist):
    max_exact = NUM_BUCKETS // 2
    dd = jnp.maximum(dist, 0)
    log_ratio = jnp.log(jnp.maximum(dd, 1).astype(F32) / max_exact) / math.log(MAX_DISTANCE / max_exact)
    large = jnp.minimum(max_exact + (log_ratio * (NUM_BUCKETS - max_exact)).astype(jnp.int32), NUM_BUCKETS - 1)
    return jnp.where(dd < max_exact, dd, large)


def _bias_and_band(dist, table):
    onehot = (_rel_bucket(dist)[:, :, None] == jnp.arange(NUM_BUCKETS)[None, None, :]).astype(F32)
    bias = jnp.einsum('qkb,bh->hqk', onehot, table.astype(F32), precision=HIGHEST)
    band = ((dist >= 0) & (dist <= WINDOW)).astype(F32)
    return bias, band


def _layer(xp, xs, ck, cv, h0re, h0im, mod_p, mod_s, table, p):
    nb, seq, d = xp.shape
    nbs, t, _ = xs.shape
    w = ck.shape[1]
    kv_w = N_KV_HEADS * HEAD_DIM
    w_in_bf = p["w_in"].astype(BF16)
    sw = _ssm_weights(p["ssm_lam_re"], p["ssm_lam_im"], p["ssm_log_dt"], p["ssm_b_re"], p["ssm_b_im"],
                      p["ssm_c_re"], p["ssm_c_im"], p["ssm_d"], p["w_glu"], p["b_glu"], p["g_ssm_out"])
    fw = _ffn_weights(p["g_post_mix"], p["g_pre_ffn"], p["g_post_ffn"], p["w_out"],
                      p["w_group_router"], p["b_group_router"], p["w_expert_router"], p["b_expert_router"],
                      p["w_exp_gate"], p["w_exp_up"], p["w_exp_down"])
    n_state = sw["lre"].shape[1]
    ng = p["ssm_lam_re"].shape[0]

    u2, q, k, v = _inproj(xp, mod_p, p["g_pre_mix"], w_in_bf, 1, INPROJ_TILE, chunk=SSM_CHUNK)
    blk = WINDOW
    dist_p = jnp.arange(blk)[:, None] + blk - jnp.arange(2 * blk)[None, :]
    bias_p, band_p = _bias_and_band(dist_p, table)
    attn_p = _attn_prompt(q, k, v, p["attn_sinks"], bias_p, band_p, p["g_attn_out"], nb, seq)
    kw = _ssm_chunk_weights(p["ssm_lam_re"], p["ssm_lam_im"], p["ssm_log_dt"], p["ssm_b_re"], p["ssm_b_im"],
                            p["ssm_c_re"], p["ssm_c_im"], SSM_CHUNK, seq // SSM_CHUNK)
    yt, hend = _ssm_chunked(_ssm_transpose(u2, p["ssm_b_re"].shape[2]), kw)
    ssm_p = _ssm_tail_call(yt, u2, sw)
    yp = _ffn(xp, ssm_p, attn_p, mod_p, fw, 1, TOKEN_TILE, chunk=SSM_CHUNK)
    wp = min(WINDOW, seq)
    k_p = k.reshape(nb, seq, kv_w)[:, seq - wp:].reshape(nb, wp, N_KV_HEADS, HEAD_DIM)
    v_p = v.reshape(nb, seq, kv_w)[:, seq - wp:].reshape(nb, wp, N_KV_HEADS, HEAD_DIM)
    n_p = n_state // ng
    hre_p = hend[:, :, 0, :n_p]
    him_p = hend[:, :, 0, n_p:]

    gs = TOKEN_TILE // t
    us, qs, ks, vs = _inproj(xs, mod_s, p["g_pre_mix"], w_in_bf, gs, t)
    dist_s = jnp.arange(t)[:, None] + w - jnp.arange(w + t)[None, :]
    bias_s, band_s = _bias_and_band(dist_s, table)
    attn_s, k_s, v_s = _attn_sample(qs.reshape(nbs, t, -1), ks.reshape(nbs, t, kv_w), vs.reshape(nbs, t, kv_w),
                                    ck.reshape(nbs, w, kv_w), cv.reshape(nbs, w, kv_w),
                                    p["attn_sinks"], bias_s, band_s, p["g_attn_out"], 16)
    u_tm = jnp.swapaxes(us.reshape(nbs, t, -1), 0, 1)
    ssm_tm, hre_s, him_s = _ssm_sample(u_tm, h0re.reshape(nbs, n_state), h0im.reshape(nbs, n_state), sw)
    ssm_s = jnp.swapaxes(ssm_tm, 0, 1).reshape(nbs * t, -1)
    ys = _ffn(xs, ssm_s, attn_s.reshape(nbs * t, -1), mod_s, fw, gs, t)
    k_s = k_s.reshape(nbs, w, N_KV_HEADS, HEAD_DIM)
    v_s = v_s.reshape(nbs, w, N_KV_HEADS, HEAD_DIM)
    hre_s = hre_s.reshape(nbs, ng, n_state // ng)
    him_s = him_s.reshape(nbs, ng, n_state // ng)
    return yp, ys, k_p, v_p, hre_p, him_p, k_s, v_s, hre_s, him_s


def kernel(x_prompt, x_sample, cache_k, cache_v, state_ssm_re, state_ssm_im, c_prompt, c_sample, rel_bias_table,
           w_ada, b_ada, g_pre_mix, g_post_mix, g_pre_ffn, g_post_ffn, w_in, ssm_lam_re, ssm_lam_im, ssm_log_dt,
           ssm_b_re, ssm_b_im, ssm_c_re, ssm_c_im, ssm_d, w_glu, b_glu, attn_sinks, g_ssm_out, g_attn_out, w_out,
           w_group_router, b_group_router, w_expert_router, b_expert_router, w_exp_gate, w_exp_up, w_exp_down):
    params = dict(
        w_ada=w_ada, b_ada=b_ada, g_pre_mix=g_pre_mix, g_post_mix=g_post_mix, g_pre_ffn=g_pre_ffn,
        g_post_ffn=g_post_ffn, w_in=w_in, ssm_lam_re=ssm_lam_re, ssm_lam_im=ssm_lam_im, ssm_log_dt=ssm_log_dt,
        ssm_b_re=ssm_b_re, ssm_b_im=ssm_b_im, ssm_c_re=ssm_c_re, ssm_c_im=ssm_c_im, ssm_d=ssm_d, w_glu=w_glu,
        b_glu=b_glu, attn_sinks=attn_sinks, g_ssm_out=g_ssm_out, g_attn_out=g_attn_out, w_out=w_out,
        w_group_router=w_group_router, b_group_router=b_group_router, w_expert_router=w_expert_router,
        b_expert_router=b_expert_router, w_exp_gate=w_exp_gate, w_exp_up=w_exp_up, w_exp_down=w_exp_down)
    depth = w_in.shape[0]
    nb, nbs = x_prompt.shape[0], x_sample.shape[0]
    d = x_prompt.shape[2]
    pad = (-nb) % 8
    yp, ys = x_prompt, x_sample
    outs = [[] for _ in range(8)]
    for li in range(depth):
        p = {name: val[li] for name, val in params.items()}
        c_all = jnp.concatenate([c_prompt, jnp.zeros((pad, d), F32), c_sample], axis=0)
        mod = _modulation(c_all, p["w_ada"], p["b_ada"])
        mod_p = mod[:nb].reshape(nb, N_MOD, 1, d)
        mod_s = mod[nb + pad:].reshape(nbs, N_MOD, 1, d)
        res = _layer(yp, ys, cache_k[li], cache_v[li], state_ssm_re[li], state_ssm_im[li], mod_p, mod_s,
                     rel_bias_table, p)
        yp, ys = res[0], res[1]
        for acc, val in zip(outs, res[2:]):
            acc.append(val)
    return (yp, ys) + tuple(jnp.stack(o) for o in outs)
```

```python
import functools
import math

import jax
import jax.numpy as jnp
from jax import lax
from jax.experimental import pallas as pl
from jax.experimental.pallas import tpu as pltpu

F32 = jnp.float32
BF16 = jnp.bfloat16
HIGHEST = lax.Precision.HIGHEST

EPS = 1e-6
NEG_INF = -1e30

SSM_GROUP = 16
SSM_STATE = 64
HEAD_DIM = 64
N_KV_HEADS = 2
GQA = 4
N_HEADS = N_KV_HEADS * GQA
WINDOW = 128
NUM_BUCKETS = 32
MAX_DISTANCE = 128
N_EXPERT_GROUPS = 4
EXPERTS_PER_GROUP = 4
N_EXPERTS = N_EXPERT_GROUPS * EXPERTS_PER_GROUP
N_MOD = 6

LANES = 128
SUBLANES = 8
VMEM_LIMIT_BYTES = 56 * 1024 * 1024

TOKEN_TILE = 512
INPROJ_TILE = 1024
INPROJ_SUB = 256
FFN_SUB = 128
ATTN_TILE = 512
SSM_CHUNK = 32
SSM_GROUP_BLOCK = 8
SSM_TAIL_BLOCK = 8
MOE_ROW_BLOCK = 128
ROUTE_ROWS = 32


def _cparams(*sem):
    return pltpu.CompilerParams(dimension_semantics=sem, vmem_limit_bytes=VMEM_LIMIT_BYTES)


def _resident(shape):
    zeros = (0,) * len(shape)
    return pl.BlockSpec(shape, lambda *_: zeros, pipeline_mode=pl.Buffered(1))


def _rms(x, g):
    return x * lax.rsqrt(jnp.mean(x * x, axis=-1, keepdims=True) + EPS) * g


def _mod_body(c_ref, w_ref, b_ref, o_ref):
    a = jax.nn.silu(c_ref[...])
    w = w_ref[...]
    a_hi, w_hi = a.astype(BF16), w.astype(BF16)
    a_lo, w_lo = (a - a_hi.astype(F32)).astype(BF16), (w - w_hi.astype(F32)).astype(BF16)
    dot = functools.partial(jnp.dot, preferred_element_type=F32)
    o_ref[...] = dot(a_hi, w_hi) + dot(a_lo, w_hi) + dot(a_hi, w_lo) + b_ref[...]


def _modulation(c, w_ada, b_ada):
    rows, d = c.shape
    n = w_ada.shape[1]
    tn = 1024
    return pl.pallas_call(
        _mod_body,
        out_shape=jax.ShapeDtypeStruct((rows, n), F32),
        grid=(n // tn,),
        in_specs=[pl.BlockSpec((rows, d), lambda j: (0, 0)),
                  pl.BlockSpec((d, tn), lambda j: (0, j)),
                  pl.BlockSpec((1, tn), lambda j: (0, j))],
        out_specs=pl.BlockSpec((rows, tn), lambda j: (0, j)),
        compiler_params=_cparams("arbitrary"),
        name="modulation",
    )(c, w_ada, b_ada.reshape(1, n))


def _inproj_body(x_ref, mod_ref, g_ref, w_ref, u_ref, q_ref, k_ref, v_ref, *, chunked):
    g, r, d = x_ref.shape
    ssm_w = u_ref.shape[-1]
    attn_w = q_ref.shape[1]
    kv_w = k_ref.shape[1]
    rows = g * r
    sub = min(rows, INPROJ_SUB)
    hns = []
    for c in range(rows // sub):
        if g == 1:
            x = x_ref[:, c * sub:(c + 1) * sub, :]
            hn = _rms(x, g_ref[...]) * (1.0 + mod_ref[:, 1]) + mod_ref[:, 0]
        else:
            gs = sub // r
            x = x_ref[c * gs:(c + 1) * gs]
            hn = _rms(x, g_ref[...]) * (1.0 + mod_ref[c * gs:(c + 1) * gs, 1]) + mod_ref[c * gs:(c + 1) * gs, 0]
        hns.append(hn.reshape(sub, d).astype(BF16))
    for c, hn in enumerate(hns):
        rs = slice(c * sub, (c + 1) * sub)
        proj = jnp.dot(hn, w_ref[...], preferred_element_type=F32)
        if chunked:
            t = u_ref.shape[0]
            nc = sub // t
            u_ref[:, c * nc:(c + 1) * nc, :] = jnp.swapaxes(proj[:, :ssm_w].reshape(nc, t, ssm_w), 0, 1)
        else:
            u_ref[rs, :] = proj[:, :ssm_w]
        q_ref[rs, :] = (proj[:, ssm_w:ssm_w + attn_w] * (HEAD_DIM ** -0.5)).astype(BF16)
        k_ref[rs, :] = proj[:, ssm_w + attn_w:ssm_w + attn_w + kv_w]
        v_ref[rs, :] = proj[:, ssm_w + attn_w + kv_w:]


def _inproj(x3, mod4, g_pre, w_in_bf, gb, rb, chunk=None):
    nb, nr, d = x3.shape
    assert gb == 1 or rb == nr
    n_in = w_in_bf.shape[1]
    kv_w = N_KV_HEADS * HEAD_DIM
    attn_w = N_HEADS * HEAD_DIM
    ssm_w = n_in - attn_w - 2 * kv_w
    rows = nb * nr
    nj = nr // rb
    tm = gb * rb
    out_map = lambda i, j: (i * nj + j, 0)
    if chunk is None:
        u_shape = jax.ShapeDtypeStruct((rows, ssm_w), F32)
        u_spec = pl.BlockSpec((tm, ssm_w), out_map)
    else:
        assert gb == 1 and rb % chunk == 0
        u_shape = jax.ShapeDtypeStruct((nb, chunk, nr // chunk, ssm_w), F32)
        u_spec = pl.BlockSpec((None, chunk, rb // chunk, ssm_w), lambda i, j: (i, 0, j, 0))
    return pl.pallas_call(
        functools.partial(_inproj_body, chunked=chunk is not None),
        out_shape=(u_shape,
                   jax.ShapeDtypeStruct((rows, attn_w), BF16),
                   jax.ShapeDtypeStruct((rows, kv_w), F32),
                   jax.ShapeDtypeStruct((rows, kv_w), F32)),
        grid=(nb // gb, nj),
        in_specs=[pl.BlockSpec((gb, rb, d), lambda i, j: (i, j, 0)),
                  pl.BlockSpec((gb, 2, 1, d), lambda i, j: (i, 0, 0, 0)),
                  _resident((1, 1, d)),
                  _resident((d, n_in))],
        out_specs=(u_spec,
                   pl.BlockSpec((tm, attn_w), out_map),
                   pl.BlockSpec((tm, kv_w), out_map),
                   pl.BlockSpec((tm, kv_w), out_map)),
        compiler_params=_cparams("arbitrary", "arbitrary"),
        name="in_projection",
    )(x3, mod4, g_pre.reshape(1, 1, d), w_in_bf)


def _attn_prompt_body(sinks_ref, q_ref, kc_ref, vc_ref, kp_ref, vp_ref, bias_ref, g_ref, o_ref, s_scr, p_scr):
    tq = q_ref.shape[0]
    blk = bias_ref.shape[2] // 2
    first_tile = pl.program_id(1) == 0
    kk = jnp.concatenate([kp_ref[...], kc_ref[...]], axis=0).astype(BF16)
    vv = jnp.concatenate([vp_ref[...], vc_ref[...]], axis=0).astype(BF16)
    shape = (GQA * blk, 2 * blk)
    col = lax.broadcasted_iota(jnp.int32, shape, 1)
    row = lax.broadcasted_iota(jnp.int32, (GQA * blk, 1), 0)
    no_prev = first_tile & (col < blk)
    sink_cols = []
    for hk in range(N_KV_HEADS):
        sk = jnp.full((GQA * blk, 1), sinks_ref[hk * GQA], F32)
        for gq in range(1, GQA):
            sk = jnp.where(row >= gq * blk, sinks_ref[hk * GQA + gq], sk)
        sink_cols.append(sk)
    pairs = [(j, hk) for j in range(tq // blk) for hk in range(N_KV_HEADS)]
    for idx, (j, hk) in enumerate(pairs):
        qg = jnp.concatenate([q_ref[j * blk:(j + 1) * blk, h * HEAD_DIM:(h + 1) * HEAD_DIM]
                              for h in range(hk * GQA, (hk + 1) * GQA)], axis=0)
        s = lax.dot_general(qg, kk[j * blk:(j + 2) * blk, hk * HEAD_DIM:(hk + 1) * HEAD_DIM],
                            (((1,), (1,)), ((), ())), preferred_element_type=F32) + bias_ref[hk]
        s_scr[idx] = jnp.where(no_prev, NEG_INF, s) if j == 0 else s
    dens = []
    for idx, (j, hk) in enumerate(pairs):
        s = s_scr[idx]
        m = jnp.maximum(jnp.max(s, axis=-1, keepdims=True), sink_cols[hk])
        p = jnp.exp(s - m)
        dens.append(jnp.sum(p, axis=-1, keepdims=True) + jnp.exp(sink_cols[hk] - m))
        p_scr[idx] = p.astype(BF16)
    outs = {}
    for idx, (j, hk) in enumerate(pairs):
        o = jnp.dot(p_scr[idx], vv[j * blk:(j + 2) * blk, hk * HEAD_DIM:(hk + 1) * HEAD_DIM],
                    preferred_element_type=F32) / dens[idx]
        for gq in range(GQA):
            outs[j, hk * GQA + gq] = o[gq * blk:(gq + 1) * blk]
    for j in range(tq // blk):
        att = jnp.concatenate([outs[j, h] for h in range(N_HEADS)], axis=1)
        o_ref[j * blk:(j + 1) * blk, :] = _rms(att, g_ref[...]).astype(BF16)


def _attn_prompt(q, k, v, sinks, bias, band, g_attn, nb, seq):
    rows, attn_w = q.shape
    kv_w = k.shape[1]
    blk = band.shape[0]
    bias = jnp.where(band > 0.5, bias, NEG_INF).reshape(N_KV_HEADS, GQA * blk, 2 * blk)
    tq = ATTN_TILE
    nt = seq // tq
    per = tq // blk
    cur = lambda b, i: (b * nt + i, 0)
    prev = lambda b, i: (b * nt * per + jnp.maximum(i * per - 1, 0), 0)
    return pl.pallas_call(
        _attn_prompt_body,
        out_shape=jax.ShapeDtypeStruct((rows, attn_w), BF16),
        grid=(nb, nt),
        in_specs=[pl.BlockSpec(memory_space=pltpu.SMEM),
                  pl.BlockSpec((tq, attn_w), cur),
                  pl.BlockSpec((tq, kv_w), cur),
                  pl.BlockSpec((tq, kv_w), cur),
                  pl.BlockSpec((blk, kv_w), prev),
                  pl.BlockSpec((blk, kv_w), prev),
                  _resident(bias.shape),
                  _resident((1, attn_w))],
        out_specs=pl.BlockSpec((tq, attn_w), cur),
        scratch_shapes=[pltpu.VMEM((per * N_KV_HEADS, GQA * blk, 2 * blk), F32),
                        pltpu.VMEM((per * N_KV_HEADS, GQA * blk, 2 * blk), BF16)],
        compiler_params=_cparams("arbitrary", "arbitrary"),
        name="attention_prompt",
    )(sinks, q, k, v, k, v, bias, g_attn.reshape(1, attn_w))


def _attn_sample_body(sinks_ref, q_ref, kn_ref, vn_ref, ck_ref, cv_ref, bias_ref, g_ref, o_ref, ko_ref, vo_ref):
    t = q_ref.shape[1]
    kk = jnp.concatenate([ck_ref[...], kn_ref[...]], axis=1)
    vv = jnp.concatenate([cv_ref[...], vn_ref[...]], axis=1)
    ko_ref[...] = kk[:, t:, :]
    vo_ref[...] = vv[:, t:, :]
    kkb = kk.astype(BF16)
    vvb = vv.astype(BF16)
    row = lax.broadcasted_iota(jnp.int32, (GQA * t, 1), 0)
    outs = [None] * N_HEADS
    for hk in range(N_KV_HEADS):
        sk = jnp.full((GQA * t, 1), sinks_ref[hk * GQA], F32)
        for gq in range(1, GQA):
            sk = jnp.where(row >= gq * t, sinks_ref[hk * GQA + gq], sk)
        qg = jnp.concatenate([q_ref[:, :, h * HEAD_DIM:(h + 1) * HEAD_DIM]
                              for h in range(hk * GQA, (hk + 1) * GQA)], axis=1)
        s = jnp.einsum('bqd,bkd->bqk', qg, kkb[:, :, hk * HEAD_DIM:(hk + 1) * HEAD_DIM],
                       preferred_element_type=F32) + bias_ref[hk]
        m = jnp.maximum(jnp.max(s, axis=-1, keepdims=True), sk)
        p = jnp.exp(s - m)
        den = jnp.sum(p, axis=-1, keepdims=True) + jnp.exp(sk - m)
        o = jnp.einsum('bqk,bkd->bqd', p.astype(BF16), vvb[:, :, hk * HEAD_DIM:(hk + 1) * HEAD_DIM],
                       preferred_element_type=F32) / den
        for gq in range(GQA):
            outs[hk * GQA + gq] = o[:, gq * t:(gq + 1) * t, :]
    att = jnp.concatenate(outs, axis=2)
    o_ref[...] = _rms(att, g_ref[...]).astype(BF16)


def _attn_sample(q3, k3, v3, ck, cv, sinks, bias, band, g_attn, gb):
    nb, t, attn_w = q3.shape
    w, kv_w = ck.shape[1], ck.shape[2]
    bias = jnp.where(band > 0.5, bias, NEG_INF).reshape(N_KV_HEADS, GQA * t, w + t)
    blk3 = lambda last: pl.BlockSpec((gb, last[0], last[1]), lambda i: (i, 0, 0))
    return pl.pallas_call(
        _attn_sample_body,
        out_shape=(jax.ShapeDtypeStruct((nb, t, attn_w), BF16),
                   jax.ShapeDtypeStruct((nb, w, kv_w), F32),
                   jax.ShapeDtypeStruct((nb, w, kv_w), F32)),
        grid=(nb // gb,),
        in_specs=[pl.BlockSpec(memory_space=pltpu.SMEM),
                  blk3((t, attn_w)), blk3((t, kv_w)), blk3((t, kv_w)),
                  blk3((w, kv_w)), blk3((w, kv_w)),
                  _resident(bias.shape), _resident((1, 1, attn_w))],
        out_specs=(blk3((t, attn_w)), blk3((w, kv_w)), blk3((w, kv_w))),
        compiler_params=_cparams("arbitrary"),
        name="attention_sample",
    )(sinks, q3, k3, v3, ck, cv, bias, g_attn.reshape(1, 1, attn_w))


def _ssm_tail(y, u, d_ref, wglu_ref, bglu_ref, g_ref):
    z = jax.nn.gelu(y + d_ref[...] * u)
    gate = jax.nn.sigmoid(jnp.dot(z.astype(BF16), wglu_ref[...], preferred_element_type=F32) + bglu_ref[...])
    return _rms(z * gate, g_ref[...]).astype(BF16)


def _ssm_chunk_body(ut_ref, m_ref, e_ref, f_ref, w1_ref, w2_ref, yt_ref, hend_ref, s_scr):
    t, gb, ch, nc = ut_ref.shape
    n_levels = w1_ref.shape[1]
    two_p = e_ref.shape[1]
    row = lax.broadcasted_iota(jnp.int32, (nc, two_p), 0)
    rs = [ut_ref[:, gl].reshape(t * ch, nc) for gl in range(gb)]
    hs = []
    for gl in range(gb):
        s_scr[gl] = jnp.dot(e_ref[gl], rs[gl], preferred_element_type=F32)
        hs.append(s_scr[gl].T)

    def conv(gl):
        yt_ref[:, gl * ch:(gl + 1) * ch, :] = jnp.dot(m_ref[gl], rs[gl], preferred_element_type=F32).reshape(t, ch, nc)

    for lv in range(n_levels):
        sh = 1 << lv
        for gl in range(gb):
            prev = jnp.where(row >= sh, pltpu.roll(hs[gl], sh, axis=0), 0.0)
            hs[gl] = hs[gl] + w1_ref[gl, lv] * prev + w2_ref[gl, lv] * pltpu.roll(prev, two_p // 2, axis=1)
        for gl in range(lv * gb // n_levels, (lv + 1) * gb // n_levels):
            conv(gl)
    for gl in range(gb):
        hend_ref[gl] = hs[gl][nc - 1:nc, :]
        s_scr[gl] = jnp.where(row >= 1, pltpu.roll(hs[gl], 1, axis=0), 0.0).T
        yt_ref[:, gl * ch:(gl + 1) * ch, :] += jnp.dot(f_ref[gl], s_scr[gl].astype(BF16),
                                                       preferred_element_type=F32).reshape(t, ch, nc)


def _ssm_transpose_body(u_ref, o_ref):
    for s in range(u_ref.shape[0]):
        o_ref[s] = u_ref[s].T.astype(BF16).reshape(o_ref.shape[1:])


def _ssm_transpose(u2, ch):
    nb, t, nc, ssm_w = u2.shape
    tb = SSM_TAIL_BLOCK
    return pl.pallas_call(
        _ssm_transpose_body,
        out_shape=jax.ShapeDtypeStruct((nb, t, ssm_w // ch, ch, nc), BF16),
        grid=(nb, t // tb),
        in_specs=[pl.BlockSpec((None, tb, nc, ssm_w), lambda b, i: (b, i, 0, 0))],
        out_specs=pl.BlockSpec((None, tb, ssm_w // ch, ch, nc), lambda b, i: (b, i, 0, 0, 0)),
        compiler_params=_cparams("arbitrary", "arbitrary"),
        name="ssm_transpose",
    )(u2)


def _ssm_chunked(ut, kw):
    nb, t, ng, ch, nc = ut.shape
    tc = t * ch
    two_p = kw["e"].shape[1]
    n_levels = kw["w1"].shape[1]
    gb = SSM_GROUP_BLOCK
    grp = lambda shape: pl.BlockSpec((gb,) + shape, lambda b, g: (g,) + (0,) * len(shape))
    return pl.pallas_call(
        _ssm_chunk_body,
        out_shape=(jax.ShapeDtypeStruct((nb, t, ng * ch, nc), F32),
                   jax.ShapeDtypeStruct((nb, ng, 1, two_p), F32)),
        grid=(nb, ng // gb),
        in_specs=[pl.BlockSpec((None, t, gb, ch, nc), lambda b, g: (b, 0, g, 0, 0)),
                  grp((tc, tc)), grp((two_p, tc)), grp((tc, two_p)),
                  grp((n_levels, 1, two_p)), grp((n_levels, 1, two_p))],
        out_specs=(pl.BlockSpec((None, t, gb * ch, nc), lambda b, g: (b, 0, g, 0)),
                   pl.BlockSpec((None, gb, 1, two_p), lambda b, g: (b, g, 0, 0))),
        scratch_shapes=[pltpu.VMEM((gb, two_p, nc), F32)],
        compiler_params=_cparams("arbitrary", "arbitrary"),
        name="ssm_chunked",
    )(ut, kw["m"], kw["e"], kw["f"], kw["w1"], kw["w2"])


def _ssm_tail_body(yt_ref, u_ref, d_ref, wglu_ref, bglu_ref, g_ref, o_ref):
    for i in range(yt_ref.shape[0]):
        o_ref[i] = _ssm_tail(yt_ref[i].T, u_ref[i], d_ref, wglu_ref, bglu_ref, g_ref)


def _ssm_tail_call(yt, u2, sw):
    nb, t, nc, ssm_w = u2.shape
    tb = SSM_TAIL_BLOCK
    return pl.pallas_call(
        _ssm_tail_body,
        out_shape=jax.ShapeDtypeStruct((nb, t, nc, ssm_w), BF16),
        grid=(nb, t // tb),
        in_specs=[pl.BlockSpec((None, tb, ssm_w, nc), lambda b, i: (b, i, 0, 0)),
                  pl.BlockSpec((None, tb, nc, ssm_w), lambda b, i: (b, i, 0, 0)),
                  _resident((1, ssm_w)), _resident((ssm_w, ssm_w)), _resident((1, ssm_w)), _resident((1, ssm_w))],
        out_specs=pl.BlockSpec((None, tb, nc, ssm_w), lambda b, i: (b, i, 0, 0)),
        compiler_params=_cparams("arbitrary", "arbitrary"),
        name="ssm_tail",
    )(yt, u2, sw["d"], sw["wglu"], sw["bglu"], sw["g"])


def _lag_matrix_body(z_ref, m_ref, *, t):
    gb, ch, _ = z_ref.shape
    width = t * ch
    for gl in range(gb):
        z = z_ref[gl]
        for r in range(t):
            off = (t - 1 - r) * ch
            m_ref[gl, r * ch:(r + 1) * ch, :] = z[:, off:off + width].astype(BF16)


def _lag_matrix(z, t):
    ng, ch, zw = z.shape
    gb = SSM_GROUP_BLOCK
    return pl.pallas_call(
        functools.partial(_lag_matrix_body, t=t),
        out_shape=jax.ShapeDtypeStruct((ng, t * ch, zw // 2), BF16),
        grid=(ng // gb,),
        in_specs=[pl.BlockSpec((gb, ch, zw), lambda g: (g, 0, 0))],
        out_specs=pl.BlockSpec((gb, t * ch, zw // 2), lambda g: (g, 0, 0)),
        compiler_params=_cparams("arbitrary"),
        name="ssm_lag_matrix",
    )(z)


def _ssm_chunk_weights(lam_re, lam_im, log_dt, b_re, b_im, c_re, c_im, t, n_chunks):
    ng, ns = lam_re.shape
    nc = b_re.shape[2]
    dt = jnp.exp(log_dt)[:, None]
    are, aim = lam_re * dt, lam_im * dt
    d = jnp.arange(t + 1, dtype=F32)[:, None, None]
    mag = jnp.exp(d * are)
    pre, pim = mag * jnp.cos(d * aim), mag * jnp.sin(d * aim)
    lre, lim = pre[1], pim[1]
    den = lam_re * lam_re + lam_im * lam_im
    fre = ((lre - 1.0) * lam_re + lim * lam_im) / den
    fim = (lim * lam_re - (lre - 1.0) * lam_im) / den
    bbr = fre[..., None] * b_re - fim[..., None] * b_im
    bbi = fre[..., None] * b_im + fim[..., None] * b_re
    pre_g, pim_g = jnp.transpose(pre, (1, 0, 2)), jnp.transpose(pim, (1, 0, 2))
    xr = c_re[:, None] * pre_g[:, :, None, :] - c_im[:, None] * pim_g[:, :, None, :]
    xi = c_re[:, None] * pim_g[:, :, None, :] + c_im[:, None] * pre_g[:, :, None, :]
    kern = jnp.einsum('gdap,gpc->gdac', jnp.concatenate([xr[:, :t], -xi[:, :t]], axis=3),
                      jnp.concatenate([bbr, bbi], axis=1), precision=HIGHEST)
    lag_rows = jnp.transpose(kern[:, ::-1], (0, 2, 1, 3)).reshape(ng, nc, t * nc)
    m = _lag_matrix(jnp.concatenate([lag_rows, jnp.zeros_like(lag_rows)], axis=2), t)
    rev_re = jnp.transpose(pre_g[:, t - 1::-1][:, :t], (0, 2, 1))
    rev_im = jnp.transpose(pim_g[:, t - 1::-1][:, :t], (0, 2, 1))
    er = rev_re[..., None] * bbr[:, :, None, :] - rev_im[..., None] * bbi[:, :, None, :]
    ei = rev_re[..., None] * bbi[:, :, None, :] + rev_im[..., None] * bbr[:, :, None, :]
    e = jnp.concatenate([er.reshape(ng, ns, t * nc), ei.reshape(ng, ns, t * nc)], axis=1).astype(BF16)
    f = jnp.concatenate([xr[:, 1:].reshape(ng, t * nc, ns), -xi[:, 1:].reshape(ng, t * nc, ns)], axis=2).astype(BF16)
    wr, wi = pre[t], pim[t]
    w1, w2 = [], []
    for _ in range(max(1, (n_chunks - 1).bit_length())):
        w1.append(jnp.concatenate([wr, wr], axis=1))
        w2.append(jnp.concatenate([-wi, wi], axis=1))
        wr, wi = wr * wr - wi * wi, 2.0 * wr * wi
    w1 = jnp.stack(w1, axis=1)[:, :, None, :]
    w2 = jnp.stack(w2, axis=1)[:, :, None, :]
    return {"m": m, "e": e, "f": f, "w1": w1, "w2": w2}


def _ssm_sample_body(u_ref, h0re_ref, h0im_ref, bre_ref, bim_ref, cre_ref, cim_ref, lre_ref, lim_ref,
                     d_ref, wglu_ref, bglu_ref, g_ref, o_ref, hre_ref, him_ref):
    steps = u_ref.shape[0]
    half_in = bre_ref.shape[2]
    dot = functools.partial(jnp.dot, preferred_element_type=F32)
    ar = lre_ref[...]
    ai = lim_ref[...]
    hr = h0re_ref[...]
    hi = h0im_ref[...]
    for t in range(steps):
        u = u_ref[t]
        u_hi = u.astype(BF16)
        u_lo = (u - u_hi.astype(F32)).astype(BF16)

        def bu(b_ref):
            cols = []
            for hf in range(2):
                lanes = slice(hf * half_in, (hf + 1) * half_in)
                cols.append(dot(u_hi[:, lanes], b_ref[0, hf]) + dot(u_lo[:, lanes], b_ref[0, hf])
                            + dot(u_hi[:, lanes], b_ref[1, hf]))
            return jnp.concatenate(cols, axis=1)

        hr, hi = ar * hr - ai * hi + bu(bre_ref), ar * hi + ai * hr + bu(bim_ref)
        half_st = hr.shape[1] // 2
        hrb, hib = hr.astype(BF16), hi.astype(BF16)
        y = jnp.concatenate(
            [dot(hrb[:, hf * half_st:(hf + 1) * half_st], cre_ref[hf])
             + dot(hib[:, hf * half_st:(hf + 1) * half_st], cim_ref[hf]) for hf in range(2)], axis=1)
        o_ref[t] = _ssm_tail(y, u, d_ref, wglu_ref, bglu_ref, g_ref)
    hre_ref[...] = hr
    him_ref[...] = hi


def _ssm_sample(u_tm, h0re, h0im, sw):
    steps, nb, ssm_w = u_tm.shape
    n_state = h0re.shape[1]
    return pl.pallas_call(
        _ssm_sample_body,
        out_shape=(jax.ShapeDtypeStruct((steps, nb, ssm_w), BF16),
                   jax.ShapeDtypeStruct((nb, n_state), F32),
                   jax.ShapeDtypeStruct((nb, n_state), F32)),
        compiler_params=pltpu.CompilerParams(vmem_limit_bytes=VMEM_LIMIT_BYTES),
        name="ssm_sample",
    )(u_tm, h0re, h0im, _hi_lo(sw["bre"]), _hi_lo(sw["bim"]), sw["cre"].astype(BF16), sw["cim"].astype(BF16),
      sw["lre"], sw["lim"], sw["d"], sw["wglu"], sw["bglu"], sw["g"])


def _hi_lo(w):
    hi = w.astype(BF16)
    return jnp.stack([hi, (w - hi.astype(F32)).astype(BF16)])


def _ssm_weights(lam_re, lam_im, log_dt, b_re, b_im, c_re, c_im, d_skip, w_glu, b_glu, g_ssm):
    ng, ns = lam_re.shape
    nc = b_re.shape[2]
    dt = jnp.exp(log_dt)[:, None]
    mag = jnp.exp(lam_re * dt)
    lre = mag * jnp.cos(lam_im * dt)
    lim = mag * jnp.sin(lam_im * dt)
    den = lam_re * lam_re + lam_im * lam_im
    fre = ((lre - 1.0) * lam_re + lim * lam_im) / den
    fim = (lim * lam_re - (lre - 1.0) * lam_im) / den
    bbar_re = fre[..., None] * b_re - fim[..., None] * b_im
    bbar_im = fre[..., None] * b_im + fim[..., None] * b_re
    eye = jnp.eye(ng // 2, dtype=F32)

    def in_blocks(b):
        b2 = b.reshape(2, ng // 2, ns, nc)
        return jnp.einsum('hgpc,gk->hgckp', b2, eye).reshape(2, ng // 2 * nc, ng // 2 * ns)

    def out_blocks(c):
        c2 = c.reshape(2, ng // 2, nc, ns)
        return jnp.einsum('hgcp,gk->hgpkc', c2, eye).reshape(2, ng // 2 * ns, ng // 2 * nc)

    ssm_w = ng * nc
    return {
        "bre": in_blocks(bbar_re), "bim": in_blocks(bbar_im),
        "cre": out_blocks(c_re), "cim": out_blocks(-c_im),
        "lre": lre.reshape(1, ng * ns), "lim": lim.reshape(1, ng * ns),
        "d": d_skip.reshape(1, ssm_w), "wglu": w_glu.astype(BF16), "bglu": b_glu.reshape(1, ssm_w),
        "g": g_ssm.reshape(1, ssm_w),
    }


def _route(logits_t):
    big = jnp.int32(10 ** 6)
    gl = logits_t[:N_EXPERT_GROUPS]
    grow = lax.broadcasted_iota(jnp.int32, gl.shape, 0)
    gmax = jnp.max(gl, axis=0, keepdims=True)
    gidx = jnp.min(jnp.where(gl == gmax, grow, big), axis=0, keepdims=True)
    g_p = 1.0 / jnp.sum(jnp.exp(gl - gmax), axis=0, keepdims=True)
    ex = logits_t[N_EXPERT_GROUPS:N_EXPERT_GROUPS + N_EXPERTS]
    erow = lax.broadcasted_iota(jnp.int32, ex.shape, 0)
    lo = gidx * EXPERTS_PER_GROUP
    in_group = (erow >= lo) & (erow < lo + EXPERTS_PER_GROUP)
    el = jnp.where(in_group, ex, NEG_INF)
    ee = jnp.exp(el - jnp.max(el, axis=0, keepdims=True))
    prob = ee / jnp.sum(ee, axis=0, keepdims=True)
    p1 = jnp.max(jnp.where(in_group, prob, -1.0), axis=0, keepdims=True)
    i1 = jnp.min(jnp.where(in_group & (prob == p1), erow, big), axis=0, keepdims=True)
    rest = in_group & (erow != i1)
    p2 = jnp.max(jnp.where(rest, prob, -1.0), axis=0, keepdims=True)
    i2 = jnp.min(jnp.where(rest & (prob == p2), erow, big), axis=0, keepdims=True)
    tot = p1 + p2
    return jnp.where(erow == i1, g_p * p1 / tot, jnp.where(erow == i2, g_p * p2 / tot, 0.0)), gidx


def _split3(x):
    a = x.astype(BF16)
    r = x - a.astype(F32)
    b = r.astype(BF16)
    return a, b, (r - b.astype(F32)).astype(BF16)


def _moe_sorted(hb, gates_t, gidx, tri_ref, wg_ref, wu_ref, wd_ref, xs_scr, gs_scr, ys_scr):
    rows, d = hb.shape
    grow = lax.broadcasted_iota(jnp.int32, (SUBLANES, rows), 0)
    member = grow == gidx
    csum = jnp.dot(member.astype(BF16), tri_ref[...], preferred_element_type=F32)
    rank = jnp.sum(jnp.where(member, csum, 0.0), axis=0, keepdims=True)
    counts = [jnp.sum((gidx == grp).astype(jnp.int32)) for grp in range(N_EXPERT_GROUPS - 1)]
    offs = [jnp.int32(0)]
    for c in counts:
        offs.append(offs[-1] + c)
    offs.append(jnp.int32(rows))
    base = jnp.zeros_like(gidx)
    for grp in range(1, N_EXPERT_GROUPS):
        base = jnp.where(gidx == grp, offs[grp], base)
    pos_lanes = base.astype(F32) + rank - 1.0
    pos = jnp.broadcast_to(pos_lanes, (SUBLANES, rows)).T[:, 0:1]
    col = lax.broadcasted_iota(jnp.int32, (rows, rows), 1).astype(F32)
    row = lax.broadcasted_iota(jnp.int32, (rows, rows), 0).astype(F32)
    unsort = (col == pos).astype(BF16)
    sort = (row == pos_lanes).astype(BF16)
    xs_scr[...] = jnp.dot(sort, hb, preferred_element_type=F32).astype(BF16)
    pad = jnp.zeros((LANES - N_EXPERT_GROUPS - N_EXPERTS, rows), F32)
    gates_pad = jnp.concatenate([jnp.zeros((N_EXPERT_GROUPS, rows), F32), gates_t, pad], axis=0)
    gs_t = sum(jnp.dot(part, unsort, preferred_element_type=F32) for part in _split3(gates_pad))
    gs_scr[...] = gs_t.T
    ys_scr[...] = jnp.zeros_like(ys_scr)
    blk = MOE_ROW_BLOCK
    n_blk = rows // blk
    cuts = [jnp.int32(k * blk) for k in range(1, n_blk)] + offs[1:N_EXPERT_GROUPS]
    assert len(cuts) == 6
    for a, b in ((0, 5), (1, 3), (2, 4), (1, 2), (3, 4), (0, 3), (2, 5), (0, 1), (2, 3), (4, 5), (1, 2), (3, 4)):
        cuts[a], cuts[b] = jnp.minimum(cuts[a], cuts[b]), jnp.maximum(cuts[a], cuts[b])
    cuts = [jnp.int32(0)] + cuts + [jnp.int32(rows)]
    lane_b = lax.broadcasted_iota(jnp.int32, (blk, LANES), 1)
    for j in range(len(cuts) - 1):
        start = cuts[j]
        live = (cuts[j + 1] > start).astype(F32)
        k = jnp.minimum(start // blk, n_blk - 1)
        grp = sum((offs[q] <= start).astype(jnp.int32) for q in range(1, N_EXPERT_GROUPS))
        sl = pl.ds(pl.multiple_of(k * blk, blk), blk)
        xb = xs_scr[sl, :]
        gb = gs_scr[sl, :] * live
        acts = []
        for i in range(EXPERTS_PER_GROUP):
            e = grp * EXPERTS_PER_GROUP + i
            gate = jnp.sum(jnp.where(lane_b == N_EXPERT_GROUPS + e, gb, 0.0), axis=-1, keepdims=True)
            hg = jnp.dot(xb, wg_ref[e], preferred_element_type=F32)
            hu = jnp.dot(xb, wu_ref[e], preferred_element_type=F32)
            acts.append((jax.nn.silu(hg) * hu * gate).astype(BF16))
        ys_scr[sl, :] += jnp.dot(jnp.concatenate(acts, axis=1), wd_ref[grp], preferred_element_type=F32)
    hi = ys_scr[...].astype(BF16)
    lo = (ys_scr[...] - hi.astype(F32)).astype(BF16)
    return unsort, hi, lo


def _ffn_body(x_ref, ssm_ref, attn_ref, mod_ref, gpost_ref, gpre_ref, gffn_ref, wout_ref, wr_ref, br_ref,
              tri_ref, wg_ref, wu_ref, wd_ref, o_ref, xs_scr, gs_scr, ys_scr, *, chunked):
    g, r, d = x_ref.shape
    rows = g * r
    if chunked:
        ssm = jnp.swapaxes(ssm_ref[...].astype(F32), 0, 1).reshape(rows, ssm_ref.shape[-1]).astype(BF16)
    else:
        ssm = ssm_ref[...]
    sub = min(rows, FFN_SUB)
    x1s, hbs, hlos = [], [], []
    for c in range(rows // sub):
        rs = slice(c * sub, (c + 1) * sub)
        if g == 1:
            x = x_ref[:, rs, :]
            md = mod_ref
        else:
            gsl = slice(c * (sub // r), (c + 1) * (sub // r))
            x = x_ref[gsl]
            md = mod_ref.at[gsl]
        gg = x.shape[0]
        cat = jnp.concatenate([ssm[rs], attn_ref[rs, :]], axis=1)
        mixed = jnp.dot(cat, wout_ref[...], preferred_element_type=F32)
        x1 = x + md[:, 2] * _rms(mixed, gpost_ref[0]).reshape(gg, -1, d)
        hn = (_rms(x1, gpre_ref[...]) * (1.0 + md[:, 4]) + md[:, 3]).reshape(sub, d)
        hb = hn.astype(BF16)
        x1s.append(x1)
        hbs.append(hb)
        hlos.append((hn - hb.astype(F32)).astype(BF16))
    hb = jnp.concatenate(hbs, axis=0)
    hlo = jnp.concatenate(hlos, axis=0)
    nt = functools.partial(lax.dot_general, dimension_numbers=(((1,), (1,)), ((), ())), preferred_element_type=F32)
    logits_t = nt(wr_ref[0], hb) + nt(wr_ref[0], hlo) + nt(wr_ref[1], hb) + br_ref[...]
    gates_t, gidx = _route(logits_t)
    unsort, hi, lo = _moe_sorted(hb, gates_t, gidx, tri_ref, wg_ref, wu_ref, wd_ref, xs_scr, gs_scr, ys_scr)
    for c in range(rows // sub):
        rs = slice(c * sub, (c + 1) * sub)
        ffn = (jnp.dot(unsort[rs], hi, preferred_element_type=F32) + jnp.dot(unsort[rs], lo, preferred_element_type=F32))
        if g == 1:
            o_ref[:, rs, :] = x1s[c] + mod_ref[:, 5] * _rms(ffn, gffn_ref[0]).reshape(1, sub, d)
        else:
            gsl = slice(c * (sub // r), (c + 1) * (sub // r))
            o_ref[gsl] = x1s[c] + mod_ref[gsl, 5] * _rms(ffn, gffn_ref[0]).reshape(sub // r, r, d)


def _ffn(x3, ssm_n, attn_n, mod4, fw, gb, rb, chunk=None):
    nb, nr, d = x3.shape
    assert gb == 1 or rb == nr
    half = ssm_n.shape[-1]
    nj = nr // rb
    tm = gb * rb
    row_map = lambda i, j: (i * nj + j, 0)
    vec = _resident((1, 1, d))
    if chunk is None:
        ssm_spec = pl.BlockSpec((tm, half), row_map)
    else:
        assert gb == 1 and rb % chunk == 0
        ssm_spec = pl.BlockSpec((None, chunk, rb // chunk, half), lambda i, j: (i, 0, j, 0))
    return pl.pallas_call(
        functools.partial(_ffn_body, chunked=chunk is not None),
        out_shape=jax.ShapeDtypeStruct((nb, nr, d), F32),
        grid=(nb // gb, nj),
        in_specs=[pl.BlockSpec((gb, rb, d), lambda i, j: (i, j, 0)),
                  ssm_spec,
                  pl.BlockSpec((tm, attn_n.shape[1]), row_map),
                  pl.BlockSpec((gb, N_MOD, 1, d), lambda i, j: (i, 0, 0, 0)),
                  vec, vec, vec,
                  _resident(fw["wout"].shape), _resident(fw["wr"].shape), _resident(fw["br"].shape),
                  _resident((tm, tm)),
                  _resident(fw["wg"].shape), _resident(fw["wu"].shape), _resident(fw["wd"].shape)],
        out_specs=pl.BlockSpec((gb, rb, d), lambda i, j: (i, j, 0)),
        scratch_shapes=[pltpu.VMEM((tm, d), BF16), pltpu.VMEM((tm, LANES), F32), pltpu.VMEM((tm, d), F32)],
        compiler_params=_cparams("arbitrary", "arbitrary"),
        name="outproj_moe",
    )(x3, ssm_n, attn_n, mod4, fw["gpost"], fw["gpre"], fw["gffn"], fw["wout"], fw["wr"], fw["br"],
      jnp.triu(jnp.ones((tm, tm), BF16)), fw["wg"], fw["wu"], fw["wd"])


def _ffn_weights(g_post_mix, g_pre_ffn, g_post_ffn, w_out, w_gr, b_gr, w_er, b_er, w_gate, w_up, w_down):
    d = w_out.shape[0]
    n_route = N_EXPERT_GROUPS + N_EXPERTS
    wr = jnp.concatenate([w_gr, w_er.reshape(d, N_EXPERTS)], axis=1)
    br = jnp.concatenate([b_gr, b_er.reshape(N_EXPERTS)])
    wr = jnp.pad(wr, ((0, 0), (0, ROUTE_ROWS - n_route))).T
    return {
        "gpost": g_post_mix.reshape(1, 1, d), "gpre": g_pre_ffn.reshape(1, 1, d), "gffn": g_post_ffn.reshape(1, 1, d),
        "wout": w_out.astype(BF16),
        "wr": _hi_lo(wr),
        "br": jnp.pad(br, (0, ROUTE_ROWS - n_route)).reshape(ROUTE_ROWS, 1),
        "wg": w_gate.astype(BF16), "wu": w_up.astype(BF16),
        "wd": w_down.astype(BF16).reshape(N_EXPERT_GROUPS, -1, d),
    }


def _rel_bucket(dist):
    max_exact = NUM_BUCKETS // 2
    dd = jnp.maximum(dist, 0)
    log_ratio = jnp.log(jnp.maximum(dd, 1).astype(F32) / max_exact) / math.log(MAX_DISTANCE / max_exact)
    large = jnp.minimum(max_exact + (log_ratio * (NUM_BUCKETS - max_exact)).astype(jnp.int32), NUM_BUCKETS - 1)
    return jnp.where(dd < max_exact, dd, large)


def _bias_and_band(dist, table):
    onehot = (_rel_bucket(dist)[:, :, None] == jnp.arange(NUM_BUCKETS)[None, None, :]).astype(F32)
    bias = jnp.einsum('qkb,bh->hqk', onehot, table.astype(F32), precision=HIGHEST)
    band = ((dist >= 0) & (dist <= WINDOW)).astype(F32)
    return bias, band


def _layer(xp, xs, ck, cv, h0re, h0im, mod_p, mod_s, table, p):
    nb, seq, d = xp.shape
    nbs, t, _ = xs.shape
    w = ck.shape[1]
    kv_w = N_KV_HEADS * HEAD_DIM
    w_in_bf = p["w_in"].astype(BF16)
    sw = _ssm_weights(p["ssm_lam_re"], p["ssm_lam_im"], p["ssm_log_dt"], p["ssm_b_re"], p["ssm_b_im"],
                      p["ssm_c_re"], p["ssm_c_im"], p["ssm_d"], p["w_glu"], p["b_glu"], p["g_ssm_out"])
    fw = _ffn_weights(p["g_post_mix"], p["g_pre_ffn"], p["g_post_ffn"], p["w_out"],
                      p["w_group_router"], p["b_group_router"], p["w_expert_router"], p["b_expert_router"],
                      p["w_exp_gate"], p["w_exp_up"], p["w_exp_down"])
    n_state = sw["lre"].shape[1]
    ng = p["ssm_lam_re"].shape[0]

    u2, q, k, v = _inproj(xp, mod_p, p["g_pre_mix"], w_in_bf, 1, INPROJ_TILE, chunk=SSM_CHUNK)
    blk = WINDOW
    dist_p = jnp.arange(blk)[:, None] + blk - jnp.arange(2 * blk)[None, :]
    bias_p, band_p = _bias_and_band(dist_p, table)
    attn_p = _attn_prompt(q, k, v, p["attn_sinks"], bias_p, band_p, p["g_attn_out"], nb, seq)
    kw = _ssm_chunk_weights(p["ssm_lam_re"], p["ssm_lam_im"], p["ssm_log_dt"], p["ssm_b_re"], p["ssm_b_im"],
                            p["ssm_c_re"], p["ssm_c_im"], SSM_CHUNK, seq // SSM_CHUNK)
    yt, hend = _ssm_chunked(_ssm_transpose(u2, p["ssm_b_re"].shape[2]), kw)
    ssm_p = _ssm_tail_call(yt, u2, sw)
    yp = _ffn(xp, ssm_p, attn_p, mod_p, fw, 1, TOKEN_TILE, chunk=SSM_CHUNK)
    wp = min(WINDOW, seq)
    k_p = k.reshape(nb, seq, kv_w)[:, seq - wp:].reshape(nb, wp, N_KV_HEADS, HEAD_DIM)
    v_p = v.reshape(nb, seq, kv_w)[:, seq - wp:].reshape(nb, wp, N_KV_HEADS, HEAD_DIM)
    n_p = n_state // ng
    hre_p = hend[:, :, 0, :n_p]
    him_p = hend[:, :, 0, n_p:]

    gs = TOKEN_TILE // t
    us, qs, ks, vs = _inproj(xs, mod_s, p["g_pre_mix"], w_in_bf, gs, t)
    dist_s = jnp.arange(t)[:, None] + w - jnp.arange(w + t)[None, :]
    bias_s, band_s = _bias_and_band(dist_s, table)
    attn_s, k_s, v_s = _attn_sample(qs.reshape(nbs, t, -1), ks.reshape(nbs, t, kv_w), vs.reshape(nbs, t, kv_w),
                                    ck.reshape(nbs, w, kv_w), cv.reshape(nbs, w, kv_w),
                                    p["attn_sinks"], bias_s, band_s, p["g_attn_out"], 16)
    u_tm = jnp.swapaxes(us.reshape(nbs, t, -1), 0, 1)
    ssm_tm, hre_s, him_s = _ssm_sample(u_tm, h0re.reshape(nbs, n_state), h0im.reshape(nbs, n_state), sw)
    ssm_s = jnp.swapaxes(ssm_tm, 0, 1).reshape(nbs * t, -1)
    ys = _ffn(xs, ssm_s, attn_s.reshape(nbs * t, -1), mod_s, fw, gs, t)
    k_s = k_s.reshape(nbs, w, N_KV_HEADS, HEAD_DIM)
    v_s = v_s.reshape(nbs, w, N_KV_HEADS, HEAD_DIM)
    hre_s = hre_s.reshape(nbs, ng, n_state // ng)
    him_s = him_s.reshape(nbs, ng, n_state // ng)
    return yp, ys, k_p, v_p, hre_p, him_p, k_s, v_s, hre_s, him_s


def kernel(x_prompt, x_sample, cache_k, cache_v, state_ssm_re, state_ssm_im, c_prompt, c_sample, rel_bias_table,
           w_ada, b_ada, g_pre_mix, g_post_mix, g_pre_ffn, g_post_ffn, w_in, ssm_lam_re, ssm_lam_im, ssm_log_dt,
           ssm_b_re, ssm_b_im, ssm_c_re, ssm_c_im, ssm_d, w_glu, b_glu, attn_sinks, g_ssm_out, g_attn_out, w_out,
           w_group_router, b_group_router, w_expert_router, b_expert_router, w_exp_gate, w_exp_up, w_exp_down):
    params = dict(
        w_ada=w_ada, b_ada=b_ada, g_pre_mix=g_pre_mix, g_post_mix=g_post_mix, g_pre_ffn=g_pre_ffn,
        g_post_ffn=g_post_ffn, w_in=w_in, ssm_lam_re=ssm_lam_re, ssm_lam_im=ssm_lam_im, ssm_log_dt=ssm_log_dt,
        ssm_b_re=ssm_b_re, ssm_b_im=ssm_b_im, ssm_c_re=ssm_c_re, ssm_c_im=ssm_c_im, ssm_d=ssm_d, w_glu=w_glu,
        b_glu=b_glu, attn_sinks=attn_sinks, g_ssm_out=g_ssm_out, g_attn_out=g_attn_out, w_out=w_out,
        w_group_router=w_group_router, b_group_router=b_group_router, w_expert_router=w_expert_router,
        b_expert_router=b_expert_router, w_exp_gate=w_exp_gate, w_exp_up=w_exp_up, w_exp_down=w_exp_down)
    depth = w_in.shape[0]
    nb, nbs = x_prompt.shape[0], x_sample.shape[0]
    d = x_prompt.shape[2]
    pad = (-nb) % 8
    yp, ys = x_prompt, x_sample
    outs = [[] for _ in range(8)]
    for li in range(depth):
        p = {name: val[li] for name, val in params.items()}
        c_all = jnp.concatenate([c_prompt, jnp.zeros((pad, d), F32), c_sample], axis=0)
        mod = _modulation(c_all, p["w_ada"], p["b_ada"])
        mod_p = mod[:nb].reshape(nb, N_MOD, 1, d)
        mod_s = mod[nb + pad:].reshape(nbs, N_MOD, 1, d)
        res = _layer(yp, ys, cache_k[li], cache_v[li], state_ssm_re[li], state_ssm_im[li], mod_p, mod_s,
                     rel_bias_table, p)
        yp, ys = res[0], res[1]
        for acc, val in zip(outs, res[2:]):
            acc.append(val)
    return (yp, ys) + tuple(jnp.stack(o) for o in outs)
```

```python
import functools
import math

import jax
import jax.numpy as jnp
from jax import lax
from jax.experimental import pallas as pl
from jax.experimental.pallas import tpu as pltpu

F32 = jnp.float32
BF16 = jnp.bfloat16
HIGHEST = lax.Precision.HIGHEST

EPS = 1e-6
NEG_INF = -1e30

SSM_GROUP = 16
SSM_STATE = 64
HEAD_DIM = 64
N_KV_HEADS = 2
GQA = 4
N_HEADS = N_KV_HEADS * GQA
WINDOW = 128
NUM_BUCKETS = 32
MAX_DISTANCE = 128
N_EXPERT_GROUPS = 4
EXPERTS_PER_GROUP = 4
N_EXPERTS = N_EXPERT_GROUPS * EXPERTS_PER_GROUP
N_MOD = 6

LANES = 128
SUBLANES = 8
VMEM_LIMIT_BYTES = 56 * 1024 * 1024

TOKEN_TILE = 512
INPROJ_TILE = 1024
INPROJ_SUB = 256
FFN_SUB = 128
ATTN_TILE = 1024
SSM_CHUNK = 32
SSM_GROUP_BLOCK = 16
SSM_TAIL_BLOCK = 8
MOE_ROW_BLOCK = 128
ROUTE_ROWS = 32


def _cparams(*sem):
    return pltpu.CompilerParams(dimension_semantics=sem, vmem_limit_bytes=VMEM_LIMIT_BYTES)


def _resident(shape):
    zeros = (0,) * len(shape)
    return pl.BlockSpec(shape, lambda *_: zeros, pipeline_mode=pl.Buffered(1))


def _rms(x, g):
    return x * lax.rsqrt(jnp.mean(x * x, axis=-1, keepdims=True) + EPS) * g


def _mod_body(c_ref, w_ref, b_ref, o_ref):
    a = jax.nn.silu(c_ref[...])
    w = w_ref[...]
    a_hi, w_hi = a.astype(BF16), w.astype(BF16)
    a_lo, w_lo = (a - a_hi.astype(F32)).astype(BF16), (w - w_hi.astype(F32)).astype(BF16)
    dot = functools.partial(jnp.dot, preferred_element_type=F32)
    o_ref[...] = dot(a_hi, w_hi) + dot(a_lo, w_hi) + dot(a_hi, w_lo) + b_ref[...]


def _modulation(c, w_ada, b_ada):
    rows, d = c.shape
    n = w_ada.shape[1]
    tn = 1024
    return pl.pallas_call(
        _mod_body,
        out_shape=jax.ShapeDtypeStruct((rows, n), F32),
        grid=(n // tn,),
        in_specs=[pl.BlockSpec((rows, d), lambda j: (0, 0)),
                  pl.BlockSpec((d, tn), lambda j: (0, j)),
                  pl.BlockSpec((1, tn), lambda j: (0, j))],
        out_specs=pl.BlockSpec((rows, tn), lambda j: (0, j)),
        compiler_params=_cparams("arbitrary"),
        name="modulation",
    )(c, w_ada, b_ada.reshape(1, n))


def _inproj_body(x_ref, mod_ref, g_ref, w_ref, u_ref, q_ref, k_ref, v_ref, *, chunked):
    g, r, d = x_ref.shape
    ssm_w = u_ref.shape[-1]
    attn_w = q_ref.shape[1]
    kv_w = k_ref.shape[1]
    rows = g * r
    sub = min(rows, INPROJ_SUB)
    hns = []
    for c in range(rows // sub):
        if g == 1:
            x = x_ref[:, c * sub:(c + 1) * sub, :]
            hn = _rms(x, g_ref[...]) * (1.0 + mod_ref[:, 1]) + mod_ref[:, 0]
        else:
            gs = sub // r
            x = x_ref[c * gs:(c + 1) * gs]
            hn = _rms(x, g_ref[...]) * (1.0 + mod_ref[c * gs:(c + 1) * gs, 1]) + mod_ref[c * gs:(c + 1) * gs, 0]
        hns.append(hn.reshape(sub, d).astype(BF16))
    for c, hn in enumerate(hns):
        rs = slice(c * sub, (c + 1) * sub)
        proj = jnp.dot(hn, w_ref[...], preferred_element_type=F32)
        if chunked:
            t = u_ref.shape[0]
            nc = sub // t
            u_ref[:, c * nc:(c + 1) * nc, :] = jnp.swapaxes(proj[:, :ssm_w].reshape(nc, t, ssm_w), 0, 1)
        else:
            u_ref[rs, :] = proj[:, :ssm_w]
        q_ref[rs, :] = (proj[:, ssm_w:ssm_w + attn_w] * (HEAD_DIM ** -0.5)).astype(BF16)
        k_ref[rs, :] = proj[:, ssm_w + attn_w:ssm_w + attn_w + kv_w]
        v_ref[rs, :] = proj[:, ssm_w + attn_w + kv_w:]


def _inproj(x3, mod4, g_pre, w_in_bf, gb, rb, chunk=None):
    nb, nr, d = x3.shape
    assert gb == 1 or rb == nr
    n_in = w_in_bf.shape[1]
    kv_w = N_KV_HEADS * HEAD_DIM
    attn_w = N_HEADS * HEAD_DIM
    ssm_w = n_in - attn_w - 2 * kv_w
    rows = nb * nr
    nj = nr // rb
    tm = gb * rb
    out_map = lambda i, j: (i * nj + j, 0)
    if chunk is None:
        u_shape = jax.ShapeDtypeStruct((rows, ssm_w), F32)
        u_spec = pl.BlockSpec((tm, ssm_w), out_map)
    else:
        assert gb == 1 and rb % chunk == 0
        u_shape = jax.ShapeDtypeStruct((nb, chunk, nr // chunk, ssm_w), F32)
        u_spec = pl.BlockSpec((None, chunk, rb // chunk, ssm_w), lambda i, j: (i, 0, j, 0))
    return pl.pallas_call(
        functools.partial(_inproj_body, chunked=chunk is not None),
        out_shape=(u_shape,
                   jax.ShapeDtypeStruct((rows, attn_w), BF16),
                   jax.ShapeDtypeStruct((rows, kv_w), F32),
                   jax.ShapeDtypeStruct((rows, kv_w), F32)),
        grid=(nb // gb, nj),
        in_specs=[pl.BlockSpec((gb, rb, d), lambda i, j: (i, j, 0)),
                  pl.BlockSpec((gb, 2, 1, d), lambda i, j: (i, 0, 0, 0)),
                  _resident((1, 1, d)),
                  _resident((d, n_in))],
        out_specs=(u_spec,
                   pl.BlockSpec((tm, attn_w), out_map),
                   pl.BlockSpec((tm, kv_w), out_map),
                   pl.BlockSpec((tm, kv_w), out_map)),
        compiler_params=_cparams("arbitrary", "arbitrary"),
        name="in_projection",
    )(x3, mod4, g_pre.reshape(1, 1, d), w_in_bf)


def _attn_prompt_body(sinks_ref, q_ref, kc_ref, vc_ref, kp_ref, vp_ref, bias_ref, g_ref, o_ref, s_scr, p_scr):
    tq = q_ref.shape[0]
    blk = bias_ref.shape[2] // 2
    first_tile = pl.program_id(1) == 0
    kk = jnp.concatenate([kp_ref[...], kc_ref[...]], axis=0).astype(BF16)
    vv = jnp.concatenate([vp_ref[...], vc_ref[...]], axis=0).astype(BF16)
    shape = (GQA * blk, 2 * blk)
    col = lax.broadcasted_iota(jnp.int32, shape, 1)
    row = lax.broadcasted_iota(jnp.int32, (GQA * blk, 1), 0)
    no_prev = first_tile & (col < blk)
    sink_cols = []
    for hk in range(N_KV_HEADS):
        sk = jnp.full((GQA * blk, 1), sinks_ref[hk * GQA], F32)
        for gq in range(1, GQA):
            sk = jnp.where(row >= gq * blk, sinks_ref[hk * GQA + gq], sk)
        sink_cols.append(sk)
    pairs = [(j, hk) for j in range(tq // blk) for hk in range(N_KV_HEADS)]
    for idx, (j, hk) in enumerate(pairs):
        qg = jnp.concatenate([q_ref[j * blk:(j + 1) * blk, h * HEAD_DIM:(h + 1) * HEAD_DIM]
                              for h in range(hk * GQA, (hk + 1) * GQA)], axis=0)
        s = lax.dot_general(qg, kk[j * blk:(j + 2) * blk, hk * HEAD_DIM:(hk + 1) * HEAD_DIM],
                            (((1,), (1,)), ((), ())), preferred_element_type=F32) + bias_ref[hk]
        s_scr[idx] = jnp.where(no_prev, NEG_INF, s) if j == 0 else s
    dens = []
    for idx, (j, hk) in enumerate(pairs):
        s = s_scr[idx]
        m = jnp.maximum(jnp.max(s, axis=-1, keepdims=True), sink_cols[hk])
        p = jnp.exp(s - m)
        dens.append(jnp.sum(p, axis=-1, keepdims=True) + jnp.exp(sink_cols[hk] - m))
        p_scr[idx] = p.astype(BF16)
    outs = {}
    for idx, (j, hk) in enumerate(pairs):
        o = jnp.dot(p_scr[idx], vv[j * blk:(j + 2) * blk, hk * HEAD_DIM:(hk + 1) * HEAD_DIM],
                    preferred_element_type=F32) / dens[idx]
        for gq in range(GQA):
            outs[j, hk * GQA + gq] = o[gq * blk:(gq + 1) * blk]
    for j in range(tq // blk):
        att = jnp.concatenate([outs[j, h] for h in range(N_HEADS)], axis=1)
        o_ref[j * blk:(j + 1) * blk, :] = _rms(att, g_ref[...]).astype(BF16)


def _attn_prompt(q, k, v, sinks, bias, band, g_attn, nb, seq):
    rows, attn_w = q.shape
    kv_w = k.shape[1]
    blk = band.shape[0]
    bias = jnp.where(band > 0.5, bias, NEG_INF).reshape(N_KV_HEADS, GQA * blk, 2 * blk)
    tq = ATTN_TILE
    nt = seq // tq
    per = tq // blk
    cur = lambda b, i: (b * nt + i, 0)
    prev = lambda b, i: (b * nt * per + jnp.maximum(i * per - 1, 0), 0)
    return pl.pallas_call(
        _attn_prompt_body,
        out_shape=jax.ShapeDtypeStruct((rows, attn_w), BF16),
        grid=(nb, nt),
        in_specs=[pl.BlockSpec(memory_space=pltpu.SMEM),
                  pl.BlockSpec((tq, attn_w), cur),
                  pl.BlockSpec((tq, kv_w), cur),
                  pl.BlockSpec((tq, kv_w), cur),
                  pl.BlockSpec((blk, kv_w), prev),
                  pl.BlockSpec((blk, kv_w), prev),
                  _resident(bias.shape),
                  _resident((1, attn_w))],
        out_specs=pl.BlockSpec((tq, attn_w), cur),
        scratch_shapes=[pltpu.VMEM((per * N_KV_HEADS, GQA * blk, 2 * blk), F32),
                        pltpu.VMEM((per * N_KV_HEADS, GQA * blk, 2 * blk), BF16)],
        compiler_params=_cparams("arbitrary", "arbitrary"),
        name="attention_prompt",
    )(sinks, q, k, v, k, v, bias, g_attn.reshape(1, attn_w))


def _attn_sample_body(sinks_ref, q_ref, kn_ref, vn_ref, ck_ref, cv_ref, bias_ref, g_ref, o_ref, ko_ref, vo_ref):
    t = q_ref.shape[1]
    kk = jnp.concatenate([ck_ref[...], kn_ref[...]], axis=1)
    vv = jnp.concatenate([cv_ref[...], vn_ref[...]], axis=1)
    ko_ref[...] = kk[:, t:, :]
    vo_ref[...] = vv[:, t:, :]
    kkb = kk.astype(BF16)
    vvb = vv.astype(BF16)
    row = lax.broadcasted_iota(jnp.int32, (GQA * t, 1), 0)
    outs = [None] * N_HEADS
    for hk in range(N_KV_HEADS):
        sk = jnp.full((GQA * t, 1), sinks_ref[hk * GQA], F32)
        for gq in range(1, GQA):
            sk = jnp.where(row >= gq * t, sinks_ref[hk * GQA + gq], sk)
        qg = jnp.concatenate([q_ref[:, :, h * HEAD_DIM:(h + 1) * HEAD_DIM]
                              for h in range(hk * GQA, (hk + 1) * GQA)], axis=1)
        s = jnp.einsum('bqd,bkd->bqk', qg, kkb[:, :, hk * HEAD_DIM:(hk + 1) * HEAD_DIM],
                       preferred_element_type=F32) + bias_ref[hk]
        m = jnp.maximum(jnp.max(s, axis=-1, keepdims=True), sk)
        p = jnp.exp(s - m)
        den = jnp.sum(p, axis=-1, keepdims=True) + jnp.exp(sk - m)
        o = jnp.einsum('bqk,bkd->bqd', p.astype(BF16), vvb[:, :, hk * HEAD_DIM:(hk + 1) * HEAD_DIM],
                       preferred_element_type=F32) / den
        for gq in range(GQA):
            outs[hk * GQA + gq] = o[:, gq * t:(gq + 1) * t, :]
    att = jnp.concatenate(outs, axis=2)
    o_ref[...] = _rms(att, g_ref[...]).astype(BF16)


def _attn_sample(q3, k3, v3, ck, cv, sinks, bias, band, g_attn, gb):
    nb, t, attn_w = q3.shape
    w, kv_w = ck.shape[1], ck.shape[2]
    bias = jnp.where(band > 0.5, bias, NEG_INF).reshape(N_KV_HEADS, GQA * t, w + t)
    blk3 = lambda last: pl.BlockSpec((gb, last[0], last[1]), lambda i: (i, 0, 0))
    return pl.pallas_call(
        _attn_sample_body,
        out_shape=(jax.ShapeDtypeStruct((nb, t, attn_w), BF16),
                   jax.ShapeDtypeStruct((nb, w, kv_w), F32),
                   jax.ShapeDtypeStruct((nb, w, kv_w), F32)),
        grid=(nb // gb,),
        in_specs=[pl.BlockSpec(memory_space=pltpu.SMEM),
                  blk3((t, attn_w)), blk3((t, kv_w)), blk3((t, kv_w)),
                  blk3((w, kv_w)), blk3((w, kv_w)),
                  _resident(bias.shape), _resident((1, 1, attn_w))],
        out_specs=(blk3((t, attn_w)), blk3((w, kv_w)), blk3((w, kv_w))),
        compiler_params=_cparams("arbitrary"),
        name="attention_sample",
    )(sinks, q3, k3, v3, ck, cv, bias, g_attn.reshape(1, 1, attn_w))


def _ssm_tail(y, u, d_ref, wglu_ref, bglu_ref, g_ref):
    z = jax.nn.gelu(y + d_ref[...] * u)
    gate = jax.nn.sigmoid(jnp.dot(z.astype(BF16), wglu_ref[...], preferred_element_type=F32) + bglu_ref[...])
    return _rms(z * gate, g_ref[...]).astype(BF16)


def _ssm_chunk_body(ut_ref, m_ref, e_ref, f_ref, w1_ref, w2_ref, yt_ref, hend_ref, s_scr):
    t, gb, ch, nc = ut_ref.shape
    n_levels = w1_ref.shape[1]
    two_p = e_ref.shape[1]
    row = lax.broadcasted_iota(jnp.int32, (nc, two_p), 0)
    rs = [ut_ref[:, gl].reshape(t * ch, nc) for gl in range(gb)]
    hs = []
    for gl in range(gb):
        s_scr[gl] = jnp.dot(e_ref[gl], rs[gl], preferred_element_type=F32)
        hs.append(s_scr[gl].T)

    def conv(gl):
        yt_ref[:, gl * ch:(gl + 1) * ch, :] = jnp.dot(m_ref[gl], rs[gl], preferred_element_type=F32).reshape(t, ch, nc)

    for lv in range(n_levels):
        sh = 1 << lv
        for gl in range(gb):
            prev = jnp.where(row >= sh, pltpu.roll(hs[gl], sh, axis=0), 0.0)
            hs[gl] = hs[gl] + w1_ref[gl, lv] * prev + w2_ref[gl, lv] * pltpu.roll(prev, two_p // 2, axis=1)
        for gl in range(lv * gb // n_levels, (lv + 1) * gb // n_levels):
            conv(gl)
    for gl in range(gb):
        hend_ref[gl] = hs[gl][nc - 1:nc, :]
        s_scr[gl] = jnp.where(row >= 1, pltpu.roll(hs[gl], 1, axis=0), 0.0).T
        yt_ref[:, gl * ch:(gl + 1) * ch, :] += jnp.dot(f_ref[gl], s_scr[gl].astype(BF16),
                                                       preferred_element_type=F32).reshape(t, ch, nc)


def _ssm_transpose_body(u_ref, o_ref):
    for s in range(u_ref.shape[0]):
        o_ref[s] = u_ref[s].T.astype(BF16).reshape(o_ref.shape[1:])


def _ssm_transpose(u2, ch):
    nb, t, nc, ssm_w = u2.shape
    tb = SSM_TAIL_BLOCK
    return pl.pallas_call(
        _ssm_transpose_body,
        out_shape=jax.ShapeDtypeStruct((nb, t, ssm_w // ch, ch, nc), BF16),
        grid=(nb, t // tb),
        in_specs=[pl.BlockSpec((None, tb, nc, ssm_w), lambda b, i: (b, i, 0, 0))],
        out_specs=pl.BlockSpec((None, tb, ssm_w // ch, ch, nc), lambda b, i: (b, i, 0, 0, 0)),
        compiler_params=_cparams("arbitrary", "arbitrary"),
        name="ssm_transpose",
    )(u2)


def _ssm_chunked(ut, kw):
    nb, t, ng, ch, nc = ut.shape
    tc = t * ch
    two_p = kw["e"].shape[1]
    n_levels = kw["w1"].shape[1]
    gb = SSM_GROUP_BLOCK
    grp = lambda shape: pl.BlockSpec((gb,) + shape, lambda b, g: (g,) + (0,) * len(shape))
    return pl.pallas_call(
        _ssm_chunk_body,
        out_shape=(jax.ShapeDtypeStruct((nb, t, ng * ch, nc), F32),
                   jax.ShapeDtypeStruct((nb, ng, 1, two_p), F32)),
        grid=(nb, ng // gb),
        in_specs=[pl.BlockSpec((None, t, gb, ch, nc), lambda b, g: (b, 0, g, 0, 0)),
                  grp((tc, tc)), grp((two_p, tc)), grp((tc, two_p)),
                  grp((n_levels, 1, two_p)), grp((n_levels, 1, two_p))],
        out_specs=(pl.BlockSpec((None, t, gb * ch, nc), lambda b, g: (b, 0, g, 0)),
                   pl.BlockSpec((None, gb, 1, two_p), lambda b, g: (b, g, 0, 0))),
        scratch_shapes=[pltpu.VMEM((gb, two_p, nc), F32)],
        compiler_params=_cparams("arbitrary", "arbitrary"),
        name="ssm_chunked",
    )(ut, kw["m"], kw["e"], kw["f"], kw["w1"], kw["w2"])


def _ssm_tail_body(yt_ref, u_ref, d_ref, wglu_ref, bglu_ref, g_ref, o_ref):
    for i in range(yt_ref.shape[0]):
        o_ref[i] = _ssm_tail(yt_ref[i].T, u_ref[i], d_ref, wglu_ref, bglu_ref, g_ref)


def _ssm_tail_call(yt, u2, sw):
    nb, t, nc, ssm_w = u2.shape
    tb = SSM_TAIL_BLOCK
    return pl.pallas_call(
        _ssm_tail_body,
        out_shape=jax.ShapeDtypeStruct((nb, t, nc, ssm_w), BF16),
        grid=(nb, t // tb),
        in_specs=[pl.BlockSpec((None, tb, ssm_w, nc), lambda b, i: (b, i, 0, 0)),
                  pl.BlockSpec((None, tb, nc, ssm_w), lambda b, i: (b, i, 0, 0)),
                  _resident((1, ssm_w)), _resident((ssm_w, ssm_w)), _resident((1, ssm_w)), _resident((1, ssm_w))],
        out_specs=pl.BlockSpec((None, tb, nc, ssm_w), lambda b, i: (b, i, 0, 0)),
        compiler_params=_cparams("arbitrary", "arbitrary"),
        name="ssm_tail",
    )(yt, u2, sw["d"], sw["wglu"], sw["bglu"], sw["g"])


def _lag_matrix_body(z_ref, m_ref, *, t):
    gb, ch, _ = z_ref.shape
    width = t * ch
    for gl in range(gb):
        z = z_ref[gl]
        for r in range(t):
            off = (t - 1 - r) * ch
            m_ref[gl, r * ch:(r + 1) * ch, :] = z[:, off:off + width].astype(BF16)


def _lag_matrix(z, t):
    ng, ch, zw = z.shape
    gb = SSM_GROUP_BLOCK
    return pl.pallas_call(
        functools.partial(_lag_matrix_body, t=t),
        out_shape=jax.ShapeDtypeStruct((ng, t * ch, zw // 2), BF16),
        grid=(ng // gb,),
        in_specs=[pl.BlockSpec((gb, ch, zw), lambda g: (g, 0, 0))],
        out_specs=pl.BlockSpec((gb, t * ch, zw // 2), lambda g: (g, 0, 0)),
        compiler_params=_cparams("arbitrary"),
        name="ssm_lag_matrix",
    )(z)


def _ssm_chunk_weights(lam_re, lam_im, log_dt, b_re, b_im, c_re, c_im, t, n_chunks):
    ng, ns = lam_re.shape
    nc = b_re.shape[2]
    dt = jnp.exp(log_dt)[:, None]
    are, aim = lam_re * dt, lam_im * dt
    d = jnp.arange(t + 1, dtype=F32)[:, None, None]
    mag = jnp.exp(d * are)
    pre, pim = mag * jnp.cos(d * aim), mag * jnp.sin(d * aim)
    lre, lim = pre[1], pim[1]
    den = lam_re * lam_re + lam_im * lam_im
    fre = ((lre - 1.0) * lam_re + lim * lam_im) / den
    fim = (lim * lam_re - (lre - 1.0) * lam_im) / den
    bbr = fre[..., None] * b_re - fim[..., None] * b_im
    bbi = fre[..., None] * b_im + fim[..., None] * b_re
    pre_g, pim_g = jnp.transpose(pre, (1, 0, 2)), jnp.transpose(pim, (1, 0, 2))
    xr = c_re[:, None] * pre_g[:, :, None, :] - c_im[:, None] * pim_g[:, :, None, :]
    xi = c_re[:, None] * pim_g[:, :, None, :] + c_im[:, None] * pre_g[:, :, None, :]
    kern = jnp.einsum('gdap,gpc->gdac', jnp.concatenate([xr[:, :t], -xi[:, :t]], axis=3),
                      jnp.concatenate([bbr, bbi], axis=1), precision=HIGHEST)
    lag_rows = jnp.transpose(kern[:, ::-1], (0, 2, 1, 3)).reshape(ng, nc, t * nc)
    m = _lag_matrix(jnp.concatenate([lag_rows, jnp.zeros_like(lag_rows)], axis=2), t)
    rev_re = jnp.transpose(pre_g[:, t - 1::-1][:, :t], (0, 2, 1))
    rev_im = jnp.transpose(pim_g[:, t - 1::-1][:, :t], (0, 2, 1))
    er = rev_re[..., None] * bbr[:, :, None, :] - rev_im[..., None] * bbi[:, :, None, :]
    ei = rev_re[..., None] * bbi[:, :, None, :] + rev_im[..., None] * bbr[:, :, None, :]
    e = jnp.concatenate([er.reshape(ng, ns, t * nc), ei.reshape(ng, ns, t * nc)], axis=1).astype(BF16)
    f = jnp.concatenate([xr[:, 1:].reshape(ng, t * nc, ns), -xi[:, 1:].reshape(ng, t * nc, ns)], axis=2).astype(BF16)
    wr, wi = pre[t], pim[t]
    w1, w2 = [], []
    for _ in range(max(1, (n_chunks - 1).bit_length())):
        w1.append(jnp.concatenate([wr, wr], axis=1))
        w2.append(jnp.concatenate([-wi, wi], axis=1))
        wr, wi = wr * wr - wi * wi, 2.0 * wr * wi
    w1 = jnp.stack(w1, axis=1)[:, :, None, :]
    w2 = jnp.stack(w2, axis=1)[:, :, None, :]
    return {"m": m, "e": e, "f": f, "w1": w1, "w2": w2}


def _ssm_sample_body(u_ref, h0re_ref, h0im_ref, bre_ref, bim_ref, cre_ref, cim_ref, lre_ref, lim_ref,
                     d_ref, wglu_ref, bglu_ref, g_ref, o_ref, hre_ref, him_ref):
    steps = u_ref.shape[0]
    half_in = bre_ref.shape[2]
    dot = functools.partial(jnp.dot, preferred_element_type=F32)
    ar = lre_ref[...]
    ai = lim_ref[...]
    hr = h0re_ref[...]
    hi = h0im_ref[...]
    for t in range(steps):
        u = u_ref[t]
        u_hi = u.astype(BF16)
        u_lo = (u - u_hi.astype(F32)).astype(BF16)

        def bu(b_ref):
            cols = []
            for hf in range(2):
                lanes = slice(hf * half_in, (hf + 1) * half_in)
                cols.append(dot(u_hi[:, lanes], b_ref[0, hf]) + dot(u_lo[:, lanes], b_ref[0, hf])
                            + dot(u_hi[:, lanes], b_ref[1, hf]))
            return jnp.concatenate(cols, axis=1)

        hr, hi = ar * hr - ai * hi + bu(bre_ref), ar * hi + ai * hr + bu(bim_ref)
        half_st = hr.shape[1] // 2
        hrb, hib = hr.astype(BF16), hi.astype(BF16)
        y = jnp.concatenate(
            [dot(hrb[:, hf * half_st:(hf + 1) * half_st], cre_ref[hf])
             + dot(hib[:, hf * half_st:(hf + 1) * half_st], cim_ref[hf]) for hf in range(2)], axis=1)
        o_ref[t] = _ssm_tail(y, u, d_ref, wglu_ref, bglu_ref, g_ref)
    hre_ref[...] = hr
    him_ref[...] = hi


def _ssm_sample(u_tm, h0re, h0im, sw):
    steps, nb, ssm_w = u_tm.shape
    n_state = h0re.shape[1]
    return pl.pallas_call(
        _ssm_sample_body,
        out_shape=(jax.ShapeDtypeStruct((steps, nb, ssm_w), BF16),
                   jax.ShapeDtypeStruct((nb, n_state), F32),
                   jax.ShapeDtypeStruct((nb, n_state), F32)),
        compiler_params=pltpu.CompilerParams(vmem_limit_bytes=VMEM_LIMIT_BYTES),
        name="ssm_sample",
    )(u_tm, h0re, h0im, _hi_lo(sw["bre"]), _hi_lo(sw["bim"]), sw["cre"].astype(BF16), sw["cim"].astype(BF16),
      sw["lre"], sw["lim"], sw["d"], sw["wglu"], sw["bglu"], sw["g"])


def _hi_lo(w):
    hi = w.astype(BF16)
    return jnp.stack([hi, (w - hi.astype(F32)).astype(BF16)])


def _ssm_weights(lam_re, lam_im, log_dt, b_re, b_im, c_re, c_im, d_skip, w_glu, b_glu, g_ssm):
    ng, ns = lam_re.shape
    nc = b_re.shape[2]
    dt = jnp.exp(log_dt)[:, None]
    mag = jnp.exp(lam_re * dt)
    lre = mag * jnp.cos(lam_im * dt)
    lim = mag * jnp.sin(lam_im * dt)
    den = lam_re * lam_re + lam_im * lam_im
    fre = ((lre - 1.0) * lam_re + lim * lam_im) / den
    fim = (lim * lam_re - (lre - 1.0) * lam_im) / den
    bbar_re = fre[..., None] * b_re - fim[..., None] * b_im
    bbar_im = fre[..., None] * b_im + fim[..., None] * b_re
    eye = jnp.eye(ng // 2, dtype=F32)

    def in_blocks(b):
        b2 = b.reshape(2, ng // 2, ns, nc)
        return jnp.einsum('hgpc,gk->hgckp', b2, eye).reshape(2, ng // 2 * nc, ng // 2 * ns)

    def out_blocks(c):
        c2 = c.reshape(2, ng // 2, nc, ns)
        return jnp.einsum('hgcp,gk->hgpkc', c2, eye).reshape(2, ng // 2 * ns, ng // 2 * nc)

    ssm_w = ng * nc
    return {
        "bre": in_blocks(bbar_re), "bim": in_blocks(bbar_im),
        "cre": out_blocks(c_re), "cim": out_blocks(-c_im),
        "lre": lre.reshape(1, ng * ns), "lim": lim.reshape(1, ng * ns),
        "d": d_skip.reshape(1, ssm_w), "wglu": w_glu.astype(BF16), "bglu": b_glu.reshape(1, ssm_w),
        "g": g_ssm.reshape(1, ssm_w),
    }


def _route(logits_t):
    big = jnp.int32(10 ** 6)
    gl = logits_t[:N_EXPERT_GROUPS]
    grow = lax.broadcasted_iota(jnp.int32, gl.shape, 0)
    gmax = jnp.max(gl, axis=0, keepdims=True)
    gidx = jnp.min(jnp.where(gl == gmax, grow, big), axis=0, keepdims=True)
    g_p = 1.0 / jnp.sum(jnp.exp(gl - gmax), axis=0, keepdims=True)
    ex = logits_t[N_EXPERT_GROUPS:N_EXPERT_GROUPS + N_EXPERTS]
    erow = lax.broadcasted_iota(jnp.int32, ex.shape, 0)
    lo = gidx * EXPERTS_PER_GROUP
    in_group = (erow >= lo) & (erow < lo + EXPERTS_PER_GROUP)
    el = jnp.where(in_group, ex, NEG_INF)
    ee = jnp.exp(el - jnp.max(el, axis=0, keepdims=True))
    prob = ee / jnp.sum(ee, axis=0, keepdims=True)
    p1 = jnp.max(jnp.where(in_group, prob, -1.0), axis=0, keepdims=True)
    i1 = jnp.min(jnp.where(in_group & (prob == p1), erow, big), axis=0, keepdims=True)
    rest = in_group & (erow != i1)
    p2 = jnp.max(jnp.where(rest, prob, -1.0), axis=0, keepdims=True)
    i2 = jnp.min(jnp.where(rest & (prob == p2), erow, big), axis=0, keepdims=True)
    tot = p1 + p2
    return jnp.where(erow == i1, g_p * p1 / tot, jnp.where(erow == i2, g_p * p2 / tot, 0.0)), gidx


def _split3(x):
    a = x.astype(BF16)
    r = x - a.astype(F32)
    b = r.astype(BF16)
    return a, b, (r - b.astype(F32)).astype(BF16)


def _moe_sorted(hb, gates_t, gidx, tri_ref, wg_ref, wu_ref, wd_ref, xs_scr, gs_scr, ys_scr):
    rows, d = hb.shape
    grow = lax.broadcasted_iota(jnp.int32, (SUBLANES, rows), 0)
    member = grow == gidx
    csum = jnp.dot(member.astype(BF16), tri_ref[...], preferred_element_type=F32)
    rank = jnp.sum(jnp.where(member, csum, 0.0), axis=0, keepdims=True)
    counts = [jnp.sum((gidx == grp).astype(jnp.int32)) for grp in range(N_EXPERT_GROUPS - 1)]
    offs = [jnp.int32(0)]
    for c in counts:
        offs.append(offs[-1] + c)
    offs.append(jnp.int32(rows))
    base = jnp.zeros_like(gidx)
    for grp in range(1, N_EXPERT_GROUPS):
        base = jnp.where(gidx == grp, offs[grp], base)
    pos_lanes = base.astype(F32) + rank - 1.0
    pos = jnp.broadcast_to(pos_lanes, (SUBLANES, rows)).T[:, 0:1]
    col = lax.broadcasted_iota(jnp.int32, (rows, rows), 1).astype(F32)
    row = lax.broadcasted_iota(jnp.int32, (rows, rows), 0).astype(F32)
    unsort = (col == pos).astype(BF16)
    sort = (row == pos_lanes).astype(BF16)
    xs_scr[...] = jnp.dot(sort, hb, preferred_element_type=F32).astype(BF16)
    pad = jnp.zeros((LANES - N_EXPERT_GROUPS - N_EXPERTS, rows), F32)
    gates_pad = jnp.concatenate([jnp.zeros((N_EXPERT_GROUPS, rows), F32), gates_t, pad], axis=0)
    gs_t = sum(jnp.dot(part, unsort, preferred_element_type=F32) for part in _split3(gates_pad))
    gs_scr[...] = gs_t.T
    ys_scr[...] = jnp.zeros_like(ys_scr)
    blk = MOE_ROW_BLOCK
    n_blk = rows // blk
    cuts = [jnp.int32(k * blk) for k in range(1, n_blk)] + offs[1:N_EXPERT_GROUPS]
    assert len(cuts) == 6
    for a, b in ((0, 5), (1, 3), (2, 4), (1, 2), (3, 4), (0, 3), (2, 5), (0, 1), (2, 3), (4, 5), (1, 2), (3, 4)):
        cuts[a], cuts[b] = jnp.minimum(cuts[a], cuts[b]), jnp.maximum(cuts[a], cuts[b])
    cuts = [jnp.int32(0)] + cuts + [jnp.int32(rows)]
    lane_b = lax.broadcasted_iota(jnp.int32, (blk, LANES), 1)
    for j in range(len(cuts) - 1):
        start = cuts[j]
        live = (cuts[j + 1] > start).astype(F32)
        k = jnp.minimum(start // blk, n_blk - 1)
        grp = sum((offs[q] <= start).astype(jnp.int32) for q in range(1, N_EXPERT_GROUPS))
        sl = pl.ds(pl.multiple_of(k * blk, blk), blk)
        xb = xs_scr[sl, :]
        gb = gs_scr[sl, :] * live
        acts = []
        for i in range(EXPERTS_PER_GROUP):
            e = grp * EXPERTS_PER_GROUP + i
            gate = jnp.sum(jnp.where(lane_b == N_EXPERT_GROUPS + e, gb, 0.0), axis=-1, keepdims=True)
            hg = jnp.dot(xb, wg_ref[e], preferred_element_type=F32)
            hu = jnp.dot(xb, wu_ref[e], preferred_element_type=F32)
            acts.append((jax.nn.silu(hg) * hu * gate).astype(BF16))
        ys_scr[sl, :] += jnp.dot(jnp.concatenate(acts, axis=1), wd_ref[grp], preferred_element_type=F32)
    hi = ys_scr[...].astype(BF16)
    lo = (ys_scr[...] - hi.astype(F32)).astype(BF16)
    return unsort, hi, lo


def _ffn_body(x_ref, ssm_ref, attn_ref, mod_ref, gpost_ref, gpre_ref, gffn_ref, wout_ref, wr_ref, br_ref,
              tri_ref, wg_ref, wu_ref, wd_ref, o_ref, xs_scr, gs_scr, ys_scr, *, chunked):
    g, r, d = x_ref.shape
    rows = g * r
    if chunked:
        ssm = jnp.swapaxes(ssm_ref[...].astype(F32), 0, 1).reshape(rows, ssm_ref.shape[-1]).astype(BF16)
    else:
        ssm = ssm_ref[...]
    sub = min(rows, FFN_SUB)
    x1s, hbs, hlos = [], [], []
    for c in range(rows // sub):
        rs = slice(c * sub, (c + 1) * sub)
        if g == 1:
            x = x_ref[:, rs, :]
            md = mod_ref
        else:
            gsl = slice(c * (sub // r), (c + 1) * (sub // r))
            x = x_ref[gsl]
            md = mod_ref.at[gsl]
        gg = x.shape[0]
        cat = jnp.concatenate([ssm[rs], attn_ref[rs, :]], axis=1)
        mixed = jnp.dot(cat, wout_ref[...], preferred_element_type=F32)
        x1 = x + md[:, 2] * _rms(mixed, gpost_ref[0]).reshape(gg, -1, d)
        hn = (_rms(x1, gpre_ref[...]) * (1.0 + md[:, 4]) + md[:, 3]).reshape(sub, d)
        hb = hn.astype(BF16)
        x1s.append(x1)
        hbs.append(hb)
        hlos.append((hn - hb.astype(F32)).astype(BF16))
    hb = jnp.concatenate(hbs, axis=0)
    hlo = jnp.concatenate(hlos, axis=0)
    nt = functools.partial(lax.dot_general, dimension_numbers=(((1,), (1,)), ((), ())), preferred_element_type=F32)
    logits_t = nt(wr_ref[0], hb) + nt(wr_ref[0], hlo) + nt(wr_ref[1], hb) + br_ref[...]
    gates_t, gidx = _route(logits_t)
    unsort, hi, lo = _moe_sorted(hb, gates_t, gidx, tri_ref, wg_ref, wu_ref, wd_ref, xs_scr, gs_scr, ys_scr)
    for c in range(rows // sub):
        rs = slice(c * sub, (c + 1) * sub)
        ffn = (jnp.dot(unsort[rs], hi, preferred_element_type=F32) + jnp.dot(unsort[rs], lo, preferred_element_type=F32))
        if g == 1:
            o_ref[:, rs, :] = x1s[c] + mod_ref[:, 5] * _rms(ffn, gffn_ref[0]).reshape(1, sub, d)
        else:
            gsl = slice(c * (sub // r), (c + 1) * (sub // r))
            o_ref[gsl] = x1s[c] + mod_ref[gsl, 5] * _rms(ffn, gffn_ref[0]).reshape(sub // r, r, d)


def _ffn(x3, ssm_n, attn_n, mod4, fw, gb, rb, chunk=None):
    nb, nr, d = x3.shape
    assert gb == 1 or rb == nr
    half = ssm_n.shape[-1]
    nj = nr // rb
    tm = gb * rb
    row_map = lambda i, j: (i * nj + j, 0)
    vec = _resident((1, 1, d))
    if chunk is None:
        ssm_spec = pl.BlockSpec((tm, half), row_map)
    else:
        assert gb == 1 and rb % chunk == 0
        ssm_spec = pl.BlockSpec((None, chunk, rb // chunk, half), lambda i, j: (i, 0, j, 0))
    return pl.pallas_call(
        functools.partial(_ffn_body, chunked=chunk is not None),
        out_shape=jax.ShapeDtypeStruct((nb, nr, d), F32),
        grid=(nb // gb, nj),
        in_specs=[pl.BlockSpec((gb, rb, d), lambda i, j: (i, j, 0)),
                  ssm_spec,
                  pl.BlockSpec((tm, attn_n.shape[1]), row_map),
                  pl.BlockSpec((gb, N_MOD, 1, d), lambda i, j: (i, 0, 0, 0)),
                  vec, vec, vec,
                  _resident(fw["wout"].shape), _resident(fw["wr"].shape), _resident(fw["br"].shape),
                  _resident((tm, tm)),
                  _resident(fw["wg"].shape), _resident(fw["wu"].shape), _resident(fw["wd"].shape)],
        out_specs=pl.BlockSpec((gb, rb, d), lambda i, j: (i, j, 0)),
        scratch_shapes=[pltpu.VMEM((tm, d), BF16), pltpu.VMEM((tm, LANES), F32), pltpu.VMEM((tm, d), F32)],
        compiler_params=_cparams("arbitrary", "arbitrary"),
        name="outproj_moe",
    )(x3, ssm_n, attn_n, mod4, fw["gpost"], fw["gpre"], fw["gffn"], fw["wout"], fw["wr"], fw["br"],
      jnp.triu(jnp.ones((tm, tm), BF16)), fw["wg"], fw["wu"], fw["wd"])


def _ffn_weights(g_post_mix, g_pre_ffn, g_post_ffn, w_out, w_gr, b_gr, w_er, b_er, w_gate, w_up, w_down):
    d = w_out.shape[0]
    n_route = N_EXPERT_GROUPS + N_EXPERTS
    wr = jnp.concatenate([w_gr, w_er.reshape(d, N_EXPERTS)], axis=1)
    br = jnp.concatenate([b_gr, b_er.reshape(N_EXPERTS)])
    wr = jnp.pad(wr, ((0, 0), (0, ROUTE_ROWS - n_route))).T
    return {
        "gpost": g_post_mix.reshape(1, 1, d), "gpre": g_pre_ffn.reshape(1, 1, d), "gffn": g_post_ffn.reshape(1, 1, d),
        "wout": w_out.astype(BF16),
        "wr": _hi_lo(wr),
        "br": jnp.pad(br, (0, ROUTE_ROWS - n_route)).reshape(ROUTE_ROWS, 1),
        "wg": w_gate.astype(BF16), "wu": w_up.astype(BF16),
        "wd": w_down.astype(BF16).reshape(N_EXPERT_GROUPS, -1, d),
    }


def _rel_bucket(dist):
    max_exact = NUM_BUCKETS // 2
    dd = jnp.maximum(dist, 0)
    log_ratio = jnp.log(jnp.maximum(dd, 1).astype(F32) / max_exact) / math.log(MAX_DISTANCE / max_exact)
    large = jnp.minimum(max_exact + (log_ratio * (NUM_BUCKETS - max_exact)).astype(jnp.int32), NUM_BUCKETS - 1)
    return jnp.where(dd < max_exact, dd, large)


def _bias_and_band(dist, table):
    onehot = (_rel_bucket(dist)[:, :, None] == jnp.arange(NUM_BUCKETS)[None, None, :]).astype(F32)
    bias = jnp.einsum('qkb,bh->hqk', onehot, table.astype(F32), precision=HIGHEST)
    band = ((dist >= 0) & (dist <= WINDOW)).astype(F32)
    return bias, band


def _layer(xp, xs, ck, cv, h0re, h0im, mod_p, mod_s, table, p):
    nb, seq, d = xp.shape
    nbs, t, _ = xs.shape
    w = ck.shape[1]
    kv_w = N_KV_HEADS * HEAD_DIM
    w_in_bf = p["w_in"].astype(BF16)
    sw = _ssm_weights(p["ssm_lam_re"], p["ssm_lam_im"], p["ssm_log_dt"], p["ssm_b_re"], p["ssm_b_im"],
                      p["ssm_c_re"], p["ssm_c_im"], p["ssm_d"], p["w_glu"], p["b_glu"], p["g_ssm_out"])
    fw = _ffn_weights(p["g_post_mix"], p["g_pre_ffn"], p["g_post_ffn"], p["w_out"],
                      p["w_group_router"], p["b_group_router"], p["w_expert_router"], p["b_expert_router"],
                      p["w_exp_gate"], p["w_exp_up"], p["w_exp_down"])
    n_state = sw["lre"].shape[1]
    ng = p["ssm_lam_re"].shape[0]

    u2, q, k, v = _inproj(xp, mod_p, p["g_pre_mix"], w_in_bf, 1, INPROJ_TILE, chunk=SSM_CHUNK)
    blk = WINDOW
    dist_p = jnp.arange(blk)[:, None] + blk - jnp.arange(2 * blk)[None, :]
    bias_p, band_p = _bias_and_band(dist_p, table)
    attn_p = _attn_prompt(q, k, v, p["attn_sinks"], bias_p, band_p, p["g_attn_out"], nb, seq)
    kw = _ssm_chunk_weights(p["ssm_lam_re"], p["ssm_lam_im"], p["ssm_log_dt"], p["ssm_b_re"], p["ssm_b_im"],
                            p["ssm_c_re"], p["ssm_c_im"], SSM_CHUNK, seq // SSM_CHUNK)
    yt, hend = _ssm_chunked(_ssm_transpose(u2, p["ssm_b_re"].shape[2]), kw)
    ssm_p = _ssm_tail_call(yt, u2, sw)
    yp = _ffn(xp, ssm_p, attn_p, mod_p, fw, 1, TOKEN_TILE, chunk=SSM_CHUNK)
    wp = min(WINDOW, seq)
    k_p = k.reshape(nb, seq, kv_w)[:, seq - wp:].reshape(nb, wp, N_KV_HEADS, HEAD_DIM)
    v_p = v.reshape(nb, seq, kv_w)[:, seq - wp:].reshape(nb, wp, N_KV_HEADS, HEAD_DIM)
    n_p = n_state // ng
    hre_p = hend[:, :, 0, :n_p]
    him_p = hend[:, :, 0, n_p:]

    gs = TOKEN_TILE // t
    us, qs, ks, vs = _inproj(xs, mod_s, p["g_pre_mix"], w_in_bf, gs, t)
    dist_s = jnp.arange(t)[:, None] + w - jnp.arange(w + t)[None, :]
    bias_s, band_s = _bias_and_band(dist_s, table)
    attn_s, k_s, v_s = _attn_sample(qs.reshape(nbs, t, -1), ks.reshape(nbs, t, kv_w), vs.reshape(nbs, t, kv_w),
                                    ck.reshape(nbs, w, kv_w), cv.reshape(nbs, w, kv_w),
                                    p["attn_sinks"], bias_s, band_s, p["g_attn_out"], 16)
    u_tm = jnp.swapaxes(us.reshape(nbs, t, -1), 0, 1)
    ssm_tm, hre_s, him_s = _ssm_sample(u_tm, h0re.reshape(nbs, n_state), h0im.reshape(nbs, n_state), sw)
    ssm_s = jnp.swapaxes(ssm_tm, 0, 1).reshape(nbs * t, -1)
    ys = _ffn(xs, ssm_s, attn_s.reshape(nbs * t, -1), mod_s, fw, gs, t)
    k_s = k_s.reshape(nbs, w, N_KV_HEADS, HEAD_DIM)
    v_s = v_s.reshape(nbs, w, N_KV_HEADS, HEAD_DIM)
    hre_s = hre_s.reshape(nbs, ng, n_state // ng)
    him_s = him_s.reshape(nbs, ng, n_state // ng)
    return yp, ys, k_p, v_p, hre_p, him_p, k_s, v_s, hre_s, him_s


def kernel(x_prompt, x_sample, cache_k, cache_v, state_ssm_re, state_ssm_im, c_prompt, c_sample, rel_bias_table,
           w_ada, b_ada, g_pre_mix, g_post_mix, g_pre_ffn, g_post_ffn, w_in, ssm_lam_re, ssm_lam_im, ssm_log_dt,
           ssm_b_re, ssm_b_im, ssm_c_re, ssm_c_im, ssm_d, w_glu, b_glu, attn_sinks, g_ssm_out, g_attn_out, w_out,
           w_group_router, b_group_router, w_expert_router, b_expert_router, w_exp_gate, w_exp_up, w_exp_down):
    params = dict(
        w_ada=w_ada, b_ada=b_ada, g_pre_mix=g_pre_mix, g_post_mix=g_post_mix, g_pre_ffn=g_pre_ffn,
        g_post_ffn=g_post_ffn, w_in=w_in, ssm_lam_re=ssm_lam_re, ssm_lam_im=ssm_lam_im, ssm_log_dt=ssm_log_dt,
        ssm_b_re=ssm_b_re, ssm_b_im=ssm_b_im, ssm_c_re=ssm_c_re, ssm_c_im=ssm_c_im, ssm_d=ssm_d, w_glu=w_glu,
        b_glu=b_glu, attn_sinks=attn_sinks, g_ssm_out=g_ssm_out, g_attn_out=g_attn_out, w_out=w_out,
        w_group_router=w_group_router, b_group_router=b_group_router, w_expert_router=w_expert_router,
        b_expert_router=b_expert_router, w_exp_gate=w_exp_gate, w_exp_up=w_exp_up, w_exp_down=w_exp_down)
    depth = w_in.shape[0]
    nb, nbs = x_prompt.shape[0], x_sample.shape[0]
    d = x_prompt.shape[2]
    pad = (-nb) % 8
    yp, ys = x_prompt, x_sample
    outs = [[] for _ in range(8)]
    for li in range(depth):
        p = {name: val[li] for name, val in params.items()}
        c_all = jnp.concatenate([c_prompt, jnp.zeros((pad, d), F32), c_sample], axis=0)
        mod = _modulation(c_all, p["w_ada"], p["b_ada"])
        mod_p = mod[:nb].reshape(nb, N_MOD, 1, d)
        mod_s = mod[nb + pad:].reshape(nbs, N_MOD, 1, d)
        res = _layer(yp, ys, cache_k[li], cache_v[li], state_ssm_re[li], state_ssm_im[li], mod_p, mod_s,
                     rel_bias_table, p)
        yp, ys = res[0], res[1]
        for acc, val in zip(outs, res[2:]):
            acc.append(val)
    return (yp, ys) + tuple(jnp.stack(o) for o in outs)
```

```python
import functools
import math

import jax
import jax.numpy as jnp
from jax import lax
from jax.experimental import pallas as pl
from jax.experimental.pallas import tpu as pltpu

F32 = jnp.float32
BF16 = jnp.bfloat16
HIGHEST = lax.Precision.HIGHEST

EPS = 1e-6
NEG_INF = -1e30

SSM_GROUP = 16
SSM_STATE = 64
HEAD_DIM = 64
N_KV_HEADS = 2
GQA = 4
N_HEADS = N_KV_HEADS * GQA
WINDOW = 128
NUM_BUCKETS = 32
MAX_DISTANCE = 128
N_EXPERT_GROUPS = 4
EXPERTS_PER_GROUP = 4
N_EXPERTS = N_EXPERT_GROUPS * EXPERTS_PER_GROUP
N_MOD = 6

LANES = 128
SUBLANES = 8
VMEM_LIMIT_BYTES = 56 * 1024 * 1024

TOKEN_TILE = 512
INPROJ_TILE = 2048
INPROJ_SUB = 256
FFN_SUB = 128
ATTN_TILE = 1024
SSM_CHUNK = 32
SSM_GROUP_BLOCK = 16
SSM_TAIL_BLOCK = 16
MOE_ROW_BLOCK = 128
ROUTE_ROWS = 32


def _cparams(*sem):
    return pltpu.CompilerParams(dimension_semantics=sem, vmem_limit_bytes=VMEM_LIMIT_BYTES)


def _resident(shape):
    zeros = (0,) * len(shape)
    return pl.BlockSpec(shape, lambda *_: zeros, pipeline_mode=pl.Buffered(1))


def _rms(x, g):
    return x * lax.rsqrt(jnp.mean(x * x, axis=-1, keepdims=True) + EPS) * g


def _mod_body(c_ref, w_ref, b_ref, o_ref):
    a = jax.nn.silu(c_ref[...])
    w = w_ref[...]
    a_hi, w_hi = a.astype(BF16), w.astype(BF16)
    a_lo, w_lo = (a - a_hi.astype(F32)).astype(BF16), (w - w_hi.astype(F32)).astype(BF16)
    dot = functools.partial(jnp.dot, preferred_element_type=F32)
    o_ref[...] = dot(a_hi, w_hi) + dot(a_lo, w_hi) + dot(a_hi, w_lo) + b_ref[...]


def _modulation(c, w_ada, b_ada):
    rows, d = c.shape
    n = w_ada.shape[1]
    tn = 1024
    return pl.pallas_call(
        _mod_body,
        out_shape=jax.ShapeDtypeStruct((rows, n), F32),
        grid=(n // tn,),
        in_specs=[pl.BlockSpec((rows, d), lambda j: (0, 0)),
                  pl.BlockSpec((d, tn), lambda j: (0, j)),
                  pl.BlockSpec((1, tn), lambda j: (0, j))],
        out_specs=pl.BlockSpec((rows, tn), lambda j: (0, j)),
        compiler_params=_cparams("arbitrary"),
        name="modulation",
    )(c, w_ada, b_ada.reshape(1, n))


def _inproj_body(x_ref, mod_ref, g_ref, w_ref, u_ref, q_ref, k_ref, v_ref, *, chunked):
    g, r, d = x_ref.shape
    ssm_w = u_ref.shape[-1]
    attn_w = q_ref.shape[1]
    kv_w = k_ref.shape[1]
    rows = g * r
    sub = min(rows, INPROJ_SUB)
    hns = []
    for c in range(rows // sub):
        if g == 1:
            x = x_ref[:, c * sub:(c + 1) * sub, :]
            hn = _rms(x, g_ref[...]) * (1.0 + mod_ref[:, 1]) + mod_ref[:, 0]
        else:
            gs = sub // r
            x = x_ref[c * gs:(c + 1) * gs]
            hn = _rms(x, g_ref[...]) * (1.0 + mod_ref[c * gs:(c + 1) * gs, 1]) + mod_ref[c * gs:(c + 1) * gs, 0]
        hns.append(hn.reshape(sub, d).astype(BF16))
    for c, hn in enumerate(hns):
        rs = slice(c * sub, (c + 1) * sub)
        proj = jnp.dot(hn, w_ref[...], preferred_element_type=F32)
        if chunked:
            t = u_ref.shape[0]
            nc = sub // t
            u_ref[:, c * nc:(c + 1) * nc, :] = jnp.swapaxes(proj[:, :ssm_w].reshape(nc, t, ssm_w), 0, 1)
        else:
            u_ref[rs, :] = proj[:, :ssm_w]
        q_ref[rs, :] = (proj[:, ssm_w:ssm_w + attn_w] * (HEAD_DIM ** -0.5)).astype(BF16)
        k_ref[rs, :] = proj[:, ssm_w + attn_w:ssm_w + attn_w + kv_w]
        v_ref[rs, :] = proj[:, ssm_w + attn_w + kv_w:]


def _inproj(x3, mod4, g_pre, w_in_bf, gb, rb, chunk=None):
    nb, nr, d = x3.shape
    assert gb == 1 or rb == nr
    n_in = w_in_bf.shape[1]
    kv_w = N_KV_HEADS * HEAD_DIM
    attn_w = N_HEADS * HEAD_DIM
    ssm_w = n_in - attn_w - 2 * kv_w
    rows = nb * nr
    nj = nr // rb
    tm = gb * rb
    out_map = lambda i, j: (i * nj + j, 0)
    if chunk is None:
        u_shape = jax.ShapeDtypeStruct((rows, ssm_w), F32)
        u_spec = pl.BlockSpec((tm, ssm_w), out_map)
    else:
        assert gb == 1 and rb % chunk == 0
        u_shape = jax.ShapeDtypeStruct((nb, chunk, nr // chunk, ssm_w), F32)
        u_spec = pl.BlockSpec((None, chunk, rb // chunk, ssm_w), lambda i, j: (i, 0, j, 0))
    return pl.pallas_call(
        functools.partial(_inproj_body, chunked=chunk is not None),
        out_shape=(u_shape,
                   jax.ShapeDtypeStruct((rows, attn_w), BF16),
                   jax.ShapeDtypeStruct((rows, kv_w), F32),
                   jax.ShapeDtypeStruct((rows, kv_w), F32)),
        grid=(nb // gb, nj),
        in_specs=[pl.BlockSpec((gb, rb, d), lambda i, j: (i, j, 0)),
                  pl.BlockSpec((gb, 2, 1, d), lambda i, j: (i, 0, 0, 0)),
                  _resident((1, 1, d)),
                  _resident((d, n_in))],
        out_specs=(u_spec,
                   pl.BlockSpec((tm, attn_w), out_map),
                   pl.BlockSpec((tm, kv_w), out_map),
                   pl.BlockSpec((tm, kv_w), out_map)),
        compiler_params=_cparams("arbitrary", "arbitrary"),
        name="in_projection",
    )(x3, mod4, g_pre.reshape(1, 1, d), w_in_bf)


def _attn_prompt_body(sinks_ref, q_ref, kc_ref, vc_ref, kp_ref, vp_ref, bias_ref, g_ref, o_ref, s_scr, p_scr):
    tq = q_ref.shape[0]
    blk = bias_ref.shape[2] // 2
    first_tile = pl.program_id(1) == 0
    kk = jnp.concatenate([kp_ref[...], kc_ref[...]], axis=0).astype(BF16)
    vv = jnp.concatenate([vp_ref[...], vc_ref[...]], axis=0).astype(BF16)
    shape = (GQA * blk, 2 * blk)
    col = lax.broadcasted_iota(jnp.int32, shape, 1)
    row = lax.broadcasted_iota(jnp.int32, (GQA * blk, 1), 0)
    no_prev = first_tile & (col < blk)
    sink_cols = []
    for hk in range(N_KV_HEADS):
        sk = jnp.full((GQA * blk, 1), sinks_ref[hk * GQA], F32)
        for gq in range(1, GQA):
            sk = jnp.where(row >= gq * blk, sinks_ref[hk * GQA + gq], sk)
        sink_cols.append(sk)
    pairs = [(j, hk) for j in range(tq // blk) for hk in range(N_KV_HEADS)]
    for idx, (j, hk) in enumerate(pairs):
        qg = jnp.concatenate([q_ref[j * blk:(j + 1) * blk, h * HEAD_DIM:(h + 1) * HEAD_DIM]
                              for h in range(hk * GQA, (hk + 1) * GQA)], axis=0)
        s = lax.dot_general(qg, kk[j * blk:(j + 2) * blk, hk * HEAD_DIM:(hk + 1) * HEAD_DIM],
                            (((1,), (1,)), ((), ())), preferred_element_type=F32) + bias_ref[hk]
        s_scr[idx] = jnp.where(no_prev, NEG_INF, s) if j == 0 else s
    dens = []
    for idx, (j, hk) in enumerate(pairs):
        s = s_scr[idx]
        m = jnp.maximum(jnp.max(s, axis=-1, keepdims=True), sink_cols[hk])
        p = jnp.exp(s - m)
        dens.append(jnp.sum(p, axis=-1, keepdims=True) + jnp.exp(sink_cols[hk] - m))
        p_scr[idx] = p.astype(BF16)
    outs = {}
    for idx, (j, hk) in enumerate(pairs):
        o = jnp.dot(p_scr[idx], vv[j * blk:(j + 2) * blk, hk * HEAD_DIM:(hk + 1) * HEAD_DIM],
                    preferred_element_type=F32) / dens[idx]
        for gq in range(GQA):
            outs[j, hk * GQA + gq] = o[gq * blk:(gq + 1) * blk]
    for j in range(tq // blk):
        att = jnp.concatenate([outs[j, h] for h in range(N_HEADS)], axis=1)
        o_ref[j * blk:(j + 1) * blk, :] = _rms(att, g_ref[...]).astype(BF16)


def _attn_prompt(q, k, v, sinks, bias, band, g_attn, nb, seq):
    rows, attn_w = q.shape
    kv_w = k.shape[1]
    blk = band.shape[0]
    bias = jnp.where(band > 0.5, bias, NEG_INF).reshape(N_KV_HEADS, GQA * blk, 2 * blk)
    tq = ATTN_TILE
    nt = seq // tq
    per = tq // blk
    cur = lambda b, i: (b * nt + i, 0)
    prev = lambda b, i: (b * nt * per + jnp.maximum(i * per - 1, 0), 0)
    return pl.pallas_call(
        _attn_prompt_body,
        out_shape=jax.ShapeDtypeStruct((rows, attn_w), BF16),
        grid=(nb, nt),
        in_specs=[pl.BlockSpec(memory_space=pltpu.SMEM),
                  pl.BlockSpec((tq, attn_w), cur),
                  pl.BlockSpec((tq, kv_w), cur),
                  pl.BlockSpec((tq, kv_w), cur),
                  pl.BlockSpec((blk, kv_w), prev),
                  pl.BlockSpec((blk, kv_w), prev),
                  _resident(bias.shape),
                  _resident((1, attn_w))],
        out_specs=pl.BlockSpec((tq, attn_w), cur),
        scratch_shapes=[pltpu.VMEM((per * N_KV_HEADS, GQA * blk, 2 * blk), F32),
                        pltpu.VMEM((per * N_KV_HEADS, GQA * blk, 2 * blk), BF16)],
        compiler_params=_cparams("arbitrary", "arbitrary"),
        name="attention_prompt",
    )(sinks, q, k, v, k, v, bias, g_attn.reshape(1, attn_w))


def _attn_sample_body(sinks_ref, q_ref, kn_ref, vn_ref, ck_ref, cv_ref, bias_ref, g_ref, o_ref, ko_ref, vo_ref):
    t = q_ref.shape[1]
    kk = jnp.concatenate([ck_ref[...], kn_ref[...]], axis=1)
    vv = jnp.concatenate([cv_ref[...], vn_ref[...]], axis=1)
    ko_ref[...] = kk[:, t:, :]
    vo_ref[...] = vv[:, t:, :]
    kkb = kk.astype(BF16)
    vvb = vv.astype(BF16)
    row = lax.broadcasted_iota(jnp.int32, (GQA * t, 1), 0)
    outs = [None] * N_HEADS
    for hk in range(N_KV_HEADS):
        sk = jnp.full((GQA * t, 1), sinks_ref[hk * GQA], F32)
        for gq in range(1, GQA):
            sk = jnp.where(row >= gq * t, sinks_ref[hk * GQA + gq], sk)
        qg = jnp.concatenate([q_ref[:, :, h * HEAD_DIM:(h + 1) * HEAD_DIM]
                              for h in range(hk * GQA, (hk + 1) * GQA)], axis=1)
        s = jnp.einsum('bqd,bkd->bqk', qg, kkb[:, :, hk * HEAD_DIM:(hk + 1) * HEAD_DIM],
                       preferred_element_type=F32) + bias_ref[hk]
        m = jnp.maximum(jnp.max(s, axis=-1, keepdims=True), sk)
        p = jnp.exp(s - m)
        den = jnp.sum(p, axis=-1, keepdims=True) + jnp.exp(sk - m)
        o = jnp.einsum('bqk,bkd->bqd', p.astype(BF16), vvb[:, :, hk * HEAD_DIM:(hk + 1) * HEAD_DIM],
                       preferred_element_type=F32) / den
        for gq in range(GQA):
            outs[hk * GQA + gq] = o[:, gq * t:(gq + 1) * t, :]
    att = jnp.concatenate(outs, axis=2)
    o_ref[...] = _rms(att, g_ref[...]).astype(BF16)


def _attn_sample(q3, k3, v3, ck, cv, sinks, bias, band, g_attn, gb):
    nb, t, attn_w = q3.shape
    w, kv_w = ck.shape[1], ck.shape[2]
    bias = jnp.where(band > 0.5, bias, NEG_INF).reshape(N_KV_HEADS, GQA * t, w + t)
    blk3 = lambda last: pl.BlockSpec((gb, last[0], last[1]), lambda i: (i, 0, 0))
    return pl.pallas_call(
        _attn_sample_body,
        out_shape=(jax.ShapeDtypeStruct((nb, t, attn_w), BF16),
                   jax.ShapeDtypeStruct((nb, w, kv_w), F32),
                   jax.ShapeDtypeStruct((nb, w, kv_w), F32)),
        grid=(nb // gb,),
        in_specs=[pl.BlockSpec(memory_space=pltpu.SMEM),
                  blk3((t, attn_w)), blk3((t, kv_w)), blk3((t, kv_w)),
                  blk3((w, kv_w)), blk3((w, kv_w)),
                  _resident(bias.shape), _resident((1, 1, attn_w))],
        out_specs=(blk3((t, attn_w)), blk3((w, kv_w)), blk3((w, kv_w))),
        compiler_params=_cparams("arbitrary"),
        name="attention_sample",
    )(sinks, q3, k3, v3, ck, cv, bias, g_attn.reshape(1, 1, attn_w))


def _ssm_tail(y, u, d_ref, wglu_ref, bglu_ref, g_ref):
    z = jax.nn.gelu(y + d_ref[...] * u)
    gate = jax.nn.sigmoid(jnp.dot(z.astype(BF16), wglu_ref[...], preferred_element_type=F32) + bglu_ref[...])
    return _rms(z * gate, g_ref[...]).astype(BF16)


def _ssm_chunk_body(ut_ref, m_ref, e_ref, f_ref, w1_ref, w2_ref, yt_ref, hend_ref, s_scr):
    t, gb, ch, nc = ut_ref.shape
    n_levels = w1_ref.shape[1]
    two_p = e_ref.shape[1]
    row = lax.broadcasted_iota(jnp.int32, (nc, two_p), 0)
    rs = [ut_ref[:, gl].reshape(t * ch, nc) for gl in range(gb)]
    hs = []
    for gl in range(gb):
        s_scr[gl] = jnp.dot(e_ref[gl], rs[gl], preferred_element_type=F32)
        hs.append(s_scr[gl].T)

    def conv(gl):
        yt_ref[:, gl * ch:(gl + 1) * ch, :] = jnp.dot(m_ref[gl], rs[gl], preferred_element_type=F32).reshape(t, ch, nc)

    for lv in range(n_levels):
        sh = 1 << lv
        for gl in range(gb):
            prev = jnp.where(row >= sh, pltpu.roll(hs[gl], sh, axis=0), 0.0)
            hs[gl] = hs[gl] + w1_ref[gl, lv] * prev + w2_ref[gl, lv] * pltpu.roll(prev, two_p // 2, axis=1)
        for gl in range(lv * gb // n_levels, (lv + 1) * gb // n_levels):
            conv(gl)
    for gl in range(gb):
        hend_ref[gl] = hs[gl][nc - 1:nc, :]
        s_scr[gl] = jnp.where(row >= 1, pltpu.roll(hs[gl], 1, axis=0), 0.0).T
        yt_ref[:, gl * ch:(gl + 1) * ch, :] += jnp.dot(f_ref[gl], s_scr[gl].astype(BF16),
                                                       preferred_element_type=F32).reshape(t, ch, nc)


def _ssm_transpose_body(u_ref, o_ref):
    for s in range(u_ref.shape[0]):
        o_ref[s] = u_ref[s].T.astype(BF16).reshape(o_ref.shape[1:])


def _ssm_transpose(u2, ch):
    nb, t, nc, ssm_w = u2.shape
    tb = SSM_TAIL_BLOCK
    return pl.pallas_call(
        _ssm_transpose_body,
        out_shape=jax.ShapeDtypeStruct((nb, t, ssm_w // ch, ch, nc), BF16),
        grid=(nb, t // tb),
        in_specs=[pl.BlockSpec((None, tb, nc, ssm_w), lambda b, i: (b, i, 0, 0))],
        out_specs=pl.BlockSpec((None, tb, ssm_w // ch, ch, nc), lambda b, i: (b, i, 0, 0, 0)),
        compiler_params=_cparams("arbitrary", "arbitrary"),
        name="ssm_transpose",
    )(u2)


def _ssm_chunked(ut, kw):
    nb, t, ng, ch, nc = ut.shape
    tc = t * ch
    two_p = kw["e"].shape[1]
    n_levels = kw["w1"].shape[1]
    gb = SSM_GROUP_BLOCK
    grp = lambda shape: pl.BlockSpec((gb,) + shape, lambda b, g: (g,) + (0,) * len(shape))
    return pl.pallas_call(
        _ssm_chunk_body,
        out_shape=(jax.ShapeDtypeStruct((nb, t, ng * ch, nc), F32),
                   jax.ShapeDtypeStruct((nb, ng, 1, two_p), F32)),
        grid=(nb, ng // gb),
        in_specs=[pl.BlockSpec((None, t, gb, ch, nc), lambda b, g: (b, 0, g, 0, 0)),
                  grp((tc, tc)), grp((two_p, tc)), grp((tc, two_p)),
                  grp((n_levels, 1, two_p)), grp((n_levels, 1, two_p))],
        out_specs=(pl.BlockSpec((None, t, gb * ch, nc), lambda b, g: (b, 0, g, 0)),
                   pl.BlockSpec((None, gb, 1, two_p), lambda b, g: (b, g, 0, 0))),
        scratch_shapes=[pltpu.VMEM((gb, two_p, nc), F32)],
        compiler_params=_cparams("arbitrary", "arbitrary"),
        name="ssm_chunked",
    )(ut, kw["m"], kw["e"], kw["f"], kw["w1"], kw["w2"])


def _ssm_tail_body(yt_ref, u_ref, d_ref, wglu_ref, bglu_ref, g_ref, o_ref):
    for i in range(yt_ref.shape[0]):
        o_ref[i] = _ssm_tail(yt_ref[i].T, u_ref[i], d_ref, wglu_ref, bglu_ref, g_ref)


def _ssm_tail_call(yt, u2, sw):
    nb, t, nc, ssm_w = u2.shape
    tb = SSM_TAIL_BLOCK
    return pl.pallas_call(
        _ssm_tail_body,
        out_shape=jax.ShapeDtypeStruct((nb, t, nc, ssm_w), BF16),
        grid=(nb, t // tb),
        in_specs=[pl.BlockSpec((None, tb, ssm_w, nc), lambda b, i: (b, i, 0, 0)),
                  pl.BlockSpec((None, tb, nc, ssm_w), lambda b, i: (b, i, 0, 0)),
                  _resident((1, ssm_w)), _resident((ssm_w, ssm_w)), _resident((1, ssm_w)), _resident((1, ssm_w))],
        out_specs=pl.BlockSpec((None, tb, nc, ssm_w), lambda b, i: (b, i, 0, 0)),
        compiler_params=_cparams("arbitrary", "arbitrary"),
        name="ssm_tail",
    )(yt, u2, sw["d"], sw["wglu"], sw["bglu"], sw["g"])


def _lag_matrix_body(z_ref, m_ref, *, t):
    gb, ch, _ = z_ref.shape
    width = t * ch
    for gl in range(gb):
        z = z_ref[gl]
        for r in range(t):
            off = (t - 1 - r) * ch
            m_ref[gl, r * ch:(r + 1) * ch, :] = z[:, off:off + width].astype(BF16)


def _lag_matrix(z, t):
    ng, ch, zw = z.shape
    gb = SSM_GROUP_BLOCK
    return pl.pallas_call(
        functools.partial(_lag_matrix_body, t=t),
        out_shape=jax.ShapeDtypeStruct((ng, t * ch, zw // 2), BF16),
        grid=(ng // gb,),
        in_specs=[pl.BlockSpec((gb, ch, zw), lambda g: (g, 0, 0))],
        out_specs=pl.BlockSpec((gb, t * ch, zw // 2), lambda g: (g, 0, 0)),
        compiler_params=_cparams("arbitrary"),
        name="ssm_lag_matrix",
    )(z)


def _ssm_chunk_weights(lam_re, lam_im, log_dt, b_re, b_im, c_re, c_im, t, n_chunks):
    ng, ns = lam_re.shape
    nc = b_re.shape[2]
    dt = jnp.exp(log_dt)[:, None]
    are, aim = lam_re * dt, lam_im * dt
    d = jnp.arange(t + 1, dtype=F32)[:, None, None]
    mag = jnp.exp(d * are)
    pre, pim = mag * jnp.cos(d * aim), mag * jnp.sin(d * aim)
    lre, lim = pre[1], pim[1]
    den = lam_re * lam_re + lam_im * lam_im
    fre = ((lre - 1.0) * lam_re + lim * lam_im) / den
    fim = (lim * lam_re - (lre - 1.0) * lam_im) / den
    bbr = fre[..., None] * b_re - fim[..., None] * b_im
    bbi = fre[..., None] * b_im + fim[..., None] * b_re
    pre_g, pim_g = jnp.transpose(pre, (1, 0, 2)), jnp.transpose(pim, (1, 0, 2))
    xr = c_re[:, None] * pre_g[:, :, None, :] - c_im[:, None] * pim_g[:, :, None, :]
    xi = c_re[:, None] * pim_g[:, :, None, :] + c_im[:, None] * pre_g[:, :, None, :]
    kern = jnp.einsum('gdap,gpc->gdac', jnp.concatenate([xr[:, :t], -xi[:, :t]], axis=3),
                      jnp.concatenate([bbr, bbi], axis=1), precision=HIGHEST)
    lag_rows = jnp.transpose(kern[:, ::-1], (0, 2, 1, 3)).reshape(ng, nc, t * nc)
    m = _lag_matrix(jnp.concatenate([lag_rows, jnp.zeros_like(lag_rows)], axis=2), t)
    rev_re = jnp.transpose(pre_g[:, t - 1::-1][:, :t], (0, 2, 1))
    rev_im = jnp.transpose(pim_g[:, t - 1::-1][:, :t], (0, 2, 1))
    er = rev_re[..., None] * bbr[:, :, None, :] - rev_im[..., None] * bbi[:, :, None, :]
    ei = rev_re[..., None] * bbi[:, :, None, :] + rev_im[..., None] * bbr[:, :, None, :]
    e = jnp.concatenate([er.reshape(ng, ns, t * nc), ei.reshape(ng, ns, t * nc)], axis=1).astype(BF16)
    f = jnp.concatenate([xr[:, 1:].reshape(ng, t * nc, ns), -xi[:, 1:].reshape(ng, t * nc, ns)], axis=2).astype(BF16)
    wr, wi = pre[t], pim[t]
    w1, w2 = [], []
    for _ in range(max(1, (n_chunks - 1).bit_length())):
        w1.append(jnp.concatenate([wr, wr], axis=1))
        w2.append(jnp.concatenate([-wi, wi], axis=1))
        wr, wi = wr * wr - wi * wi, 2.0 * wr * wi
    w1 = jnp.stack(w1, axis=1)[:, :, None, :]
    w2 = jnp.stack(w2, axis=1)[:, :, None, :]
    return {"m": m, "e": e, "f": f, "w1": w1, "w2": w2}


def _ssm_sample_body(u_ref, h0re_ref, h0im_ref, bre_ref, bim_ref, cre_ref, cim_ref, lre_ref, lim_ref,
                     d_ref, wglu_ref, bglu_ref, g_ref, o_ref, hre_ref, him_ref):
    steps = u_ref.shape[0]
    half_in = bre_ref.shape[2]
    dot = functools.partial(jnp.dot, preferred_element_type=F32)
    ar = lre_ref[...]
    ai = lim_ref[...]
    hr = h0re_ref[...]
    hi = h0im_ref[...]
    for t in range(steps):
        u = u_ref[t]
        u_hi = u.astype(BF16)
        u_lo = (u - u_hi.astype(F32)).astype(BF16)

        def bu(b_ref):
            cols = []
            for hf in range(2):
                lanes = slice(hf * half_in, (hf + 1) * half_in)
                cols.append(dot(u_hi[:, lanes], b_ref[0, hf]) + dot(u_lo[:, lanes], b_ref[0, hf])
                            + dot(u_hi[:, lanes], b_ref[1, hf]))
            return jnp.concatenate(cols, axis=1)

        hr, hi = ar * hr - ai * hi + bu(bre_ref), ar * hi + ai * hr + bu(bim_ref)
        half_st = hr.shape[1] // 2
        hrb, hib = hr.astype(BF16), hi.astype(BF16)
        y = jnp.concatenate(
            [dot(hrb[:, hf * half_st:(hf + 1) * half_st], cre_ref[hf])
             + dot(hib[:, hf * half_st:(hf + 1) * half_st], cim_ref[hf]) for hf in range(2)], axis=1)
        o_ref[t] = _ssm_tail(y, u, d_ref, wglu_ref, bglu_ref, g_ref)
    hre_ref[...] = hr
    him_ref[...] = hi


def _ssm_sample(u_tm, h0re, h0im, sw):
    steps, nb, ssm_w = u_tm.shape
    n_state = h0re.shape[1]
    return pl.pallas_call(
        _ssm_sample_body,
        out_shape=(jax.ShapeDtypeStruct((steps, nb, ssm_w), BF16),
                   jax.ShapeDtypeStruct((nb, n_state), F32),
                   jax.ShapeDtypeStruct((nb, n_state), F32)),
        compiler_params=pltpu.CompilerParams(vmem_limit_bytes=VMEM_LIMIT_BYTES),
        name="ssm_sample",
    )(u_tm, h0re, h0im, _hi_lo(sw["bre"]), _hi_lo(sw["bim"]), sw["cre"].astype(BF16), sw["cim"].astype(BF16),
      sw["lre"], sw["lim"], sw["d"], sw["wglu"], sw["bglu"], sw["g"])


def _hi_lo(w):
    hi = w.astype(BF16)
    return jnp.stack([hi, (w - hi.astype(F32)).astype(BF16)])


def _ssm_weights(lam_re, lam_im, log_dt, b_re, b_im, c_re, c_im, d_skip, w_glu, b_glu, g_ssm):
    ng, ns = lam_re.shape
    nc = b_re.shape[2]
    dt = jnp.exp(log_dt)[:, None]
    mag = jnp.exp(lam_re * dt)
    lre = mag * jnp.cos(lam_im * dt)
    lim = mag * jnp.sin(lam_im * dt)
    den = lam_re * lam_re + lam_im * lam_im
    fre = ((lre - 1.0) * lam_re + lim * lam_im) / den
    fim = (lim * lam_re - (lre - 1.0) * lam_im) / den
    bbar_re = fre[..., None] * b_re - fim[..., None] * b_im
    bbar_im = fre[..., None] * b_im + fim[..., None] * b_re
    eye = jnp.eye(ng // 2, dtype=F32)

    def in_blocks(b):
        b2 = b.reshape(2, ng // 2, ns, nc)
        return jnp.einsum('hgpc,gk->hgckp', b2, eye).reshape(2, ng // 2 * nc, ng // 2 * ns)

    def out_blocks(c):
        c2 = c.reshape(2, ng // 2, nc, ns)
        return jnp.einsum('hgcp,gk->hgpkc', c2, eye).reshape(2, ng // 2 * ns, ng // 2 * nc)

    ssm_w = ng * nc
    return {
        "bre": in_blocks(bbar_re), "bim": in_blocks(bbar_im),
        "cre": out_blocks(c_re), "cim": out_blocks(-c_im),
        "lre": lre.reshape(1, ng * ns), "lim": lim.reshape(1, ng * ns),
        "d": d_skip.reshape(1, ssm_w), "wglu": w_glu.astype(BF16), "bglu": b_glu.reshape(1, ssm_w),
        "g": g_ssm.reshape(1, ssm_w),
    }


def _route(logits_t):
    big = jnp.int32(10 ** 6)
    gl = logits_t[:N_EXPERT_GROUPS]
    grow = lax.broadcasted_iota(jnp.int32, gl.shape, 0)
    gmax = jnp.max(gl, axis=0, keepdims=True)
    gidx = jnp.min(jnp.where(gl == gmax, grow, big), axis=0, keepdims=True)
    g_p = 1.0 / jnp.sum(jnp.exp(gl - gmax), axis=0, keepdims=True)
    ex = logits_t[N_EXPERT_GROUPS:N_EXPERT_GROUPS + N_EXPERTS]
    erow = lax.broadcasted_iota(jnp.int32, ex.shape, 0)
    lo = gidx * EXPERTS_PER_GROUP
    in_group = (erow >= lo) & (erow < lo + EXPERTS_PER_GROUP)
    el = jnp.where(in_group, ex, NEG_INF)
    ee = jnp.exp(el - jnp.max(el, axis=0, keepdims=True))
    prob = ee / jnp.sum(ee, axis=0, keepdims=True)
    p1 = jnp.max(jnp.where(in_group, prob, -1.0), axis=0, keepdims=True)
    i1 = jnp.min(jnp.where(in_group & (prob == p1), erow, big), axis=0, keepdims=True)
    rest = in_group & (erow != i1)
    p2 = jnp.max(jnp.where(rest, prob, -1.0), axis=0, keepdims=True)
    i2 = jnp.min(jnp.where(rest & (prob == p2), erow, big), axis=0, keepdims=True)
    tot = p1 + p2
    return jnp.where(erow == i1, g_p * p1 / tot, jnp.where(erow == i2, g_p * p2 / tot, 0.0)), gidx


def _split3(x):
    a = x.astype(BF16)
    r = x - a.astype(F32)
    b = r.astype(BF16)
    return a, b, (r - b.astype(F32)).astype(BF16)


def _moe_sorted(hb, gates_t, gidx, tri_ref, wg_ref, wu_ref, wd_ref, xs_scr, gs_scr, ys_scr):
    rows, d = hb.shape
    grow = lax.broadcasted_iota(jnp.int32, (SUBLANES, rows), 0)
    member = grow == gidx
    csum = jnp.dot(member.astype(BF16), tri_ref[...], preferred_element_type=F32)
    rank = jnp.sum(jnp.where(member, csum, 0.0), axis=0, keepdims=True)
    counts = [jnp.sum((gidx == grp).astype(jnp.int32)) for grp in range(N_EXPERT_GROUPS - 1)]
    offs = [jnp.int32(0)]
    for c in counts:
        offs.append(offs[-1] + c)
    offs.append(jnp.int32(rows))
    base = jnp.zeros_like(gidx)
    for grp in range(1, N_EXPERT_GROUPS):
        base = jnp.where(gidx == grp, offs[grp], base)
    pos_lanes = base.astype(F32) + rank - 1.0
    pos = jnp.broadcast_to(pos_lanes, (SUBLANES, rows)).T[:, 0:1]
    col = lax.broadcasted_iota(jnp.int32, (rows, rows), 1).astype(F32)
    row = lax.broadcasted_iota(jnp.int32, (rows, rows), 0).astype(F32)
    unsort = (col == pos).astype(BF16)
    sort = (row == pos_lanes).astype(BF16)
    xs_scr[...] = jnp.dot(sort, hb, preferred_element_type=F32).astype(BF16)
    pad = jnp.zeros((LANES - N_EXPERT_GROUPS - N_EXPERTS, rows), F32)
    gates_pad = jnp.concatenate([jnp.zeros((N_EXPERT_GROUPS, rows), F32), gates_t, pad], axis=0)
    gs_t = sum(jnp.dot(part, unsort, preferred_element_type=F32) for part in _split3(gates_pad))
    gs_scr[...] = gs_t.T
    ys_scr[...] = jnp.zeros_like(ys_scr)
    blk = MOE_ROW_BLOCK
    n_blk = rows // blk
    cuts = [jnp.int32(k * blk) for k in range(1, n_blk)] + offs[1:N_EXPERT_GROUPS]
    assert len(cuts) == 6
    for a, b in ((0, 5), (1, 3), (2, 4), (1, 2), (3, 4), (0, 3), (2, 5), (0, 1), (2, 3), (4, 5), (1, 2), (3, 4)):
        cuts[a], cuts[b] = jnp.minimum(cuts[a], cuts[b]), jnp.maximum(cuts[a], cuts[b])
    cuts = [jnp.int32(0)] + cuts + [jnp.int32(rows)]
    lane_b = lax.broadcasted_iota(jnp.int32, (blk, LANES), 1)
    for j in range(len(cuts) - 1):
        start = cuts[j]
        live = (cuts[j + 1] > start).astype(F32)
        k = jnp.minimum(start // blk, n_blk - 1)
        grp = sum((offs[q] <= start).astype(jnp.int32) for q in range(1, N_EXPERT_GROUPS))
        sl = pl.ds(pl.multiple_of(k * blk, blk), blk)
        xb = xs_scr[sl, :]
        gb = gs_scr[sl, :] * live
        acts = []
        for i in range(EXPERTS_PER_GROUP):
            e = grp * EXPERTS_PER_GROUP + i
            gate = jnp.sum(jnp.where(lane_b == N_EXPERT_GROUPS + e, gb, 0.0), axis=-1, keepdims=True)
            hg = jnp.dot(xb, wg_ref[e], preferred_element_type=F32)
            hu = jnp.dot(xb, wu_ref[e], preferred_element_type=F32)
            acts.append((jax.nn.silu(hg) * hu * gate).astype(BF16))
        ys_scr[sl, :] += jnp.dot(jnp.concatenate(acts, axis=1), wd_ref[grp], preferred_element_type=F32)
    hi = ys_scr[...].astype(BF16)
    lo = (ys_scr[...] - hi.astype(F32)).astype(BF16)
    return unsort, hi, lo


def _ffn_body(x_ref, ssm_ref, attn_ref, mod_ref, gpost_ref, gpre_ref, gffn_ref, wout_ref, wr_ref, br_ref,
              tri_ref, wg_ref, wu_ref, wd_ref, o_ref, xs_scr, gs_scr, ys_scr, *, chunked):
    g, r, d = x_ref.shape
    rows = g * r
    if chunked:
        ssm = jnp.swapaxes(ssm_ref[...].astype(F32), 0, 1).reshape(rows, ssm_ref.shape[-1]).astype(BF16)
    else:
        ssm = ssm_ref[...]
    sub = min(rows, FFN_SUB)
    x1s, hbs, hlos = [], [], []
    for c in range(rows // sub):
        rs = slice(c * sub, (c + 1) * sub)
        if g == 1:
            x = x_ref[:, rs, :]
            md = mod_ref
        else:
            gsl = slice(c * (sub // r), (c + 1) * (sub // r))
            x = x_ref[gsl]
            md = mod_ref.at[gsl]
        gg = x.shape[0]
        cat = jnp.concatenate([ssm[rs], attn_ref[rs, :]], axis=1)
        mixed = jnp.dot(cat, wout_ref[...], preferred_element_type=F32)
        x1 = x + md[:, 2] * _rms(mixed, gpost_ref[0]).reshape(gg, -1, d)
        hn = (_rms(x1, gpre_ref[...]) * (1.0 + md[:, 4]) + md[:, 3]).reshape(sub, d)
        hb = hn.astype(BF16)
        x1s.append(x1)
        hbs.append(hb)
        hlos.append((hn - hb.astype(F32)).astype(BF16))
    hb = jnp.concatenate(hbs, axis=0)
    hlo = jnp.concatenate(hlos, axis=0)
    nt = functools.partial(lax.dot_general, dimension_numbers=(((1,), (1,)), ((), ())), preferred_element_type=F32)
    logits_t = nt(wr_ref[0], hb) + nt(wr_ref[0], hlo) + nt(wr_ref[1], hb) + br_ref[...]
    gates_t, gidx = _route(logits_t)
    unsort, hi, lo = _moe_sorted(hb, gates_t, gidx, tri_ref, wg_ref, wu_ref, wd_ref, xs_scr, gs_scr, ys_scr)
    for c in range(rows // sub):
        rs = slice(c * sub, (c + 1) * sub)
        ffn = (jnp.dot(unsort[rs], hi, preferred_element_type=F32) + jnp.dot(unsort[rs], lo, preferred_element_type=F32))
        if g == 1:
            o_ref[:, rs, :] = x1s[c] + mod_ref[:, 5] * _rms(ffn, gffn_ref[0]).reshape(1, sub, d)
        else:
            gsl = slice(c * (sub // r), (c + 1) * (sub // r))
            o_ref[gsl] = x1s[c] + mod_ref[gsl, 5] * _rms(ffn, gffn_ref[0]).reshape(sub // r, r, d)


def _ffn(x3, ssm_n, attn_n, mod4, fw, gb, rb, chunk=None):
    nb, nr, d = x3.shape
    assert gb == 1 or rb == nr
    half = ssm_n.shape[-1]
    nj = nr // rb
    tm = gb * rb
    row_map = lambda i, j: (i * nj + j, 0)
    vec = _resident((1, 1, d))
    if chunk is None:
        ssm_spec = pl.BlockSpec((tm, half), row_map)
    else:
        assert gb == 1 and rb % chunk == 0
        ssm_spec = pl.BlockSpec((None, chunk, rb // chunk, half), lambda i, j: (i, 0, j, 0))
    return pl.pallas_call(
        functools.partial(_ffn_body, chunked=chunk is not None),
        out_shape=jax.ShapeDtypeStruct((nb, nr, d), F32),
        grid=(nb // gb, nj),
        in_specs=[pl.BlockSpec((gb, rb, d), lambda i, j: (i, j, 0)),
                  ssm_spec,
                  pl.BlockSpec((tm, attn_n.shape[1]), row_map),
                  pl.BlockSpec((gb, N_MOD, 1, d), lambda i, j: (i, 0, 0, 0)),
                  vec, vec, vec,
                  _resident(fw["wout"].shape), _resident(fw["wr"].shape), _resident(fw["br"].shape),
                  _resident((tm, tm)),
                  _resident(fw["wg"].shape), _resident(fw["wu"].shape), _resident(fw["wd"].shape)],
        out_specs=pl.BlockSpec((gb, rb, d), lambda i, j: (i, j, 0)),
        scratch_shapes=[pltpu.VMEM((tm, d), BF16), pltpu.VMEM((tm, LANES), F32), pltpu.VMEM((tm, d), F32)],
        compiler_params=_cparams("arbitrary", "arbitrary"),
        name="outproj_moe",
    )(x3, ssm_n, attn_n, mod4, fw["gpost"], fw["gpre"], fw["gffn"], fw["wout"], fw["wr"], fw["br"],
      jnp.triu(jnp.ones((tm, tm), BF16)), fw["wg"], fw["wu"], fw["wd"])


def _ffn_weights(g_post_mix, g_pre_ffn, g_post_ffn, w_out, w_gr, b_gr, w_er, b_er, w_gate, w_up, w_down):
    d = w_out.shape[0]
    n_route = N_EXPERT_GROUPS + N_EXPERTS
    wr = jnp.concatenate([w_gr, w_er.reshape(d, N_EXPERTS)], axis=1)
    br = jnp.concatenate([b_gr, b_er.reshape(N_EXPERTS)])
    wr = jnp.pad(wr, ((0, 0), (0, ROUTE_ROWS - n_route))).T
    return {
        "gpost": g_post_mix.reshape(1, 1, d), "gpre": g_pre_ffn.reshape(1, 1, d), "gffn": g_post_ffn.reshape(1, 1, d),
        "wout": w_out.astype(BF16),
        "wr": _hi_lo(wr),
        "br": jnp.pad(br, (0, ROUTE_ROWS - n_route)).reshape(ROUTE_ROWS, 1),
        "wg": w_gate.astype(BF16), "wu": w_up.astype(BF16),
        "wd": w_down.astype(BF16).reshape(N_EXPERT_GROUPS, -1, d),
    }


def _rel_bucket(dist):
    max_exact = NUM_BUCKETS // 2
    dd = jnp.maximum(dist, 0)
    log_ratio = jnp.log(jnp.maximum(dd, 1).astype(F32) / max_exact) / math.log(MAX_DISTANCE / max_exact)
    large = jnp.minimum(max_exact + (log_ratio * (NUM_BUCKETS - max_exact)).astype(jnp.int32), NUM_BUCKETS - 1)
    return jnp.where(dd < max_exact, dd, large)


def _bias_and_band(dist, table):
    onehot = (_rel_bucket(dist)[:, :, None] == jnp.arange(NUM_BUCKETS)[None, None, :]).astype(F32)
    bias = jnp.einsum('qkb,bh->hqk', onehot, table.astype(F32), precision=HIGHEST)
    band = ((dist >= 0) & (dist <= WINDOW)).astype(F32)
    return bias, band


def _layer(xp, xs, ck, cv, h0re, h0im, mod_p, mod_s, table, p):
    nb, seq, d = xp.shape
    nbs, t, _ = xs.shape
    w = ck.shape[1]
    kv_w = N_KV_HEADS * HEAD_DIM
    w_in_bf = p["w_in"].astype(BF16)
    sw = _ssm_weights(p["ssm_lam_re"], p["ssm_lam_im"], p["ssm_log_dt"], p["ssm_b_re"], p["ssm_b_im"],
                      p["ssm_c_re"], p["ssm_c_im"], p["ssm_d"], p["w_glu"], p["b_glu"], p["g_ssm_out"])
    fw = _ffn_weights(p["g_post_mix"], p["g_pre_ffn"], p["g_post_ffn"], p["w_out"],
                      p["w_group_router"], p["b_group_router"], p["w_expert_router"], p["b_expert_router"],
                      p["w_exp_gate"], p["w_exp_up"], p["w_exp_down"])
    n_state = sw["lre"].shape[1]
    ng = p["ssm_lam_re"].shape[0]

    u2, q, k, v = _inproj(xp, mod_p, p["g_pre_mix"], w_in_bf, 1, INPROJ_TILE, chunk=SSM_CHUNK)
    blk = WINDOW
    dist_p = jnp.arange(blk)[:, None] + blk - jnp.arange(2 * blk)[None, :]
    bias_p, band_p = _bias_and_band(dist_p, table)
    attn_p = _attn_prompt(q, k, v, p["attn_sinks"], bias_p, band_p, p["g_attn_out"], nb, seq)
    kw = _ssm_chunk_weights(p["ssm_lam_re"], p["ssm_lam_im"], p["ssm_log_dt"], p["ssm_b_re"], p["ssm_b_im"],
                            p["ssm_c_re"], p["ssm_c_im"], SSM_CHUNK, seq // SSM_CHUNK)
    yt, hend = _ssm_chunked(_ssm_transpose(u2, p["ssm_b_re"].shape[2]), kw)
    ssm_p = _ssm_tail_call(yt, u2, sw)
    yp = _ffn(xp, ssm_p, attn_p, mod_p, fw, 1, TOKEN_TILE, chunk=SSM_CHUNK)
    wp = min(WINDOW, seq)
    k_p = k.reshape(nb, seq, kv_w)[:, seq - wp:].reshape(nb, wp, N_KV_HEADS, HEAD_DIM)
    v_p = v.reshape(nb, seq, kv_w)[:, seq - wp:].reshape(nb, wp, N_KV_HEADS, HEAD_DIM)
    n_p = n_state // ng
    hre_p = hend[:, :, 0, :n_p]
    him_p = hend[:, :, 0, n_p:]

    gs = TOKEN_TILE // t
    us, qs, ks, vs = _inproj(xs, mod_s, p["g_pre_mix"], w_in_bf, gs, t)
    dist_s = jnp.arange(t)[:, None] + w - jnp.arange(w + t)[None, :]
    bias_s, band_s = _bias_and_band(dist_s, table)
    attn_s, k_s, v_s = _attn_sample(qs.reshape(nbs, t, -1), ks.reshape(nbs, t, kv_w), vs.reshape(nbs, t, kv_w),
                                    ck.reshape(nbs, w, kv_w), cv.reshape(nbs, w, kv_w),
                                    p["attn_sinks"], bias_s, band_s, p["g_attn_out"], 16)
    u_tm = jnp.swapaxes(us.reshape(nbs, t, -1), 0, 1)
    ssm_tm, hre_s, him_s = _ssm_sample(u_tm, h0re.reshape(nbs, n_state), h0im.reshape(nbs, n_state), sw)
    ssm_s = jnp.swapaxes(ssm_tm, 0, 1).reshape(nbs * t, -1)
    ys = _ffn(xs, ssm_s, attn_s.reshape(nbs * t, -1), mod_s, fw, gs, t)
    k_s = k_s.reshape(nbs, w, N_KV_HEADS, HEAD_DIM)
    v_s = v_s.reshape(nbs, w, N_KV_HEADS, HEAD_DIM)
    hre_s = hre_s.reshape(nbs, ng, n_state // ng)
    him_s = him_s.reshape(nbs, ng, n_state // ng)
    return yp, ys, k_p, v_p, hre_p, him_p, k_s, v_s, hre_s, him_s


def kernel(x_prompt, x_sample, cache_k, cache_v, state_ssm_re, state_ssm_im, c_prompt, c_sample, rel_bias_table,
           w_ada, b_ada, g_pre_mix, g_post_mix, g_pre_ffn, g_post_ffn, w_in, ssm_lam_re, ssm_lam_im, ssm_log_dt,
           ssm_b_re, ssm_b_im, ssm_c_re, ssm_c_im, ssm_d, w_glu, b_glu, attn_sinks, g_ssm_out, g_attn_out, w_out,
           w_group_router, b_group_router, w_expert_router, b_expert_router, w_exp_gate, w_exp_up, w_exp_down):
    params = dict(
        w_ada=w_ada, b_ada=b_ada, g_pre_mix=g_pre_mix, g_post_mix=g_post_mix, g_pre_ffn=g_pre_ffn,
        g_post_ffn=g_post_ffn, w_in=w_in, ssm_lam_re=ssm_lam_re, ssm_lam_im=ssm_lam_im, ssm_log_dt=ssm_log_dt,
        ssm_b_re=ssm_b_re, ssm_b_im=ssm_b_im, ssm_c_re=ssm_c_re, ssm_c_im=ssm_c_im, ssm_d=ssm_d, w_glu=w_glu,
        b_glu=b_glu, attn_sinks=attn_sinks, g_ssm_out=g_ssm_out, g_attn_out=g_attn_out, w_out=w_out,
        w_group_router=w_group_router, b_group_router=b_group_router, w_expert_router=w_expert_router,
        b_expert_router=b_expert_router, w_exp_gate=w_exp_gate, w_exp_up=w_exp_up, w_exp_down=w_exp_down)
    depth = w_in.shape[0]
    nb, nbs = x_prompt.shape[0], x_sample.shape[0]
    d = x_prompt.shape[2]
    pad = (-nb) % 8
    yp, ys = x_prompt, x_sample
    outs = [[] for _ in range(8)]
    for li in range(depth):
        p = {name: val[li] for name, val in params.items()}
        c_all = jnp.concatenate([c_prompt, jnp.zeros((pad, d), F32), c_sample], axis=0)
        mod = _modulation(c_all, p["w_ada"], p["b_ada"])
        mod_p = mod[:nb].reshape(nb, N_MOD, 1, d)
        mod_s = mod[nb + pad:].reshape(nbs, N_MOD, 1, d)
        res = _layer(yp, ys, cache_k[li], cache_v[li], state_ssm_re[li], state_ssm_im[li], mod_p, mod_s,
                     rel_bias_table, p)
        yp, ys = res[0], res[1]
        for acc, val in zip(outs, res[2:]):
            acc.append(val)
    return (yp, ys) + tuple(jnp.stack(o) for o in outs)
```

```python
import functools
import math

import jax
import jax.numpy as jnp
from jax import lax
from jax.experimental import pallas as pl
from jax.experimental.pallas import tpu as pltpu

F32 = jnp.float32
BF16 = jnp.bfloat16
HIGHEST = lax.Precision.HIGHEST

EPS = 1e-6
NEG_INF = -1e30

HEAD_DIM = 64
N_KV_HEADS = 2
GQA = 4
N_HEADS = N_KV_HEADS * GQA
WINDOW = 128
NUM_BUCKETS = 32
MAX_DISTANCE = 128
N_EXPERT_GROUPS = 4
EXPERTS_PER_GROUP = 4
N_EXPERTS = N_EXPERT_GROUPS * EXPERTS_PER_GROUP
N_MOD = 6

LANES = 128
SUBLANES = 8
VMEM_LIMIT_BYTES = 56 * 1024 * 1024

TOKEN_TILE = 512
INPROJ_TILE = 2048
INPROJ_SUB = 256
FFN_SUB = 128
ATTN_TILE = 1024
ATTN_SAMPLE_BLOCK = 32
MOD_COL_TILE = 2048
SSM_CHUNK = 32
SSM_GROUP_BLOCK = 16
SSM_TAIL_BLOCK = 16
MOE_ROW_BLOCK = 128
ROUTE_ROWS = 32


def _cparams(*sem):
    return pltpu.CompilerParams(dimension_semantics=sem, vmem_limit_bytes=VMEM_LIMIT_BYTES)


def _resident(shape):
    zeros = (0,) * len(shape)
    return pl.BlockSpec(shape, lambda *_: zeros, pipeline_mode=pl.Buffered(1))


def _rms(x, g):
    return x * lax.rsqrt(jnp.mean(x * x, axis=-1, keepdims=True) + EPS) * g


def _mod_body(c_ref, w_ref, b_ref, o_ref):
    a = jax.nn.silu(c_ref[...])
    w = w_ref[...]
    a_hi, w_hi = a.astype(BF16), w.astype(BF16)
    a_lo, w_lo = (a - a_hi.astype(F32)).astype(BF16), (w - w_hi.astype(F32)).astype(BF16)
    dot = functools.partial(jnp.dot, preferred_element_type=F32)
    o_ref[...] = dot(a_hi, w_hi) + dot(a_lo, w_hi) + dot(a_hi, w_lo) + b_ref[...]


def _modulation(c, w_ada, b_ada):
    rows, d = c.shape
    n = w_ada.shape[1]
    tn = MOD_COL_TILE
    return pl.pallas_call(
        _mod_body,
        out_shape=jax.ShapeDtypeStruct((rows, n), F32),
        grid=(n // tn,),
        in_specs=[pl.BlockSpec((rows, d), lambda j: (0, 0)),
                  pl.BlockSpec((d, tn), lambda j: (0, j)),
                  pl.BlockSpec((1, tn), lambda j: (0, j))],
        out_specs=pl.BlockSpec((rows, tn), lambda j: (0, j)),
        compiler_params=_cparams("arbitrary"),
        name="modulation",
    )(c, w_ada, b_ada.reshape(1, n))


def _inproj_body(x_ref, mod_ref, g_ref, w_ref, u_ref, q_ref, k_ref, v_ref, *, chunked):
    g, r, d = x_ref.shape
    ssm_w = u_ref.shape[-1]
    attn_w = q_ref.shape[1]
    kv_w = k_ref.shape[1]
    rows = g * r
    sub = min(rows, INPROJ_SUB)
    hns = []
    for c in range(rows // sub):
        if g == 1:
            x = x_ref[:, c * sub:(c + 1) * sub, :]
            hn = _rms(x, g_ref[...]) * (1.0 + mod_ref[:, 1]) + mod_ref[:, 0]
        else:
            gs = sub // r
            x = x_ref[c * gs:(c + 1) * gs]
            hn = _rms(x, g_ref[...]) * (1.0 + mod_ref[c * gs:(c + 1) * gs, 1]) + mod_ref[c * gs:(c + 1) * gs, 0]
        hns.append(hn.reshape(sub, d).astype(BF16))
    for c, hn in enumerate(hns):
        rs = slice(c * sub, (c + 1) * sub)
        proj = jnp.dot(hn, w_ref[...], preferred_element_type=F32)
        if chunked:
            t = u_ref.shape[0]
            nc = sub // t
            u_ref[:, c * nc:(c + 1) * nc, :] = jnp.swapaxes(proj[:, :ssm_w].reshape(nc, t, ssm_w), 0, 1)
        else:
            u_ref[rs, :] = proj[:, :ssm_w]
        q_ref[rs, :] = (proj[:, ssm_w:ssm_w + attn_w] * (HEAD_DIM ** -0.5)).astype(BF16)
        k_ref[rs, :] = proj[:, ssm_w + attn_w:ssm_w + attn_w + kv_w]
        v_ref[rs, :] = proj[:, ssm_w + attn_w + kv_w:]


def _inproj(x3, mod4, g_pre, w_in_bf, gb, rb, chunk=None):
    nb, nr, d = x3.shape
    assert (gb == 1 or rb == nr) and nr % rb == 0 and nb % gb == 0
    n_in = w_in_bf.shape[1]
    kv_w = N_KV_HEADS * HEAD_DIM
    attn_w = N_HEADS * HEAD_DIM
    ssm_w = n_in - attn_w - 2 * kv_w
    rows = nb * nr
    nj = nr // rb
    tm = gb * rb
    out_map = lambda i, j: (i * nj + j, 0)
    if chunk is None:
        u_shape = jax.ShapeDtypeStruct((rows, ssm_w), F32)
        u_spec = pl.BlockSpec((tm, ssm_w), out_map)
    else:
        assert gb == 1 and rb % chunk == 0
        u_shape = jax.ShapeDtypeStruct((nb, chunk, nr // chunk, ssm_w), F32)
        u_spec = pl.BlockSpec((None, chunk, rb // chunk, ssm_w), lambda i, j: (i, 0, j, 0))
    return pl.pallas_call(
        functools.partial(_inproj_body, chunked=chunk is not None),
        out_shape=(u_shape,
                   jax.ShapeDtypeStruct((rows, attn_w), BF16),
                   jax.ShapeDtypeStruct((rows, kv_w), F32),
                   jax.ShapeDtypeStruct((rows, kv_w), F32)),
        grid=(nb // gb, nj),
        in_specs=[pl.BlockSpec((gb, rb, d), lambda i, j: (i, j, 0)),
                  pl.BlockSpec((gb, 2, 1, d), lambda i, j: (i, 0, 0, 0)),
                  _resident((1, 1, d)),
                  _resident((d, n_in))],
        out_specs=(u_spec,
                   pl.BlockSpec((tm, attn_w), out_map),
                   pl.BlockSpec((tm, kv_w), out_map),
                   pl.BlockSpec((tm, kv_w), out_map)),
        compiler_params=_cparams("arbitrary", "arbitrary"),
        name="in_projection",
    )(x3, mod4, g_pre.reshape(1, 1, d), w_in_bf)


def _attn_prompt_body(sinks_ref, q_ref, kc_ref, vc_ref, kp_ref, vp_ref, bias_ref, g_ref, o_ref, s_scr, p_scr):
    tq = q_ref.shape[0]
    blk = bias_ref.shape[2] // 2
    first_tile = pl.program_id(1) == 0
    kk = jnp.concatenate([kp_ref[...], kc_ref[...]], axis=0).astype(BF16)
    vv = jnp.concatenate([vp_ref[...], vc_ref[...]], axis=0).astype(BF16)
    shape = (GQA * blk, 2 * blk)
    col = lax.broadcasted_iota(jnp.int32, shape, 1)
    row = lax.broadcasted_iota(jnp.int32, (GQA * blk, 1), 0)
    no_prev = first_tile & (col < blk)
    sink_cols = []
    for hk in range(N_KV_HEADS):
        sk = jnp.full((GQA * blk, 1), sinks_ref[hk * GQA], F32)
        for gq in range(1, GQA):
            sk = jnp.where(row >= gq * blk, sinks_ref[hk * GQA + gq], sk)
        sink_cols.append(sk)
    pairs = [(j, hk) for j in range(tq // blk) for hk in range(N_KV_HEADS)]
    for idx, (j, hk) in enumerate(pairs):
        qg = jnp.concatenate([q_ref[j * blk:(j + 1) * blk, h * HEAD_DIM:(h + 1) * HEAD_DIM]
                              for h in range(hk * GQA, (hk + 1) * GQA)], axis=0)
        s = lax.dot_general(qg, kk[j * blk:(j + 2) * blk, hk * HEAD_DIM:(hk + 1) * HEAD_DIM],
                            (((1,), (1,)), ((), ())), preferred_element_type=F32) + bias_ref[hk]
        s_scr[idx] = jnp.where(no_prev, NEG_INF, s) if j == 0 else s
    dens = []
    for idx, (j, hk) in enumerate(pairs):
        s = s_scr[idx]
        m = jnp.maximum(jnp.max(s, axis=-1, keepdims=True), sink_cols[hk])
        p = jnp.exp(s - m)
        dens.append(jnp.sum(p, axis=-1, keepdims=True) + jnp.exp(sink_cols[hk] - m))
        p_scr[idx] = p.astype(BF16)
    outs = {}
    for idx, (j, hk) in enumerate(pairs):
        o = jnp.dot(p_scr[idx], vv[j * blk:(j + 2) * blk, hk * HEAD_DIM:(hk + 1) * HEAD_DIM],
                    preferred_element_type=F32) / dens[idx]
        for gq in range(GQA):
            outs[j, hk * GQA + gq] = o[gq * blk:(gq + 1) * blk]
    for j in range(tq // blk):
        att = jnp.concatenate([outs[j, h] for h in range(N_HEADS)], axis=1)
        o_ref[j * blk:(j + 1) * blk, :] = _rms(att, g_ref[...]).astype(BF16)


def _attn_prompt(q, k, v, sinks, bias, band, g_attn, nb, seq):
    rows, attn_w = q.shape
    kv_w = k.shape[1]
    blk = band.shape[0]
    bias = jnp.where(band > 0.5, bias, NEG_INF).reshape(N_KV_HEADS, GQA * blk, 2 * blk)
    tq = ATTN_TILE
    assert seq % tq == 0 and tq % blk == 0
    nt = seq // tq
    per = tq // blk
    cur = lambda b, i: (b * nt + i, 0)
    prev = lambda b, i: (b * nt * per + jnp.maximum(i * per - 1, 0), 0)
    return pl.pallas_call(
        _attn_prompt_body,
        out_shape=jax.ShapeDtypeStruct((rows, attn_w), BF16),
        grid=(nb, nt),
        in_specs=[pl.BlockSpec(memory_space=pltpu.SMEM),
                  pl.BlockSpec((tq, attn_w), cur),
                  pl.BlockSpec((tq, kv_w), cur),
                  pl.BlockSpec((tq, kv_w), cur),
                  pl.BlockSpec((blk, kv_w), prev),
                  pl.BlockSpec((blk, kv_w), prev),
                  _resident(bias.shape),
                  _resident((1, attn_w))],
        out_specs=pl.BlockSpec((tq, attn_w), cur),
        scratch_shapes=[pltpu.VMEM((per * N_KV_HEADS, GQA * blk, 2 * blk), F32),
                        pltpu.VMEM((per * N_KV_HEADS, GQA * blk, 2 * blk), BF16)],
        compiler_params=_cparams("arbitrary", "arbitrary"),
        name="attention_prompt",
    )(sinks, q, k, v, k, v, bias, g_attn.reshape(1, attn_w))


def _attn_sample_body(sinks_ref, q_ref, kn_ref, vn_ref, ck_ref, cv_ref, bias_ref, g_ref, o_ref, ko_ref, vo_ref):
    t = q_ref.shape[1]
    kk = jnp.concatenate([ck_ref[...], kn_ref[...]], axis=1)
    vv = jnp.concatenate([cv_ref[...], vn_ref[...]], axis=1)
    ko_ref[...] = kk[:, t:, :]
    vo_ref[...] = vv[:, t:, :]
    kkb = kk.astype(BF16)
    vvb = vv.astype(BF16)
    row = lax.broadcasted_iota(jnp.int32, (GQA * t, 1), 0)
    outs = [None] * N_HEADS
    for hk in range(N_KV_HEADS):
        sk = jnp.full((GQA * t, 1), sinks_ref[hk * GQA], F32)
        for gq in range(1, GQA):
            sk = jnp.where(row >= gq * t, sinks_ref[hk * GQA + gq], sk)
        qg = jnp.concatenate([q_ref[:, :, h * HEAD_DIM:(h + 1) * HEAD_DIM]
                              for h in range(hk * GQA, (hk + 1) * GQA)], axis=1)
        s = jnp.einsum('bqd,bkd->bqk', qg, kkb[:, :, hk * HEAD_DIM:(hk + 1) * HEAD_DIM],
                       preferred_element_type=F32) + bias_ref[hk]
        m = jnp.maximum(jnp.max(s, axis=-1, keepdims=True), sk)
        p = jnp.exp(s - m)
        den = jnp.sum(p, axis=-1, keepdims=True) + jnp.exp(sk - m)
        o = jnp.einsum('bqk,bkd->bqd', p.astype(BF16), vvb[:, :, hk * HEAD_DIM:(hk + 1) * HEAD_DIM],
                       preferred_element_type=F32) / den
        for gq in range(GQA):
            outs[hk * GQA + gq] = o[:, gq * t:(gq + 1) * t, :]
    att = jnp.concatenate(outs, axis=2)
    o_ref[...] = _rms(att, g_ref[...]).astype(BF16)


def _attn_sample(q3, k3, v3, ck, cv, sinks, bias, band, g_attn, gb):
    nb, t, attn_w = q3.shape
    w, kv_w = ck.shape[1], ck.shape[2]
    bias = jnp.where(band > 0.5, bias, NEG_INF).reshape(N_KV_HEADS, GQA * t, w + t)
    blk3 = lambda last: pl.BlockSpec((gb, last[0], last[1]), lambda i: (i, 0, 0))
    return pl.pallas_call(
        _attn_sample_body,
        out_shape=(jax.ShapeDtypeStruct((nb, t, attn_w), BF16),
                   jax.ShapeDtypeStruct((nb, w, kv_w), F32),
                   jax.ShapeDtypeStruct((nb, w, kv_w), F32)),
        grid=(nb // gb,),
        in_specs=[pl.BlockSpec(memory_space=pltpu.SMEM),
                  blk3((t, attn_w)), blk3((t, kv_w)), blk3((t, kv_w)),
                  blk3((w, kv_w)), blk3((w, kv_w)),
                  _resident(bias.shape), _resident((1, 1, attn_w))],
        out_specs=(blk3((t, attn_w)), blk3((w, kv_w)), blk3((w, kv_w))),
        compiler_params=_cparams("arbitrary"),
        name="attention_sample",
    )(sinks, q3, k3, v3, ck, cv, bias, g_attn.reshape(1, 1, attn_w))


def _ssm_tail(y, u, d_ref, wglu_ref, bglu_ref, g_ref):
    z = jax.nn.gelu(y + d_ref[...] * u)
    gate = jax.nn.sigmoid(jnp.dot(z.astype(BF16), wglu_ref[...], preferred_element_type=F32) + bglu_ref[...])
    return _rms(z * gate, g_ref[...]).astype(BF16)


def _ssm_chunk_body(ut_ref, m_ref, e_ref, f_ref, w1_ref, w2_ref, yt_ref, hend_ref, s_scr):
    t, gb, ch, nc = ut_ref.shape
    n_levels = w1_ref.shape[1]
    two_p = e_ref.shape[1]
    row = lax.broadcasted_iota(jnp.int32, (nc, two_p), 0)
    rs = [ut_ref[:, gl].reshape(t * ch, nc) for gl in range(gb)]
    hs = []
    for gl in range(gb):
        s_scr[gl] = jnp.dot(e_ref[gl], rs[gl], preferred_element_type=F32)
        hs.append(s_scr[gl].T)

    def conv(gl):
        yt_ref[:, gl * ch:(gl + 1) * ch, :] = jnp.dot(m_ref[gl], rs[gl], preferred_element_type=F32).reshape(t, ch, nc)

    for lv in range(n_levels):
        sh = 1 << lv
        for gl in range(gb):
            prev = jnp.where(row >= sh, pltpu.roll(hs[gl], sh, axis=0), 0.0)
            hs[gl] = hs[gl] + w1_ref[gl, lv] * prev + w2_ref[gl, lv] * pltpu.roll(prev, two_p // 2, axis=1)
        for gl in range(lv * gb // n_levels, (lv + 1) * gb // n_levels):
            conv(gl)
    for gl in range(gb):
        hend_ref[gl] = hs[gl][nc - 1:nc, :]
        s_scr[gl] = jnp.where(row >= 1, pltpu.roll(hs[gl], 1, axis=0), 0.0).T
        yt_ref[:, gl * ch:(gl + 1) * ch, :] += jnp.dot(f_ref[gl], s_scr[gl].astype(BF16),
                                                       preferred_element_type=F32).reshape(t, ch, nc)


def _ssm_transpose_body(u_ref, o_ref):
    for s in range(u_ref.shape[0]):
        o_ref[s] = u_ref[s].T.astype(BF16).reshape(o_ref.shape[1:])


def _ssm_transpose(u2, ch):
    nb, t, nc, ssm_w = u2.shape
    tb = SSM_TAIL_BLOCK
    return pl.pallas_call(
        _ssm_transpose_body,
        out_shape=jax.ShapeDtypeStruct((nb, t, ssm_w // ch, ch, nc), BF16),
        grid=(nb, t // tb),
        in_specs=[pl.BlockSpec((None, tb, nc, ssm_w), lambda b, i: (b, i, 0, 0))],
        out_specs=pl.BlockSpec((None, tb, ssm_w // ch, ch, nc), lambda b, i: (b, i, 0, 0, 0)),
        compiler_params=_cparams("arbitrary", "arbitrary"),
        name="ssm_transpose",
    )(u2)


def _ssm_chunked(ut, kw):
    nb, t, ng, ch, nc = ut.shape
    tc = t * ch
    two_p = kw["e"].shape[1]
    n_levels = kw["w1"].shape[1]
    gb = SSM_GROUP_BLOCK
    grp = lambda shape: pl.BlockSpec((gb,) + shape, lambda b, g: (g,) + (0,) * len(shape))
    return pl.pallas_call(
        _ssm_chunk_body,
        out_shape=(jax.ShapeDtypeStruct((nb, t, ng * ch, nc), F32),
                   jax.ShapeDtypeStruct((nb, ng, 1, two_p), F32)),
        grid=(nb, ng // gb),
        in_specs=[pl.BlockSpec((None, t, gb, ch, nc), lambda b, g: (b, 0, g, 0, 0)),
                  grp((tc, tc)), grp((two_p, tc)), grp((tc, two_p)),
                  grp((n_levels, 1, two_p)), grp((n_levels, 1, two_p))],
        out_specs=(pl.BlockSpec((None, t, gb * ch, nc), lambda b, g: (b, 0, g, 0)),
                   pl.BlockSpec((None, gb, 1, two_p), lambda b, g: (b, g, 0, 0))),
        scratch_shapes=[pltpu.VMEM((gb, two_p, nc), F32)],
        compiler_params=_cparams("arbitrary", "arbitrary"),
        name="ssm_chunked",
    )(ut, kw["m"], kw["e"], kw["f"], kw["w1"], kw["w2"])


def _ssm_tail_body(yt_ref, u_ref, d_ref, wglu_ref, bglu_ref, g_ref, o_ref):
    for i in range(yt_ref.shape[0]):
        o_ref[i] = _ssm_tail(yt_ref[i].T, u_ref[i], d_ref, wglu_ref, bglu_ref, g_ref)


def _ssm_tail_call(yt, u2, sw):
    nb, t, nc, ssm_w = u2.shape
    tb = SSM_TAIL_BLOCK
    return pl.pallas_call(
        _ssm_tail_body,
        out_shape=jax.ShapeDtypeStruct((nb, t, nc, ssm_w), BF16),
        grid=(nb, t // tb),
        in_specs=[pl.BlockSpec((None, tb, ssm_w, nc), lambda b, i: (b, i, 0, 0)),
                  pl.BlockSpec((None, tb, nc, ssm_w), lambda b, i: (b, i, 0, 0)),
                  _resident((1, ssm_w)), _resident((ssm_w, ssm_w)), _resident((1, ssm_w)), _resident((1, ssm_w))],
        out_specs=pl.BlockSpec((None, tb, nc, ssm_w), lambda b, i: (b, i, 0, 0)),
        compiler_params=_cparams("arbitrary", "arbitrary"),
        name="ssm_tail",
    )(yt, u2, sw["d"], sw["wglu"], sw["bglu"], sw["g"])


def _lag_matrix_body(z_ref, m_ref, *, t):
    gb, ch, _ = z_ref.shape
    width = t * ch
    for gl in range(gb):
        z = z_ref[gl]
        for r in range(t):
            off = (t - 1 - r) * ch
            m_ref[gl, r * ch:(r + 1) * ch, :] = z[:, off:off + width].astype(BF16)


def _lag_matrix(z, t):
    ng, ch, zw = z.shape
    gb = SSM_GROUP_BLOCK
    return pl.pallas_call(
        functools.partial(_lag_matrix_body, t=t),
        out_shape=jax.ShapeDtypeStruct((ng, t * ch, zw // 2), BF16),
        grid=(ng // gb,),
        in_specs=[pl.BlockSpec((gb, ch, zw), lambda g: (g, 0, 0))],
        out_specs=pl.BlockSpec((gb, t * ch, zw // 2), lambda g: (g, 0, 0)),
        compiler_params=_cparams("arbitrary"),
        name="ssm_lag_matrix",
    )(z)


def _ssm_chunk_weights(lam_re, lam_im, log_dt, b_re, b_im, c_re, c_im, t, n_chunks):
    ng, ns = lam_re.shape
    nc = b_re.shape[2]
    dt = jnp.exp(log_dt)[:, None]
    are, aim = lam_re * dt, lam_im * dt
    d = jnp.arange(t + 1, dtype=F32)[:, None, None]
    mag = jnp.exp(d * are)
    pre, pim = mag * jnp.cos(d * aim), mag * jnp.sin(d * aim)
    lre, lim = pre[1], pim[1]
    den = lam_re * lam_re + lam_im * lam_im
    fre = ((lre - 1.0) * lam_re + lim * lam_im) / den
    fim = (lim * lam_re - (lre - 1.0) * lam_im) / den
    bbr = fre[..., None] * b_re - fim[..., None] * b_im
    bbi = fre[..., None] * b_im + fim[..., None] * b_re
    pre_g, pim_g = jnp.transpose(pre, (1, 0, 2)), jnp.transpose(pim, (1, 0, 2))
    xr = c_re[:, None] * pre_g[:, :, None, :] - c_im[:, None] * pim_g[:, :, None, :]
    xi = c_re[:, None] * pim_g[:, :, None, :] + c_im[:, None] * pre_g[:, :, None, :]
    kern = jnp.einsum('gdap,gpc->gdac', jnp.concatenate([xr[:, :t], -xi[:, :t]], axis=3),
                      jnp.concatenate([bbr, bbi], axis=1), precision=HIGHEST)
    lag_rows = jnp.transpose(kern[:, ::-1], (0, 2, 1, 3)).reshape(ng, nc, t * nc)
    m = _lag_matrix(jnp.concatenate([lag_rows, jnp.zeros_like(lag_rows)], axis=2), t)
    rev_re = jnp.transpose(pre_g[:, t - 1::-1][:, :t], (0, 2, 1))
    rev_im = jnp.transpose(pim_g[:, t - 1::-1][:, :t], (0, 2, 1))
    er = rev_re[..., None] * bbr[:, :, None, :] - rev_im[..., None] * bbi[:, :, None, :]
    ei = rev_re[..., None] * bbi[:, :, None, :] + rev_im[..., None] * bbr[:, :, None, :]
    e = jnp.concatenate([er.reshape(ng, ns, t * nc), ei.reshape(ng, ns, t * nc)], axis=1).astype(BF16)
    f = jnp.concatenate([xr[:, 1:].reshape(ng, t * nc, ns), -xi[:, 1:].reshape(ng, t * nc, ns)], axis=2).astype(BF16)
    wr, wi = pre[t], pim[t]
    w1, w2 = [], []
    for _ in range(max(1, (n_chunks - 1).bit_length())):
        w1.append(jnp.concatenate([wr, wr], axis=1))
        w2.append(jnp.concatenate([-wi, wi], axis=1))
        wr, wi = wr * wr - wi * wi, 2.0 * wr * wi
    w1 = jnp.stack(w1, axis=1)[:, :, None, :]
    w2 = jnp.stack(w2, axis=1)[:, :, None, :]
    return {"m": m, "e": e, "f": f, "w1": w1, "w2": w2}


def _ssm_sample_body(u_ref, h0re_ref, h0im_ref, bre_ref, bim_ref, cre_ref, cim_ref, lre_ref, lim_ref,
                     d_ref, wglu_ref, bglu_ref, g_ref, o_ref, hre_ref, him_ref):
    steps = u_ref.shape[0]
    half_in = bre_ref.shape[2]
    dot = functools.partial(jnp.dot, preferred_element_type=F32)
    ar = lre_ref[...]
    ai = lim_ref[...]
    hr = h0re_ref[...]
    hi = h0im_ref[...]
    for t in range(steps):
        u = u_ref[t]
        u_hi = u.astype(BF16)
        u_lo = (u - u_hi.astype(F32)).astype(BF16)

        def bu(b_ref):
            cols = []
            for hf in range(2):
                lanes = slice(hf * half_in, (hf + 1) * half_in)
                cols.append(dot(u_hi[:, lanes], b_ref[0, hf]) + dot(u_lo[:, lanes], b_ref[0, hf])
                            + dot(u_hi[:, lanes], b_ref[1, hf]))
            return jnp.concatenate(cols, axis=1)

        hr, hi = ar * hr - ai * hi + bu(bre_ref), ar * hi + ai * hr + bu(bim_ref)
        half_st = hr.shape[1] // 2
        hrb, hib = hr.astype(BF16), hi.astype(BF16)
        y = jnp.concatenate(
            [dot(hrb[:, hf * half_st:(hf + 1) * half_st], cre_ref[hf])
             + dot(hib[:, hf * half_st:(hf + 1) * half_st], cim_ref[hf]) for hf in range(2)], axis=1)
        o_ref[t] = _ssm_tail(y, u, d_ref, wglu_ref, bglu_ref, g_ref)
    hre_ref[...] = hr
    him_ref[...] = hi


def _ssm_sample(u_tm, h0re, h0im, sw):
    steps, nb, ssm_w = u_tm.shape
    n_state = h0re.shape[1]
    return pl.pallas_call(
        _ssm_sample_body,
        out_shape=(jax.ShapeDtypeStruct((steps, nb, ssm_w), BF16),
                   jax.ShapeDtypeStruct((nb, n_state), F32),
                   jax.ShapeDtypeStruct((nb, n_state), F32)),
        compiler_params=pltpu.CompilerParams(vmem_limit_bytes=VMEM_LIMIT_BYTES),
        name="ssm_sample",
    )(u_tm, h0re, h0im, _hi_lo(sw["bre"]), _hi_lo(sw["bim"]), sw["cre"].astype(BF16), sw["cim"].astype(BF16),
      sw["lre"], sw["lim"], sw["d"], sw["wglu"], sw["bglu"], sw["g"])


def _hi_lo(w):
    hi = w.astype(BF16)
    return jnp.stack([hi, (w - hi.astype(F32)).astype(BF16)])


def _ssm_weights(lam_re, lam_im, log_dt, b_re, b_im, c_re, c_im, d_skip, w_glu, b_glu, g_ssm):
    ng, ns = lam_re.shape
    nc = b_re.shape[2]
    dt = jnp.exp(log_dt)[:, None]
    mag = jnp.exp(lam_re * dt)
    lre = mag * jnp.cos(lam_im * dt)
    lim = mag * jnp.sin(lam_im * dt)
    den = lam_re * lam_re + lam_im * lam_im
    fre = ((lre - 1.0) * lam_re + lim * lam_im) / den
    fim = (lim * lam_re - (lre - 1.0) * lam_im) / den
    bbar_re = fre[..., None] * b_re - fim[..., None] * b_im
    bbar_im = fre[..., None] * b_im + fim[..., None] * b_re
    eye = jnp.eye(ng // 2, dtype=F32)

    def in_blocks(b):
        b2 = b.reshape(2, ng // 2, ns, nc)
        return jnp.einsum('hgpc,gk->hgckp', b2, eye).reshape(2, ng // 2 * nc, ng // 2 * ns)

    def out_blocks(c):
        c2 = c.reshape(2, ng // 2, nc, ns)
        return jnp.einsum('hgcp,gk->hgpkc', c2, eye).reshape(2, ng // 2 * ns, ng // 2 * nc)

    ssm_w = ng * nc
    return {
        "bre": in_blocks(bbar_re), "bim": in_blocks(bbar_im),
        "cre": out_blocks(c_re), "cim": out_blocks(-c_im),
        "lre": lre.reshape(1, ng * ns), "lim": lim.reshape(1, ng * ns),
        "d": d_skip.reshape(1, ssm_w), "wglu": w_glu.astype(BF16), "bglu": b_glu.reshape(1, ssm_w),
        "g": g_ssm.reshape(1, ssm_w),
    }


def _route(logits_t):
    big = jnp.int32(10 ** 6)
    gl = logits_t[:N_EXPERT_GROUPS]
    grow = lax.broadcasted_iota(jnp.int32, gl.shape, 0)
    gmax = jnp.max(gl, axis=0, keepdims=True)
    gidx = jnp.min(jnp.where(gl == gmax, grow, big), axis=0, keepdims=True)
    g_p = 1.0 / jnp.sum(jnp.exp(gl - gmax), axis=0, keepdims=True)
    ex = logits_t[N_EXPERT_GROUPS:N_EXPERT_GROUPS + N_EXPERTS]
    erow = lax.broadcasted_iota(jnp.int32, ex.shape, 0)
    lo = gidx * EXPERTS_PER_GROUP
    in_group = (erow >= lo) & (erow < lo + EXPERTS_PER_GROUP)
    el = jnp.where(in_group, ex, NEG_INF)
    ee = jnp.exp(el - jnp.max(el, axis=0, keepdims=True))
    prob = ee / jnp.sum(ee, axis=0, keepdims=True)
    p1 = jnp.max(jnp.where(in_group, prob, -1.0), axis=0, keepdims=True)
    i1 = jnp.min(jnp.where(in_group & (prob == p1), erow, big), axis=0, keepdims=True)
    rest = in_group & (erow != i1)
    p2 = jnp.max(jnp.where(rest, prob, -1.0), axis=0, keepdims=True)
    i2 = jnp.min(jnp.where(rest & (prob == p2), erow, big), axis=0, keepdims=True)
    tot = p1 + p2
    return jnp.where(erow == i1, g_p * p1 / tot, jnp.where(erow == i2, g_p * p2 / tot, 0.0)), gidx


def _split3(x):
    a = x.astype(BF16)
    r = x - a.astype(F32)
    b = r.astype(BF16)
    return a, b, (r - b.astype(F32)).astype(BF16)


def _moe_sorted(hb, gates_t, gidx, tri_ref, wg_ref, wu_ref, wd_ref, xs_scr, gs_scr, ys_scr):
    rows, d = hb.shape
    grow = lax.broadcasted_iota(jnp.int32, (SUBLANES, rows), 0)
    member = grow == gidx
    csum = jnp.dot(member.astype(BF16), tri_ref[...], preferred_element_type=F32)
    rank = jnp.sum(jnp.where(member, csum, 0.0), axis=0, keepdims=True)
    counts = [jnp.sum((gidx == grp).astype(jnp.int32)) for grp in range(N_EXPERT_GROUPS - 1)]
    offs = [jnp.int32(0)]
    for c in counts:
        offs.append(offs[-1] + c)
    offs.append(jnp.int32(rows))
    base = jnp.zeros_like(gidx)
    for grp in range(1, N_EXPERT_GROUPS):
        base = jnp.where(gidx == grp, offs[grp], base)
    pos_lanes = base.astype(F32) + rank - 1.0
    pos = jnp.broadcast_to(pos_lanes, (SUBLANES, rows)).T[:, 0:1]
    col = lax.broadcasted_iota(jnp.int32, (rows, rows), 1).astype(F32)
    row = lax.broadcasted_iota(jnp.int32, (rows, rows), 0).astype(F32)
    unsort = (col == pos).astype(BF16)
    sort = (row == pos_lanes).astype(BF16)
    xs_scr[...] = jnp.dot(sort, hb, preferred_element_type=F32).astype(BF16)
    pad = jnp.zeros((LANES - N_EXPERT_GROUPS - N_EXPERTS, rows), F32)
    gates_pad = jnp.concatenate([jnp.zeros((N_EXPERT_GROUPS, rows), F32), gates_t, pad], axis=0)
    gs_t = sum(jnp.dot(part, unsort, preferred_element_type=F32) for part in _split3(gates_pad))
    gs_scr[...] = gs_t.T
    ys_scr[...] = jnp.zeros_like(ys_scr)
    blk = MOE_ROW_BLOCK
    n_blk = rows // blk
    cuts = [jnp.int32(k * blk) for k in range(1, n_blk)] + offs[1:N_EXPERT_GROUPS]
    assert len(cuts) == 6
    for a, b in ((0, 5), (1, 3), (2, 4), (1, 2), (3, 4), (0, 3), (2, 5), (0, 1), (2, 3), (4, 5), (1, 2), (3, 4)):
        cuts[a], cuts[b] = jnp.minimum(cuts[a], cuts[b]), jnp.maximum(cuts[a], cuts[b])
    cuts = [jnp.int32(0)] + cuts + [jnp.int32(rows)]
    lane_b = lax.broadcasted_iota(jnp.int32, (blk, LANES), 1)
    for j in range(len(cuts) - 1):
        start = cuts[j]
        live = (cuts[j + 1] > start).astype(F32)
        k = jnp.minimum(start // blk, n_blk - 1)
        grp = sum((offs[q] <= start).astype(jnp.int32) for q in range(1, N_EXPERT_GROUPS))
        sl = pl.ds(pl.multiple_of(k * blk, blk), blk)
        xb = xs_scr[sl, :]
        gb = gs_scr[sl, :] * live
        acts = []
        for i in range(EXPERTS_PER_GROUP):
            e = grp * EXPERTS_PER_GROUP + i
            gate = jnp.sum(jnp.where(lane_b == N_EXPERT_GROUPS + e, gb, 0.0), axis=-1, keepdims=True)
            hg = jnp.dot(xb, wg_ref[e], preferred_element_type=F32)
            hu = jnp.dot(xb, wu_ref[e], preferred_element_type=F32)
            acts.append((jax.nn.silu(hg) * hu * gate).astype(BF16))
        ys_scr[sl, :] += jnp.dot(jnp.concatenate(acts, axis=1), wd_ref[grp], preferred_element_type=F32)
    hi = ys_scr[...].astype(BF16)
    lo = (ys_scr[...] - hi.astype(F32)).astype(BF16)
    return unsort, hi, lo


def _ffn_body(x_ref, ssm_ref, attn_ref, mod_ref, gpost_ref, gpre_ref, gffn_ref, wout_ref, wr_ref, br_ref,
              tri_ref, wg_ref, wu_ref, wd_ref, o_ref, xs_scr, gs_scr, ys_scr, *, chunked):
    g, r, d = x_ref.shape
    rows = g * r
    if chunked:
        ssm = jnp.swapaxes(ssm_ref[...].astype(F32), 0, 1).reshape(rows, ssm_ref.shape[-1]).astype(BF16)
    else:
        ssm = ssm_ref[...]
    sub = min(rows, FFN_SUB)
    x1s, hbs, hlos = [], [], []
    for c in range(rows // sub):
        rs = slice(c * sub, (c + 1) * sub)
        if g == 1:
            x = x_ref[:, rs, :]
            md = mod_ref
        else:
            gsl = slice(c * (sub // r), (c + 1) * (sub // r))
            x = x_ref[gsl]
            md = mod_ref.at[gsl]
        gg = x.shape[0]
        cat = jnp.concatenate([ssm[rs], attn_ref[rs, :]], axis=1)
        mixed = jnp.dot(cat, wout_ref[...], preferred_element_type=F32)
        x1 = x + md[:, 2] * _rms(mixed, gpost_ref[0]).reshape(gg, -1, d)
        hn = (_rms(x1, gpre_ref[...]) * (1.0 + md[:, 4]) + md[:, 3]).reshape(sub, d)
        hb = hn.astype(BF16)
        x1s.append(x1)
        hbs.append(hb)
        hlos.append((hn - hb.astype(F32)).astype(BF16))
    hb = jnp.concatenate(hbs, axis=0)
    hlo = jnp.concatenate(hlos, axis=0)
    nt = functools.partial(lax.dot_general, dimension_numbers=(((1,), (1,)), ((), ())), preferred_element_type=F32)
    logits_t = nt(wr_ref[0], hb) + nt(wr_ref[0], hlo) + nt(wr_ref[1], hb) + br_ref[...]
    gates_t, gidx = _route(logits_t)
    unsort, hi, lo = _moe_sorted(hb, gates_t, gidx, tri_ref, wg_ref, wu_ref, wd_ref, xs_scr, gs_scr, ys_scr)
    for c in range(rows // sub):
        rs = slice(c * sub, (c + 1) * sub)
        ffn = (jnp.dot(unsort[rs], hi, preferred_element_type=F32) + jnp.dot(unsort[rs], lo, preferred_element_type=F32))
        if g == 1:
            o_ref[:, rs, :] = x1s[c] + mod_ref[:, 5] * _rms(ffn, gffn_ref[0]).reshape(1, sub, d)
        else:
            gsl = slice(c * (sub // r), (c + 1) * (sub // r))
            o_ref[gsl] = x1s[c] + mod_ref[gsl, 5] * _rms(ffn, gffn_ref[0]).reshape(sub // r, r, d)


def _ffn(x3, ssm_n, attn_n, mod4, fw, gb, rb, chunk=None):
    nb, nr, d = x3.shape
    assert (gb == 1 or rb == nr) and nr % rb == 0 and nb % gb == 0
    half = ssm_n.shape[-1]
    nj = nr // rb
    tm = gb * rb
    row_map = lambda i, j: (i * nj + j, 0)
    vec = _resident((1, 1, d))
    if chunk is None:
        ssm_spec = pl.BlockSpec((tm, half), row_map)
    else:
        assert gb == 1 and rb % chunk == 0
        ssm_spec = pl.BlockSpec((None, chunk, rb // chunk, half), lambda i, j: (i, 0, j, 0))
    return pl.pallas_call(
        functools.partial(_ffn_body, chunked=chunk is not None),
        out_shape=jax.ShapeDtypeStruct((nb, nr, d), F32),
        grid=(nb // gb, nj),
        in_specs=[pl.BlockSpec((gb, rb, d), lambda i, j: (i, j, 0)),
                  ssm_spec,
                  pl.BlockSpec((tm, attn_n.shape[1]), row_map),
                  pl.BlockSpec((gb, N_MOD, 1, d), lambda i, j: (i, 0, 0, 0)),
                  vec, vec, vec,
                  _resident(fw["wout"].shape), _resident(fw["wr"].shape), _resident(fw["br"].shape),
                  _resident((tm, tm)),
                  _resident(fw["wg"].shape), _resident(fw["wu"].shape), _resident(fw["wd"].shape)],
        out_specs=pl.BlockSpec((gb, rb, d), lambda i, j: (i, j, 0)),
        scratch_shapes=[pltpu.VMEM((tm, d), BF16), pltpu.VMEM((tm, LANES), F32), pltpu.VMEM((tm, d), F32)],
        compiler_params=_cparams("arbitrary", "arbitrary"),
        name="outproj_moe",
    )(x3, ssm_n, attn_n, mod4, fw["gpost"], fw["gpre"], fw["gffn"], fw["wout"], fw["wr"], fw["br"],
      jnp.triu(jnp.ones((tm, tm), BF16)), fw["wg"], fw["wu"], fw["wd"])


def _ffn_weights(g_post_mix, g_pre_ffn, g_post_ffn, w_out, w_gr, b_gr, w_er, b_er, w_gate, w_up, w_down):
    d = w_out.shape[0]
    n_route = N_EXPERT_GROUPS + N_EXPERTS
    wr = jnp.concatenate([w_gr, w_er.reshape(d, N_EXPERTS)], axis=1)
    br = jnp.concatenate([b_gr, b_er.reshape(N_EXPERTS)])
    wr = jnp.pad(wr, ((0, 0), (0, ROUTE_ROWS - n_route))).T
    return {
        "gpost": g_post_mix.reshape(1, 1, d), "gpre": g_pre_ffn.reshape(1, 1, d), "gffn": g_post_ffn.reshape(1, 1, d),
        "wout": w_out.astype(BF16),
        "wr": _hi_lo(wr),
        "br": jnp.pad(br, (0, ROUTE_ROWS - n_route)).reshape(ROUTE_ROWS, 1),
        "wg": w_gate.astype(BF16), "wu": w_up.astype(BF16),
        "wd": w_down.astype(BF16).reshape(N_EXPERT_GROUPS, -1, d),
    }


def _rel_bucket(dist):
    max_exact = NUM_BUCKETS // 2
    dd = jnp.maximum(dist, 0)
    log_ratio = jnp.log(jnp.maximum(dd, 1).astype(F32) / max_exact) / math.log(MAX_DISTANCE / max_exact)
    large = jnp.minimum(max_exact + (log_ratio * (NUM_BUCKETS - max_exact)).astype(jnp.int32), NUM_BUCKETS - 1)
    return jnp.where(dd < max_exact, dd, large)


def _bias_and_band(dist, table):
    onehot = (_rel_bucket(dist)[:, :, None] == jnp.arange(NUM_BUCKETS)[None, None, :]).astype(F32)
    bias = jnp.einsum('qkb,bh->hqk', onehot, table.astype(F32), precision=HIGHEST)
    band = ((dist >= 0) & (dist <= WINDOW)).astype(F32)
    return bias, band


def _layer(xp, xs, ck, cv, h0re, h0im, mod_p, mod_s, table, p):
    nb, seq, d = xp.shape
    nbs, t, _ = xs.shape
    w = ck.shape[1]
    kv_w = N_KV_HEADS * HEAD_DIM
    w_in_bf = p["w_in"].astype(BF16)
    sw = _ssm_weights(p["ssm_lam_re"], p["ssm_lam_im"], p["ssm_log_dt"], p["ssm_b_re"], p["ssm_b_im"],
                      p["ssm_c_re"], p["ssm_c_im"], p["ssm_d"], p["w_glu"], p["b_glu"], p["g_ssm_out"])
    fw = _ffn_weights(p["g_post_mix"], p["g_pre_ffn"], p["g_post_ffn"], p["w_out"],
                      p["w_group_router"], p["b_group_router"], p["w_expert_router"], p["b_expert_router"],
                      p["w_exp_gate"], p["w_exp_up"], p["w_exp_down"])
    n_state = sw["lre"].shape[1]
    ng = p["ssm_lam_re"].shape[0]

    u2, q, k, v = _inproj(xp, mod_p, p["g_pre_mix"], w_in_bf, 1, INPROJ_TILE, chunk=SSM_CHUNK)
    blk = WINDOW
    dist_p = jnp.arange(blk)[:, None] + blk - jnp.arange(2 * blk)[None, :]
    bias_p, band_p = _bias_and_band(dist_p, table)
    attn_p = _attn_prompt(q, k, v, p["attn_sinks"], bias_p, band_p, p["g_attn_out"], nb, seq)
    kw = _ssm_chunk_weights(p["ssm_lam_re"], p["ssm_lam_im"], p["ssm_log_dt"], p["ssm_b_re"], p["ssm_b_im"],
                            p["ssm_c_re"], p["ssm_c_im"], SSM_CHUNK, seq // SSM_CHUNK)
    yt, hend = _ssm_chunked(_ssm_transpose(u2, p["ssm_b_re"].shape[2]), kw)
    ssm_p = _ssm_tail_call(yt, u2, sw)
    yp = _ffn(xp, ssm_p, attn_p, mod_p, fw, 1, TOKEN_TILE, chunk=SSM_CHUNK)
    wp = min(WINDOW, seq)
    k_p = k.reshape(nb, seq, kv_w)[:, seq - wp:].reshape(nb, wp, N_KV_HEADS, HEAD_DIM)
    v_p = v.reshape(nb, seq, kv_w)[:, seq - wp:].reshape(nb, wp, N_KV_HEADS, HEAD_DIM)
    n_p = n_state // ng
    hre_p = hend[:, :, 0, :n_p]
    him_p = hend[:, :, 0, n_p:]

    gs = TOKEN_TILE // t
    us, qs, ks, vs = _inproj(xs, mod_s, p["g_pre_mix"], w_in_bf, gs, t)
    dist_s = jnp.arange(t)[:, None] + w - jnp.arange(w + t)[None, :]
    bias_s, band_s = _bias_and_band(dist_s, table)
    attn_s, k_s, v_s = _attn_sample(qs.reshape(nbs, t, -1), ks.reshape(nbs, t, kv_w), vs.reshape(nbs, t, kv_w),
                                    ck.reshape(nbs, w, kv_w), cv.reshape(nbs, w, kv_w),
                                    p["attn_sinks"], bias_s, band_s, p["g_attn_out"], ATTN_SAMPLE_BLOCK)
    u_tm = jnp.swapaxes(us.reshape(nbs, t, -1), 0, 1)
    ssm_tm, hre_s, him_s = _ssm_sample(u_tm, h0re.reshape(nbs, n_state), h0im.reshape(nbs, n_state), sw)
    ssm_s = jnp.swapaxes(ssm_tm, 0, 1).reshape(nbs * t, -1)
    ys = _ffn(xs, ssm_s, attn_s.reshape(nbs * t, -1), mod_s, fw, gs, t)
    k_s = k_s.reshape(nbs, w, N_KV_HEADS, HEAD_DIM)
    v_s = v_s.reshape(nbs, w, N_KV_HEADS, HEAD_DIM)
    hre_s = hre_s.reshape(nbs, ng, n_state // ng)
    him_s = him_s.reshape(nbs, ng, n_state // ng)
    return yp, ys, k_p, v_p, hre_p, him_p, k_s, v_s, hre_s, him_s


def kernel(x_prompt, x_sample, cache_k, cache_v, state_ssm_re, state_ssm_im, c_prompt, c_sample, rel_bias_table,
           w_ada, b_ada, g_pre_mix, g_post_mix, g_pre_ffn, g_post_ffn, w_in, ssm_lam_re, ssm_lam_im, ssm_log_dt,
           ssm_b_re, ssm_b_im, ssm_c_re, ssm_c_im, ssm_d, w_glu, b_glu, attn_sinks, g_ssm_out, g_attn_out, w_out,
           w_group_router, b_group_router, w_expert_router, b_expert_router, w_exp_gate, w_exp_up, w_exp_down):
    params = dict(
        w_ada=w_ada, b_ada=b_ada, g_pre_mix=g_pre_mix, g_post_mix=g_post_mix, g_pre_ffn=g_pre_ffn,
        g_post_ffn=g_post_ffn, w_in=w_in, ssm_lam_re=ssm_lam_re, ssm_lam_im=ssm_lam_im, ssm_log_dt=ssm_log_dt,
        ssm_b_re=ssm_b_re, ssm_b_im=ssm_b_im, ssm_c_re=ssm_c_re, ssm_c_im=ssm_c_im, ssm_d=ssm_d, w_glu=w_glu,
        b_glu=b_glu, attn_sinks=attn_sinks, g_ssm_out=g_ssm_out, g_attn_out=g_attn_out, w_out=w_out,
        w_group_router=w_group_router, b_group_router=b_group_router, w_expert_router=w_expert_router,
        b_expert_router=b_expert_router, w_exp_gate=w_exp_gate, w_exp_up=w_exp_up, w_exp_down=w_exp_down)
    depth = w_in.shape[0]
    nb, nbs = x_prompt.shape[0], x_sample.shape[0]
    d = x_prompt.shape[2]
    pad = (-nb) % 8
    yp, ys = x_prompt, x_sample
    outs = [[] for _ in range(8)]
    for li in range(depth):
        p = {name: val[li] for name, val in params.items()}
        c_all = jnp.concatenate([c_prompt, jnp.zeros((pad, d), F32), c_sample], axis=0)
        mod = _modulation(c_all, p["w_ada"], p["b_ada"])
        mod_p = mod[:nb].reshape(nb, N_MOD, 1, d)
        mod_s = mod[nb + pad:].reshape(nbs, N_MOD, 1, d)
        res = _layer(yp, ys, cache_k[li], cache_v[li], state_ssm_re[li], state_ssm_im[li], mod_p, mod_s,
                     rel_bias_table, p)
        yp, ys = res[0], res[1]
        for acc, val in zip(outs, res[2:]):
            acc.append(val)
    return (yp, ys) + tuple(jnp.stack(o) for o in outs)
```

```python
import functools
import math

import jax
import jax.numpy as jnp
from jax import lax
from jax.experimental import pallas as pl
from jax.experimental.pallas import tpu as pltpu

F32 = jnp.float32
BF16 = jnp.bfloat16
HIGHEST = lax.Precision.HIGHEST

EPS = 1e-6
NEG_INF = -1e30

HEAD_DIM = 64
N_KV_HEADS = 2
GQA = 4
N_HEADS = N_KV_HEADS * GQA
WINDOW = 128
NUM_BUCKETS = 32
MAX_DISTANCE = 128
N_EXPERT_GROUPS = 4
EXPERTS_PER_GROUP = 4
N_EXPERTS = N_EXPERT_GROUPS * EXPERTS_PER_GROUP
N_MOD = 6

LANES = 128
SUBLANES = 8
VMEM_LIMIT_BYTES = 56 * 1024 * 1024

TOKEN_TILE = 512
INPROJ_TILE = 2048
INPROJ_SUB = 256
FFN_SUB = 128
ATTN_TILE = 2048
ATTN_SAMPLE_BLOCK = 16
MOD_COL_TILE = 1024
SSM_CHUNK = 32
SSM_GROUP_BLOCK = 16
SSM_TAIL_BLOCK = 16
MOE_ROW_BLOCK = 128
ROUTE_ROWS = 32


def _cparams(*sem):
    return pltpu.CompilerParams(dimension_semantics=sem, vmem_limit_bytes=VMEM_LIMIT_BYTES)


def _resident(shape):
    zeros = (0,) * len(shape)
    return pl.BlockSpec(shape, lambda *_: zeros, pipeline_mode=pl.Buffered(1))


def _rms(x, g):
    return x * lax.rsqrt(jnp.mean(x * x, axis=-1, keepdims=True) + EPS) * g


def _mod_body(c_ref, w_ref, b_ref, o_ref):
    a = jax.nn.silu(c_ref[...])
    w = w_ref[...]
    a_hi, w_hi = a.astype(BF16), w.astype(BF16)
    a_lo, w_lo = (a - a_hi.astype(F32)).astype(BF16), (w - w_hi.astype(F32)).astype(BF16)
    dot = functools.partial(jnp.dot, preferred_element_type=F32)
    o_ref[...] = dot(a_hi, w_hi) + dot(a_lo, w_hi) + dot(a_hi, w_lo) + b_ref[...]


def _modulation(c, w_ada, b_ada):
    rows, d = c.shape
    n = w_ada.shape[1]
    tn = MOD_COL_TILE
    return pl.pallas_call(
        _mod_body,
        out_shape=jax.ShapeDtypeStruct((rows, n), F32),
        grid=(n // tn,),
        in_specs=[pl.BlockSpec((rows, d), lambda j: (0, 0)),
                  pl.BlockSpec((d, tn), lambda j: (0, j)),
                  pl.BlockSpec((1, tn), lambda j: (0, j))],
        out_specs=pl.BlockSpec((rows, tn), lambda j: (0, j)),
        compiler_params=_cparams("arbitrary"),
        name="modulation",
    )(c, w_ada, b_ada.reshape(1, n))


def _inproj_body(x_ref, mod_ref, g_ref, w_ref, u_ref, q_ref, k_ref, v_ref, *, chunked):
    g, r, d = x_ref.shape
    ssm_w = u_ref.shape[-1]
    attn_w = q_ref.shape[1]
    kv_w = k_ref.shape[1]
    rows = g * r
    sub = min(rows, INPROJ_SUB)
    hns = []
    for c in range(rows // sub):
        if g == 1:
            x = x_ref[:, c * sub:(c + 1) * sub, :]
            hn = _rms(x, g_ref[...]) * (1.0 + mod_ref[:, 1]) + mod_ref[:, 0]
        else:
            gs = sub // r
            x = x_ref[c * gs:(c + 1) * gs]
            hn = _rms(x, g_ref[...]) * (1.0 + mod_ref[c * gs:(c + 1) * gs, 1]) + mod_ref[c * gs:(c + 1) * gs, 0]
        hns.append(hn.reshape(sub, d).astype(BF16))
    for c, hn in enumerate(hns):
        rs = slice(c * sub, (c + 1) * sub)
        proj = jnp.dot(hn, w_ref[...], preferred_element_type=F32)
        if chunked:
            t = u_ref.shape[0]
            nc = sub // t
            u_ref[:, c * nc:(c + 1) * nc, :] = jnp.swapaxes(proj[:, :ssm_w].reshape(nc, t, ssm_w), 0, 1)
        else:
            u_ref[rs, :] = proj[:, :ssm_w]
        q_ref[rs, :] = (proj[:, ssm_w:ssm_w + attn_w] * (HEAD_DIM ** -0.5)).astype(BF16)
        k_ref[rs, :] = proj[:, ssm_w + attn_w:ssm_w + attn_w + kv_w]
        v_ref[rs, :] = proj[:, ssm_w + attn_w + kv_w:]


def _inproj(x3, mod4, g_pre, w_in_bf, gb, rb, chunk=None):
    nb, nr, d = x3.shape
    assert (gb == 1 or rb == nr) and nr % rb == 0 and nb % gb == 0
    n_in = w_in_bf.shape[1]
    kv_w = N_KV_HEADS * HEAD_DIM
    attn_w = N_HEADS * HEAD_DIM
    ssm_w = n_in - attn_w - 2 * kv_w
    rows = nb * nr
    nj = nr // rb
    tm = gb * rb
    out_map = lambda i, j: (i * nj + j, 0)
    if chunk is None:
        u_shape = jax.ShapeDtypeStruct((rows, ssm_w), F32)
        u_spec = pl.BlockSpec((tm, ssm_w), out_map)
    else:
        assert gb == 1 and rb % chunk == 0
        u_shape = jax.ShapeDtypeStruct((nb, chunk, nr // chunk, ssm_w), F32)
        u_spec = pl.BlockSpec((None, chunk, rb // chunk, ssm_w), lambda i, j: (i, 0, j, 0))
    return pl.pallas_call(
        functools.partial(_inproj_body, chunked=chunk is not None),
        out_shape=(u_shape,
                   jax.ShapeDtypeStruct((rows, attn_w), BF16),
                   jax.ShapeDtypeStruct((rows, kv_w), F32),
                   jax.ShapeDtypeStruct((rows, kv_w), F32)),
        grid=(nb // gb, nj),
        in_specs=[pl.BlockSpec((gb, rb, d), lambda i, j: (i, j, 0)),
                  pl.BlockSpec((gb, 2, 1, d), lambda i, j: (i, 0, 0, 0)),
                  _resident((1, 1, d)),
                  _resident((d, n_in))],
        out_specs=(u_spec,
                   pl.BlockSpec((tm, attn_w), out_map),
                   pl.BlockSpec((tm, kv_w), out_map),
                   pl.BlockSpec((tm, kv_w), out_map)),
        compiler_params=_cparams("arbitrary", "arbitrary"),
        name="in_projection",
    )(x3, mod4, g_pre.reshape(1, 1, d), w_in_bf)


def _attn_prompt_body(sinks_ref, q_ref, kc_ref, vc_ref, kp_ref, vp_ref, bias_ref, g_ref, o_ref, s_scr, p_scr):
    tq = q_ref.shape[0]
    blk = bias_ref.shape[2] // 2
    first_tile = pl.program_id(1) == 0
    kk = jnp.concatenate([kp_ref[...], kc_ref[...]], axis=0).astype(BF16)
    vv = jnp.concatenate([vp_ref[...], vc_ref[...]], axis=0).astype(BF16)
    shape = (GQA * blk, 2 * blk)
    col = lax.broadcasted_iota(jnp.int32, shape, 1)
    row = lax.broadcasted_iota(jnp.int32, (GQA * blk, 1), 0)
    no_prev = first_tile & (col < blk)
    sink_cols = []
    for hk in range(N_KV_HEADS):
        sk = jnp.full((GQA * blk, 1), sinks_ref[hk * GQA], F32)
        for gq in range(1, GQA):
            sk = jnp.where(row >= gq * blk, sinks_ref[hk * GQA + gq], sk)
        sink_cols.append(sk)
    pairs = [(j, hk) for j in range(tq // blk) for hk in range(N_KV_HEADS)]
    for idx, (j, hk) in enumerate(pairs):
        qg = jnp.concatenate([q_ref[j * blk:(j + 1) * blk, h * HEAD_DIM:(h + 1) * HEAD_DIM]
                              for h in range(hk * GQA, (hk + 1) * GQA)], axis=0)
        s = lax.dot_general(qg, kk[j * blk:(j + 2) * blk, hk * HEAD_DIM:(hk + 1) * HEAD_DIM],
                            (((1,), (1,)), ((), ())), preferred_element_type=F32) + bias_ref[hk]
        s_scr[idx] = jnp.where(no_prev, NEG_INF, s) if j == 0 else s
    dens = []
    for idx, (j, hk) in enumerate(pairs):
        s = s_scr[idx]
        m = jnp.maximum(jnp.max(s, axis=-1, keepdims=True), sink_cols[hk])
        p = jnp.exp(s - m)
        dens.append(jnp.sum(p, axis=-1, keepdims=True) + jnp.exp(sink_cols[hk] - m))
        p_scr[idx] = p.astype(BF16)
    outs = {}
    for idx, (j, hk) in enumerate(pairs):
        o = jnp.dot(p_scr[idx], vv[j * blk:(j + 2) * blk, hk * HEAD_DIM:(hk + 1) * HEAD_DIM],
                    preferred_element_type=F32) / dens[idx]
        for gq in range(GQA):
            outs[j, hk * GQA + gq] = o[gq * blk:(gq + 1) * blk]
    for j in range(tq // blk):
        att = jnp.concatenate([outs[j, h] for h in range(N_HEADS)], axis=1)
        o_ref[j * blk:(j + 1) * blk, :] = _rms(att, g_ref[...]).astype(BF16)


def _attn_prompt(q, k, v, sinks, bias, band, g_attn, nb, seq):
    rows, attn_w = q.shape
    kv_w = k.shape[1]
    blk = band.shape[0]
    bias = jnp.where(band > 0.5, bias, NEG_INF).reshape(N_KV_HEADS, GQA * blk, 2 * blk)
    tq = ATTN_TILE
    assert seq % tq == 0 and tq % blk == 0
    nt = seq // tq
    per = tq // blk
    cur = lambda b, i: (b * nt + i, 0)
    prev = lambda b, i: (b * nt * per + jnp.maximum(i * per - 1, 0), 0)
    return pl.pallas_call(
        _attn_prompt_body,
        out_shape=jax.ShapeDtypeStruct((rows, attn_w), BF16),
        grid=(nb, nt),
        in_specs=[pl.BlockSpec(memory_space=pltpu.SMEM),
                  pl.BlockSpec((tq, attn_w), cur),
                  pl.BlockSpec((tq, kv_w), cur),
                  pl.BlockSpec((tq, kv_w), cur),
                  pl.BlockSpec((blk, kv_w), prev),
                  pl.BlockSpec((blk, kv_w), prev),
                  _resident(bias.shape),
                  _resident((1, attn_w))],
        out_specs=pl.BlockSpec((tq, attn_w), cur),
        scratch_shapes=[pltpu.VMEM((per * N_KV_HEADS, GQA * blk, 2 * blk), F32),
                        pltpu.VMEM((per * N_KV_HEADS, GQA * blk, 2 * blk), BF16)],
        compiler_params=_cparams("arbitrary", "arbitrary"),
        name="attention_prompt",
    )(sinks, q, k, v, k, v, bias, g_attn.reshape(1, attn_w))


def _attn_sample_body(sinks_ref, q_ref, kn_ref, vn_ref, ck_ref, cv_ref, bias_ref, g_ref, o_ref, ko_ref, vo_ref):
    t = q_ref.shape[1]
    kk = jnp.concatenate([ck_ref[...], kn_ref[...]], axis=1)
    vv = jnp.concatenate([cv_ref[...], vn_ref[...]], axis=1)
    ko_ref[...] = kk[:, t:, :]
    vo_ref[...] = vv[:, t:, :]
    kkb = kk.astype(BF16)
    vvb = vv.astype(BF16)
    row = lax.broadcasted_iota(jnp.int32, (GQA * t, 1), 0)
    outs = [None] * N_HEADS
    for hk in range(N_KV_HEADS):
        sk = jnp.full((GQA * t, 1), sinks_ref[hk * GQA], F32)
        for gq in range(1, GQA):
            sk = jnp.where(row >= gq * t, sinks_ref[hk * GQA + gq], sk)
        qg = jnp.concatenate([q_ref[:, :, h * HEAD_DIM:(h + 1) * HEAD_DIM]
                              for h in range(hk * GQA, (hk + 1) * GQA)], axis=1)
        s = jnp.einsum('bqd,bkd->bqk', qg, kkb[:, :, hk * HEAD_DIM:(hk + 1) * HEAD_DIM],
                       preferred_element_type=F32) + bias_ref[hk]
        m = jnp.maximum(jnp.max(s, axis=-1, keepdims=True), sk)
        p = jnp.exp(s - m)
        den = jnp.sum(p, axis=-1, keepdims=True) + jnp.exp(sk - m)
        o = jnp.einsum('bqk,bkd->bqd', p.astype(BF16), vvb[:, :, hk * HEAD_DIM:(hk + 1) * HEAD_DIM],
                       preferred_element_type=F32) / den
        for gq in range(GQA):
            outs[hk * GQA + gq] = o[:, gq * t:(gq + 1) * t, :]
    att = jnp.concatenate(outs, axis=2)
    o_ref[...] = _rms(att, g_ref[...]).astype(BF16)


def _attn_sample(q3, k3, v3, ck, cv, sinks, bias, band, g_attn, gb):
    nb, t, attn_w = q3.shape
    w, kv_w = ck.shape[1], ck.shape[2]
    bias = jnp.where(band > 0.5, bias, NEG_INF).reshape(N_KV_HEADS, GQA * t, w + t)
    blk3 = lambda last: pl.BlockSpec((gb, last[0], last[1]), lambda i: (i, 0, 0))
    return pl.pallas_call(
        _attn_sample_body,
        out_shape=(jax.ShapeDtypeStruct((nb, t, attn_w), BF16),
                   jax.ShapeDtypeStruct((nb, w, kv_w), F32),
                   jax.ShapeDtypeStruct((nb, w, kv_w), F32)),
        grid=(nb // gb,),
        in_specs=[pl.BlockSpec(memory_space=pltpu.SMEM),
                  blk3((t, attn_w)), blk3((t, kv_w)), blk3((t, kv_w)),
                  blk3((w, kv_w)), blk3((w, kv_w)),
                  _resident(bias.shape), _resident((1, 1, attn_w))],
        out_specs=(blk3((t, attn_w)), blk3((w, kv_w)), blk3((w, kv_w))),
        compiler_params=_cparams("arbitrary"),
        name="attention_sample",
    )(sinks, q3, k3, v3, ck, cv, bias, g_attn.reshape(1, 1, attn_w))


def _ssm_tail(y, u, d_ref, wglu_ref, bglu_ref, g_ref):
    z = jax.nn.gelu(y + d_ref[...] * u)
    gate = jax.nn.sigmoid(jnp.dot(z.astype(BF16), wglu_ref[...], preferred_element_type=F32) + bglu_ref[...])
    return _rms(z * gate, g_ref[...]).astype(BF16)


def _ssm_chunk_body(ut_ref, m_ref, e_ref, f_ref, w1_ref, w2_ref, yt_ref, hend_ref, s_scr):
    t, gb, ch, nc = ut_ref.shape
    n_levels = w1_ref.shape[1]
    two_p = e_ref.shape[1]
    row = lax.broadcasted_iota(jnp.int32, (nc, two_p), 0)
    rs = [ut_ref[:, gl].reshape(t * ch, nc) for gl in range(gb)]
    hs = []
    for gl in range(gb):
        s_scr[gl] = jnp.dot(e_ref[gl], rs[gl], preferred_element_type=F32)
        hs.append(s_scr[gl].T)

    def conv(gl):
        yt_ref[:, gl * ch:(gl + 1) * ch, :] = jnp.dot(m_ref[gl], rs[gl], preferred_element_type=F32).reshape(t, ch, nc)

    for lv in range(n_levels):
        sh = 1 << lv
        for gl in range(gb):
            prev = jnp.where(row >= sh, pltpu.roll(hs[gl], sh, axis=0), 0.0)
            hs[gl] = hs[gl] + w1_ref[gl, lv] * prev + w2_ref[gl, lv] * pltpu.roll(prev, two_p // 2, axis=1)
        for gl in range(lv * gb // n_levels, (lv + 1) * gb // n_levels):
            conv(gl)
    for gl in range(gb):
        hend_ref[gl] = hs[gl][nc - 1:nc, :]
        s_scr[gl] = jnp.where(row >= 1, pltpu.roll(hs[gl], 1, axis=0), 0.0).T
        yt_ref[:, gl * ch:(gl + 1) * ch, :] += jnp.dot(f_ref[gl], s_scr[gl].astype(BF16),
                                                       preferred_element_type=F32).reshape(t, ch, nc)


def _ssm_transpose_body(u_ref, o_ref):
    for s in range(u_ref.shape[0]):
        o_ref[s] = u_ref[s].T.astype(BF16).reshape(o_ref.shape[1:])


def _ssm_transpose(u2, ch):
    nb, t, nc, ssm_w = u2.shape
    tb = SSM_TAIL_BLOCK
    return pl.pallas_call(
        _ssm_transpose_body,
        out_shape=jax.ShapeDtypeStruct((nb, t, ssm_w // ch, ch, nc), BF16),
        grid=(nb, t // tb),
        in_specs=[pl.BlockSpec((None, tb, nc, ssm_w), lambda b, i: (b, i, 0, 0))],
        out_specs=pl.BlockSpec((None, tb, ssm_w // ch, ch, nc), lambda b, i: (b, i, 0, 0, 0)),
        compiler_params=_cparams("arbitrary", "arbitrary"),
        name="ssm_transpose",
    )(u2)


def _ssm_chunked(ut, kw):
    nb, t, ng, ch, nc = ut.shape
    tc = t * ch
    two_p = kw["e"].shape[1]
    n_levels = kw["w1"].shape[1]
    gb = SSM_GROUP_BLOCK
    grp = lambda shape: pl.BlockSpec((gb,) + shape, lambda b, g: (g,) + (0,) * len(shape))
    return pl.pallas_call(
        _ssm_chunk_body,
        out_shape=(jax.ShapeDtypeStruct((nb, t, ng * ch, nc), F32),
                   jax.ShapeDtypeStruct((nb, ng, 1, two_p), F32)),
        grid=(nb, ng // gb),
        in_specs=[pl.BlockSpec((None, t, gb, ch, nc), lambda b, g: (b, 0, g, 0, 0)),
                  grp((tc, tc)), grp((two_p, tc)), grp((tc, two_p)),
                  grp((n_levels, 1, two_p)), grp((n_levels, 1, two_p))],
        out_specs=(pl.BlockSpec((None, t, gb * ch, nc), lambda b, g: (b, 0, g, 0)),
                   pl.BlockSpec((None, gb, 1, two_p), lambda b, g: (b, g, 0, 0))),
        scratch_shapes=[pltpu.VMEM((gb, two_p, nc), F32)],
        compiler_params=_cparams("arbitrary", "arbitrary"),
        name="ssm_chunked",
    )(ut, kw["m"], kw["e"], kw["f"], kw["w1"], kw["w2"])


def _ssm_tail_body(yt_ref, u_ref, d_ref, wglu_ref, bglu_ref, g_ref, o_ref):
    for i in range(yt_ref.shape[0]):
        o_ref[i] = _ssm_tail(yt_ref[i].T, u_ref[i], d_ref, wglu_ref, bglu_ref, g_ref)


def _ssm_tail_call(yt, u2, sw):
    nb, t, nc, ssm_w = u2.shape
    tb = SSM_TAIL_BLOCK
    return pl.pallas_call(
        _ssm_tail_body,
        out_shape=jax.ShapeDtypeStruct((nb, t, nc, ssm_w), BF16),
        grid=(nb, t // tb),
        in_specs=[pl.BlockSpec((None, tb, ssm_w, nc), lambda b, i: (b, i, 0, 0)),
                  pl.BlockSpec((None, tb, nc, ssm_w), lambda b, i: (b, i, 0, 0)),
                  _resident((1, ssm_w)), _resident((ssm_w, ssm_w)), _resident((1, ssm_w)), _resident((1, ssm_w))],
        out_specs=pl.BlockSpec((None, tb, nc, ssm_w), lambda b, i: (b, i, 0, 0)),
        compiler_params=_cparams("arbitrary", "arbitrary"),
        name="ssm_tail",
    )(yt, u2, sw["d"], sw["wglu"], sw["bglu"], sw["g"])


def _lag_matrix_body(z_ref, m_ref, *, t):
    gb, ch, _ = z_ref.shape
    width = t * ch
    for gl in range(gb):
        z = z_ref[gl]
        for r in range(t):
            off = (t - 1 - r) * ch
            m_ref[gl, r * ch:(r + 1) * ch, :] = z[:, off:off + width].astype(BF16)


def _lag_matrix(z, t):
    ng, ch, zw = z.shape
    gb = SSM_GROUP_BLOCK
    return pl.pallas_call(
        functools.partial(_lag_matrix_body, t=t),
        out_shape=jax.ShapeDtypeStruct((ng, t * ch, zw // 2), BF16),
        grid=(ng // gb,),
        in_specs=[pl.BlockSpec((gb, ch, zw), lambda g: (g, 0, 0))],
        out_specs=pl.BlockSpec((gb, t * ch, zw // 2), lambda g: (g, 0, 0)),
        compiler_params=_cparams("arbitrary"),
        name="ssm_lag_matrix",
    )(z)


def _ssm_chunk_weights(lam_re, lam_im, log_dt, b_re, b_im, c_re, c_im, t, n_chunks):
    ng, ns = lam_re.shape
    nc = b_re.shape[2]
    dt = jnp.exp(log_dt)[:, None]
    are, aim = lam_re * dt, lam_im * dt
    d = jnp.arange(t + 1, dtype=F32)[:, None, None]
    mag = jnp.exp(d * are)
    pre, pim = mag * jnp.cos(d * aim), mag * jnp.sin(d * aim)
    lre, lim = pre[1], pim[1]
    den = lam_re * lam_re + lam_im * lam_im
    fre = ((lre - 1.0) * lam_re + lim * lam_im) / den
    fim = (lim * lam_re - (lre - 1.0) * lam_im) / den
    bbr = fre[..., None] * b_re - fim[..., None] * b_im
    bbi = fre[..., None] * b_im + fim[..., None] * b_re
    pre_g, pim_g = jnp.transpose(pre, (1, 0, 2)), jnp.transpose(pim, (1, 0, 2))
    xr = c_re[:, None] * pre_g[:, :, None, :] - c_im[:, None] * pim_g[:, :, None, :]
    xi = c_re[:, None] * pim_g[:, :, None, :] + c_im[:, None] * pre_g[:, :, None, :]
    kern = jnp.einsum('gdap,gpc->gdac', jnp.concatenate([xr[:, :t], -xi[:, :t]], axis=3),
                      jnp.concatenate([bbr, bbi], axis=1), precision=HIGHEST)
    lag_rows = jnp.transpose(kern[:, ::-1], (0, 2, 1, 3)).reshape(ng, nc, t * nc)
    m = _lag_matrix(jnp.concatenate([lag_rows, jnp.zeros_like(lag_rows)], axis=2), t)
    rev_re = jnp.transpose(pre_g[:, t - 1::-1][:, :t], (0, 2, 1))
    rev_im = jnp.transpose(pim_g[:, t - 1::-1][:, :t], (0, 2, 1))
    er = rev_re[..., None] * bbr[:, :, None, :] - rev_im[..., None] * bbi[:, :, None, :]
    ei = rev_re[..., None] * bbi[:, :, None, :] + rev_im[..., None] * bbr[:, :, None, :]
    e = jnp.concatenate([er.reshape(ng, ns, t * nc), ei.reshape(ng, ns, t * nc)], axis=1).astype(BF16)
    f = jnp.concatenate([xr[:, 1:].reshape(ng, t * nc, ns), -xi[:, 1:].reshape(ng, t * nc, ns)], axis=2).astype(BF16)
    wr, wi = pre[t], pim[t]
    w1, w2 = [], []
    for _ in range(max(1, (n_chunks - 1).bit_length())):
        w1.append(jnp.concatenate([wr, wr], axis=1))
        w2.append(jnp.concatenate([-wi, wi], axis=1))
        wr, wi = wr * wr - wi * wi, 2.0 * wr * wi
    w1 = jnp.stack(w1, axis=1)[:, :, None, :]
    w2 = jnp.stack(w2, axis=1)[:, :, None, :]
    return {"m": m, "e": e, "f": f, "w1": w1, "w2": w2}


def _ssm_sample_body(u_ref, h0re_ref, h0im_ref, bre_ref, bim_ref, cre_ref, cim_ref, lre_ref, lim_ref,
                     d_ref, wglu_ref, bglu_ref, g_ref, o_ref, hre_ref, him_ref):
    steps = u_ref.shape[0]
    half_in = bre_ref.shape[2]
    dot = functools.partial(jnp.dot, preferred_element_type=F32)
    ar = lre_ref[...]
    ai = lim_ref[...]
    hr = h0re_ref[...]
    hi = h0im_ref[...]
    for t in range(steps):
        u = u_ref[t]
        u_hi = u.astype(BF16)
        u_lo = (u - u_hi.astype(F32)).astype(BF16)

        def bu(b_ref):
            cols = []
            for hf in range(2):
                lanes = slice(hf * half_in, (hf + 1) * half_in)
                cols.append(dot(u_hi[:, lanes], b_ref[0, hf]) + dot(u_lo[:, lanes], b_ref[0, hf])
                            + dot(u_hi[:, lanes], b_ref[1, hf]))
            return jnp.concatenate(cols, axis=1)

        hr, hi = ar * hr - ai * hi + bu(bre_ref), ar * hi + ai * hr + bu(bim_ref)
        half_st = hr.shape[1] // 2
        hrb, hib = hr.astype(BF16), hi.astype(BF16)
        y = jnp.concatenate(
            [dot(hrb[:, hf * half_st:(hf + 1) * half_st], cre_ref[hf])
             + dot(hib[:, hf * half_st:(hf + 1) * half_st], cim_ref[hf]) for hf in range(2)], axis=1)
        o_ref[t] = _ssm_tail(y, u, d_ref, wglu_ref, bglu_ref, g_ref)
    hre_ref[...] = hr
    him_ref[...] = hi


def _ssm_sample(u_tm, h0re, h0im, sw):
    steps, nb, ssm_w = u_tm.shape
    n_state = h0re.shape[1]
    return pl.pallas_call(
        _ssm_sample_body,
        out_shape=(jax.ShapeDtypeStruct((steps, nb, ssm_w), BF16),
                   jax.ShapeDtypeStruct((nb, n_state), F32),
                   jax.ShapeDtypeStruct((nb, n_state), F32)),
        compiler_params=pltpu.CompilerParams(vmem_limit_bytes=VMEM_LIMIT_BYTES),
        name="ssm_sample",
    )(u_tm, h0re, h0im, _hi_lo(sw["bre"]), _hi_lo(sw["bim"]), sw["cre"].astype(BF16), sw["cim"].astype(BF16),
      sw["lre"], sw["lim"], sw["d"], sw["wglu"], sw["bglu"], sw["g"])


def _hi_lo(w):
    hi = w.astype(BF16)
    return jnp.stack([hi, (w - hi.astype(F32)).astype(BF16)])


def _ssm_weights(lam_re, lam_im, log_dt, b_re, b_im, c_re, c_im, d_skip, w_glu, b_glu, g_ssm):
    ng, ns = lam_re.shape
    nc = b_re.shape[2]
    dt = jnp.exp(log_dt)[:, None]
    mag = jnp.exp(lam_re * dt)
    lre = mag * jnp.cos(lam_im * dt)
    lim = mag * jnp.sin(lam_im * dt)
    den = lam_re * lam_re + lam_im * lam_im
    fre = ((lre - 1.0) * lam_re + lim * lam_im) / den
    fim = (lim * lam_re - (lre - 1.0) * lam_im) / den
    bbar_re = fre[..., None] * b_re - fim[..., None] * b_im
    bbar_im = fre[..., None] * b_im + fim[..., None] * b_re
    eye = jnp.eye(ng // 2, dtype=F32)

    def in_blocks(b):
        b2 = b.reshape(2, ng // 2, ns, nc)
        return jnp.einsum('hgpc,gk->hgckp', b2, eye).reshape(2, ng // 2 * nc, ng // 2 * ns)

    def out_blocks(c):
        c2 = c.reshape(2, ng // 2, nc, ns)
        return jnp.einsum('hgcp,gk->hgpkc', c2, eye).reshape(2, ng // 2 * ns, ng // 2 * nc)

    ssm_w = ng * nc
    return {
        "bre": in_blocks(bbar_re), "bim": in_blocks(bbar_im),
        "cre": out_blocks(c_re), "cim": out_blocks(-c_im),
        "lre": lre.reshape(1, ng * ns), "lim": lim.reshape(1, ng * ns),
        "d": d_skip.reshape(1, ssm_w), "wglu": w_glu.astype(BF16), "bglu": b_glu.reshape(1, ssm_w),
        "g": g_ssm.reshape(1, ssm_w),
    }


def _route(logits_t):
    big = jnp.int32(10 ** 6)
    gl = logits_t[:N_EXPERT_GROUPS]
    grow = lax.broadcasted_iota(jnp.int32, gl.shape, 0)
    gmax = jnp.max(gl, axis=0, keepdims=True)
    gidx = jnp.min(jnp.where(gl == gmax, grow, big), axis=0, keepdims=True)
    g_p = 1.0 / jnp.sum(jnp.exp(gl - gmax), axis=0, keepdims=True)
    ex = logits_t[N_EXPERT_GROUPS:N_EXPERT_GROUPS + N_EXPERTS]
    erow = lax.broadcasted_iota(jnp.int32, ex.shape, 0)
    lo = gidx * EXPERTS_PER_GROUP
    in_group = (erow >= lo) & (erow < lo + EXPERTS_PER_GROUP)
    el = jnp.where(in_group, ex, NEG_INF)
    ee = jnp.exp(el - jnp.max(el, axis=0, keepdims=True))
    prob = ee / jnp.sum(ee, axis=0, keepdims=True)
    p1 = jnp.max(jnp.where(in_group, prob, -1.0), axis=0, keepdims=True)
    i1 = jnp.min(jnp.where(in_group & (prob == p1), erow, big), axis=0, keepdims=True)
    rest = in_group & (erow != i1)
    p2 = jnp.max(jnp.where(rest, prob, -1.0), axis=0, keepdims=True)
    i2 = jnp.min(jnp.where(rest & (prob == p2), erow, big), axis=0, keepdims=True)
    tot = p1 + p2
    return jnp.where(erow == i1, g_p * p1 / tot, jnp.where(erow == i2, g_p * p2 / tot, 0.0)), gidx


def _split3(x):
    a = x.astype(BF16)
    r = x - a.astype(F32)
    b = r.astype(BF16)
    return a, b, (r - b.astype(F32)).astype(BF16)


def _moe_sorted(hb, gates_t, gidx, tri_ref, wg_ref, wu_ref, wd_ref, xs_scr, gs_scr, ys_scr):
    rows, d = hb.shape
    grow = lax.broadcasted_iota(jnp.int32, (SUBLANES, rows), 0)
    member = grow == gidx
    csum = jnp.dot(member.astype(BF16), tri_ref[...], preferred_element_type=F32)
    rank = jnp.sum(jnp.where(member, csum, 0.0), axis=0, keepdims=True)
    counts = [jnp.sum((gidx == grp).astype(jnp.int32)) for grp in range(N_EXPERT_GROUPS - 1)]
    offs = [jnp.int32(0)]
    for c in counts:
        offs.append(offs[-1] + c)
    offs.append(jnp.int32(rows))
    base = jnp.zeros_like(gidx)
    for grp in range(1, N_EXPERT_GROUPS):
        base = jnp.where(gidx == grp, offs[grp], base)
    pos_lanes = base.astype(F32) + rank - 1.0
    pos = jnp.broadcast_to(pos_lanes, (SUBLANES, rows)).T[:, 0:1]
    col = lax.broadcasted_iota(jnp.int32, (rows, rows), 1).astype(F32)
    row = lax.broadcasted_iota(jnp.int32, (rows, rows), 0).astype(F32)
    unsort = (col == pos).astype(BF16)
    sort = (row == pos_lanes).astype(BF16)
    xs_scr[...] = jnp.dot(sort, hb, preferred_element_type=F32).astype(BF16)
    pad = jnp.zeros((LANES - N_EXPERT_GROUPS - N_EXPERTS, rows), F32)
    gates_pad = jnp.concatenate([jnp.zeros((N_EXPERT_GROUPS, rows), F32), gates_t, pad], axis=0)
    gs_t = sum(jnp.dot(part, unsort, preferred_element_type=F32) for part in _split3(gates_pad))
    gs_scr[...] = gs_t.T
    ys_scr[...] = jnp.zeros_like(ys_scr)
    blk = MOE_ROW_BLOCK
    n_blk = rows // blk
    cuts = [jnp.int32(k * blk) for k in range(1, n_blk)] + offs[1:N_EXPERT_GROUPS]
    assert len(cuts) == 6
    for a, b in ((0, 5), (1, 3), (2, 4), (1, 2), (3, 4), (0, 3), (2, 5), (0, 1), (2, 3), (4, 5), (1, 2), (3, 4)):
        cuts[a], cuts[b] = jnp.minimum(cuts[a], cuts[b]), jnp.maximum(cuts[a], cuts[b])
    cuts = [jnp.int32(0)] + cuts + [jnp.int32(rows)]
    lane_b = lax.broadcasted_iota(jnp.int32, (blk, LANES), 1)
    for j in range(len(cuts) - 1):
        start = cuts[j]
        live = (cuts[j + 1] > start).astype(F32)
        k = jnp.minimum(start // blk, n_blk - 1)
        grp = sum((offs[q] <= start).astype(jnp.int32) for q in range(1, N_EXPERT_GROUPS))
        sl = pl.ds(pl.multiple_of(k * blk, blk), blk)
        xb = xs_scr[sl, :]
        gb = gs_scr[sl, :] * live
        acts = []
        for i in range(EXPERTS_PER_GROUP):
            e = grp * EXPERTS_PER_GROUP + i
            gate = jnp.sum(jnp.where(lane_b == N_EXPERT_GROUPS + e, gb, 0.0), axis=-1, keepdims=True)
            hg = jnp.dot(xb, wg_ref[e], preferred_element_type=F32)
            hu = jnp.dot(xb, wu_ref[e], preferred_element_type=F32)
            acts.append((jax.nn.silu(hg) * hu * gate).astype(BF16))
        ys_scr[sl, :] += jnp.dot(jnp.concatenate(acts, axis=1), wd_ref[grp], preferred_element_type=F32)
    hi = ys_scr[...].astype(BF16)
    lo = (ys_scr[...] - hi.astype(F32)).astype(BF16)
    return unsort, hi, lo


def _ffn_body(x_ref, ssm_ref, attn_ref, mod_ref, gpost_ref, gpre_ref, gffn_ref, wout_ref, wr_ref, br_ref,
              tri_ref, wg_ref, wu_ref, wd_ref, o_ref, xs_scr, gs_scr, ys_scr, *, chunked):
    g, r, d = x_ref.shape
    rows = g * r
    if chunked:
        ssm = jnp.swapaxes(ssm_ref[...].astype(F32), 0, 1).reshape(rows, ssm_ref.shape[-1]).astype(BF16)
    else:
        ssm = ssm_ref[...]
    sub = min(rows, FFN_SUB)
    x1s, hbs, hlos = [], [], []
    for c in range(rows // sub):
        rs = slice(c * sub, (c + 1) * sub)
        if g == 1:
            x = x_ref[:, rs, :]
            md = mod_ref
        else:
            gsl = slice(c * (sub // r), (c + 1) * (sub // r))
            x = x_ref[gsl]
            md = mod_ref.at[gsl]
        gg = x.shape[0]
        cat = jnp.concatenate([ssm[rs], attn_ref[rs, :]], axis=1)
        mixed = jnp.dot(cat, wout_ref[...], preferred_element_type=F32)
        x1 = x + md[:, 2] * _rms(mixed, gpost_ref[0]).reshape(gg, -1, d)
        hn = (_rms(x1, gpre_ref[...]) * (1.0 + md[:, 4]) + md[:, 3]).reshape(sub, d)
        hb = hn.astype(BF16)
        x1s.append(x1)
        hbs.append(hb)
        hlos.append((hn - hb.astype(F32)).astype(BF16))
    hb = jnp.concatenate(hbs, axis=0)
    hlo = jnp.concatenate(hlos, axis=0)
    nt = functools.partial(lax.dot_general, dimension_numbers=(((1,), (1,)), ((), ())), preferred_element_type=F32)
    logits_t = nt(wr_ref[0], hb) + nt(wr_ref[0], hlo) + nt(wr_ref[1], hb) + br_ref[...]
    gates_t, gidx = _route(logits_t)
    unsort, hi, lo = _moe_sorted(hb, gates_t, gidx, tri_ref, wg_ref, wu_ref, wd_ref, xs_scr, gs_scr, ys_scr)
    for c in range(rows // sub):
        rs = slice(c * sub, (c + 1) * sub)
        ffn = (jnp.dot(unsort[rs], hi, preferred_element_type=F32) + jnp.dot(unsort[rs], lo, preferred_element_type=F32))
        if g == 1:
            o_ref[:, rs, :] = x1s[c] + mod_ref[:, 5] * _rms(ffn, gffn_ref[0]).reshape(1, sub, d)
        else:
            gsl = slice(c * (sub // r), (c + 1) * (sub // r))
            o_ref[gsl] = x1s[c] + mod_ref[gsl, 5] * _rms(ffn, gffn_ref[0]).reshape(sub // r, r, d)


def _ffn(x3, ssm_n, attn_n, mod4, fw, gb, rb, chunk=None):
    nb, nr, d = x3.shape
    assert (gb == 1 or rb == nr) and nr % rb == 0 and nb % gb == 0
    half = ssm_n.shape[-1]
    nj = nr // rb
    tm = gb * rb
    row_map = lambda i, j: (i * nj + j, 0)
    vec = _resident((1, 1, d))
    if chunk is None:
        ssm_spec = pl.BlockSpec((tm, half), row_map)
    else:
        assert gb == 1 and rb % chunk == 0
        ssm_spec = pl.BlockSpec((None, chunk, rb // chunk, half), lambda i, j: (i, 0, j, 0))
    return pl.pallas_call(
        functools.partial(_ffn_body, chunked=chunk is not None),
        out_shape=jax.ShapeDtypeStruct((nb, nr, d), F32),
        grid=(nb // gb, nj),
        in_specs=[pl.BlockSpec((gb, rb, d), lambda i, j: (i, j, 0)),
                  ssm_spec,
                  pl.BlockSpec((tm, attn_n.shape[1]), row_map),
                  pl.BlockSpec((gb, N_MOD, 1, d), lambda i, j: (i, 0, 0, 0)),
                  vec, vec, vec,
                  _resident(fw["wout"].shape), _resident(fw["wr"].shape), _resident(fw["br"].shape),
                  _resident((tm, tm)),
                  _resident(fw["wg"].shape), _resident(fw["wu"].shape), _resident(fw["wd"].shape)],
        out_specs=pl.BlockSpec((gb, rb, d), lambda i, j: (i, j, 0)),
        scratch_shapes=[pltpu.VMEM((tm, d), BF16), pltpu.VMEM((tm, LANES), F32), pltpu.VMEM((tm, d), F32)],
        compiler_params=_cparams("arbitrary", "arbitrary"),
        name="outproj_moe",
    )(x3, ssm_n, attn_n, mod4, fw["gpost"], fw["gpre"], fw["gffn"], fw["wout"], fw["wr"], fw["br"],
      jnp.triu(jnp.ones((tm, tm), BF16)), fw["wg"], fw["wu"], fw["wd"])


def _ffn_weights(g_post_mix, g_pre_ffn, g_post_ffn, w_out, w_gr, b_gr, w_er, b_er, w_gate, w_up, w_down):
    d = w_out.shape[0]
    n_route = N_EXPERT_GROUPS + N_EXPERTS
    wr = jnp.concatenate([w_gr, w_er.reshape(d, N_EXPERTS)], axis=1)
    br = jnp.concatenate([b_gr, b_er.reshape(N_EXPERTS)])
    wr = jnp.pad(wr, ((0, 0), (0, ROUTE_ROWS - n_route))).T
    return {
        "gpost": g_post_mix.reshape(1, 1, d), "gpre": g_pre_ffn.reshape(1, 1, d), "gffn": g_post_ffn.reshape(1, 1, d),
        "wout": w_out.astype(BF16),
        "wr": _hi_lo(wr),
        "br": jnp.pad(br, (0, ROUTE_ROWS - n_route)).reshape(ROUTE_ROWS, 1),
        "wg": w_gate.astype(BF16), "wu": w_up.astype(BF16),
        "wd": w_down.astype(BF16).reshape(N_EXPERT_GROUPS, -1, d),
    }


def _rel_bucket(dist):
    max_exact = NUM_BUCKETS // 2
    dd = jnp.maximum(dist, 0)
    log_ratio = jnp.log(jnp.maximum(dd, 1).astype(F32) / max_exact) / math.log(MAX_DISTANCE / max_exact)
    large = jnp.minimum(max_exact + (log_ratio * (NUM_BUCKETS - max_exact)).astype(jnp.int32), NUM_BUCKETS - 1)
    return jnp.where(dd < max_exact, dd, large)


def _bias_and_band(dist, table):
    onehot = (_rel_bucket(dist)[:, :, None] == jnp.arange(NUM_BUCKETS)[None, None, :]).astype(F32)
    bias = jnp.einsum('qkb,bh->hqk', onehot, table.astype(F32), precision=HIGHEST)
    band = ((dist >= 0) & (dist <= WINDOW)).astype(F32)
    return bias, band


def _layer(xp, xs, ck, cv, h0re, h0im, mod_p, mod_s, table, p):
    nb, seq, d = xp.shape
    nbs, t, _ = xs.shape
    w = ck.shape[1]
    kv_w = N_KV_HEADS * HEAD_DIM
    w_in_bf = p["w_in"].astype(BF16)
    sw = _ssm_weights(p["ssm_lam_re"], p["ssm_lam_im"], p["ssm_log_dt"], p["ssm_b_re"], p["ssm_b_im"],
                      p["ssm_c_re"], p["ssm_c_im"], p["ssm_d"], p["w_glu"], p["b_glu"], p["g_ssm_out"])
    fw = _ffn_weights(p["g_post_mix"], p["g_pre_ffn"], p["g_post_ffn"], p["w_out"],
                      p["w_group_router"], p["b_group_router"], p["w_expert_router"], p["b_expert_router"],
                      p["w_exp_gate"], p["w_exp_up"], p["w_exp_down"])
    n_state = sw["lre"].shape[1]
    ng = p["ssm_lam_re"].shape[0]

    u2, q, k, v = _inproj(xp, mod_p, p["g_pre_mix"], w_in_bf, 1, INPROJ_TILE, chunk=SSM_CHUNK)
    blk = WINDOW
    dist_p = jnp.arange(blk)[:, None] + blk - jnp.arange(2 * blk)[None, :]
    bias_p, band_p = _bias_and_band(dist_p, table)
    attn_p = _attn_prompt(q, k, v, p["attn_sinks"], bias_p, band_p, p["g_attn_out"], nb, seq)
    kw = _ssm_chunk_weights(p["ssm_lam_re"], p["ssm_lam_im"], p["ssm_log_dt"], p["ssm_b_re"], p["ssm_b_im"],
                            p["ssm_c_re"], p["ssm_c_im"], SSM_CHUNK, seq // SSM_CHUNK)
    yt, hend = _ssm_chunked(_ssm_transpose(u2, p["ssm_b_re"].shape[2]), kw)
    ssm_p = _ssm_tail_call(yt, u2, sw)
    yp = _ffn(xp, ssm_p, attn_p, mod_p, fw, 1, TOKEN_TILE, chunk=SSM_CHUNK)
    wp = min(WINDOW, seq)
    k_p = k.reshape(nb, seq, kv_w)[:, seq - wp:].reshape(nb, wp, N_KV_HEADS, HEAD_DIM)
    v_p = v.reshape(nb, seq, kv_w)[:, seq - wp:].reshape(nb, wp, N_KV_HEADS, HEAD_DIM)
    n_p = n_state // ng
    hre_p = hend[:, :, 0, :n_p]
    him_p = hend[:, :, 0, n_p:]

    gs = TOKEN_TILE // t
    us, qs, ks, vs = _inproj(xs, mod_s, p["g_pre_mix"], w_in_bf, gs, t)
    dist_s = jnp.arange(t)[:, None] + w - jnp.arange(w + t)[None, :]
    bias_s, band_s = _bias_and_band(dist_s, table)
    attn_s, k_s, v_s = _attn_sample(qs.reshape(nbs, t, -1), ks.reshape(nbs, t, kv_w), vs.reshape(nbs, t, kv_w),
                                    ck.reshape(nbs, w, kv_w), cv.reshape(nbs, w, kv_w),
                                    p["attn_sinks"], bias_s, band_s, p["g_attn_out"], ATTN_SAMPLE_BLOCK)
    u_tm = jnp.swapaxes(us.reshape(nbs, t, -1), 0, 1)
    ssm_tm, hre_s, him_s = _ssm_sample(u_tm, h0re.reshape(nbs, n_state), h0im.reshape(nbs, n_state), sw)
    ssm_s = jnp.swapaxes(ssm_tm, 0, 1).reshape(nbs * t, -1)
    ys = _ffn(xs, ssm_s, attn_s.reshape(nbs * t, -1), mod_s, fw, gs, t)
    k_s = k_s.reshape(nbs, w, N_KV_HEADS, HEAD_DIM)
    v_s = v_s.reshape(nbs, w, N_KV_HEADS, HEAD_DIM)
    hre_s = hre_s.reshape(nbs, ng, n_state // ng)
    him_s = him_s.reshape(nbs, ng, n_state // ng)
    return yp, ys, k_p, v_p, hre_p, him_p, k_s, v_s, hre_s, him_s


def kernel(x_prompt, x_sample, cache_k, cache_v, state_ssm_re, state_ssm_im, c_prompt, c_sample, rel_bias_table,
           w_ada, b_ada, g_pre_mix, g_post_mix, g_pre_ffn, g_post_ffn, w_in, ssm_lam_re, ssm_lam_im, ssm_log_dt,
           ssm_b_re, ssm_b_im, ssm_c_re, ssm_c_im, ssm_d, w_glu, b_glu, attn_sinks, g_ssm_out, g_attn_out, w_out,
           w_group_router, b_group_router, w_expert_router, b_expert_router, w_exp_gate, w_exp_up, w_exp_down):
    params = dict(
        w_ada=w_ada, b_ada=b_ada, g_pre_mix=g_pre_mix, g_post_mix=g_post_mix, g_pre_ffn=g_pre_ffn,
        g_post_ffn=g_post_ffn, w_in=w_in, ssm_lam_re=ssm_lam_re, ssm_lam_im=ssm_lam_im, ssm_log_dt=ssm_log_dt,
        ssm_b_re=ssm_b_re, ssm_b_im=ssm_b_im, ssm_c_re=ssm_c_re, ssm_c_im=ssm_c_im, ssm_d=ssm_d, w_glu=w_glu,
        b_glu=b_glu, attn_sinks=attn_sinks, g_ssm_out=g_ssm_out, g_attn_out=g_attn_out, w_out=w_out,
        w_group_router=w_group_router, b_group_router=b_group_router, w_expert_router=w_expert_router,
        b_expert_router=b_expert_router, w_exp_gate=w_exp_gate, w_exp_up=w_exp_up, w_exp_down=w_exp_down)
    depth = w_in.shape[0]
    nb, nbs = x_prompt.shape[0], x_sample.shape[0]
    d = x_prompt.shape[2]
    pad = (-nb) % 8
    yp, ys = x_prompt, x_sample
    outs = [[] for _ in range(8)]
    for li in range(depth):
        p = {name: val[li] for name, val in params.items()}
        c_all = jnp.concatenate([c_prompt, jnp.zeros((pad, d), F32), c_sample], axis=0)
        mod = _modulation(c_all, p["w_ada"], p["b_ada"])
        mod_p = mod[:nb].reshape(nb, N_MOD, 1, d)
        mod_s = mod[nb + pad:].reshape(nbs, N_MOD, 1, d)
        res = _layer(yp, ys, cache_k[li], cache_v[li], state_ssm_re[li], state_ssm_im[li], mod_p, mod_s,
                     rel_bias_table, p)
        yp, ys = res[0], res[1]
        for acc, val in zip(outs, res[2:]):
            acc.append(val)
    return (yp, ys) + tuple(jnp.stack(o) for o in outs)
```

```python
import functools
import math

import jax
import jax.numpy as jnp
from jax import lax
from jax.experimental import pallas as pl
from jax.experimental.pallas import tpu as pltpu

F32 = jnp.float32
BF16 = jnp.bfloat16
HIGHEST = lax.Precision.HIGHEST

EPS = 1e-6
NEG_INF = -1e30

HEAD_DIM = 64
N_KV_HEADS = 2
GQA = 4
N_HEADS = N_KV_HEADS * GQA
WINDOW = 128
NUM_BUCKETS = 32
MAX_DISTANCE = 128
N_EXPERT_GROUPS = 4
EXPERTS_PER_GROUP = 4
N_EXPERTS = N_EXPERT_GROUPS * EXPERTS_PER_GROUP
N_MOD = 6

LANES = 128
SUBLANES = 8
VMEM_LIMIT_BYTES = 56 * 1024 * 1024

TOKEN_TILE = 512
INPROJ_TILE = 2048
INPROJ_SUB = 256
FFN_SUB = 128
ATTN_TILE = 2048
ATTN_SAMPLE_BLOCK = 16
MOD_COL_TILE = 1024
SSM_CHUNK = 32
SSM_GROUP_BLOCK = 16
SSM_TAIL_BLOCK = 16
MOE_ROW_BLOCK = 128
ROUTE_ROWS = 32


def _cparams(*sem):
    return pltpu.CompilerParams(dimension_semantics=sem, vmem_limit_bytes=VMEM_LIMIT_BYTES)


def _resident(shape):
    zeros = (0,) * len(shape)
    return pl.BlockSpec(shape, lambda *_: zeros, pipeline_mode=pl.Buffered(1))


def _rms(x, g):
    return x * lax.rsqrt(jnp.mean(x * x, axis=-1, keepdims=True) + EPS) * g


def _mod_body(c_ref, w_ref, b_ref, o_ref):
    a = jax.nn.silu(c_ref[...])
    w = w_ref[...]
    a_hi, w_hi = a.astype(BF16), w.astype(BF16)
    a_lo, w_lo = (a - a_hi.astype(F32)).astype(BF16), (w - w_hi.astype(F32)).astype(BF16)
    dot = functools.partial(jnp.dot, preferred_element_type=F32)
    o_ref[...] = dot(a_hi, w_hi) + dot(a_lo, w_hi) + dot(a_hi, w_lo) + b_ref[...]


def _modulation(c, w_ada, b_ada):
    rows, d = c.shape
    n = w_ada.shape[1]
    tn = MOD_COL_TILE
    return pl.pallas_call(
        _mod_body,
        out_shape=jax.ShapeDtypeStruct((rows, n), F32),
        grid=(n // tn,),
        in_specs=[pl.BlockSpec((rows, d), lambda j: (0, 0)),
                  pl.BlockSpec((d, tn), lambda j: (0, j)),
                  pl.BlockSpec((1, tn), lambda j: (0, j))],
        out_specs=pl.BlockSpec((rows, tn), lambda j: (0, j)),
        compiler_params=_cparams("arbitrary"),
        name="modulation",
    )(c, w_ada, b_ada.reshape(1, n))


def _inproj_body(x_ref, mod_ref, g_ref, w_ref, u_ref, q_ref, k_ref, v_ref, *, chunked):
    g, r, d = x_ref.shape
    ssm_w = u_ref.shape[-1]
    attn_w = q_ref.shape[1]
    kv_w = k_ref.shape[1]
    rows = g * r
    sub = min(rows, INPROJ_SUB)
    hns = []
    for c in range(rows // sub):
        if g == 1:
            x = x_ref[:, c * sub:(c + 1) * sub, :]
            hn = _rms(x, g_ref[...]) * (1.0 + mod_ref[:, 1]) + mod_ref[:, 0]
        else:
            gs = sub // r
            x = x_ref[c * gs:(c + 1) * gs]
            hn = _rms(x, g_ref[...]) * (1.0 + mod_ref[c * gs:(c + 1) * gs, 1]) + mod_ref[c * gs:(c + 1) * gs, 0]
        hns.append(hn.reshape(sub, d).astype(BF16))
    for c, hn in enumerate(hns):
        rs = slice(c * sub, (c + 1) * sub)
        proj = jnp.dot(hn, w_ref[...], preferred_element_type=F32)
        if chunked:
            t = u_ref.shape[0]
            nc = sub // t
            u_ref[:, c * nc:(c + 1) * nc, :] = jnp.swapaxes(proj[:, :ssm_w].reshape(nc, t, ssm_w), 0, 1)
        else:
            u_ref[rs, :] = proj[:, :ssm_w]
        q_ref[rs, :] = (proj[:, ssm_w:ssm_w + attn_w] * (HEAD_DIM ** -0.5)).astype(BF16)
        k_ref[rs, :] = proj[:, ssm_w + attn_w:ssm_w + attn_w + kv_w]
        v_ref[rs, :] = proj[:, ssm_w + attn_w + kv_w:]


def _inproj(x3, mod4, g_pre, w_in_bf, gb, rb, chunk=None):
    nb, nr, d = x3.shape
    assert (gb == 1 or rb == nr) and nr % rb == 0 and nb % gb == 0
    n_in = w_in_bf.shape[1]
    kv_w = N_KV_HEADS * HEAD_DIM
    attn_w = N_HEADS * HEAD_DIM
    ssm_w = n_in - attn_w - 2 * kv_w
    rows = nb * nr
    nj = nr // rb
    tm = gb * rb
    out_map = lambda i, j: (i * nj + j, 0)
    if chunk is None:
        u_shape = jax.ShapeDtypeStruct((rows, ssm_w), F32)
        u_spec = pl.BlockSpec((tm, ssm_w), out_map)
    else:
        assert gb == 1 and rb % chunk == 0
        u_shape = jax.ShapeDtypeStruct((nb, chunk, nr // chunk, ssm_w), F32)
        u_spec = pl.BlockSpec((None, chunk, rb // chunk, ssm_w), lambda i, j: (i, 0, j, 0))
    return pl.pallas_call(
        functools.partial(_inproj_body, chunked=chunk is not None),
        out_shape=(u_shape,
                   jax.ShapeDtypeStruct((rows, attn_w), BF16),
                   jax.ShapeDtypeStruct((rows, kv_w), F32),
                   jax.ShapeDtypeStruct((rows, kv_w), F32)),
        grid=(nb // gb, nj),
        in_specs=[pl.BlockSpec((gb, rb, d), lambda i, j: (i, j, 0)),
                  pl.BlockSpec((gb, 2, 1, d), lambda i, j: (i, 0, 0, 0)),
                  _resident((1, 1, d)),
                  _resident((d, n_in))],
        out_specs=(u_spec,
                   pl.BlockSpec((tm, attn_w), out_map),
                   pl.BlockSpec((tm, kv_w), out_map),
                   pl.BlockSpec((tm, kv_w), out_map)),
        compiler_params=_cparams("arbitrary", "arbitrary"),
        name="in_projection",
    )(x3, mod4, g_pre.reshape(1, 1, d), w_in_bf)


def _attn_prompt_body(sinks_ref, q_ref, kc_ref, vc_ref, kp_ref, vp_ref, bias_ref, g_ref, o_ref, s_scr, p_scr):
    tq = q_ref.shape[0]
    blk = bias_ref.shape[2] // 2
    first_tile = pl.program_id(1) == 0
    kk = jnp.concatenate([kp_ref[...], kc_ref[...]], axis=0).astype(BF16)
    vv = jnp.concatenate([vp_ref[...], vc_ref[...]], axis=0).astype(BF16)
    shape = (GQA * blk, 2 * blk)
    col = lax.broadcasted_iota(jnp.int32, shape, 1)
    row = lax.broadcasted_iota(jnp.int32, (GQA * blk, 1), 0)
    no_prev = first_tile & (col < blk)
    sink_cols = []
    for hk in range(N_KV_HEADS):
        sk = jnp.full((GQA * blk, 1), sinks_ref[hk * GQA], F32)
        for gq in range(1, GQA):
            sk = jnp.where(row >= gq * blk, sinks_ref[hk * GQA + gq], sk)
        sink_cols.append(sk)
    pairs = [(j, hk) for j in range(tq // blk) for hk in range(N_KV_HEADS)]
    for idx, (j, hk) in enumerate(pairs):
        qg = jnp.concatenate([q_ref[j * blk:(j + 1) * blk, h * HEAD_DIM:(h + 1) * HEAD_DIM]
                              for h in range(hk * GQA, (hk + 1) * GQA)], axis=0)
        s = lax.dot_general(qg, kk[j * blk:(j + 2) * blk, hk * HEAD_DIM:(hk + 1) * HEAD_DIM],
                            (((1,), (1,)), ((), ())), preferred_element_type=F32) + bias_ref[hk]
        s_scr[idx] = jnp.where(no_prev, NEG_INF, s) if j == 0 else s
    ms = [jnp.maximum(jnp.max(s_scr[idx], axis=-1, keepdims=True), sink_cols[hk]) for idx, (j, hk) in enumerate(pairs)]
    dens = []
    for idx, (j, hk) in enumerate(pairs):
        p = jnp.exp(s_scr[idx] - ms[idx])
        dens.append(jnp.sum(p, axis=-1, keepdims=True) + jnp.exp(sink_cols[hk] - ms[idx]))
        p_scr[idx] = p.astype(BF16)
    outs = {}
    for idx, (j, hk) in enumerate(pairs):
        o = jnp.dot(p_scr[idx], vv[j * blk:(j + 2) * blk, hk * HEAD_DIM:(hk + 1) * HEAD_DIM],
                    preferred_element_type=F32) / dens[idx]
        for gq in range(GQA):
            outs[j, hk * GQA + gq] = o[gq * blk:(gq + 1) * blk]
    for j in range(tq // blk):
        att = jnp.concatenate([outs[j, h] for h in range(N_HEADS)], axis=1)
        o_ref[j * blk:(j + 1) * blk, :] = _rms(att, g_ref[...]).astype(BF16)


def _attn_prompt(q, k, v, sinks, bias, band, g_attn, nb, seq):
    rows, attn_w = q.shape
    kv_w = k.shape[1]
    blk = band.shape[0]
    bias = jnp.where(band > 0.5, bias, NEG_INF).reshape(N_KV_HEADS, GQA * blk, 2 * blk)
    tq = ATTN_TILE
    assert seq % tq == 0 and tq % blk == 0
    nt = seq // tq
    per = tq // blk
    cur = lambda b, i: (b * nt + i, 0)
    prev = lambda b, i: (b * nt * per + jnp.maximum(i * per - 1, 0), 0)
    return pl.pallas_call(
        _attn_prompt_body,
        out_shape=jax.ShapeDtypeStruct((rows, attn_w), BF16),
        grid=(nb, nt),
        in_specs=[pl.BlockSpec(memory_space=pltpu.SMEM),
                  pl.BlockSpec((tq, attn_w), cur),
                  pl.BlockSpec((tq, kv_w), cur),
                  pl.BlockSpec((tq, kv_w), cur),
                  pl.BlockSpec((blk, kv_w), prev),
                  pl.BlockSpec((blk, kv_w), prev),
                  _resident(bias.shape),
                  _resident((1, attn_w))],
        out_specs=pl.BlockSpec((tq, attn_w), cur),
        scratch_shapes=[pltpu.VMEM((per * N_KV_HEADS, GQA * blk, 2 * blk), F32),
                        pltpu.VMEM((per * N_KV_HEADS, GQA * blk, 2 * blk), BF16)],
        compiler_params=_cparams("arbitrary", "arbitrary"),
        name="attention_prompt",
    )(sinks, q, k, v, k, v, bias, g_attn.reshape(1, attn_w))


def _attn_sample_body(sinks_ref, q_ref, kn_ref, vn_ref, ck_ref, cv_ref, bias_ref, g_ref, o_ref, ko_ref, vo_ref):
    t = q_ref.shape[1]
    kk = jnp.concatenate([ck_ref[...], kn_ref[...]], axis=1)
    vv = jnp.concatenate([cv_ref[...], vn_ref[...]], axis=1)
    ko_ref[...] = kk[:, t:, :]
    vo_ref[...] = vv[:, t:, :]
    kkb = kk.astype(BF16)
    vvb = vv.astype(BF16)
    row = lax.broadcasted_iota(jnp.int32, (GQA * t, 1), 0)
    outs = [None] * N_HEADS
    for hk in range(N_KV_HEADS):
        sk = jnp.full((GQA * t, 1), sinks_ref[hk * GQA], F32)
        for gq in range(1, GQA):
            sk = jnp.where(row >= gq * t, sinks_ref[hk * GQA + gq], sk)
        qg = jnp.concatenate([q_ref[:, :, h * HEAD_DIM:(h + 1) * HEAD_DIM]
                              for h in range(hk * GQA, (hk + 1) * GQA)], axis=1)
        s = jnp.einsum('bqd,bkd->bqk', qg, kkb[:, :, hk * HEAD_DIM:(hk + 1) * HEAD_DIM],
                       preferred_element_type=F32) + bias_ref[hk]
        m = jnp.maximum(jnp.max(s, axis=-1, keepdims=True), sk)
        p = jnp.exp(s - m)
        den = jnp.sum(p, axis=-1, keepdims=True) + jnp.exp(sk - m)
        o = jnp.einsum('bqk,bkd->bqd', p.astype(BF16), vvb[:, :, hk * HEAD_DIM:(hk + 1) * HEAD_DIM],
                       preferred_element_type=F32) / den
        for gq in range(GQA):
            outs[hk * GQA + gq] = o[:, gq * t:(gq + 1) * t, :]
    att = jnp.concatenate(outs, axis=2)
    o_ref[...] = _rms(att, g_ref[...]).astype(BF16)


def _attn_sample(q3, k3, v3, ck, cv, sinks, bias, band, g_attn, gb):
    nb, t, attn_w = q3.shape
    w, kv_w = ck.shape[1], ck.shape[2]
    bias = jnp.where(band > 0.5, bias, NEG_INF).reshape(N_KV_HEADS, GQA * t, w + t)
    blk3 = lambda last: pl.BlockSpec((gb, last[0], last[1]), lambda i: (i, 0, 0))
    return pl.pallas_call(
        _attn_sample_body,
        out_shape=(jax.ShapeDtypeStruct((nb, t, attn_w), BF16),
                   jax.ShapeDtypeStruct((nb, w, kv_w), F32),
                   jax.ShapeDtypeStruct((nb, w, kv_w), F32)),
        grid=(nb // gb,),
        in_specs=[pl.BlockSpec(memory_space=pltpu.SMEM),
                  blk3((t, attn_w)), blk3((t, kv_w)), blk3((t, kv_w)),
                  blk3((w, kv_w)), blk3((w, kv_w)),
                  _resident(bias.shape), _resident((1, 1, attn_w))],
        out_specs=(blk3((t, attn_w)), blk3((w, kv_w)), blk3((w, kv_w))),
        compiler_params=_cparams("arbitrary"),
        name="attention_sample",
    )(sinks, q3, k3, v3, ck, cv, bias, g_attn.reshape(1, 1, attn_w))


def _ssm_tail(y, u, d_ref, wglu_ref, bglu_ref, g_ref):
    z = jax.nn.gelu(y + d_ref[...] * u)
    gate = jax.nn.sigmoid(jnp.dot(z.astype(BF16), wglu_ref[...], preferred_element_type=F32) + bglu_ref[...])
    return _rms(z * gate, g_ref[...]).astype(BF16)


def _ssm_chunk_body(ut_ref, m_ref, e_ref, f_ref, w1_ref, w2_ref, yt_ref, hend_ref, s_scr):
    t, gb, ch, nc = ut_ref.shape
    n_levels = w1_ref.shape[1]
    two_p = e_ref.shape[1]
    row = lax.broadcasted_iota(jnp.int32, (nc, two_p), 0)
    rs = [ut_ref[:, gl].reshape(t * ch, nc) for gl in range(gb)]
    hs = []
    for gl in range(gb):
        s_scr[gl] = jnp.dot(e_ref[gl], rs[gl], preferred_element_type=F32)
        hs.append(s_scr[gl].T)

    def conv(gl):
        yt_ref[:, gl * ch:(gl + 1) * ch, :] = jnp.dot(m_ref[gl], rs[gl], preferred_element_type=F32).reshape(t, ch, nc)

    for lv in range(n_levels):
        sh = 1 << lv
        for gl in range(gb):
            prev = jnp.where(row >= sh, pltpu.roll(hs[gl], sh, axis=0), 0.0)
            hs[gl] = hs[gl] + w1_ref[gl, lv] * prev + w2_ref[gl, lv] * pltpu.roll(prev, two_p // 2, axis=1)
        for gl in range(lv * gb // n_levels, (lv + 1) * gb // n_levels):
            conv(gl)
    for gl in range(gb):
        hend_ref[gl] = hs[gl][nc - 1:nc, :]
        s_scr[gl] = jnp.where(row >= 1, pltpu.roll(hs[gl], 1, axis=0), 0.0).T
        yt_ref[:, gl * ch:(gl + 1) * ch, :] += jnp.dot(f_ref[gl], s_scr[gl].astype(BF16),
                                                       preferred_element_type=F32).reshape(t, ch, nc)


def _ssm_transpose_body(u_ref, o_ref):
    for s in range(u_ref.shape[0]):
        o_ref[s] = u_ref[s].T.astype(BF16).reshape(o_ref.shape[1:])


def _ssm_transpose(u2, ch):
    nb, t, nc, ssm_w = u2.shape
    tb = SSM_TAIL_BLOCK
    return pl.pallas_call(
        _ssm_transpose_body,
        out_shape=jax.ShapeDtypeStruct((nb, t, ssm_w // ch, ch, nc), BF16),
        grid=(nb, t // tb),
        in_specs=[pl.BlockSpec((None, tb, nc, ssm_w), lambda b, i: (b, i, 0, 0))],
        out_specs=pl.BlockSpec((None, tb, ssm_w // ch, ch, nc), lambda b, i: (b, i, 0, 0, 0)),
        compiler_params=_cparams("arbitrary", "arbitrary"),
        name="ssm_transpose",
    )(u2)


def _ssm_chunked(ut, kw):
    nb, t, ng, ch, nc = ut.shape
    tc = t * ch
    two_p = kw["e"].shape[1]
    n_levels = kw["w1"].shape[1]
    gb = SSM_GROUP_BLOCK
    grp = lambda shape: pl.BlockSpec((gb,) + shape, lambda b, g: (g,) + (0,) * len(shape))
    return pl.pallas_call(
        _ssm_chunk_body,
        out_shape=(jax.ShapeDtypeStruct((nb, t, ng * ch, nc), F32),
                   jax.ShapeDtypeStruct((nb, ng, 1, two_p), F32)),
        grid=(nb, ng // gb),
        in_specs=[pl.BlockSpec((None, t, gb, ch, nc), lambda b, g: (b, 0, g, 0, 0)),
                  grp((tc, tc)), grp((two_p, tc)), grp((tc, two_p)),
                  grp((n_levels, 1, two_p)), grp((n_levels, 1, two_p))],
        out_specs=(pl.BlockSpec((None, t, gb * ch, nc), lambda b, g: (b, 0, g, 0)),
                   pl.BlockSpec((None, gb, 1, two_p), lambda b, g: (b, g, 0, 0))),
        scratch_shapes=[pltpu.VMEM((gb, two_p, nc), F32)],
        compiler_params=_cparams("arbitrary", "arbitrary"),
        name="ssm_chunked",
    )(ut, kw["m"], kw["e"], kw["f"], kw["w1"], kw["w2"])


def _ssm_tail_body(yt_ref, u_ref, d_ref, wglu_ref, bglu_ref, g_ref, o_ref):
    for i in range(yt_ref.shape[0]):
        o_ref[i] = _ssm_tail(yt_ref[i].T, u_ref[i], d_ref, wglu_ref, bglu_ref, g_ref)


def _ssm_tail_call(yt, u2, sw):
    nb, t, nc, ssm_w = u2.shape
    tb = SSM_TAIL_BLOCK
    return pl.pallas_call(
        _ssm_tail_body,
        out_shape=jax.ShapeDtypeStruct((nb, t, nc, ssm_w), BF16),
        grid=(nb, t // tb),
        in_specs=[pl.BlockSpec((None, tb, ssm_w, nc), lambda b, i: (b, i, 0, 0)),
                  pl.BlockSpec((None, tb, nc, ssm_w), lambda b, i: (b, i, 0, 0)),
                  _resident((1, ssm_w)), _resident((ssm_w, ssm_w)), _resident((1, ssm_w)), _resident((1, ssm_w))],
        out_specs=pl.BlockSpec((None, tb, nc, ssm_w), lambda b, i: (b, i, 0, 0)),
        compiler_params=_cparams("arbitrary", "arbitrary"),
        name="ssm_tail",
    )(yt, u2, sw["d"], sw["wglu"], sw["bglu"], sw["g"])


def _lag_matrix_body(z_ref, m_ref, *, t):
    gb, ch, _ = z_ref.shape
    width = t * ch
    for gl in range(gb):
        z = z_ref[gl]
        for r in range(t):
            off = (t - 1 - r) * ch
            m_ref[gl, r * ch:(r + 1) * ch, :] = z[:, off:off + width].astype(BF16)


def _lag_matrix(z, t):
    ng, ch, zw = z.shape
    gb = SSM_GROUP_BLOCK
    return pl.pallas_call(
        functools.partial(_lag_matrix_body, t=t),
        out_shape=jax.ShapeDtypeStruct((ng, t * ch, zw // 2), BF16),
        grid=(ng // gb,),
        in_specs=[pl.BlockSpec((gb, ch, zw), lambda g: (g, 0, 0))],
        out_specs=pl.BlockSpec((gb, t * ch, zw // 2), lambda g: (g, 0, 0)),
        compiler_params=_cparams("arbitrary"),
        name="ssm_lag_matrix",
    )(z)


def _ssm_chunk_weights(lam_re, lam_im, log_dt, b_re, b_im, c_re, c_im, t, n_chunks):
    ng, ns = lam_re.shape
    nc = b_re.shape[2]
    dt = jnp.exp(log_dt)[:, None]
    are, aim = lam_re * dt, lam_im * dt
    d = jnp.arange(t + 1, dtype=F32)[:, None, None]
    mag = jnp.exp(d * are)
    pre, pim = mag * jnp.cos(d * aim), mag * jnp.sin(d * aim)
    lre, lim = pre[1], pim[1]
    den = lam_re * lam_re + lam_im * lam_im
    fre = ((lre - 1.0) * lam_re + lim * lam_im) / den
    fim = (lim * lam_re - (lre - 1.0) * lam_im) / den
    bbr = fre[..., None] * b_re - fim[..., None] * b_im
    bbi = fre[..., None] * b_im + fim[..., None] * b_re
    pre_g, pim_g = jnp.transpose(pre, (1, 0, 2)), jnp.transpose(pim, (1, 0, 2))
    xr = c_re[:, None] * pre_g[:, :, None, :] - c_im[:, None] * pim_g[:, :, None, :]
    xi = c_re[:, None] * pim_g[:, :, None, :] + c_im[:, None] * pre_g[:, :, None, :]
    kern = jnp.einsum('gdap,gpc->gdac', jnp.concatenate([xr[:, :t], -xi[:, :t]], axis=3),
                      jnp.concatenate([bbr, bbi], axis=1), precision=HIGHEST)
    lag_rows = jnp.transpose(kern[:, ::-1], (0, 2, 1, 3)).reshape(ng, nc, t * nc)
    m = _lag_matrix(jnp.concatenate([lag_rows, jnp.zeros_like(lag_rows)], axis=2), t)
    rev_re = jnp.transpose(pre_g[:, t - 1::-1][:, :t], (0, 2, 1))
    rev_im = jnp.transpose(pim_g[:, t - 1::-1][:, :t], (0, 2, 1))
    er = rev_re[..., None] * bbr[:, :, None, :] - rev_im[..., None] * bbi[:, :, None, :]
    ei = rev_re[..., None] * bbi[:, :, None, :] + rev_im[..., None] * bbr[:, :, None, :]
    e = jnp.concatenate([er.reshape(ng, ns, t * nc), ei.reshape(ng, ns, t * nc)], axis=1).astype(BF16)
    f = jnp.concatenate([xr[:, 1:].reshape(ng, t * nc, ns), -xi[:, 1:].reshape(ng, t * nc, ns)], axis=2).astype(BF16)
    wr, wi = pre[t], pim[t]
    w1, w2 = [], []
    for _ in range(max(1, (n_chunks - 1).bit_length())):
        w1.append(jnp.concatenate([wr, wr], axis=1))
        w2.append(jnp.concatenate([-wi, wi], axis=1))
        wr, wi = wr * wr - wi * wi, 2.0 * wr * wi
    w1 = jnp.stack(w1, axis=1)[:, :, None, :]
    w2 = jnp.stack(w2, axis=1)[:, :, None, :]
    return {"m": m, "e": e, "f": f, "w1": w1, "w2": w2}


def _ssm_sample_body(u_ref, h0re_ref, h0im_ref, bre_ref, bim_ref, cre_ref, cim_ref, lre_ref, lim_ref,
                     d_ref, wglu_ref, bglu_ref, g_ref, o_ref, hre_ref, him_ref):
    steps = u_ref.shape[0]
    half_in = bre_ref.shape[2]
    dot = functools.partial(jnp.dot, preferred_element_type=F32)
    ar = lre_ref[...]
    ai = lim_ref[...]
    hr = h0re_ref[...]
    hi = h0im_ref[...]
    for t in range(steps):
        u = u_ref[t]
        u_hi = u.astype(BF16)
        u_lo = (u - u_hi.astype(F32)).astype(BF16)

        def bu(b_ref):
            cols = []
            for hf in range(2):
                lanes = slice(hf * half_in, (hf + 1) * half_in)
                cols.append(dot(u_hi[:, lanes], b_ref[0, hf]) + dot(u_lo[:, lanes], b_ref[0, hf])
                            + dot(u_hi[:, lanes], b_ref[1, hf]))
            return jnp.concatenate(cols, axis=1)

        hr, hi = ar * hr - ai * hi + bu(bre_ref), ar * hi + ai * hr + bu(bim_ref)
        half_st = hr.shape[1] // 2
        hrb, hib = hr.astype(BF16), hi.astype(BF16)
        y = jnp.concatenate(
            [dot(hrb[:, hf * half_st:(hf + 1) * half_st], cre_ref[hf])
             + dot(hib[:, hf * half_st:(hf + 1) * half_st], cim_ref[hf]) for hf in range(2)], axis=1)
        o_ref[t] = _ssm_tail(y, u, d_ref, wglu_ref, bglu_ref, g_ref)
    hre_ref[...] = hr
    him_ref[...] = hi


def _ssm_sample(u_tm, h0re, h0im, sw):
    steps, nb, ssm_w = u_tm.shape
    n_state = h0re.shape[1]
    return pl.pallas_call(
        _ssm_sample_body,
        out_shape=(jax.ShapeDtypeStruct((steps, nb, ssm_w), BF16),
                   jax.ShapeDtypeStruct((nb, n_state), F32),
                   jax.ShapeDtypeStruct((nb, n_state), F32)),
        compiler_params=pltpu.CompilerParams(vmem_limit_bytes=VMEM_LIMIT_BYTES),
        name="ssm_sample",
    )(u_tm, h0re, h0im, _hi_lo(sw["bre"]), _hi_lo(sw["bim"]), sw["cre"].astype(BF16), sw["cim"].astype(BF16),
      sw["lre"], sw["lim"], sw["d"], sw["wglu"], sw["bglu"], sw["g"])


def _hi_lo(w):
    hi = w.astype(BF16)
    return jnp.stack([hi, (w - hi.astype(F32)).astype(BF16)])


def _ssm_weights(lam_re, lam_im, log_dt, b_re, b_im, c_re, c_im, d_skip, w_glu, b_glu, g_ssm):
    ng, ns = lam_re.shape
    nc = b_re.shape[2]
    dt = jnp.exp(log_dt)[:, None]
    mag = jnp.exp(lam_re * dt)
    lre = mag * jnp.cos(lam_im * dt)
    lim = mag * jnp.sin(lam_im * dt)
    den = lam_re * lam_re + lam_im * lam_im
    fre = ((lre - 1.0) * lam_re + lim * lam_im) / den
    fim = (lim * lam_re - (lre - 1.0) * lam_im) / den
    bbar_re = fre[..., None] * b_re - fim[..., None] * b_im
    bbar_im = fre[..., None] * b_im + fim[..., None] * b_re
    eye = jnp.eye(ng // 2, dtype=F32)

    def in_blocks(b):
        b2 = b.reshape(2, ng // 2, ns, nc)
        return jnp.einsum('hgpc,gk->hgckp', b2, eye).reshape(2, ng // 2 * nc, ng // 2 * ns)

    def out_blocks(c):
        c2 = c.reshape(2, ng // 2, nc, ns)
        return jnp.einsum('hgcp,gk->hgpkc', c2, eye).reshape(2, ng // 2 * ns, ng // 2 * nc)

    ssm_w = ng * nc
    return {
        "bre": in_blocks(bbar_re), "bim": in_blocks(bbar_im),
        "cre": out_blocks(c_re), "cim": out_blocks(-c_im),
        "lre": lre.reshape(1, ng * ns), "lim": lim.reshape(1, ng * ns),
        "d": d_skip.reshape(1, ssm_w), "wglu": w_glu.astype(BF16), "bglu": b_glu.reshape(1, ssm_w),
        "g": g_ssm.reshape(1, ssm_w),
    }


def _route(logits_t):
    big = jnp.int32(10 ** 6)
    gl = logits_t[:N_EXPERT_GROUPS]
    grow = lax.broadcasted_iota(jnp.int32, gl.shape, 0)
    gmax = jnp.max(gl, axis=0, keepdims=True)
    gidx = jnp.min(jnp.where(gl == gmax, grow, big), axis=0, keepdims=True)
    g_p = 1.0 / jnp.sum(jnp.exp(gl - gmax), axis=0, keepdims=True)
    ex = logits_t[N_EXPERT_GROUPS:N_EXPERT_GROUPS + N_EXPERTS]
    erow = lax.broadcasted_iota(jnp.int32, ex.shape, 0)
    lo = gidx * EXPERTS_PER_GROUP
    in_group = (erow >= lo) & (erow < lo + EXPERTS_PER_GROUP)
    el = jnp.where(in_group, ex, NEG_INF)
    ee = jnp.exp(el - jnp.max(el, axis=0, keepdims=True))
    prob = ee / jnp.sum(ee, axis=0, keepdims=True)
    p1 = jnp.max(jnp.where(in_group, prob, -1.0), axis=0, keepdims=True)
    i1 = jnp.min(jnp.where(in_group & (prob == p1), erow, big), axis=0, keepdims=True)
    rest = in_group & (erow != i1)
    p2 = jnp.max(jnp.where(rest, prob, -1.0), axis=0, keepdims=True)
    i2 = jnp.min(jnp.where(rest & (prob == p2), erow, big), axis=0, keepdims=True)
    tot = p1 + p2
    return jnp.where(erow == i1, g_p * p1 / tot, jnp.where(erow == i2, g_p * p2 / tot, 0.0)), gidx


def _split3(x):
    a = x.astype(BF16)
    r = x - a.astype(F32)
    b = r.astype(BF16)
    return a, b, (r - b.astype(F32)).astype(BF16)


def _moe_sorted(hb, gates_t, gidx, tri_ref, wg_ref, wu_ref, wd_ref, xs_scr, gs_scr, ys_scr):
    rows, d = hb.shape
    grow = lax.broadcasted_iota(jnp.int32, (SUBLANES, rows), 0)
    member = grow == gidx
    csum = jnp.dot(member.astype(BF16), tri_ref[...], preferred_element_type=F32)
    rank = jnp.sum(jnp.where(member, csum, 0.0), axis=0, keepdims=True)
    counts = [jnp.sum((gidx == grp).astype(jnp.int32)) for grp in range(N_EXPERT_GROUPS - 1)]
    offs = [jnp.int32(0)]
    for c in counts:
        offs.append(offs[-1] + c)
    offs.append(jnp.int32(rows))
    base = jnp.zeros_like(gidx)
    for grp in range(1, N_EXPERT_GROUPS):
        base = jnp.where(gidx == grp, offs[grp], base)
    pos_lanes = base.astype(F32) + rank - 1.0
    pos = jnp.broadcast_to(pos_lanes, (SUBLANES, rows)).T[:, 0:1]
    col = lax.broadcasted_iota(jnp.int32, (rows, rows), 1).astype(F32)
    row = lax.broadcasted_iota(jnp.int32, (rows, rows), 0).astype(F32)
    unsort = (col == pos).astype(BF16)
    sort = (row == pos_lanes).astype(BF16)
    xs_scr[...] = jnp.dot(sort, hb, preferred_element_type=F32).astype(BF16)
    pad = jnp.zeros((LANES - N_EXPERT_GROUPS - N_EXPERTS, rows), F32)
    gates_pad = jnp.concatenate([jnp.zeros((N_EXPERT_GROUPS, rows), F32), gates_t, pad], axis=0)
    gs_t = sum(jnp.dot(part, unsort, preferred_element_type=F32) for part in _split3(gates_pad))
    gs_scr[...] = gs_t.T
    ys_scr[...] = jnp.zeros_like(ys_scr)
    blk = MOE_ROW_BLOCK
    n_blk = rows // blk
    cuts = [jnp.int32(k * blk) for k in range(1, n_blk)] + offs[1:N_EXPERT_GROUPS]
    assert len(cuts) == 6
    for a, b in ((0, 5), (1, 3), (2, 4), (1, 2), (3, 4), (0, 3), (2, 5), (0, 1), (2, 3), (4, 5), (1, 2), (3, 4)):
        cuts[a], cuts[b] = jnp.minimum(cuts[a], cuts[b]), jnp.maximum(cuts[a], cuts[b])
    cuts = [jnp.int32(0)] + cuts + [jnp.int32(rows)]
    lane_b = lax.broadcasted_iota(jnp.int32, (blk, LANES), 1)
    for j in range(len(cuts) - 1):
        start = cuts[j]
        live = (cuts[j + 1] > start).astype(F32)
        k = jnp.minimum(start // blk, n_blk - 1)
        grp = sum((offs[q] <= start).astype(jnp.int32) for q in range(1, N_EXPERT_GROUPS))
        sl = pl.ds(pl.multiple_of(k * blk, blk), blk)
        xb = xs_scr[sl, :]
        gb = gs_scr[sl, :] * live
        acts = []
        for i in range(EXPERTS_PER_GROUP):
            e = grp * EXPERTS_PER_GROUP + i
            gate = jnp.sum(jnp.where(lane_b == N_EXPERT_GROUPS + e, gb, 0.0), axis=-1, keepdims=True)
            hg = jnp.dot(xb, wg_ref[e], preferred_element_type=F32)
            hu = jnp.dot(xb, wu_ref[e], preferred_element_type=F32)
            acts.append((jax.nn.silu(hg) * hu * gate).astype(BF16))
        ys_scr[sl, :] += jnp.dot(jnp.concatenate(acts, axis=1), wd_ref[grp], preferred_element_type=F32)
    hi = ys_scr[...].astype(BF16)
    lo = (ys_scr[...] - hi.astype(F32)).astype(BF16)
    return unsort, hi, lo


def _ffn_body(x_ref, ssm_ref, attn_ref, mod_ref, gpost_ref, gpre_ref, gffn_ref, wout_ref, wr_ref, br_ref,
              tri_ref, wg_ref, wu_ref, wd_ref, o_ref, xs_scr, gs_scr, ys_scr, *, chunked):
    g, r, d = x_ref.shape
    rows = g * r
    if chunked:
        ssm = jnp.swapaxes(ssm_ref[...].astype(F32), 0, 1).reshape(rows, ssm_ref.shape[-1]).astype(BF16)
    else:
        ssm = ssm_ref[...]
    sub = min(rows, FFN_SUB)
    x1s, hbs, hlos = [], [], []
    for c in range(rows // sub):
        rs = slice(c * sub, (c + 1) * sub)
        if g == 1:
            x = x_ref[:, rs, :]
            md = mod_ref
        else:
            gsl = slice(c * (sub // r), (c + 1) * (sub // r))
            x = x_ref[gsl]
            md = mod_ref.at[gsl]
        gg = x.shape[0]
        cat = jnp.concatenate([ssm[rs], attn_ref[rs, :]], axis=1)
        mixed = jnp.dot(cat, wout_ref[...], preferred_element_type=F32)
        x1 = x + md[:, 2] * _rms(mixed, gpost_ref[0]).reshape(gg, -1, d)
        hn = (_rms(x1, gpre_ref[...]) * (1.0 + md[:, 4]) + md[:, 3]).reshape(sub, d)
        hb = hn.astype(BF16)
        x1s.append(x1)
        hbs.append(hb)
        hlos.append((hn - hb.astype(F32)).astype(BF16))
    hb = jnp.concatenate(hbs, axis=0)
    hlo = jnp.concatenate(hlos, axis=0)
    nt = functools.partial(lax.dot_general, dimension_numbers=(((1,), (1,)), ((), ())), preferred_element_type=F32)
    logits_t = nt(wr_ref[0], hb) + nt(wr_ref[0], hlo) + nt(wr_ref[1], hb) + br_ref[...]
    gates_t, gidx = _route(logits_t)
    unsort, hi, lo = _moe_sorted(hb, gates_t, gidx, tri_ref, wg_ref, wu_ref, wd_ref, xs_scr, gs_scr, ys_scr)
    for c in range(rows // sub):
        rs = slice(c * sub, (c + 1) * sub)
        ffn = (jnp.dot(unsort[rs], hi, preferred_element_type=F32) + jnp.dot(unsort[rs], lo, preferred_element_type=F32))
        if g == 1:
            o_ref[:, rs, :] = x1s[c] + mod_ref[:, 5] * _rms(ffn, gffn_ref[0]).reshape(1, sub, d)
        else:
            gsl = slice(c * (sub // r), (c + 1) * (sub // r))
            o_ref[gsl] = x1s[c] + mod_ref[gsl, 5] * _rms(ffn, gffn_ref[0]).reshape(sub // r, r, d)


def _ffn(x3, ssm_n, attn_n, mod4, fw, gb, rb, chunk=None):
    nb, nr, d = x3.shape
    assert (gb == 1 or rb == nr) and nr % rb == 0 and nb % gb == 0
    half = ssm_n.shape[-1]
    nj = nr // rb
    tm = gb * rb
    row_map = lambda i, j: (i * nj + j, 0)
    vec = _resident((1, 1, d))
    if chunk is None:
        ssm_spec = pl.BlockSpec((tm, half), row_map)
    else:
        assert gb == 1 and rb % chunk == 0
        ssm_spec = pl.BlockSpec((None, chunk, rb // chunk, half), lambda i, j: (i, 0, j, 0))
    return pl.pallas_call(
        functools.partial(_ffn_body, chunked=chunk is not None),
        out_shape=jax.ShapeDtypeStruct((nb, nr, d), F32),
        grid=(nb // gb, nj),
        in_specs=[pl.BlockSpec((gb, rb, d), lambda i, j: (i, j, 0)),
                  ssm_spec,
                  pl.BlockSpec((tm, attn_n.shape[1]), row_map),
                  pl.BlockSpec((gb, N_MOD, 1, d), lambda i, j: (i, 0, 0, 0)),
                  vec, vec, vec,
                  _resident(fw["wout"].shape), _resident(fw["wr"].shape), _resident(fw["br"].shape),
                  _resident((tm, tm)),
                  _resident(fw["wg"].shape), _resident(fw["wu"].shape), _resident(fw["wd"].shape)],
        out_specs=pl.BlockSpec((gb, rb, d), lambda i, j: (i, j, 0)),
        scratch_shapes=[pltpu.VMEM((tm, d), BF16), pltpu.VMEM((tm, LANES), F32), pltpu.VMEM((tm, d), F32)],
        compiler_params=_cparams("arbitrary", "arbitrary"),
        name="outproj_moe",
    )(x3, ssm_n, attn_n, mod4, fw["gpost"], fw["gpre"], fw["gffn"], fw["wout"], fw["wr"], fw["br"],
      jnp.triu(jnp.ones((tm, tm), BF16)), fw["wg"], fw["wu"], fw["wd"])


def _ffn_weights(g_post_mix, g_pre_ffn, g_post_ffn, w_out, w_gr, b_gr, w_er, b_er, w_gate, w_up, w_down):
    d = w_out.shape[0]
    n_route = N_EXPERT_GROUPS + N_EXPERTS
    wr = jnp.concatenate([w_gr, w_er.reshape(d, N_EXPERTS)], axis=1)
    br = jnp.concatenate([b_gr, b_er.reshape(N_EXPERTS)])
    wr = jnp.pad(wr, ((0, 0), (0, ROUTE_ROWS - n_route))).T
    return {
        "gpost": g_post_mix.reshape(1, 1, d), "gpre": g_pre_ffn.reshape(1, 1, d), "gffn": g_post_ffn.reshape(1, 1, d),
        "wout": w_out.astype(BF16),
        "wr": _hi_lo(wr),
        "br": jnp.pad(br, (0, ROUTE_ROWS - n_route)).reshape(ROUTE_ROWS, 1),
        "wg": w_gate.astype(BF16), "wu": w_up.astype(BF16),
        "wd": w_down.astype(BF16).reshape(N_EXPERT_GROUPS, -1, d),
    }


def _rel_bucket(dist):
    max_exact = NUM_BUCKETS // 2
    dd = jnp.maximum(dist, 0)
    log_ratio = jnp.log(jnp.maximum(dd, 1).astype(F32) / max_exact) / math.log(MAX_DISTANCE / max_exact)
    large = jnp.minimum(max_exact + (log_ratio * (NUM_BUCKETS - max_exact)).astype(jnp.int32), NUM_BUCKETS - 1)
    return jnp.where(dd < max_exact, dd, large)


def _bias_and_band(dist, table):
    onehot = (_rel_bucket(dist)[:, :, None] == jnp.arange(NUM_BUCKETS)[None, None, :]).astype(F32)
    bias = jnp.einsum('qkb,bh->hqk', onehot, table.astype(F32), precision=HIGHEST)
    band = ((dist >= 0) & (dist <= WINDOW)).astype(F32)
    return bias, band


def _layer(xp, xs, ck, cv, h0re, h0im, mod_p, mod_s, table, p):
    nb, seq, d = xp.shape
    nbs, t, _ = xs.shape
    w = ck.shape[1]
    kv_w = N_KV_HEADS * HEAD_DIM
    w_in_bf = p["w_in"].astype(BF16)
    sw = _ssm_weights(p["ssm_lam_re"], p["ssm_lam_im"], p["ssm_log_dt"], p["ssm_b_re"], p["ssm_b_im"],
                      p["ssm_c_re"], p["ssm_c_im"], p["ssm_d"], p["w_glu"], p["b_glu"], p["g_ssm_out"])
    fw = _ffn_weights(p["g_post_mix"], p["g_pre_ffn"], p["g_post_ffn"], p["w_out"],
                      p["w_group_router"], p["b_group_router"], p["w_expert_router"], p["b_expert_router"],
                      p["w_exp_gate"], p["w_exp_up"], p["w_exp_down"])
    n_state = sw["lre"].shape[1]
    ng = p["ssm_lam_re"].shape[0]

    u2, q, k, v = _inproj(xp, mod_p, p["g_pre_mix"], w_in_bf, 1, INPROJ_TILE, chunk=SSM_CHUNK)
    blk = WINDOW
    dist_p = jnp.arange(blk)[:, None] + blk - jnp.arange(2 * blk)[None, :]
    bias_p, band_p = _bias_and_band(dist_p, table)
    attn_p = _attn_prompt(q, k, v, p["attn_sinks"], bias_p, band_p, p["g_attn_out"], nb, seq)
    kw = _ssm_chunk_weights(p["ssm_lam_re"], p["ssm_lam_im"], p["ssm_log_dt"], p["ssm_b_re"], p["ssm_b_im"],
                            p["ssm_c_re"], p["ssm_c_im"], SSM_CHUNK, seq // SSM_CHUNK)
    yt, hend = _ssm_chunked(_ssm_transpose(u2, p["ssm_b_re"].shape[2]), kw)
    ssm_p = _ssm_tail_call(yt, u2, sw)
    yp = _ffn(xp, ssm_p, attn_p, mod_p, fw, 1, TOKEN_TILE, chunk=SSM_CHUNK)
    wp = min(WINDOW, seq)
    k_p = k.reshape(nb, seq, kv_w)[:, seq - wp:].reshape(nb, wp, N_KV_HEADS, HEAD_DIM)
    v_p = v.reshape(nb, seq, kv_w)[:, seq - wp:].reshape(nb, wp, N_KV_HEADS, HEAD_DIM)
    n_p = n_state // ng
    hre_p = hend[:, :, 0, :n_p]
    him_p = hend[:, :, 0, n_p:]

    gs = TOKEN_TILE // t
    us, qs, ks, vs = _inproj(xs, mod_s, p["g_pre_mix"], w_in_bf, gs, t)
    dist_s = jnp.arange(t)[:, None] + w - jnp.arange(w + t)[None, :]
    bias_s, band_s = _bias_and_band(dist_s, table)
    attn_s, k_s, v_s = _attn_sample(qs.reshape(nbs, t, -1), ks.reshape(nbs, t, kv_w), vs.reshape(nbs, t, kv_w),
                                    ck.reshape(nbs, w, kv_w), cv.reshape(nbs, w, kv_w),
                                    p["attn_sinks"], bias_s, band_s, p["g_attn_out"], ATTN_SAMPLE_BLOCK)
    u_tm = jnp.swapaxes(us.reshape(nbs, t, -1), 0, 1)
    ssm_tm, hre_s, him_s = _ssm_sample(u_tm, h0re.reshape(nbs, n_state), h0im.reshape(nbs, n_state), sw)
    ssm_s = jnp.swapaxes(ssm_tm, 0, 1).reshape(nbs * t, -1)
    ys = _ffn(xs, ssm_s, attn_s.reshape(nbs * t, -1), mod_s, fw, gs, t)
    k_s = k_s.reshape(nbs, w, N_KV_HEADS, HEAD_DIM)
    v_s = v_s.reshape(nbs, w, N_KV_HEADS, HEAD_DIM)
    hre_s = hre_s.reshape(nbs, ng, n_state // ng)
    him_s = him_s.reshape(nbs, ng, n_state // ng)
    return yp, ys, k_p, v_p, hre_p, him_p, k_s, v_s, hre_s, him_s


def kernel(x_prompt, x_sample, cache_k, cache_v, state_ssm_re, state_ssm_im, c_prompt, c_sample, rel_bias_table,
           w_ada, b_ada, g_pre_mix, g_post_mix, g_pre_ffn, g_post_ffn, w_in, ssm_lam_re, ssm_lam_im, ssm_log_dt,
           ssm_b_re, ssm_b_im, ssm_c_re, ssm_c_im, ssm_d, w_glu, b_glu, attn_sinks, g_ssm_out, g_attn_out, w_out,
           w_group_router, b_group_router, w_expert_router, b_expert_router, w_exp_gate, w_exp_up, w_exp_down):
    params = dict(
        w_ada=w_ada, b_ada=b_ada, g_pre_mix=g_pre_mix, g_post_mix=g_post_mix, g_pre_ffn=g_pre_ffn,
        g_post_ffn=g_post_ffn, w_in=w_in, ssm_lam_re=ssm_lam_re, ssm_lam_im=ssm_lam_im, ssm_log_dt=ssm_log_dt,
        ssm_b_re=ssm_b_re, ssm_b_im=ssm_b_im, ssm_c_re=ssm_c_re, ssm_c_im=ssm_c_im, ssm_d=ssm_d, w_glu=w_glu,
        b_glu=b_glu, attn_sinks=attn_sinks, g_ssm_out=g_ssm_out, g_attn_out=g_attn_out, w_out=w_out,
        w_group_router=w_group_router, b_group_router=b_group_router, w_expert_router=w_expert_router,
        b_expert_router=b_expert_router, w_exp_gate=w_exp_gate, w_exp_up=w_exp_up, w_exp_down=w_exp_down)
    depth = w_in.shape[0]
    nb, nbs = x_prompt.shape[0], x_sample.shape[0]
    d = x_prompt.shape[2]
    pad = (-nb) % 8
    yp, ys = x_prompt, x_sample
    outs = [[] for _ in range(8)]
    for li in range(depth):
        p = {name: val[li] for name, val in params.items()}
        c_all = jnp.concatenate([c_prompt, jnp.zeros((pad, d), F32), c_sample], axis=0)
        mod = _modulation(c_all, p["w_ada"], p["b_ada"])
        mod_p = mod[:nb].reshape(nb, N_MOD, 1, d)
        mod_s = mod[nb + pad:].reshape(nbs, N_MOD, 1, d)
        res = _layer(yp, ys, cache_k[li], cache_v[li], state_ssm_re[li], state_ssm_im[li], mod_p, mod_s,
                     rel_bias_table, p)
        yp, ys = res[0], res[1]
        for acc, val in zip(outs, res[2:]):
            acc.append(val)
    return (yp, ys) + tuple(jnp.stack(o) for o in outs)
```
